```python
import math
import jax, jax.numpy as jnp
from jax import lax
import numpy as np

D_MODEL = 2048
BATCH = 8
SEQ = 8192
DEPTH = 4

HEAD_DIM = 128
GRID_W = 64
NA_HEADS = 8
NA_WIDTH = NA_HEADS * HEAD_DIM
NA_WIN_R = 8
NA_WIN_C = 16
WA_HEADS = 8
WA_KV_HEADS = 2
WA_WIDTH = WA_HEADS * HEAD_DIM
WA_KV_WIDTH = WA_KV_HEADS * HEAD_DIM
WA_WINDOW = 128
WA_BLOCK = 128
MIX_WIDTH = NA_WIDTH + WA_WIDTH
PROJ_SPLITS = (NA_WIDTH, NA_WIDTH, NA_WIDTH, WA_WIDTH, WA_KV_WIDTH, WA_KV_WIDTH)
PROJ_WIDTH = sum(PROJ_SPLITS)
D_FF = 5632
CONV_W = 3
ROPE_THETA = 10000.0
EPS = 1e-6
NEG = -1e30

kernel_name = "hybrid_na_swa_convffn_encoder"


def rms_norm(x, g):
    xf = x.astype(jnp.float32)
    y = xf * lax.rsqrt(jnp.mean(xf * xf, axis=-1, keepdims=True) + EPS)
    return (y * g.astype(jnp.float32)).astype(x.dtype)


def rope(x, positions):
    d = x.shape[-1]
    inv = ROPE_THETA ** (-jnp.arange(0, d, 2, dtype=jnp.float32) / d)
    ang = positions.astype(jnp.float32)[:, None] * inv[None, :]
    cos = jnp.cos(ang)[None, :, None, :]
    sin = jnp.sin(ang)[None, :, None, :]
    xf = x.astype(jnp.float32)
    x1, x2 = xf[..., : d // 2], xf[..., d // 2:]
    out = jnp.concatenate([x1 * cos - x2 * sin, x2 * cos + x1 * sin], axis=-1)
    return out.astype(x.dtype)


def neighborhood_attention(q, k, v, rpb):
    b, s, h, d = q.shape
    rows = s // GRID_W
    wr = min(NA_WIN_R, rows)
    wc = NA_WIN_C
    qg = q.reshape(b, rows, GRID_W, h, d)
    kg = k.reshape(b, rows, GRID_W, h, d)
    vg = v.reshape(b, rows, GRID_W, h, d)
    r = jnp.arange(rows)
    row_start = jnp.clip(r - wr // 2, 0, rows - wr)
    row_idx = row_start[:, None] + jnp.arange(wr)[None, :]
    k_rows = kg[:, row_idx]
    v_rows = vg[:, row_idx]
    c = jnp.arange(GRID_W)
    col_start = jnp.clip(c - wc // 2, 0, GRID_W - wc)
    col_mask = (c[None, :] >= col_start[:, None]) & (c[None, :] < col_start[:, None] + wc)
    dr = row_idx - r[:, None] + (NA_WIN_R - 1)
    dc = jnp.clip(c[None, :] - c[:, None], -(wc - 1), wc - 1) + (NA_WIN_C - 1)
    bias = rpb[:, dr]
    bias = bias[:, :, :, dc]
    bias = jnp.transpose(bias, (0, 1, 3, 2, 4)).astype(jnp.float32)
    scale = 1.0 / math.sqrt(d)
    sc = jnp.einsum('brqhd,brwkhd->bhrqwk', qg, k_rows).astype(jnp.float32) * scale
    sc = sc + bias[None]
    sc = jnp.where(col_mask[:, None, :], sc, NEG)
    shp = sc.shape
    p = jax.nn.softmax(sc.reshape(shp[:-2] + (wr * GRID_W,)), axis=-1).reshape(shp)
    out = jnp.einsum('bhrqwk,brwkhd->brqhd', p.astype(v.dtype), v_rows)
    return out.reshape(b, s, h * d)


def windowed_gqa_sink(q, k, v, sink):
    b, s, hq, d = q.shape
    hkv = k.shape[2]
    g = hq // hkv
    nb = s // WA_BLOCK
    qb = q.reshape(b, nb, WA_BLOCK, hkv, g, d)
    pad = ((0, 0), (WA_BLOCK, WA_BLOCK), (0, 0), (0, 0))
    kp = jnp.pad(k, pad).reshape(b, nb + 2, WA_BLOCK, hkv, d)
    vp = jnp.pad(v, pad).reshape(b, nb + 2, WA_BLOCK, hkv, d)
    kw = jnp.concatenate([kp[:, :-2], kp[:, 1:-1], kp[:, 2:]], axis=2)
    vw = jnp.concatenate([vp[:, :-2], vp[:, 1:-1], vp[:, 2:]], axis=2)
    blk = jnp.arange(nb)[:, None]
    qpos = blk * WA_BLOCK + jnp.arange(WA_BLOCK)[None, :]
    kpos = (blk - 1) * WA_BLOCK + jnp.arange(3 * WA_BLOCK)[None, :]
    diff = kpos[:, None, :] - qpos[:, :, None]
    valid = (jnp.abs(diff) <= WA_WINDOW) & (kpos[:, None, :] >= 0) & (kpos[:, None, :] < s)
    scale = 1.0 / math.sqrt(d)
    sc = jnp.einsum('bnqhgd,bnkhd->bhgnqk', qb, kw).astype(jnp.float32) * scale
    sc = jnp.where(valid, sc, NEG)
    sink_l = sink.astype(jnp.float32).reshape(hkv, g)[None, :, :, None, None, None]
    m = jnp.maximum(jnp.max(sc, axis=-1, keepdims=True), sink_l)
    e = jnp.exp(sc - m)
    p = e / (jnp.sum(e, axis=-1, keepdims=True) + jnp.exp(sink_l - m))
    out = jnp.einsum('bhgnqk,bnkhd->bnqhgd', p.astype(v.dtype), vw)
    return out.reshape(b, s, hq * d)


def depthwise_conv(u, w, bias):
    up = jnp.pad(u, ((0, 0), (1, 1), (0, 0)))
    return up[:, :-2] * w[0] + up[:, 1:-1] * w[1] + up[:, 2:] * w[2] + bias


def _fwd_setup_inputs(seed: int = 0) -> dict:
    key = jax.random.key(seed)
    ks = jax.random.split(key, 20)
    f32 = jnp.float32
    nrm = lambda k, shp, sc: jax.random.normal(k, shp, f32) * sc
    centre = jnp.zeros((CONV_W, 1), f32).at[CONV_W // 2].set(1.0)
    return {
        "x": nrm(ks[0], (BATCH, SEQ, D_MODEL), 1.0),
        "positions": jnp.arange(SEQ, dtype=jnp.int32),
        "ln1_g": 1.0 + nrm(ks[1], (DEPTH, D_MODEL), 0.02),
        "w_in": nrm(ks[2], (DEPTH, D_MODEL, PROJ_WIDTH), D_MODEL ** -0.5),
        "qn_a": 1.0 + nrm(ks[3], (DEPTH, HEAD_DIM), 0.02),
        "kn_a": 1.0 + nrm(ks[4], (DEPTH, HEAD_DIM), 0.02),
        "rpb": nrm(ks[5], (DEPTH, NA_HEADS, 2 * NA_WIN_R - 1, 2 * NA_WIN_C - 1), 0.1),
        "qn_b": 1.0 + nrm(ks[6], (DEPTH, HEAD_DIM), 0.02),
        "kn_b": 1.0 + nrm(ks[7], (DEPTH, HEAD_DIM), 0.02),
        "sink": nrm(ks[8], (DEPTH, WA_HEADS), 0.5),
        "on_a": 1.0 + nrm(ks[9], (DEPTH, NA_WIDTH), 0.02),
        "on_b": 1.0 + nrm(ks[10], (DEPTH, WA_WIDTH), 0.02),
        "w_out": nrm(ks[11], (DEPTH, MIX_WIDTH, D_MODEL), 0.5 * MIX_WIDTH ** -0.5),
        "ln2_g": 1.0 + nrm(ks[12], (DEPTH, D_MODEL), 0.02),
        "w_up": nrm(ks[13], (DEPTH, D_MODEL, 2 * D_FF), D_MODEL ** -0.5),
        "conv_w": centre[None] + nrm(ks[14], (DEPTH, CONV_W, 2 * D_FF), 0.3),
        "conv_b": nrm(ks[15], (DEPTH, 2 * D_FF), 0.01),
        "w_down": nrm(ks[16], (DEPTH, D_FF, D_MODEL), 0.5 * D_FF ** -0.5),
    }


def _fwd_reference(x, positions, ln1_g, w_in, qn_a, kn_a, rpb, qn_b, kn_b, sink,
              on_a, on_b, w_out, ln2_g, w_up, conv_w, conv_b, w_down):
    b, s, _ = x.shape
    cuts = list(np.cumsum(PROJ_SPLITS)[:-1])
    for l in range(DEPTH):
        h = rms_norm(x, ln1_g[l])
        proj = h @ w_in[l]
        qa, ka, va, qb, kb, vb = jnp.split(proj, cuts, axis=-1)
        qa = rms_norm(qa.reshape(b, s, NA_HEADS, HEAD_DIM), qn_a[l])
        ka = rms_norm(ka.reshape(b, s, NA_HEADS, HEAD_DIM), kn_a[l])
        va = va.reshape(b, s, NA_HEADS, HEAD_DIM)
        oa = neighborhood_attention(qa, ka, va, rpb[l])
        qb = rope(rms_norm(qb.reshape(b, s, WA_HEADS, HEAD_DIM), qn_b[l]), positions)
        kb = rope(rms_norm(kb.reshape(b, s, WA_KV_HEADS, HEAD_DIM), kn_b[l]), positions)
        vb = vb.reshape(b, s, WA_KV_HEADS, HEAD_DIM)
        ob = windowed_gqa_sink(qb, kb, vb, sink[l])
        o = jnp.concatenate([rms_norm(oa, on_a[l]), rms_norm(ob, on_b[l])], axis=-1)
        x = x + o @ w_out[l]
        h2 = rms_norm(x, ln2_g[l])
        u = depthwise_conv(h2 @ w_up[l], conv_w[l], conv_b[l])
        gate, up = u[..., :D_FF], u[..., D_FF:]
        x = x + (jax.nn.silu(gate) * up) @ w_down[l]
    return x


import jax as _jax
import jax.numpy as _jnp

TWIN_FORMAT = 'train_step'
FWD_PARAMS = ['x', 'positions', 'ln1_g', 'w_in', 'qn_a', 'kn_a', 'rpb', 'qn_b', 'kn_b', 'sink', 'on_a', 'on_b', 'w_out', 'ln2_g', 'w_up', 'conv_w', 'conv_b', 'w_down']
TWIN_WEIGHTS = ['ln1_g', 'w_in', 'qn_a', 'kn_a', 'rpb', 'qn_b', 'kn_b', 'sink', 'on_a', 'on_b', 'w_out', 'ln2_g', 'w_up', 'conv_w', 'conv_b', 'w_down']
TWIN_DIFF_INPUT = 'x'
TWIN_INPUTS = ['x', 'positions', 'ln1_g', 'w_in', 'qn_a', 'kn_a', 'rpb', 'qn_b', 'kn_b', 'sink', 'on_a', 'on_b', 'w_out', 'ln2_g', 'w_up', 'conv_w', 'conv_b', 'w_down', 'loss_target', 'm_ln1_g', 'm_w_in', 'm_qn_a', 'm_kn_a', 'm_rpb', 'm_qn_b', 'm_kn_b', 'm_sink', 'm_on_a', 'm_on_b', 'm_w_out', 'm_ln2_g', 'm_w_up', 'm_conv_w', 'm_conv_b', 'm_w_down', 'v_ln1_g', 'v_w_in', 'v_qn_a', 'v_kn_a', 'v_rpb', 'v_qn_b', 'v_kn_b', 'v_sink', 'v_on_a', 'v_on_b', 'v_w_out', 'v_ln2_g', 'v_w_up', 'v_conv_w', 'v_conv_b', 'v_w_down']
TWIN_OUTPUTS = ['loss', 'grad_x', 'grad_ln1_g', 'grad_w_in', 'grad_qn_a', 'grad_kn_a', 'grad_rpb', 'grad_qn_b', 'grad_kn_b', 'grad_sink', 'grad_on_a', 'grad_on_b', 'grad_w_out', 'grad_ln2_g', 'grad_w_up', 'grad_conv_w', 'grad_conv_b', 'grad_w_down', 'delta_ln1_g', 'delta_w_in', 'delta_qn_a', 'delta_kn_a', 'delta_rpb', 'delta_qn_b', 'delta_kn_b', 'delta_sink', 'delta_on_a', 'delta_on_b', 'delta_w_out', 'delta_ln2_g', 'delta_w_up', 'delta_conv_w', 'delta_conv_b', 'delta_w_down', 'new_m_ln1_g', 'new_m_w_in', 'new_m_qn_a', 'new_m_kn_a', 'new_m_rpb', 'new_m_qn_b', 'new_m_kn_b', 'new_m_sink', 'new_m_on_a', 'new_m_on_b', 'new_m_w_out', 'new_m_ln2_g', 'new_m_w_up', 'new_m_conv_w', 'new_m_conv_b', 'new_m_w_down', 'new_v_ln1_g', 'new_v_w_in', 'new_v_qn_a', 'new_v_kn_a', 'new_v_rpb', 'new_v_qn_b', 'new_v_kn_b', 'new_v_sink', 'new_v_on_a', 'new_v_on_b', 'new_v_w_out', 'new_v_ln2_g', 'new_v_w_up', 'new_v_conv_w', 'new_v_conv_b', 'new_v_w_down']
TWIN_LEAF_KINDS = {'loss': 'loss', 'grad_x': 'grad_x', 'grad_ln1_g': 'grad_w', 'grad_w_in': 'grad_w', 'grad_qn_a': 'grad_w', 'grad_kn_a': 'grad_w', 'grad_rpb': 'grad_w', 'grad_qn_b': 'grad_w', 'grad_kn_b': 'grad_w', 'grad_sink': 'grad_w', 'grad_on_a': 'grad_w', 'grad_on_b': 'grad_w', 'grad_w_out': 'grad_w', 'grad_ln2_g': 'grad_w', 'grad_w_up': 'grad_w', 'grad_conv_w': 'grad_w', 'grad_conv_b': 'grad_w', 'grad_w_down': 'grad_w', 'delta_ln1_g': 'delta_w', 'delta_w_in': 'delta_w', 'delta_qn_a': 'delta_w', 'delta_kn_a': 'delta_w', 'delta_rpb': 'delta_w', 'delta_qn_b': 'delta_w', 'delta_kn_b': 'delta_w', 'delta_sink': 'delta_w', 'delta_on_a': 'delta_w', 'delta_on_b': 'delta_w', 'delta_w_out': 'delta_w', 'delta_ln2_g': 'delta_w', 'delta_w_up': 'delta_w', 'delta_conv_w': 'delta_w', 'delta_conv_b': 'delta_w', 'delta_w_down': 'delta_w', 'new_m_ln1_g': 'new_m', 'new_m_w_in': 'new_m', 'new_m_qn_a': 'new_m', 'new_m_kn_a': 'new_m', 'new_m_rpb': 'new_m', 'new_m_qn_b': 'new_m', 'new_m_kn_b': 'new_m', 'new_m_sink': 'new_m', 'new_m_on_a': 'new_m', 'new_m_on_b': 'new_m', 'new_m_w_out': 'new_m', 'new_m_ln2_g': 'new_m', 'new_m_w_up': 'new_m', 'new_m_conv_w': 'new_m', 'new_m_conv_b': 'new_m', 'new_m_w_down': 'new_m', 'new_v_ln1_g': 'new_v', 'new_v_w_in': 'new_v', 'new_v_qn_a': 'new_v', 'new_v_kn_a': 'new_v', 'new_v_rpb': 'new_v', 'new_v_qn_b': 'new_v', 'new_v_kn_b': 'new_v', 'new_v_sink': 'new_v', 'new_v_on_a': 'new_v', 'new_v_on_b': 'new_v', 'new_v_w_out': 'new_v', 'new_v_ln2_g': 'new_v', 'new_v_w_up': 'new_v', 'new_v_conv_w': 'new_v', 'new_v_conv_b': 'new_v', 'new_v_w_down': 'new_v'}


def _forward(args):
    return _fwd_reference(*[args[k] for k in FWD_PARAMS])


def _output_shape():
    def fwd():
        inp = _fwd_setup_inputs(0)
        return _fwd_reference(*[inp[k] for k in FWD_PARAMS])
    out = _jax.eval_shape(fwd)
    return out.shape, out.dtype

N_MICROBATCH = 1
ADAM_LR = 0.001
ADAM_B1 = 0.9
ADAM_B2 = 0.999
ADAM_EPS = 1e-08
ADAM_WD = 0.01
ADAM_STEP = 10
PER_EXAMPLE_BATCH_AXIS = {'x': 0, 'loss_target': 0}
SHARED_INPUTS = ['positions']
_WEIGHT_DTYPES = {'ln1_g': _jnp.float32, 'w_in': _jnp.float32, 'qn_a': _jnp.float32, 'kn_a': _jnp.float32, 'rpb': _jnp.float32, 'qn_b': _jnp.float32, 'kn_b': _jnp.float32, 'sink': _jnp.float32, 'on_a': _jnp.float32, 'on_b': _jnp.float32, 'w_out': _jnp.float32, 'ln2_g': _jnp.float32, 'w_up': _jnp.float32, 'conv_w': _jnp.float32, 'conv_b': _jnp.float32, 'w_down': _jnp.float32}
MOMENT_SCALE = {'ln1_g': 1.955375e+00, 'w_in': 1.396183e+00, 'qn_a': 5.682298e-01, 'kn_a': 5.673482e-01, 'rpb': 8.427561e-02, 'qn_b': 6.050290e-01, 'kn_b': 7.186876e-01, 'sink': 4.024240e-02, 'on_a': 7.915507e+00, 'on_b': 7.435946e+00, 'w_out': 4.320020e+00, 'ln2_g': 1.043228e+01, 'w_up': 2.317190e-01, 'conv_w': 1.112368e+00, 'conv_b': 1.287067e+00, 'w_down': 7.794233e-01}


def _to_microbatches(a, axis):
    t = _jnp.moveaxis(a, axis, 0)
    t = t.reshape((N_MICROBATCH, t.shape[0] // N_MICROBATCH) + t.shape[1:])
    return _jnp.moveaxis(t, 1, axis + 1)


def setup_inputs(seed: int = 0) -> dict:
    inp = _fwd_setup_inputs(seed)
    key = _jax.random.fold_in(_jax.random.key(seed), 7919)
    shape, _ = _output_shape()
    out = dict(inp)
    out["loss_target"] = _jax.random.normal(_jax.random.fold_in(key, 0), shape, _jnp.float32)
    for i, name in enumerate(TWIN_WEIGHTS):
        w = inp[name].astype(_jnp.float32)
        if MOMENT_SCALE is None:
            s = _jnp.sqrt(_jnp.mean(_jnp.square(w)) + 1e-30)
        else:
            s = MOMENT_SCALE[name]
        km, kv = _jax.random.split(_jax.random.fold_in(key, i + 1))
        out[name] = w
        out["m_" + name] = s * _jax.random.normal(km, w.shape, _jnp.float32)
        out["v_" + name] = (s * s) * _jax.random.uniform(kv, w.shape, _jnp.float32, 0.5, 1.5)
    if N_MICROBATCH > 1:
        for name, axis in PER_EXAMPLE_BATCH_AXIS.items():
            out[name] = _to_microbatches(out[name], axis)
    return {'x': out['x'], 'positions': out['positions'], 'ln1_g': out['ln1_g'], 'w_in': out['w_in'], 'qn_a': out['qn_a'], 'kn_a': out['kn_a'], 'rpb': out['rpb'], 'qn_b': out['qn_b'], 'kn_b': out['kn_b'], 'sink': out['sink'], 'on_a': out['on_a'], 'on_b': out['on_b'], 'w_out': out['w_out'], 'ln2_g': out['ln2_g'], 'w_up': out['w_up'], 'conv_w': out['conv_w'], 'conv_b': out['conv_b'], 'w_down': out['w_down'], 'loss_target': out['loss_target'], 'm_ln1_g': out['m_ln1_g'], 'm_w_in': out['m_w_in'], 'm_qn_a': out['m_qn_a'], 'm_kn_a': out['m_kn_a'], 'm_rpb': out['m_rpb'], 'm_qn_b': out['m_qn_b'], 'm_kn_b': out['m_kn_b'], 'm_sink': out['m_sink'], 'm_on_a': out['m_on_a'], 'm_on_b': out['m_on_b'], 'm_w_out': out['m_w_out'], 'm_ln2_g': out['m_ln2_g'], 'm_w_up': out['m_w_up'], 'm_conv_w': out['m_conv_w'], 'm_conv_b': out['m_conv_b'], 'm_w_down': out['m_w_down'], 'v_ln1_g': out['v_ln1_g'], 'v_w_in': out['v_w_in'], 'v_qn_a': out['v_qn_a'], 'v_kn_a': out['v_kn_a'], 'v_rpb': out['v_rpb'], 'v_qn_b': out['v_qn_b'], 'v_kn_b': out['v_kn_b'], 'v_sink': out['v_sink'], 'v_on_a': out['v_on_a'], 'v_on_b': out['v_on_b'], 'v_w_out': out['v_w_out'], 'v_ln2_g': out['v_ln2_g'], 'v_w_up': out['v_w_up'], 'v_conv_w': out['v_conv_w'], 'v_conv_b': out['v_conv_b'], 'v_w_down': out['v_w_down']}


def _loss(weights, diff, rest, loss_target):
    with _jax.named_scope("forward"):
        args = {**rest, TWIN_DIFF_INPUT: diff, **{k: w.astype(_WEIGHT_DTYPES[k]) for k, w in weights.items()}}
        y = _forward(args)
    with _jax.named_scope("loss_head"):
        err = _jnp.square(y.astype(_jnp.float32) - loss_target)
        return 0.5 * _jnp.sum(_jnp.mean(err, axis=-1)) if err.ndim else 0.5 * err


def _adamw(w, g, m, v):
    m = ADAM_B1 * m + (1.0 - ADAM_B1) * g
    v = ADAM_B2 * v + (1.0 - ADAM_B2) * _jnp.square(g)
    m_hat = m / (1.0 - ADAM_B1 ** ADAM_STEP)
    v_hat = v / (1.0 - ADAM_B2 ** ADAM_STEP)
    delta = -ADAM_LR * (m_hat / (_jnp.sqrt(v_hat) + ADAM_EPS) + ADAM_WD * w)
    return delta, m, v


def reference(x, positions, ln1_g, w_in, qn_a, kn_a, rpb, qn_b, kn_b, sink, on_a, on_b, w_out, ln2_g, w_up, conv_w, conv_b, w_down, loss_target, m_ln1_g, m_w_in, m_qn_a, m_kn_a, m_rpb, m_qn_b, m_kn_b, m_sink, m_on_a, m_on_b, m_w_out, m_ln2_g, m_w_up, m_conv_w, m_conv_b, m_w_down, v_ln1_g, v_w_in, v_qn_a, v_kn_a, v_rpb, v_qn_b, v_kn_b, v_sink, v_on_a, v_on_b, v_w_out, v_ln2_g, v_w_up, v_conv_w, v_conv_b, v_w_down):
    given = dict(x=x, positions=positions, ln1_g=ln1_g, w_in=w_in, qn_a=qn_a, kn_a=kn_a, rpb=rpb, qn_b=qn_b, kn_b=kn_b, sink=sink, on_a=on_a, on_b=on_b, w_out=w_out, ln2_g=ln2_g, w_up=w_up, conv_w=conv_w, conv_b=conv_b, w_down=w_down, loss_target=loss_target, m_ln1_g=m_ln1_g, m_w_in=m_w_in, m_qn_a=m_qn_a, m_kn_a=m_kn_a, m_rpb=m_rpb, m_qn_b=m_qn_b, m_kn_b=m_kn_b, m_sink=m_sink, m_on_a=m_on_a, m_on_b=m_on_b, m_w_out=m_w_out, m_ln2_g=m_ln2_g, m_w_up=m_w_up, m_conv_w=m_conv_w, m_conv_b=m_conv_b, m_w_down=m_w_down, v_ln1_g=v_ln1_g, v_w_in=v_w_in, v_qn_a=v_qn_a, v_kn_a=v_kn_a, v_rpb=v_rpb, v_qn_b=v_qn_b, v_kn_b=v_kn_b, v_sink=v_sink, v_on_a=v_on_a, v_on_b=v_on_b, v_w_out=v_w_out, v_ln2_g=v_ln2_g, v_w_up=v_w_up, v_conv_w=v_conv_w, v_conv_b=v_conv_b, v_w_down=v_w_down)
    weights = {n: given[n] for n in TWIN_WEIGHTS}
    shared = {n: given[n] for n in SHARED_INPUTS}
    per_example = {n: given[n] for n in ['x']}
    grad_fn = _jax.value_and_grad(_loss, argnums=(0, 1))

    def one_microbatch(ex, loss_target):
        ex = dict(ex)
        diff = ex.pop(TWIN_DIFF_INPUT)
        return grad_fn(weights, diff, {**shared, **ex}, loss_target)

    if N_MICROBATCH == 1:
        loss, (grad_w, grad_x) = one_microbatch(per_example, given["loss_target"])
    else:
        def body(carry, xs):
            loss_sum, grad_sum = carry
            l_k, (gw_k, gx_k) = one_microbatch(xs[0], xs[1])
            with _jax.named_scope("update"):
                return (loss_sum + l_k, _jax.tree.map(_jnp.add, grad_sum, gw_k)), gx_k

        init = (_jnp.zeros((), _jnp.float32), _jax.tree.map(_jnp.zeros_like, weights))
        (loss, grad_w), grad_x = _jax.lax.scan(body, init, (per_example, given["loss_target"]))
    with _jax.named_scope("update"):
        delta_w, new_m, new_v = {}, {}, {}
        for n in TWIN_WEIGHTS:
            delta_w[n], new_m[n], new_v[n] = _adamw(weights[n], grad_w[n], given["m_" + n], given["v_" + n])
    return (loss, grad_x, *[grad_w[n] for n in TWIN_WEIGHTS], *[delta_w[n] for n in TWIN_WEIGHTS],
            *[new_m[n] for n in TWIN_WEIGHTS], *[new_v[n] for n in TWIN_WEIGHTS])
```

```python
import functools
import math

import jax
import jax.numpy as jnp
import numpy as np
from jax import lax
from jax.experimental import pallas as pl
from jax.experimental.pallas import tpu as pltpu

F32 = jnp.float32
BF16 = jnp.bfloat16

HEAD_DIM = 128
GRID_W = 64
NA_WIN_R = 8
NA_WIN_C = 16
WA_WINDOW = 128
WA_BLOCK = 128
ROPE_THETA = 10000.0
EPS = 1e-6
NEG = -1e30
ATTN_SCALE = 1.0 / math.sqrt(HEAD_DIM)

ADAM_LR = 0.001
ADAM_B1 = 0.9
ADAM_B2 = 0.999
ADAM_EPS = 1e-08
ADAM_WD = 0.01
ADAM_STEP = 10
ADAM_C1 = 1.0 - ADAM_B1 ** ADAM_STEP
ADAM_C2 = 1.0 - ADAM_B2 ** ADAM_STEP

V7X_VMEM_BYTES = 64 * 1024 * 1024
V7X_VMEM_CAP = V7X_VMEM_BYTES - 6 * 1024 * 1024
LANES = 128
SUBLANES = 8
N_CHIPS = 4
N_DEV = 8
MESH = pl.DeviceIdType.MESH

NN = ((1,), (0,))
NT = ((1,), (1,))
TN = ((0,), (0,))


def _tile(n, pref, mult):
    best = None
    d = mult
    while d <= min(n, pref):
        if n % d == 0:
            best = d
        d += mult
    return n if best is None else best


def _nbytes(shape, dtype):
    n = 1
    for s in shape:
        if s is not None:
            n *= s
    return n * jnp.dtype(dtype).itemsize


def _params(est_bytes=None, sem=None, **kw):
    if est_bytes is not None:
        kw["vmem_limit_bytes"] = int(min(V7X_VMEM_CAP, max(32 * 1024 * 1024, est_bytes * 5 // 4 + (4 << 20))))
    if sem is not None:
        kw["dimension_semantics"] = sem
    return pltpu.CompilerParams(**kw)


def _dot(a, b, contract):
    return lax.dot_general(a.astype(BF16), b.astype(BF16), (contract, ((), ())), preferred_element_type=F32)


def _mm(name, a, b, *, grid, a_spec, b_spec, o_spec, out_shape, contract, res=None, res_spec=None):
    nk = grid[2]
    acc_shape = tuple(s for s in o_spec.block_shape if s is not None)

    def body(*refs):
        if res is None:
            a_ref, b_ref, o_ref = refs[:3]
            r_ref = None
            scr = refs[3:]
        else:
            a_ref, b_ref, r_ref, o_ref = refs[:4]
            scr = refs[4:]
        p = _dot(a_ref[...], b_ref[...], contract)

        def finish(acc):
            if r_ref is not None:
                acc = acc + r_ref[...]
            o_ref[...] = acc.astype(o_ref.dtype)

        if nk == 1:
            finish(p)
        else:
            acc_ref = scr[0]
            k = pl.program_id(2)

            @pl.when(k == 0)
            def _():
                acc_ref[...] = p

            @pl.when(jnp.logical_and(k > 0, k < nk - 1))
            def _():
                acc_ref[...] += p

            @pl.when(k == nk - 1)
            def _():
                finish(acc_ref[...] + p)

    in_specs = [a_spec, b_spec]
    args = [a, b]
    est = 2 * (_nbytes(a_spec.block_shape, a.dtype) + _nbytes(b_spec.block_shape, b.dtype)
               + _nbytes(o_spec.block_shape, out_shape.dtype)) + 2 * _nbytes(acc_shape, F32)
    if res is not None:
        in_specs.append(res_spec)
        args.append(res)
        est += 2 * _nbytes(res_spec.block_shape, res.dtype)
    scratch = [] if nk == 1 else [pltpu.VMEM(acc_shape, F32)]
    return pl.pallas_call(
        body, name=name, grid=grid, in_specs=in_specs, out_specs=o_spec, out_shape=out_shape,
        scratch_shapes=scratch,
        compiler_params=_params(est, ("parallel", "parallel", "arbitrary")),
    )(*args)


def _mm_plain(name, a, b, contract, out_dtype, *, tm, tn, tk, res=None):
    if contract == NN:
        (M, K), N = a.shape, b.shape[1]
    elif contract == NT:
        (M, K), N = a.shape, b.shape[0]
    else:
        (K, M), N = a.shape, b.shape[1]
    tm, tn, tk = _tile(M, tm, LANES), _tile(N, tn, LANES), _tile(K, tk, LANES)
    grid = (M // tm, N // tn, K // tk)
    if contract == TN:
        a_spec = pl.BlockSpec((tk, tm), lambda i, j, k: (k, i))
    else:
        a_spec = pl.BlockSpec((tm, tk), lambda i, j, k: (i, k))
    if contract == NT:
        b_spec = pl.BlockSpec((tn, tk), lambda i, j, k: (j, k))
    else:
        b_spec = pl.BlockSpec((tk, tn), lambda i, j, k: (k, j))
    o_spec = pl.BlockSpec((tm, tn), lambda i, j, k: (i, j))
    return _mm(name, a, b, grid=grid, a_spec=a_spec, b_spec=b_spec, o_spec=o_spec,
               out_shape=jax.ShapeDtypeStruct((M, N), out_dtype), contract=contract,
               res=res, res_spec=None if res is None else pl.BlockSpec((tm, tn), lambda i, j, k: (i, j)))


class _ColShards:
    def __init__(self, cols_per_chip, tn, paired):
        self.c = cols_per_chip
        self.tn = tn
        self.tps = cols_per_chip // tn
        self.ntiles = N_CHIPS * self.tps
        self.paired = paired

    def nat(self, t):
        if not self.paired:
            return t
        return (t % 2) * (self.ntiles // 2) + t // 2

    def chip(self, t):
        return self.nat(t) // self.tps

    def within(self, t):
        return self.nat(t) % self.tps


def _mm_cols_fwd(name, a, wg, sh, out_dtype, *, tm):
    S, K = a.shape
    tm = _tile(S, tm, LANES)
    grid = (S // tm, sh.ntiles, 1)
    return _mm(name, a, wg, grid=grid,
               a_spec=pl.BlockSpec((tm, K), lambda i, j, k: (i, 0)),
               b_spec=pl.BlockSpec((None, K, sh.tn), lambda i, j, k: (sh.chip(j), 0, sh.within(j))),
               o_spec=pl.BlockSpec((tm, sh.tn), lambda i, j, k: (i, j)),
               out_shape=jax.ShapeDtypeStruct((S, sh.ntiles * sh.tn), out_dtype), contract=NN)


def _mm_cols_bwd_x(name, dy, wg, sh, out_dtype, *, tm, tn):
    S = dy.shape[0]
    K = wg.shape[1]
    tm, tn = _tile(S, tm, LANES), _tile(K, tn, LANES)
    grid = (S // tm, K // tn, sh.ntiles)
    return _mm(name, dy, wg, grid=grid,
               a_spec=pl.BlockSpec((tm, sh.tn), lambda i, j, k: (i, k)),
               b_spec=pl.BlockSpec((None, tn, sh.tn), lambda i, j, k: (sh.chip(k), j, sh.within(k))),
               o_spec=pl.BlockSpec((tm, tn), lambda i, j, k: (i, j)),
               out_shape=jax.ShapeDtypeStruct((S, K), out_dtype), contract=NT)


def _mm_cols_bwd_w(name, a, dy, sh, *, tm, tk):
    S, K = a.shape
    tm, tk = _tile(K, tm, LANES), _tile(S, tk, LANES)
    grid = (K // tm, sh.ntiles, S // tk)
    return _mm(name, a, dy, grid=grid,
               a_spec=pl.BlockSpec((tk, tm), lambda i, j, k: (k, i)),
               b_spec=pl.BlockSpec((tk, sh.tn), lambda i, j, k: (k, j)),
               o_spec=pl.BlockSpec((None, tm, sh.tn), lambda i, j, k: (sh.chip(j), i, sh.within(j))),
               out_shape=jax.ShapeDtypeStruct((N_CHIPS, K, sh.c), F32), contract=TN)


def _row_spec(tr, width):
    return pl.BlockSpec((tr, width), lambda i: (i, 0))


def _full_spec(shape):
    nd = len(shape)
    return pl.BlockSpec(shape, lambda i: (0,) * nd)


def _rms_fwd(x, g):
    S, D = x.shape
    tr = _tile(S, 512, 16)

    def body(x_ref, g_ref, h_ref):
        xv = x_ref[...]
        r = lax.rsqrt(jnp.mean(xv * xv, axis=-1, keepdims=True) + EPS)
        h_ref[...] = (xv * r * g_ref[...]).astype(BF16)

    return pl.pallas_call(
        body, name="rms_fwd", grid=(S // tr,),
        in_specs=[_row_spec(tr, D), _full_spec((1, D))], out_specs=_row_spec(tr, D),
        out_shape=jax.ShapeDtypeStruct((S, D), BF16),
        compiler_params=_params(12 * tr * D, ("parallel",)),
    )(x, g)


def _rms_bwd(dh, x, g, dres):
    S, D = x.shape
    tr = _tile(S, 256, 16)

    def body(dh_ref, x_ref, g_ref, dres_ref, dx_ref, dxb_ref, dg_ref):
        xv = x_ref[...]
        dhv = dh_ref[...]
        r = lax.rsqrt(jnp.mean(xv * xv, axis=-1, keepdims=True) + EPS)
        gy = dhv * g_ref[...]
        dot = jnp.mean(xv * gy, axis=-1, keepdims=True)
        dx = dres_ref[...] + (r * gy - xv * (r * r * r * dot))
        dx_ref[...] = dx
        dxb_ref[...] = dx.astype(BF16)
        part = jnp.sum(dhv * (xv * r), axis=0, keepdims=True)

        @pl.when(pl.program_id(0) == 0)
        def _():
            dg_ref[...] = part

        @pl.when(pl.program_id(0) > 0)
        def _():
            dg_ref[...] += part

    return pl.pallas_call(
        body, name="rms_bwd", grid=(S // tr,),
        in_specs=[_row_spec(tr, D), _row_spec(tr, D), _full_spec((1, D)), _row_spec(tr, D)],
        out_specs=[_row_spec(tr, D), _row_spec(tr, D), _full_spec((1, D))],
        out_shape=[jax.ShapeDtypeStruct((S, D), F32), jax.ShapeDtypeStruct((S, D), BF16),
                   jax.ShapeDtypeStruct((1, D), F32)],
        compiler_params=_params(40 * tr * D, ("arbitrary",)),
    )(dh, x, g, dres)


def _out_norm_fwd(oa, ob, ga, gb):
    S, W = oa.shape
    tr = _tile(S, 512, 16)

    def body(oa_ref, ob_ref, ga_ref, gb_ref, o_ref):
        for src, g_ref, off in ((oa_ref, ga_ref, 0), (ob_ref, gb_ref, W)):
            v = src[...]
            r = lax.rsqrt(jnp.mean(v * v, axis=-1, keepdims=True) + EPS)
            o_ref[:, off:off + W] = (v * r * g_ref[...]).astype(BF16)

    return pl.pallas_call(
        body, name="out_norm_fwd", grid=(S // tr,),
        in_specs=[_row_spec(tr, W), _row_spec(tr, W), _full_spec((1, W)), _full_spec((1, W))],
        out_specs=_row_spec(tr, 2 * W), out_shape=jax.ShapeDtypeStruct((S, 2 * W), BF16),
        compiler_params=_params(24 * tr * W, ("parallel",)),
    )(oa, ob, ga, gb)


def _out_norm_bwd(do, oa, ob, ga, gb):
    S, W = oa.shape
    tr = _tile(S, 256, 16)

    def body(do_ref, oa_ref, ob_ref, ga_ref, gb_ref, doa_ref, dob_ref, dga_ref, dgb_ref):
        first = pl.program_id(0) == 0
        for src, g_ref, off, d_ref, dg_ref in ((oa_ref, ga_ref, 0, doa_ref, dga_ref),
                                               (ob_ref, gb_ref, W, dob_ref, dgb_ref)):
            v = src[...]
            dv = do_ref[:, off:off + W]
            r = lax.rsqrt(jnp.mean(v * v, axis=-1, keepdims=True) + EPS)
            gy = dv * g_ref[...]
            dot = jnp.mean(v * gy, axis=-1, keepdims=True)
            d_ref[...] = (r * gy - v * (r * r * r * dot)).astype(BF16)
            part = jnp.sum(dv * (v * r), axis=0, keepdims=True)

            @pl.when(first)
            def _():
                dg_ref[...] = part

            @pl.when(jnp.logical_not(first))
            def _():
                dg_ref[...] += part

    return pl.pallas_call(
        body, name="out_norm_bwd", grid=(S // tr,),
        in_specs=[_row_spec(tr, 2 * W), _row_spec(tr, W), _row_spec(tr, W), _full_spec((1, W)), _full_spec((1, W))],
        out_specs=[_row_spec(tr, W), _row_spec(tr, W), _full_spec((1, W)), _full_spec((1, W))],
        out_shape=[jax.ShapeDtypeStruct((S, W), BF16), jax.ShapeDtypeStruct((S, W), BF16),
                   jax.ShapeDtypeStruct((1, W), F32), jax.ShapeDtypeStruct((1, W), F32)],
        compiler_params=_params(48 * tr * W, ("arbitrary",)),
    )(do, oa, ob, ga, gb)


def _loss_grad(y, t):
    S, D = y.shape
    tr = _tile(S, 256, 16)

    def body(y_ref, t_ref, dy_ref, dyb_ref, ls_ref):
        e = y_ref[...] - t_ref[...]
        dy = e * (1.0 / D)
        dy_ref[...] = dy
        dyb_ref[...] = dy.astype(BF16)
        part = jnp.sum(e * e, axis=0, keepdims=True)

        @pl.when(pl.program_id(0) == 0)
        def _():
            ls_ref[...] = part

        @pl.when(pl.program_id(0) > 0)
        def _():
            ls_ref[...] += part

    return pl.pallas_call(
        body, name="loss_grad", grid=(S // tr,),
        in_specs=[_row_spec(tr, D), _row_spec(tr, D)],
        out_specs=[_row_spec(tr, D), _row_spec(tr, D), _full_spec((1, D))],
        out_shape=[jax.ShapeDtypeStruct((S, D), F32), jax.ShapeDtypeStruct((S, D), BF16),
                   jax.ShapeDtypeStruct((1, D), F32)],
        compiler_params=_params(32 * tr * D, ("arbitrary",)),
    )(y, t)


def _head_rms(x, g):
    r = lax.rsqrt(jnp.mean(x * x, axis=-1, keepdims=True) + EPS)
    return x * r * g


def _head_rms_bwd(x, dy, g):
    r = lax.rsqrt(jnp.mean(x * x, axis=-1, keepdims=True) + EPS)
    gy = dy * g
    dot = jnp.mean(x * gy, axis=-1, keepdims=True)
    return r * gy - x * (r * r * r * dot), dy * (x * r)


def _rope(y, cos2, sgn_sin):
    return y * cos2 + pltpu.roll(y, HEAD_DIM // 2, axis=1) * sgn_sin


def _rope_t(dy, cos2, sgn_sin):
    return dy * cos2 + pltpu.roll(dy * sgn_sin, HEAD_DIM // 2, axis=1)


def _qk_prep_fwd(proj, cos2, sgn_sin, gains, dims):
    S, P = proj.shape
    naw, waw, kvw = dims
    tr = _tile(S, 256, 16)
    hd = HEAD_DIM

    def body(p_ref, c_ref, s_ref, g_ref, qa_ref, ka_ref, va_ref, qb_ref, kb_ref, vb_ref):
        c2 = c_ref[...]
        ss = s_ref[...]
        off = 0
        for dst, width, gi, rot in ((qa_ref, naw, 0, False), (ka_ref, naw, 1, False), (va_ref, naw, None, False),
                                    (qb_ref, waw, 2, True), (kb_ref, kvw, 3, True), (vb_ref, kvw, None, False)):
            for h in range(width // hd):
                xh = p_ref[:, off + h * hd:off + (h + 1) * hd]
                if gi is not None:
                    xh = _head_rms(xh, g_ref[gi:gi + 1, :])
                    if rot:
                        xh = _rope(xh, c2, ss)
                dst[:, h * hd:(h + 1) * hd] = xh.astype(BF16)
            off += width

    widths = (naw, naw, naw, waw, kvw, kvw)
    return pl.pallas_call(
        body, name="qk_prep_fwd", grid=(S // tr,),
        in_specs=[_row_spec(tr, P), _row_spec(tr, hd), _row_spec(tr, hd), _full_spec((SUBLANES, hd))],
        out_specs=[_row_spec(tr, w) for w in widths],
        out_shape=[jax.ShapeDtypeStruct((S, w), BF16) for w in widths],
        compiler_params=_params(16 * tr * P, ("parallel",)),
    )(proj, cos2, sgn_sin, gains)


def _qk_prep_bwd(proj, dqa, dka, dva, dqb, dkb, dvb, cos2, sgn_sin, gains, dims):
    S, P = proj.shape
    naw, waw, kvw = dims
    tr = _tile(S, 128, 16)
    hd = HEAD_DIM

    def body(p_ref, dqa_ref, dka_ref, dva_ref, dqb_ref, dkb_ref, dvb_ref, c_ref, s_ref, g_ref, dp_ref, dg_ref):
        c2 = c_ref[...]
        ss = s_ref[...]
        off = 0
        dgs = [jnp.zeros((1, hd), F32) for _ in range(4)]
        for src, width, gi, rot in ((dqa_ref, naw, 0, False), (dka_ref, naw, 1, False), (dva_ref, naw, None, False),
                                    (dqb_ref, waw, 2, True), (dkb_ref, kvw, 3, True), (dvb_ref, kvw, None, False)):
            for h in range(width // hd):
                dy = src[:, h * hd:(h + 1) * hd].astype(F32)
                if gi is not None:
                    if rot:
                        dy = _rope_t(dy, c2, ss)
                    xh = p_ref[:, off + h * hd:off + (h + 1) * hd]
                    dy, dgt = _head_rms_bwd(xh, dy, g_ref[gi:gi + 1, :])
                    dgs[gi] = dgs[gi] + jnp.sum(dgt, axis=0, keepdims=True)
                dp_ref[:, off + h * hd:off + (h + 1) * hd] = dy.astype(BF16)
            off += width
        part = jnp.concatenate(dgs + [jnp.zeros((SUBLANES - 4, hd), F32)], axis=0)

        @pl.when(pl.program_id(0) == 0)
        def _():
            dg_ref[...] = part

        @pl.when(pl.program_id(0) > 0)
        def _():
            dg_ref[...] += part

    return pl.pallas_call(
        body, name="qk_prep_bwd", grid=(S // tr,),
        in_specs=[_row_spec(tr, P), _row_spec(tr, naw), _row_spec(tr, naw), _row_spec(tr, naw),
                  _row_spec(tr, waw), _row_spec(tr, kvw), _row_spec(tr, kvw),
                  _row_spec(tr, hd), _row_spec(tr, hd), _full_spec((SUBLANES, hd))],
        out_specs=[_row_spec(tr, P), _full_spec((SUBLANES, hd))],
        out_shape=[jax.ShapeDtypeStruct((S, P), BF16), jax.ShapeDtypeStruct((SUBLANES, hd), F32)],
        compiler_params=_params(24 * tr * P, ("arbitrary",)),
    )(proj, dqa, dka, dva, dqb, dkb, dvb, cos2, sgn_sin, gains)


NA_KEYS = NA_WIN_R * GRID_W
NA_ROWS_PER_STEP = 8


def _na_col_geometry():
    c = np.arange(GRID_W)
    col_start = np.clip(c - NA_WIN_C // 2, 0, GRID_W - NA_WIN_C)
    mask = (c[None, :] >= col_start[:, None]) & (c[None, :] < col_start[:, None] + NA_WIN_C)
    dc = np.clip(c[None, :] - c[:, None], -(NA_WIN_C - 1), NA_WIN_C - 1) + (NA_WIN_C - 1)
    onehot = (dc[:, :, None] == np.arange(2 * NA_WIN_C - 1)[None, None, :]) & mask[:, :, None]
    return mask, onehot


def _na_bias_table(rpb_l):
    H = rpb_l.shape[0]
    mask, onehot = _na_col_geometry()
    t = jnp.sum(jnp.where(onehot[None, None], rpb_l[:, :, None, None, :], 0.0), axis=-1)
    t = jnp.where(mask[None, None], t, NEG)
    per_delta = [jnp.transpose(t[:, d:d + NA_WIN_R], (0, 2, 1, 3)).reshape(H, GRID_W, NA_KEYS) for d in range(NA_WIN_R)]
    return jnp.stack(per_delta, axis=1)


def _na_bias_table_t(db):
    H = db.shape[0]
    _, onehot = _na_col_geometry()
    d5 = db.reshape(H, NA_WIN_R, GRID_W, NA_WIN_R, GRID_W)
    folded = jnp.einsum("hdqwk,qkc->hdwc", d5, onehot.astype(np.float32), precision=lax.Precision.HIGHEST)
    return sum(jnp.pad(folded[:, d], ((0, 0), (d, NA_WIN_R - 1 - d), (0, 0))) for d in range(NA_WIN_R))


def _na_row_geometry(r, rows):
    start = jnp.clip(r - NA_WIN_R // 2, 0, rows - NA_WIN_R)
    return start, start - r + (NA_WIN_R - 1)


def _na_probs(q, kw, bias):
    s = _dot(q, kw, NT) * ATTN_SCALE + bias
    m = jnp.max(s, axis=-1, keepdims=True)
    e = jnp.exp(s - m)
    return e / jnp.sum(e, axis=-1, keepdims=True)


def _na_fwd(qa, ka, va, btab):
    S, W = qa.shape
    H = W // HEAD_DIM
    rows = S // GRID_W
    assert rows >= NA_WIN_R
    rb = _tile(rows, NA_ROWS_PER_STEP, 1)
    tq = rb * GRID_W

    def body(q_ref, k_ref, v_ref, b_ref, o_ref):
        i = pl.program_id(1)

        def row(j, carry):
            start, delta = _na_row_geometry(i * rb + j, rows)
            tok = pl.ds(pl.multiple_of(j * GRID_W, GRID_W), GRID_W)
            win = pl.ds(pl.multiple_of(start * GRID_W, GRID_W), NA_KEYS)
            p = _na_probs(q_ref[tok, :], k_ref[win, :], b_ref[delta])
            o_ref[tok, :] = _dot(p, v_ref[win, :], NN)
            return carry

        lax.fori_loop(0, rb, row, 0)

    kv_spec = pl.BlockSpec((S, HEAD_DIM), lambda h, i: (0, h))
    return pl.pallas_call(
        body, name="na_fwd", grid=(H, rows // rb),
        in_specs=[pl.BlockSpec((tq, HEAD_DIM), lambda h, i: (i, h)), kv_spec, kv_spec,
                  pl.BlockSpec((None, NA_WIN_R, GRID_W, NA_KEYS), lambda h, i: (h, 0, 0, 0))],
        out_specs=pl.BlockSpec((tq, HEAD_DIM), lambda h, i: (i, h)),
        out_shape=jax.ShapeDtypeStruct((S, W), F32),
        compiler_params=_params(8 * S * HEAD_DIM + (8 << 20), ("parallel", "arbitrary")),
    )(qa, ka, va, btab)


def _na_bwd(qa, ka, va, btab, doa):
    S, W = qa.shape
    H = W // HEAD_DIM
    rows = S // GRID_W
    rb = _tile(rows, NA_ROWS_PER_STEP, 1)
    tq = rb * GRID_W

    def body(q_ref, k_ref, v_ref, b_ref, do_ref, dq_ref, dk_ref, dv_ref, db_ref):
        i = pl.program_id(1)

        @pl.when(i == 0)
        def _():
            dk_ref[...] = jnp.zeros_like(dk_ref)
            dv_ref[...] = jnp.zeros_like(dv_ref)
            db_ref[...] = jnp.zeros_like(db_ref)

        def row(j, carry):
            start, delta = _na_row_geometry(i * rb + j, rows)
            tok = pl.ds(pl.multiple_of(j * GRID_W, GRID_W), GRID_W)
            win = pl.ds(pl.multiple_of(start * GRID_W, GRID_W), NA_KEYS)
            q = q_ref[tok, :]
            kw = k_ref[win, :]
            vw = v_ref[win, :]
            do = do_ref[tok, :]
            p = _na_probs(q, kw, b_ref[delta])
            dp = _dot(do, vw, NT)
            ds = p * (dp - jnp.sum(p * dp, axis=-1, keepdims=True))
            db_ref[delta] += ds
            dss = (ds * ATTN_SCALE).astype(BF16)
            dq_ref[tok, :] = _dot(dss, kw, NN)
            dk_ref[win, :] += _dot(dss, q, TN)
            dv_ref[win, :] += _dot(p, do, TN)
            return carry

        lax.fori_loop(0, rb, row, 0)

    kv_spec = pl.BlockSpec((S, HEAD_DIM), lambda h, i: (0, h))
    q_spec = pl.BlockSpec((tq, HEAD_DIM), lambda h, i: (i, h))
    b_spec = pl.BlockSpec((None, NA_WIN_R, GRID_W, NA_KEYS), lambda h, i: (h, 0, 0, 0))
    return pl.pallas_call(
        body, name="na_bwd", grid=(H, rows // rb),
        in_specs=[q_spec, kv_spec, kv_spec, b_spec, q_spec],
        out_specs=[q_spec, kv_spec, kv_spec, b_spec],
        out_shape=[jax.ShapeDtypeStruct((S, W), F32), jax.ShapeDtypeStruct((S, W), F32),
                   jax.ShapeDtypeStruct((S, W), F32), jax.ShapeDtypeStruct(btab.shape, F32)],
        compiler_params=_params(24 * S * HEAD_DIM + (12 << 20), ("parallel", "arbitrary")),
    )(qa, ka, va, btab, doa)


WA_KEYS = 3 * WA_BLOCK
WA_BLOCKS_PER_STEP = 4


def _wa_scores(qs, kw, n, start, g):
    s = _dot(qs, kw, NT) * ATTN_SCALE
    qpos = n * WA_BLOCK + lax.broadcasted_iota(jnp.int32, (WA_BLOCK, WA_KEYS), 0)
    kpos = start + lax.broadcasted_iota(jnp.int32, (WA_BLOCK, WA_KEYS), 1)
    valid = jnp.abs(kpos - qpos) <= WA_WINDOW
    valid = jnp.concatenate([valid] * g, axis=0)
    return jnp.where(valid, s, NEG)


def _wa_probs(s, sink):
    m = jnp.maximum(jnp.max(s, axis=-1, keepdims=True), sink)
    e = jnp.exp(s - m)
    es = jnp.exp(sink - m)
    den = jnp.sum(e, axis=-1, keepdims=True) + es
    return e / den, es / den


def _wa_stack(ref, tok, g):
    return jnp.concatenate([ref[tok, t * HEAD_DIM:(t + 1) * HEAD_DIM] for t in range(g)], axis=0)


def _wa_fwd(qb, kb, vb, sink_col):
    S, W = qb.shape
    hkv = kb.shape[1] // HEAD_DIM
    g = W // HEAD_DIM // hkv
    nb = S // WA_BLOCK
    assert S >= WA_KEYS
    qb_step = _tile(nb, WA_BLOCKS_PER_STEP, 1)
    tq = qb_step * WA_BLOCK

    def body(q_ref, k_ref, v_ref, s_ref, o_ref):
        i = pl.program_id(1)

        def blk(j, carry):
            n = i * qb_step + j
            tok = pl.ds(pl.multiple_of(j * WA_BLOCK, WA_BLOCK), WA_BLOCK)
            start = pl.multiple_of(jnp.clip((n - 1) * WA_BLOCK, 0, S - WA_KEYS), WA_BLOCK)
            win = pl.ds(start, WA_KEYS)
            p, _ = _wa_probs(_wa_scores(_wa_stack(q_ref, tok, g), k_ref[win, :], n, start, g), s_ref[...])
            o = _dot(p, v_ref[win, :], NN)
            for t in range(g):
                o_ref[tok, t * HEAD_DIM:(t + 1) * HEAD_DIM] = o[t * WA_BLOCK:(t + 1) * WA_BLOCK]
            return carry

        lax.fori_loop(0, qb_step, blk, 0)

    kv_spec = pl.BlockSpec((S, HEAD_DIM), lambda h, i: (0, h))
    q_spec = pl.BlockSpec((tq, g * HEAD_DIM), lambda h, i: (i, h))
    return pl.pallas_call(
        body, name="wa_fwd", grid=(hkv, nb // qb_step),
        in_specs=[q_spec, kv_spec, kv_spec, pl.BlockSpec((None, g * WA_BLOCK, 1), lambda h, i: (h, 0, 0))],
        out_specs=q_spec, out_shape=jax.ShapeDtypeStruct((S, W), F32),
        compiler_params=_params(8 * S * HEAD_DIM + (12 << 20), ("parallel", "arbitrary")),
    )(qb, kb, vb, sink_col)


def _wa_bwd(qb, kb, vb, sink_col, dob):
    S, W = qb.shape
    KW = kb.shape[1]
    hkv = KW // HEAD_DIM
    g = W // HEAD_DIM // hkv
    nb = S // WA_BLOCK
    qb_step = _tile(nb, WA_BLOCKS_PER_STEP, 1)
    tq = qb_step * WA_BLOCK

    def body(q_ref, k_ref, v_ref, s_ref, do_ref, dq_ref, dk_ref, dv_ref, dsink_ref):
        i = pl.program_id(1)

        @pl.when(i == 0)
        def _():
            dk_ref[...] = jnp.zeros_like(dk_ref)
            dv_ref[...] = jnp.zeros_like(dv_ref)
            dsink_ref[...] = jnp.zeros_like(dsink_ref)

        def blk(j, carry):
            n = i * qb_step + j
            tok = pl.ds(pl.multiple_of(j * WA_BLOCK, WA_BLOCK), WA_BLOCK)
            start = pl.multiple_of(jnp.clip((n - 1) * WA_BLOCK, 0, S - WA_KEYS), WA_BLOCK)
            win = pl.ds(start, WA_KEYS)
            qs = _wa_stack(q_ref, tok, g)
            dos = _wa_stack(do_ref, tok, g)
            kw = k_ref[win, :]
            vw = v_ref[win, :]
            p, ps = _wa_probs(_wa_scores(qs, kw, n, start, g), s_ref[...])
            dp = _dot(dos, vw, NT)
            dsum = jnp.sum(p * dp, axis=-1, keepdims=True)
            ds = p * (dp - dsum)
            dsink_ref[...] -= ps * dsum
            dss = (ds * ATTN_SCALE).astype(BF16)
            dq = _dot(dss, kw, NN)
            for t in range(g):
                dq_ref[tok, t * HEAD_DIM:(t + 1) * HEAD_DIM] = dq[t * WA_BLOCK:(t + 1) * WA_BLOCK]
            dk_ref[win, :] += _dot(dss, qs, TN)
            dv_ref[win, :] += _dot(p, dos, TN)
            return carry

        lax.fori_loop(0, qb_step, blk, 0)

    kv_spec = pl.BlockSpec((S, HEAD_DIM), lambda h, i: (0, h))
    q_spec = pl.BlockSpec((tq, g * HEAD_DIM), lambda h, i: (i, h))
    s_spec = pl.BlockSpec((None, g * WA_BLOCK, 1), lambda h, i: (h, 0, 0))
    return pl.pallas_call(
        body, name="wa_bwd", grid=(hkv, nb // qb_step),
        in_specs=[q_spec, kv_spec, kv_spec, s_spec, q_spec],
        out_specs=[q_spec, kv_spec, kv_spec, s_spec],
        out_shape=[jax.ShapeDtypeStruct((S, W), F32), jax.ShapeDtypeStruct((S, KW), F32),
                   jax.ShapeDtypeStruct((S, KW), F32), jax.ShapeDtypeStruct(sink_col.shape, F32)],
        compiler_params=_params(24 * S * HEAD_DIM + (16 << 20), ("parallel", "arbitrary")),
    )(qb, kb, vb, sink_col, dob)


CONV_HALO = SUBLANES


def _conv_specs(tr, width, nblk, rows_inner):
    per = tr // CONV_HALO

    def spec(shape, row_block):
        if rows_inner:
            return pl.BlockSpec(shape, lambda j, i: (row_block(i), j))
        return pl.BlockSpec(shape, lambda i, j: (row_block(i), j))

    cur = spec((tr, width), lambda i: i)
    prev = spec((CONV_HALO, width), lambda i: jnp.maximum(i * per - 1, 0))
    nxt = spec((CONV_HALO, width), lambda i: jnp.minimum((i + 1) * per, nblk * per - 1))
    return prev, cur, nxt


def _extend(prev_ref, cur_ref, next_ref, i, nblk):
    p = jnp.where(i > 0, prev_ref[...].astype(F32), 0.0)
    n = jnp.where(i < nblk - 1, next_ref[...].astype(F32), 0.0)
    return jnp.concatenate([p, cur_ref[...].astype(F32), n], axis=0)


def _shift_down(x):
    return pltpu.roll(x, 1, axis=0)


def _shift_up(x):
    return pltpu.roll(x, x.shape[0] - 1, axis=0)


def _conv(ext, w_ref, b_ref):
    return _shift_down(ext) * w_ref[0:1, :] + ext * w_ref[1:2, :] + _shift_up(ext) * w_ref[2:3, :] + b_ref[...]


def _sigmoid(x):
    return 1.0 / (1.0 + jnp.exp(-x))


def _convgate_fwd(up_pre, cw, cb, tc):
    S, C2 = up_pre.shape
    tr = _tile(S, 256, 16)
    nblk = S // tr
    prev, cur, nxt = _conv_specs(tr, 2 * tc, nblk, False)

    def body(p_ref, c_ref, n_ref, w_ref, b_ref, a_ref):
        i = pl.program_id(0)
        u = _conv(_extend(p_ref, c_ref, n_ref, i, nblk), w_ref, b_ref)[CONV_HALO:CONV_HALO + tr]
        gate, up = u[:, :tc], u[:, tc:]
        a_ref[...] = (gate * _sigmoid(gate) * up).astype(BF16)

    return pl.pallas_call(
        body, name="convgate_fwd", grid=(nblk, C2 // (2 * tc)),
        in_specs=[prev, cur, nxt, pl.BlockSpec((3, 2 * tc), lambda i, j: (0, j)),
                  pl.BlockSpec((1, 2 * tc), lambda i, j: (0, j))],
        out_specs=pl.BlockSpec((tr, tc), lambda i, j: (i, j)),
        out_shape=jax.ShapeDtypeStruct((S, C2 // 2), BF16),
        compiler_params=_params(48 * tr * tc, ("parallel", "parallel")),
    )(up_pre, up_pre, up_pre, cw, cb)


def _convgate_bwd(up_pre, dact, cw, cb, tc):
    S, C2 = up_pre.shape
    tr = _tile(S, 128, 16)
    nblk = S // tr
    prev, cur, nxt = _conv_specs(tr, 2 * tc, nblk, True)
    dprev, dcur, dnxt = _conv_specs(tr, tc, nblk, True)
    ext_rows = tr + 2 * CONV_HALO

    def body(p_ref, c_ref, n_ref, dp_ref, dc_ref, dn_ref, w_ref, b_ref, dx_ref, dw_ref, db_ref):
        i = pl.program_id(1)
        ext = _extend(p_ref, c_ref, n_ref, i, nblk)
        da = _extend(dp_ref, dc_ref, dn_ref, i, nblk)
        u = _conv(ext, w_ref, b_ref)
        gate, up = u[:, :tc], u[:, tc:]
        sg = _sigmoid(gate)
        silu = gate * sg
        du = jnp.concatenate([da * up * (sg + silu * (1.0 - sg)), da * silu], axis=1)
        dx = _shift_up(du) * w_ref[0:1, :] + du * w_ref[1:2, :] + _shift_down(du) * w_ref[2:3, :]
        dx_ref[...] = dx[CONV_HALO:CONV_HALO + tr].astype(BF16)
        row = lax.broadcasted_iota(jnp.int32, (ext_rows, 1), 0)
        duc = jnp.where(jnp.logical_and(row >= CONV_HALO, row < CONV_HALO + tr), du, 0.0)
        dw = jnp.concatenate([jnp.sum(duc * _shift_down(ext), axis=0, keepdims=True),
                              jnp.sum(duc * ext, axis=0, keepdims=True),
                              jnp.sum(duc * _shift_up(ext), axis=0, keepdims=True)], axis=0)
        db = jnp.sum(duc, axis=0, keepdims=True)

        @pl.when(i == 0)
        def _():
            dw_ref[...] = dw
            db_ref[...] = db

        @pl.when(i > 0)
        def _():
            dw_ref[...] += dw
            db_ref[...] += db

    return pl.pallas_call(
        body, name="convgate_bwd", grid=(C2 // (2 * tc), nblk),
        in_specs=[prev, cur, nxt, dprev, dcur, dnxt,
                  pl.BlockSpec((3, 2 * tc), lambda j, i: (0, j)), pl.BlockSpec((1, 2 * tc), lambda j, i: (0, j))],
        out_specs=[pl.BlockSpec((tr, 2 * tc), lambda j, i: (i, j)),
                   pl.BlockSpec((3, 2 * tc), lambda j, i: (0, j)), pl.BlockSpec((1, 2 * tc), lambda j, i: (0, j))],
        out_shape=[jax.ShapeDtypeStruct((S, C2), BF16), jax.ShapeDtypeStruct((3, C2), F32),
                   jax.ShapeDtypeStruct((1, C2), F32)],
        compiler_params=_params(160 * tr * tc, ("parallel", "arbitrary")),
    )(up_pre, up_pre, up_pre, dact, dact, dact, cw, cb)


def _rope_tables(positions):
    inv = ROPE_THETA ** (-jnp.arange(0, HEAD_DIM, 2, dtype=F32) / HEAD_DIM)
    ang = positions.astype(F32)[:, None] * inv[None, :]
    cos, sin = jnp.cos(ang), jnp.sin(ang)
    return jnp.concatenate([cos, cos], axis=1), jnp.concatenate([-sin, sin], axis=1)


def _sink_col(sink_l, hkv):
    g = sink_l.shape[0] // hkv
    return jnp.broadcast_to(sink_l.reshape(hkv, g, 1), (hkv, g, WA_BLOCK)).reshape(hkv, g * WA_BLOCK, 1)


def _sink_col_t(dcol):
    hkv, rows, _ = dcol.shape
    return jnp.sum(dcol.reshape(hkv, rows // WA_BLOCK, WA_BLOCK), axis=-1).reshape(-1)


def _to_paired(a, sh):
    return jnp.concatenate([a[:, sh.nat(t) * sh.tn:(sh.nat(t) + 1) * sh.tn] for t in range(sh.ntiles)], axis=1)


def _from_paired(a, sh):
    pos = {sh.nat(t): t for t in range(sh.ntiles)}
    return jnp.concatenate([a[:, pos[n] * sh.tn:(pos[n] + 1) * sh.tn] for n in range(sh.ntiles)], axis=1)


def _pack(arrs):
    flat = jnp.concatenate([a.reshape(-1).astype(F32) for a in arrs])
    unit = SUBLANES * LANES
    total = -(-flat.shape[0] // unit) * unit
    return jnp.pad(flat, (0, total - flat.shape[0])).reshape(-1, LANES)


def _unpack(buf, shapes):
    flat = buf.reshape(-1)
    out, off = [], 0
    for s in shapes:
        n = math.prod(s)
        out.append(flat[off:off + n].reshape(s))
        off += n
    return out


HBM_SPEC = pl.BlockSpec(memory_space=pltpu.HBM)


def _mesh_pos():
    return lax.axis_index("x"), lax.axis_index("y"), lax.axis_index("c")


def _other_chips(x, y):
    return [(1 - x, y), (x, 1 - y), (1 - x, 1 - y)]


def _remote(src, dst, send_sems, recv_sems, k, dev):
    return pltpu.make_async_remote_copy(src_ref=src, dst_ref=dst, send_sem=send_sems.at[k], recv_sem=recv_sems.at[k],
                                        device_id=dev, device_id_type=MESH)


def _all_gather_layer(shards, small):
    n = len(shards)
    halves = [s.shape[0] // 2 for s in shards]
    n_sems = 7 * n + 3

    def body(*refs):
        ins, sm_in = refs[:n], refs[n]
        outs, sm_out = refs[n + 1:2 * n + 1], refs[2 * n + 1]
        send_sems, recv_sems, loc_sems = refs[2 * n + 2:]
        x, y, c = _mesh_pos()
        q_me = 2 * x + y
        chips = _other_chips(x, y)
        sib = (x, y, 1 - c)
        rc = functools.partial(_remote, send_sems=send_sems, recv_sems=recv_sems)

        local = [pltpu.make_async_copy(ins[t], outs[t].at[q_me], loc_sems.at[t]) for t in range(n)]
        local.append(pltpu.make_async_copy(sm_in, sm_out.at[q_me], loc_sems.at[n]))
        for cp in local:
            cp.start()
        sends = []
        for t in range(n):
            mine = pl.ds(c * halves[t], halves[t])
            for k, chip in enumerate(chips):
                sends.append(rc(ins[t].at[mine], outs[t].at[q_me, mine], k=7 * t + k, dev=(*chip, c)))
        for k, chip in enumerate(chips):
            sends.append(rc(sm_in, sm_out.at[q_me], k=7 * n + k, dev=(*chip, c)))
        for cp in sends:
            cp.start()
        passed = []
        for t in range(n):
            mine = pl.ds(c * halves[t], halves[t])
            for k, (cx, cy) in enumerate(chips):
                blk = outs[t].at[2 * cx + cy, mine]
                rc(blk, blk, k=7 * t + k, dev=sib).wait_recv()
                fwd = rc(blk, blk, k=7 * t + 3 + k, dev=sib)
                fwd.start()
                passed.append(fwd)
        for t in range(n):
            theirs = pl.ds((1 - c) * halves[t], halves[t])
            for k, (cx, cy) in enumerate(chips):
                blk = outs[t].at[2 * cx + cy, theirs]
                rc(blk, blk, k=7 * t + 3 + k, dev=sib).wait_recv()
        for k, (cx, cy) in enumerate(chips):
            blk = sm_out.at[2 * cx + cy]
            rc(blk, blk, k=7 * n + k, dev=sib).wait_recv()
        for cp in sends + passed:
            cp.wait_send()
        for cp in local:
            cp.wait()

    out_shape = [jax.ShapeDtypeStruct((N_CHIPS,) + s.shape, s.dtype) for s in shards]
    out_shape.append(jax.ShapeDtypeStruct((N_CHIPS,) + small.shape, small.dtype))
    return pl.pallas_call(
        body, name="all_gather_layer",
        in_specs=[HBM_SPEC] * (n + 1), out_specs=[HBM_SPEC] * (n + 1), out_shape=out_shape,
        scratch_shapes=[pltpu.SemaphoreType.DMA((n_sems,)), pltpu.SemaphoreType.DMA((n_sems,)),
                        pltpu.SemaphoreType.DMA((n + 1,))],
    )(*shards, small)


def _rs_sibling_exchange(grads):
    n = len(grads)
    halves = [g.shape[1] // 2 for g in grads]

    def body(*refs):
        ins, mines, recvs = refs[:n], refs[n:2 * n], refs[2 * n:3 * n]
        send_sems, recv_sems, loc_sems = refs[3 * n:]
        x, y, c = _mesh_pos()
        sib = (x, y, 1 - c)
        copies = []
        for t in range(n):
            mine = pl.ds(c * halves[t], halves[t])
            theirs = pl.ds((1 - c) * halves[t], halves[t])
            copies.append(pltpu.make_async_copy(ins[t].at[:, mine], mines[t], loc_sems.at[t]))
            copies.append(_remote(ins[t].at[:, theirs], recvs[t], send_sems, recv_sems, t, sib))
        for cp in copies:
            cp.start()
        for cp in copies:
            cp.wait()

    half_shapes = [jax.ShapeDtypeStruct((N_CHIPS, h, g.shape[2]), F32) for g, h in zip(grads, halves)]
    outs = pl.pallas_call(
        body, name="rs_sibling_exchange",
        in_specs=[HBM_SPEC] * n, out_specs=[HBM_SPEC] * (2 * n), out_shape=half_shapes + half_shapes,
        scratch_shapes=[pltpu.SemaphoreType.DMA((n,)), pltpu.SemaphoreType.DMA((n,)), pltpu.SemaphoreType.DMA((n,))],
    )(*grads)
    return outs[:n], outs[n:]


def _rs_chip_exchange(pbs, pfs):
    n = len(pbs)

    def body(*refs):
        pb, pf = refs[:n], refs[n:2 * n]
        owns, recvs = refs[2 * n:3 * n], refs[3 * n:4 * n]
        send_sems, recv_sems, loc_sems = refs[4 * n:]
        x, y, c = _mesh_pos()
        q_me = 2 * x + y
        copies = []
        for t in range(n):
            copies.append(pltpu.make_async_copy(pf[t].at[q_me], owns[t], loc_sems.at[t]))
            for k, (cx, cy) in enumerate(_other_chips(x, y)):
                copies.append(_remote(pb[t].at[2 * cx + cy], recvs[t].at[k], send_sems, recv_sems, 3 * t + k, (cx, cy, c)))
        for cp in copies:
            cp.start()
        for cp in copies:
            cp.wait()

    own_shapes = [jax.ShapeDtypeStruct(p.shape[1:], F32) for p in pfs]
    recv_shapes = [jax.ShapeDtypeStruct((3,) + p.shape[1:], BF16) for p in pbs]
    outs = pl.pallas_call(
        body, name="rs_chip_exchange",
        in_specs=[HBM_SPEC] * (2 * n), out_specs=[HBM_SPEC] * (2 * n), out_shape=own_shapes + recv_shapes,
        scratch_shapes=[pltpu.SemaphoreType.DMA((3 * n,)), pltpu.SemaphoreType.DMA((3 * n,)),
                        pltpu.SemaphoreType.DMA((n,))],
    )(*pbs, *pfs)
    return outs[:n], outs[n:]


def _rs_sibling_share(ghs):
    n = len(ghs)

    def body(*refs):
        ins, outs = refs[:n], refs[n:2 * n]
        send_sems, recv_sems, loc_sems = refs[2 * n:]
        x, y, c = _mesh_pos()
        sib = (x, y, 1 - c)
        copies = []
        for t in range(n):
            h = ins[t].shape[0]
            mine = outs[t].at[pl.ds(c * h, h)]
            copies.append(pltpu.make_async_copy(ins[t], mine, loc_sems.at[t]))
            copies.append(_remote(ins[t], mine, send_sems, recv_sems, t, sib))
        for cp in copies:
            cp.start()
        for cp in copies:
            cp.wait()

    return pl.pallas_call(
        body, name="rs_sibling_share",
        in_specs=[HBM_SPEC] * n, out_specs=[HBM_SPEC] * n,
        out_shape=[jax.ShapeDtypeStruct((2 * g.shape[0], g.shape[1]), F32) for g in ghs],
        scratch_shapes=[pltpu.SemaphoreType.DMA((n,)), pltpu.SemaphoreType.DMA((n,)), pltpu.SemaphoreType.DMA((n,))],
    )(*ghs)


def _gather_small(buf):
    def body(in_ref, out_ref, send_sems, recv_sems, loc_sem):
        x, y, c = _mesh_pos()
        me = 4 * x + 2 * y + c
        local = pltpu.make_async_copy(in_ref, out_ref.at[me], loc_sem)
        local.start()
        sends, peers = [], []
        for k in range(1, N_DEV):
            bx, by, bc = (k >> 2) & 1, (k >> 1) & 1, k & 1
            peer = (x + bx - 2 * x * bx, y + by - 2 * y * by, c + bc - 2 * c * bc)
            peers.append(peer)
            sends.append(_remote(in_ref, out_ref.at[me], send_sems, recv_sems, k - 1, peer))
        for cp in sends:
            cp.start()
        for k, (px, py, pc) in enumerate(peers):
            slot = out_ref.at[4 * px + 2 * py + pc]
            _remote(slot, slot, send_sems, recv_sems, k, (px, py, pc)).wait_recv()
        for cp in sends:
            cp.wait_send()
        local.wait()

    return pl.pallas_call(
        body, name="gather_small", in_specs=[HBM_SPEC], out_specs=HBM_SPEC,
        out_shape=jax.ShapeDtypeStruct((N_DEV,) + buf.shape, F32),
        scratch_shapes=[pltpu.SemaphoreType.DMA((N_DEV - 1,)), pltpu.SemaphoreType.DMA((N_DEV - 1,)),
                        pltpu.SemaphoreType.DMA(())],
    )(buf)


def _ew_rows(rows, cols, bytes_per_elem):
    budget = 20 * 1024 * 1024
    return _tile(rows, max(16, budget // (2 * bytes_per_elem * cols) // 16 * 16), 16)


def _sum_pair(mine, recv):
    shape = mine.shape
    rows, cols = shape[0] * shape[1], shape[2]
    tr = _ew_rows(rows, cols, 14)

    def body(a_ref, b_ref, f_ref, h_ref):
        s = a_ref[...] + b_ref[...]
        f_ref[...] = s
        h_ref[...] = s.astype(BF16)

    pf, pb = pl.pallas_call(
        body, name="rs_sum_pair", grid=(rows // tr,),
        in_specs=[_row_spec(tr, cols)] * 2, out_specs=[_row_spec(tr, cols)] * 2,
        out_shape=[jax.ShapeDtypeStruct((rows, cols), F32), jax.ShapeDtypeStruct((rows, cols), BF16)],
        compiler_params=_params(28 * tr * cols, ("parallel",)),
    )(mine.reshape(rows, cols), recv.reshape(rows, cols))
    return pf.reshape(shape), pb.reshape(shape)


def _sum_four(own, recv):
    rows, cols = own.shape
    tr = _ew_rows(rows, cols, 14)

    def body(a_ref, r_ref, o_ref):
        o_ref[...] = ((a_ref[...] + r_ref[0].astype(F32)) + r_ref[1].astype(F32)) + r_ref[2].astype(F32)

    return pl.pallas_call(
        body, name="rs_sum_four", grid=(rows // tr,),
        in_specs=[_row_spec(tr, cols), pl.BlockSpec((3, tr, cols), lambda i: (0, i, 0))],
        out_specs=_row_spec(tr, cols), out_shape=jax.ShapeDtypeStruct((rows, cols), F32),
        compiler_params=_params(28 * tr * cols, ("parallel",)),
    )(own, recv)


def _sum_eight(gathered):
    _, rows, cols = gathered.shape
    tr = _tile(rows, 512, SUBLANES)

    def body(g_ref, o_ref):
        s = g_ref[0]
        for d in range(1, N_DEV):
            s = s + g_ref[d]
        o_ref[...] = s

    return pl.pallas_call(
        body, name="sum_eight", grid=(rows // tr,),
        in_specs=[pl.BlockSpec((N_DEV, tr, cols), lambda i: (0, i, 0))],
        out_specs=_row_spec(tr, cols), out_shape=jax.ShapeDtypeStruct((rows, cols), F32),
        compiler_params=_params(None, ("parallel",)),
    )(gathered)


def _cast_layer(w, l):
    _, rows, cols = w.shape
    tr = _ew_rows(rows, cols, 6)

    def body(w_ref, o_ref):
        o_ref[...] = w_ref[...].astype(BF16)

    return pl.pallas_call(
        body, name="cast_bf16", grid=(rows // tr,),
        in_specs=[pl.BlockSpec((None, tr, cols), lambda i: (l, i, 0))],
        out_specs=_row_spec(tr, cols), out_shape=jax.ShapeDtypeStruct((rows, cols), BF16),
        compiler_params=_params(12 * tr * cols, ("parallel",)),
    )(w)


def _adamw(g, w, m, v, l, prev):
    L, rows, cols = w.shape
    tr = _ew_rows(rows, cols, 32)
    layer = pl.BlockSpec((None, tr, cols), lambda i: (l, i, 0))

    def body(g_ref, w_ref, m_ref, v_ref, *rest):
        og_ref, od_ref, om_ref, ov_ref = rest[-4:]
        gv = g_ref[...]
        mn = ADAM_B1 * m_ref[...] + (1.0 - ADAM_B1) * gv
        vn = ADAM_B2 * v_ref[...] + (1.0 - ADAM_B2) * (gv * gv)
        m_hat = mn / ADAM_C1
        v_hat = vn / ADAM_C2
        od_ref[...] = -ADAM_LR * (m_hat / (jnp.sqrt(v_hat) + ADAM_EPS) + ADAM_WD * w_ref[...])
        og_ref[...] = gv
        om_ref[...] = mn
        ov_ref[...] = vn

    in_specs = [_row_spec(tr, cols), layer, layer, layer]
    args = [g, w, m, v]
    aliases = {}
    if prev is not None:
        in_specs += [HBM_SPEC] * 4
        args += list(prev)
        aliases = {4 + i: i for i in range(4)}
    return pl.pallas_call(
        body, name="adamw", grid=(rows // tr,), in_specs=in_specs, out_specs=[layer] * 4,
        out_shape=[jax.ShapeDtypeStruct((L, rows, cols), F32)] * 4, input_output_aliases=aliases,
        compiler_params=_params(64 * tr * cols, ("parallel",)),
    )(*args)


WEIGHT_ORDER = ("ln1_g", "w_in", "qn_a", "kn_a", "rpb", "qn_b", "kn_b", "sink", "on_a", "on_b", "w_out", "ln2_g",
                "w_up", "conv_w", "conv_b", "w_down")
BIG = ("w_in", "w_out", "w_up", "w_down")
SMALL = tuple(n for n in WEIGHT_ORDER if n not in BIG)


def _reduce_scatter_layer(grads):
    mines, recvs = _rs_sibling_exchange(grads)
    pairs = [_sum_pair(a, b) for a, b in zip(mines, recvs)]
    owns, gots = _rs_chip_exchange([p[1] for p in pairs], [p[0] for p in pairs])
    return _rs_sibling_share([_sum_four(a, b) for a, b in zip(owns, gots)])


def _train_step(x, positions, target, w, m, v):
    S, D = x.shape
    L = w["w_in"].shape[0]
    naw = w["rpb"].shape[1] * HEAD_DIM
    waw = w["sink"].shape[1] * HEAD_DIM
    cin = w["w_in"].shape[2]
    kvw = (N_CHIPS * cin - 3 * naw - waw) // 2
    hkv = kvw // HEAD_DIM
    dims = (naw, waw, kvw)
    assert naw == waw, "the two head groups are normalised by one kernel and must be equally wide"
    cup = w["w_up"].shape[2]
    c2 = N_CHIPS * cup
    dff = c2 // 2
    mix = naw + waw
    sh_in = _ColShards(cin, _tile(cin, 1152, LANES), False)
    sh_up = _ColShards(cup, cup // 2 if (cup // 2) % LANES == 0 else cup, True)
    tc = sh_up.tn
    cos2, sgn_sin = _rope_tables(positions)
    q_me = 2 * lax.axis_index("x") + lax.axis_index("y")
    row = lambda a: a[None]

    saved = []
    for l in range(L):
        g_in, g_out, g_up, g_dn, g_cw = _all_gather_layer(
            [_cast_layer(w[n], l) for n in BIG], w["conv_w"][l])
        g_out = g_out.reshape(mix, D)
        g_dn = g_dn.reshape(dff, D)
        cw_p = _to_paired(jnp.transpose(g_cw, (1, 0, 2)).reshape(3, c2), sh_up)
        cb_p = _to_paired(row(w["conv_b"][l]), sh_up)
        gains = jnp.concatenate([row(w["qn_a"][l]), row(w["kn_a"][l]), row(w["qn_b"][l]), row(w["kn_b"][l]),
                                 jnp.zeros((SUBLANES - 4, HEAD_DIM), F32)], axis=0)
        btab = _na_bias_table(w["rpb"][l])
        sink_col = _sink_col(w["sink"][l], hkv)

        h = _rms_fwd(x, row(w["ln1_g"][l]))
        proj = _mm_cols_fwd("mm_proj", h, g_in, sh_in, F32, tm=1024)
        qa, ka, va, qb, kb, vb = _qk_prep_fwd(proj, cos2, sgn_sin, gains, dims)
        oa = _na_fwd(qa, ka, va, btab)
        ob = _wa_fwd(qb, kb, vb, sink_col)
        o = _out_norm_fwd(oa, ob, row(w["on_a"][l]), row(w["on_b"][l]))
        x1 = _mm_plain("mm_attn_out", o, g_out, NN, F32, tm=1024, tn=1024, tk=2048, res=x)
        h2 = _rms_fwd(x1, row(w["ln2_g"][l]))
        up_pre = _mm_cols_fwd("mm_up", h2, g_up, sh_up, F32, tm=1024)
        act = _convgate_fwd(up_pre, cw_p, cb_p, tc)
        x2 = _mm_plain("mm_down", act, g_dn, NN, F32, tm=1024, tn=1024, tk=1408, res=x1)
        saved.append(dict(g_in=g_in, g_out=g_out, g_up=g_up, g_dn=g_dn, cw_p=cw_p, cb_p=cb_p, gains=gains, btab=btab,
                          sink_col=sink_col, x=x, h=h, proj=proj, qa=qa, ka=ka, va=va, qb=qb, kb=kb, vb=vb, oa=oa,
                          ob=ob, o=o, x1=x1, h2=h2, up_pre=up_pre, act=act))
        x = x2

    dx, dxb, loss_cols = _loss_grad(x, target)

    small_grads = [None] * L
    stacked = {n: None for n in BIG}
    for l in reversed(range(L)):
        s = saved[l]
        dact = _mm_plain("mm_down_dx", dxb, s["g_dn"], NT, F32, tm=1024, tn=1408, tk=2048)
        d_dn = _mm_plain("mm_down_dw", s["act"], dxb, TN, F32, tm=1408, tn=1024, tk=2048)
        dup_pre, dcw_p, dcb_p = _convgate_bwd(s["up_pre"], dact, s["cw_p"], s["cb_p"], tc)
        dh2 = _mm_cols_bwd_x("mm_up_dx", dup_pre, s["g_up"], sh_up, F32, tm=1024, tn=2048)
        d_up = _mm_cols_bwd_w("mm_up_dw", s["h2"], dup_pre, sh_up, tm=1024, tk=2048)
        dx1, dx1b, d_ln2 = _rms_bwd(dh2, s["x1"], row(w["ln2_g"][l]), dx)
        do = _mm_plain("mm_attn_out_dx", dx1b, s["g_out"], NT, F32, tm=1024, tn=1024, tk=2048)
        d_out = _mm_plain("mm_attn_out_dw", s["o"], dx1b, TN, F32, tm=1024, tn=1024, tk=2048)
        doa, dob, d_on_a, d_on_b = _out_norm_bwd(do, s["oa"], s["ob"], row(w["on_a"][l]), row(w["on_b"][l]))
        dqa, dka, dva, dbtab = _na_bwd(s["qa"], s["ka"], s["va"], s["btab"], doa)
        dqb, dkb, dvb, dsink_col = _wa_bwd(s["qb"], s["kb"], s["vb"], s["sink_col"], dob)
        dproj, dgains = _qk_prep_bwd(s["proj"], dqa, dka, dva, dqb, dkb, dvb, cos2, sgn_sin, s["gains"], dims)
        dh = _mm_cols_bwd_x("mm_proj_dx", dproj, s["g_in"], sh_in, F32, tm=1024, tn=2048)
        d_in = _mm_cols_bwd_w("mm_proj_dw", s["h"], dproj, sh_in, tm=1024, tk=2048)
        dx, dxb, d_ln1 = _rms_bwd(dh, s["x"], row(w["ln1_g"][l]), dx1)

        small_grads[l] = dict(
            ln1_g=d_ln1[0], qn_a=dgains[0], kn_a=dgains[1], rpb=_na_bias_table_t(dbtab), qn_b=dgains[2],
            kn_b=dgains[3], sink=_sink_col_t(dsink_col), on_a=d_on_a[0], on_b=d_on_b[0], ln2_g=d_ln2[0],
            conv_w=_from_paired(dcw_p, sh_up), conv_b=_from_paired(dcb_p, sh_up)[0])
        shards = _reduce_scatter_layer([d_in, d_out.reshape(N_CHIPS, mix // N_CHIPS, D), d_up,
                                        d_dn.reshape(N_CHIPS, dff // N_CHIPS, D)])
        for n, g in zip(BIG, shards):
            stacked[n] = _adamw(g, w[n], m[n], v[n], l, stacked[n])
    grad_x = dx[None]

    full_shapes = {n: (w[n].shape[1:] if n != "conv_w" else (3, c2)) for n in SMALL}
    packed = _pack([loss_cols] + [small_grads[l][n] for l in range(L) for n in SMALL])
    total = _sum_eight(_gather_small(packed))
    parts = _unpack(total, [(D,)] + [full_shapes[n] for _ in range(L) for n in SMALL])
    loss = 0.5 * jnp.sum(parts[0]) / D
    per_layer = [dict(zip(SMALL, parts[1 + l * len(SMALL):1 + (l + 1) * len(SMALL)])) for l in range(L)]
    for l in range(L):
        per_layer[l]["conv_w"] = lax.dynamic_slice_in_dim(per_layer[l]["conv_w"], q_me * cup, cup, axis=1)
    small_g = {n: jnp.stack([per_layer[l][n] for l in range(L)]) for n in SMALL}
    outs = _adamw(_pack([small_g[n] for n in SMALL]), _pack([w[n] for n in SMALL])[None],
                  _pack([m[n] for n in SMALL])[None], _pack([v[n] for n in SMALL])[None], 0, None)
    small_out = [dict(zip(SMALL, _unpack(o[0], [w[n].shape for n in SMALL]))) for o in outs]

    result = [loss, grad_x]
    for i in range(4):
        result += [stacked[n][i] if n in BIG else small_out[i][n] for n in WEIGHT_ORDER]
    return tuple(result)


def kernel(x, positions, ln1_g, w_in, qn_a, kn_a, rpb, qn_b, kn_b, sink, on_a, on_b, w_out, ln2_g, w_up, conv_w, conv_b, w_down, loss_target, m_ln1_g, m_w_in, m_qn_a, m_kn_a, m_rpb, m_qn_b, m_kn_b, m_sink, m_on_a, m_on_b, m_w_out, m_ln2_g, m_w_up, m_conv_w, m_conv_b, m_w_down, v_ln1_g, v_w_in, v_qn_a, v_kn_a, v_rpb, v_qn_b, v_kn_b, v_sink, v_on_a, v_on_b, v_w_out, v_ln2_g, v_w_up, v_conv_w, v_conv_b, v_w_down):
    given = dict(locals())
    w = {n: given[n] for n in WEIGHT_ORDER}
    m = {n: given["m_" + n] for n in WEIGHT_ORDER}
    v = {n: given["v_" + n] for n in WEIGHT_ORDER}
    return _train_step(x[0], positions, loss_target[0], w, m, v)
```

```python
import functools
import math

import jax
import jax.numpy as jnp
import numpy as np
from jax import lax
from jax.experimental import pallas as pl
from jax.experimental.pallas import tpu as pltpu

F32 = jnp.float32
BF16 = jnp.bfloat16

HEAD_DIM = 128
GRID_W = 64
NA_WIN_R = 8
NA_WIN_C = 16
WA_WINDOW = 128
WA_BLOCK = 128
ROPE_THETA = 10000.0
EPS = 1e-6
NEG = -1e30
ATTN_SCALE = 1.0 / math.sqrt(HEAD_DIM)

ADAM_LR = 0.001
ADAM_B1 = 0.9
ADAM_B2 = 0.999
ADAM_EPS = 1e-08
ADAM_WD = 0.01
ADAM_STEP = 10
ADAM_C1 = 1.0 - ADAM_B1 ** ADAM_STEP
ADAM_C2 = 1.0 - ADAM_B2 ** ADAM_STEP

V7X_VMEM_BYTES = 64 * 1024 * 1024
V7X_VMEM_CAP = V7X_VMEM_BYTES - 6 * 1024 * 1024
LANES = 128
SUBLANES = 8
N_CHIPS = 4
N_DEV = 8
MESH = pl.DeviceIdType.MESH

NN = ((1,), (0,))
NT = ((1,), (1,))
TN = ((0,), (0,))


def _tile(n, pref, mult):
    best = None
    d = mult
    while d <= min(n, pref):
        if n % d == 0:
            best = d
        d += mult
    return n if best is None else best


def _nbytes(shape, dtype):
    n = 1
    for s in shape:
        if s is not None:
            n *= s
    return n * jnp.dtype(dtype).itemsize


def _params(est_bytes=None, sem=None, **kw):
    if est_bytes is not None:
        kw["vmem_limit_bytes"] = int(min(V7X_VMEM_CAP, max(32 * 1024 * 1024, est_bytes * 5 // 4 + (4 << 20))))
    if sem is not None:
        kw["dimension_semantics"] = sem
    return pltpu.CompilerParams(**kw)


def _dot(a, b, contract):
    return lax.dot_general(a.astype(BF16), b.astype(BF16), (contract, ((), ())), preferred_element_type=F32)


def _mm(name, a, b, *, grid, a_spec, b_spec, o_spec, out_shape, contract, res=None, res_spec=None):
    nk = grid[2]
    acc_shape = tuple(s for s in o_spec.block_shape if s is not None)

    def body(*refs):
        if res is None:
            a_ref, b_ref, o_ref = refs[:3]
            r_ref = None
            scr = refs[3:]
        else:
            a_ref, b_ref, r_ref, o_ref = refs[:4]
            scr = refs[4:]
        p = _dot(a_ref[...], b_ref[...], contract)

        def finish(acc):
            if r_ref is not None:
                acc = acc + r_ref[...]
            o_ref[...] = acc.astype(o_ref.dtype)

        if nk == 1:
            finish(p)
        else:
            acc_ref = scr[0]
            k = pl.program_id(2)

            @pl.when(k == 0)
            def _():
                acc_ref[...] = p

            @pl.when(jnp.logical_and(k > 0, k < nk - 1))
            def _():
                acc_ref[...] += p

            @pl.when(k == nk - 1)
            def _():
                finish(acc_ref[...] + p)

    in_specs = [a_spec, b_spec]
    args = [a, b]
    est = 2 * (_nbytes(a_spec.block_shape, a.dtype) + _nbytes(b_spec.block_shape, b.dtype)
               + _nbytes(o_spec.block_shape, out_shape.dtype)) + 2 * _nbytes(acc_shape, F32)
    if res is not None:
        in_specs.append(res_spec)
        args.append(res)
        est += 2 * _nbytes(res_spec.block_shape, res.dtype)
    scratch = [] if nk == 1 else [pltpu.VMEM(acc_shape, F32)]
    return pl.pallas_call(
        body, name=name, grid=grid, in_specs=in_specs, out_specs=o_spec, out_shape=out_shape,
        scratch_shapes=scratch,
        compiler_params=_params(est, ("parallel", "parallel", "arbitrary")),
    )(*args)


def _mm_plain(name, a, b, contract, out_dtype, *, tm, tn, tk, res=None):
    if contract == NN:
        (M, K), N = a.shape, b.shape[1]
    elif contract == NT:
        (M, K), N = a.shape, b.shape[0]
    else:
        (K, M), N = a.shape, b.shape[1]
    tm, tn, tk = _tile(M, tm, LANES), _tile(N, tn, LANES), _tile(K, tk, LANES)
    grid = (M // tm, N // tn, K // tk)
    if contract == TN:
        a_spec = pl.BlockSpec((tk, tm), lambda i, j, k: (k, i))
    else:
        a_spec = pl.BlockSpec((tm, tk), lambda i, j, k: (i, k))
    if contract == NT:
        b_spec = pl.BlockSpec((tn, tk), lambda i, j, k: (j, k))
    else:
        b_spec = pl.BlockSpec((tk, tn), lambda i, j, k: (k, j))
    o_spec = pl.BlockSpec((tm, tn), lambda i, j, k: (i, j))
    return _mm(name, a, b, grid=grid, a_spec=a_spec, b_spec=b_spec, o_spec=o_spec,
               out_shape=jax.ShapeDtypeStruct((M, N), out_dtype), contract=contract,
               res=res, res_spec=None if res is None else pl.BlockSpec((tm, tn), lambda i, j, k: (i, j)))


class _ColShards:
    def __init__(self, cols_per_chip, tn, paired):
        self.c = cols_per_chip
        self.tn = tn
        self.tps = cols_per_chip // tn
        self.ntiles = N_CHIPS * self.tps
        self.paired = paired

    def nat(self, t):
        if not self.paired:
            return t
        return (t % 2) * (self.ntiles // 2) + t // 2

    def chip(self, t):
        return self.nat(t) // self.tps

    def within(self, t):
        return self.nat(t) % self.tps


def _mm_cols_fwd(name, a, wg, sh, out_dtype, *, tm):
    S, K = a.shape
    tm = _tile(S, tm, LANES)
    grid = (S // tm, sh.ntiles, 1)
    return _mm(name, a, wg, grid=grid,
               a_spec=pl.BlockSpec((tm, K), lambda i, j, k: (i, 0)),
               b_spec=pl.BlockSpec((None, K, sh.tn), lambda i, j, k: (sh.chip(j), 0, sh.within(j))),
               o_spec=pl.BlockSpec((tm, sh.tn), lambda i, j, k: (i, j)),
               out_shape=jax.ShapeDtypeStruct((S, sh.ntiles * sh.tn), out_dtype), contract=NN)


def _mm_cols_bwd_x(name, dy, wg, sh, out_dtype, *, tm, tn):
    S = dy.shape[0]
    K = wg.shape[1]
    tm, tn = _tile(S, tm, LANES), _tile(K, tn, LANES)
    grid = (S // tm, K // tn, sh.ntiles)
    return _mm(name, dy, wg, grid=grid,
               a_spec=pl.BlockSpec((tm, sh.tn), lambda i, j, k: (i, k)),
               b_spec=pl.BlockSpec((None, tn, sh.tn), lambda i, j, k: (sh.chip(k), j, sh.within(k))),
               o_spec=pl.BlockSpec((tm, tn), lambda i, j, k: (i, j)),
               out_shape=jax.ShapeDtypeStruct((S, K), out_dtype), contract=NT)


def _mm_cols_bwd_w(name, a, dy, sh, *, tm, tk):
    S, K = a.shape
    tm, tk = _tile(K, tm, LANES), _tile(S, tk, LANES)
    grid = (K // tm, sh.ntiles, S // tk)
    return _mm(name, a, dy, grid=grid,
               a_spec=pl.BlockSpec((tk, tm), lambda i, j, k: (k, i)),
               b_spec=pl.BlockSpec((tk, sh.tn), lambda i, j, k: (k, j)),
               o_spec=pl.BlockSpec((None, tm, sh.tn), lambda i, j, k: (sh.chip(j), i, sh.within(j))),
               out_shape=jax.ShapeDtypeStruct((N_CHIPS, K, sh.c), F32), contract=TN)


def _row_spec(tr, width):
    return pl.BlockSpec((tr, width), lambda i: (i, 0))


def _full_spec(shape):
    nd = len(shape)
    return pl.BlockSpec(shape, lambda i: (0,) * nd)


def _rms_fwd(x, g):
    S, D = x.shape
    tr = _tile(S, 512, 16)

    def body(x_ref, g_ref, h_ref):
        xv = x_ref[...]
        r = lax.rsqrt(jnp.mean(xv * xv, axis=-1, keepdims=True) + EPS)
        h_ref[...] = (xv * r * g_ref[...]).astype(BF16)

    return pl.pallas_call(
        body, name="rms_fwd", grid=(S // tr,),
        in_specs=[_row_spec(tr, D), _full_spec((1, D))], out_specs=_row_spec(tr, D),
        out_shape=jax.ShapeDtypeStruct((S, D), BF16),
        compiler_params=_params(12 * tr * D, ("parallel",)),
    )(x, g)


def _rms_bwd(dh, x, g, dres):
    S, D = x.shape
    tr = _tile(S, 256, 16)

    def body(dh_ref, x_ref, g_ref, dres_ref, dx_ref, dxb_ref, dg_ref):
        xv = x_ref[...]
        dhv = dh_ref[...]
        r = lax.rsqrt(jnp.mean(xv * xv, axis=-1, keepdims=True) + EPS)
        gy = dhv * g_ref[...]
        dot = jnp.mean(xv * gy, axis=-1, keepdims=True)
        dx = dres_ref[...] + (r * gy - xv * (r * r * r * dot))
        dx_ref[...] = dx
        dxb_ref[...] = dx.astype(BF16)
        part = jnp.sum(dhv * (xv * r), axis=0, keepdims=True)

        @pl.when(pl.program_id(0) == 0)
        def _():
            dg_ref[...] = part

        @pl.when(pl.program_id(0) > 0)
        def _():
            dg_ref[...] += part

    return pl.pallas_call(
        body, name="rms_bwd", grid=(S // tr,),
        in_specs=[_row_spec(tr, D), _row_spec(tr, D), _full_spec((1, D)), _row_spec(tr, D)],
        out_specs=[_row_spec(tr, D), _row_spec(tr, D), _full_spec((1, D))],
        out_shape=[jax.ShapeDtypeStruct((S, D), F32), jax.ShapeDtypeStruct((S, D), BF16),
                   jax.ShapeDtypeStruct((1, D), F32)],
        compiler_params=_params(40 * tr * D, ("arbitrary",)),
    )(dh, x, g, dres)


def _out_norm_fwd(oa, ob, ga, gb):
    S, W = oa.shape
    tr = _tile(S, 512, 16)

    def body(oa_ref, ob_ref, ga_ref, gb_ref, o_ref):
        for src, g_ref, off in ((oa_ref, ga_ref, 0), (ob_ref, gb_ref, W)):
            v = src[...]
            r = lax.rsqrt(jnp.mean(v * v, axis=-1, keepdims=True) + EPS)
            o_ref[:, off:off + W] = (v * r * g_ref[...]).astype(BF16)

    return pl.pallas_call(
        body, name="out_norm_fwd", grid=(S // tr,),
        in_specs=[_row_spec(tr, W), _row_spec(tr, W), _full_spec((1, W)), _full_spec((1, W))],
        out_specs=_row_spec(tr, 2 * W), out_shape=jax.ShapeDtypeStruct((S, 2 * W), BF16),
        compiler_params=_params(24 * tr * W, ("parallel",)),
    )(oa, ob, ga, gb)


def _out_norm_bwd(do, oa, ob, ga, gb):
    S, W = oa.shape
    tr = _tile(S, 256, 16)

    def body(do_ref, oa_ref, ob_ref, ga_ref, gb_ref, doa_ref, dob_ref, dga_ref, dgb_ref):
        first = pl.program_id(0) == 0
        for src, g_ref, off, d_ref, dg_ref in ((oa_ref, ga_ref, 0, doa_ref, dga_ref),
                                               (ob_ref, gb_ref, W, dob_ref, dgb_ref)):
            v = src[...]
            dv = do_ref[:, off:off + W]
            r = lax.rsqrt(jnp.mean(v * v, axis=-1, keepdims=True) + EPS)
            gy = dv * g_ref[...]
            dot = jnp.mean(v * gy, axis=-1, keepdims=True)
            d_ref[...] = (r * gy - v * (r * r * r * dot)).astype(BF16)
            part = jnp.sum(dv * (v * r), axis=0, keepdims=True)

            @pl.when(first)
            def _():
                dg_ref[...] = part

            @pl.when(jnp.logical_not(first))
            def _():
                dg_ref[...] += part

    return pl.pallas_call(
        body, name="out_norm_bwd", grid=(S // tr,),
        in_specs=[_row_spec(tr, 2 * W), _row_spec(tr, W), _row_spec(tr, W), _full_spec((1, W)), _full_spec((1, W))],
        out_specs=[_row_spec(tr, W), _row_spec(tr, W), _full_spec((1, W)), _full_spec((1, W))],
        out_shape=[jax.ShapeDtypeStruct((S, W), BF16), jax.ShapeDtypeStruct((S, W), BF16),
                   jax.ShapeDtypeStruct((1, W), F32), jax.ShapeDtypeStruct((1, W), F32)],
        compiler_params=_params(48 * tr * W, ("arbitrary",)),
    )(do, oa, ob, ga, gb)


def _loss_grad(y, t):
    S, D = y.shape
    tr = _tile(S, 256, 16)

    def body(y_ref, t_ref, dy_ref, dyb_ref, ls_ref):
        e = y_ref[...] - t_ref[...]
        dy = e * (1.0 / D)
        dy_ref[...] = dy
        dyb_ref[...] = dy.astype(BF16)
        part = jnp.sum(e * e, axis=0, keepdims=True)

        @pl.when(pl.program_id(0) == 0)
        def _():
            ls_ref[...] = part

        @pl.when(pl.program_id(0) > 0)
        def _():
            ls_ref[...] += part

    return pl.pallas_call(
        body, name="loss_grad", grid=(S // tr,),
        in_specs=[_row_spec(tr, D), _row_spec(tr, D)],
        out_specs=[_row_spec(tr, D), _row_spec(tr, D), _full_spec((1, D))],
        out_shape=[jax.ShapeDtypeStruct((S, D), F32), jax.ShapeDtypeStruct((S, D), BF16),
                   jax.ShapeDtypeStruct((1, D), F32)],
        compiler_params=_params(32 * tr * D, ("arbitrary",)),
    )(y, t)


def _head_rms(x, g):
    r = lax.rsqrt(jnp.mean(x * x, axis=-1, keepdims=True) + EPS)
    return x * r * g


def _head_rms_bwd(x, dy, g):
    r = lax.rsqrt(jnp.mean(x * x, axis=-1, keepdims=True) + EPS)
    gy = dy * g
    dot = jnp.mean(x * gy, axis=-1, keepdims=True)
    return r * gy - x * (r * r * r * dot), dy * (x * r)


def _rope(y, cos2, sgn_sin):
    return y * cos2 + pltpu.roll(y, HEAD_DIM // 2, axis=1) * sgn_sin


def _rope_t(dy, cos2, sgn_sin):
    return dy * cos2 + pltpu.roll(dy * sgn_sin, HEAD_DIM // 2, axis=1)


def _qk_prep_fwd(proj, cos2, sgn_sin, gains, dims):
    S, P = proj.shape
    naw, waw, kvw = dims
    tr = _tile(S, 256, 16)
    hd = HEAD_DIM

    def body(p_ref, c_ref, s_ref, g_ref, qa_ref, ka_ref, va_ref, qb_ref, kb_ref, vb_ref):
        c2 = c_ref[...]
        ss = s_ref[...]
        off = 0
        for dst, width, gi, rot in ((qa_ref, naw, 0, False), (ka_ref, naw, 1, False), (va_ref, naw, None, False),
                                    (qb_ref, waw, 2, True), (kb_ref, kvw, 3, True), (vb_ref, kvw, None, False)):
            for h in range(width // hd):
                xh = p_ref[:, off + h * hd:off + (h + 1) * hd]
                if gi is not None:
                    xh = _head_rms(xh, g_ref[gi:gi + 1, :])
                    if rot:
                        xh = _rope(xh, c2, ss)
                dst[:, h * hd:(h + 1) * hd] = xh.astype(BF16)
            off += width

    widths = (naw, naw, naw, waw, kvw, kvw)
    return pl.pallas_call(
        body, name="qk_prep_fwd", grid=(S // tr,),
        in_specs=[_row_spec(tr, P), _row_spec(tr, hd), _row_spec(tr, hd), _full_spec((SUBLANES, hd))],
        out_specs=[_row_spec(tr, w) for w in widths],
        out_shape=[jax.ShapeDtypeStruct((S, w), BF16) for w in widths],
        compiler_params=_params(16 * tr * P, ("parallel",)),
    )(proj, cos2, sgn_sin, gains)


def _qk_prep_bwd(proj, dqa, dka, dva, dqb, dkb, dvb, cos2, sgn_sin, gains, dims):
    S, P = proj.shape
    naw, waw, kvw = dims
    tr = _tile(S, 128, 16)
    hd = HEAD_DIM

    def body(p_ref, dqa_ref, dka_ref, dva_ref, dqb_ref, dkb_ref, dvb_ref, c_ref, s_ref, g_ref, dp_ref, dg_ref):
        c2 = c_ref[...]
        ss = s_ref[...]
        off = 0
        dgs = [jnp.zeros((1, hd), F32) for _ in range(4)]
        for src, width, gi, rot in ((dqa_ref, naw, 0, False), (dka_ref, naw, 1, False), (dva_ref, naw, None, False),
                                    (dqb_ref, waw, 2, True), (dkb_ref, kvw, 3, True), (dvb_ref, kvw, None, False)):
            for h in range(width // hd):
                dy = src[:, h * hd:(h + 1) * hd].astype(F32)
                if gi is not None:
                    if rot:
                        dy = _rope_t(dy, c2, ss)
                    xh = p_ref[:, off + h * hd:off + (h + 1) * hd]
                    dy, dgt = _head_rms_bwd(xh, dy, g_ref[gi:gi + 1, :])
                    dgs[gi] = dgs[gi] + jnp.sum(dgt, axis=0, keepdims=True)
                dp_ref[:, off + h * hd:off + (h + 1) * hd] = dy.astype(BF16)
            off += width
        part = jnp.concatenate(dgs + [jnp.zeros((SUBLANES - 4, hd), F32)], axis=0)

        @pl.when(pl.program_id(0) == 0)
        def _():
            dg_ref[...] = part

        @pl.when(pl.program_id(0) > 0)
        def _():
            dg_ref[...] += part

    return pl.pallas_call(
        body, name="qk_prep_bwd", grid=(S // tr,),
        in_specs=[_row_spec(tr, P), _row_spec(tr, naw), _row_spec(tr, naw), _row_spec(tr, naw),
                  _row_spec(tr, waw), _row_spec(tr, kvw), _row_spec(tr, kvw),
                  _row_spec(tr, hd), _row_spec(tr, hd), _full_spec((SUBLANES, hd))],
        out_specs=[_row_spec(tr, P), _full_spec((SUBLANES, hd))],
        out_shape=[jax.ShapeDtypeStruct((S, P), BF16), jax.ShapeDtypeStruct((SUBLANES, hd), F32)],
        compiler_params=_params(24 * tr * P, ("arbitrary",)),
    )(proj, dqa, dka, dva, dqb, dkb, dvb, cos2, sgn_sin, gains)


NA_KEYS = NA_WIN_R * GRID_W
NA_ROWS_PER_STEP = 8


def _na_col_geometry():
    c = np.arange(GRID_W)
    col_start = np.clip(c - NA_WIN_C // 2, 0, GRID_W - NA_WIN_C)
    mask = (c[None, :] >= col_start[:, None]) & (c[None, :] < col_start[:, None] + NA_WIN_C)
    dc = np.clip(c[None, :] - c[:, None], -(NA_WIN_C - 1), NA_WIN_C - 1) + (NA_WIN_C - 1)
    onehot = (dc[:, :, None] == np.arange(2 * NA_WIN_C - 1)[None, None, :]) & mask[:, :, None]
    return mask, onehot


def _na_bias_table(rpb_l):
    H = rpb_l.shape[0]
    mask, onehot = _na_col_geometry()
    t = jnp.sum(jnp.where(onehot[None, None], rpb_l[:, :, None, None, :], 0.0), axis=-1)
    t = jnp.where(mask[None, None], t, NEG)
    per_delta = [jnp.transpose(t[:, d:d + NA_WIN_R], (0, 2, 1, 3)).reshape(H, GRID_W, NA_KEYS) for d in range(NA_WIN_R)]
    return jnp.stack(per_delta, axis=1)


def _na_bias_table_t(db):
    H = db.shape[0]
    _, onehot = _na_col_geometry()
    d5 = db.reshape(H, NA_WIN_R, GRID_W, NA_WIN_R, GRID_W)
    folded = jnp.einsum("hdqwk,qkc->hdwc", d5, onehot.astype(np.float32), precision=lax.Precision.HIGHEST)
    return sum(jnp.pad(folded[:, d], ((0, 0), (d, NA_WIN_R - 1 - d), (0, 0))) for d in range(NA_WIN_R))


def _na_row_geometry(r, rows):
    start = jnp.clip(r - NA_WIN_R // 2, 0, rows - NA_WIN_R)
    return start, start - r + (NA_WIN_R - 1)


def _na_probs(q, kw, bias):
    s = _dot(q, kw, NT) * ATTN_SCALE + bias
    m = jnp.max(s, axis=-1, keepdims=True)
    e = jnp.exp(s - m)
    return e / jnp.sum(e, axis=-1, keepdims=True)


def _na_fwd(qa, ka, va, btab):
    S, W = qa.shape
    H = W // HEAD_DIM
    rows = S // GRID_W
    assert rows >= NA_WIN_R
    rb = _tile(rows, NA_ROWS_PER_STEP, 1)
    tq = rb * GRID_W

    def body(q_ref, k_ref, v_ref, b_ref, o_ref):
        i = pl.program_id(1)

        def row(j, carry):
            start, delta = _na_row_geometry(i * rb + j, rows)
            tok = pl.ds(pl.multiple_of(j * GRID_W, GRID_W), GRID_W)
            win = pl.ds(pl.multiple_of(start * GRID_W, GRID_W), NA_KEYS)
            p = _na_probs(q_ref[tok, :], k_ref[win, :], b_ref[delta])
            o_ref[tok, :] = _dot(p, v_ref[win, :], NN)
            return carry

        lax.fori_loop(0, rb, row, 0, unroll=True)

    kv_spec = pl.BlockSpec((S, HEAD_DIM), lambda h, i: (0, h))
    return pl.pallas_call(
        body, name="na_fwd", grid=(H, rows // rb),
        in_specs=[pl.BlockSpec((tq, HEAD_DIM), lambda h, i: (i, h)), kv_spec, kv_spec,
                  pl.BlockSpec((None, NA_WIN_R, GRID_W, NA_KEYS), lambda h, i: (h, 0, 0, 0))],
        out_specs=pl.BlockSpec((tq, HEAD_DIM), lambda h, i: (i, h)),
        out_shape=jax.ShapeDtypeStruct((S, W), F32),
        compiler_params=_params(8 * S * HEAD_DIM + (8 << 20), ("parallel", "arbitrary")),
    )(qa, ka, va, btab)


def _na_bwd(qa, ka, va, btab, doa):
    S, W = qa.shape
    H = W // HEAD_DIM
    rows = S // GRID_W
    rb = _tile(rows, NA_ROWS_PER_STEP, 1)
    tq = rb * GRID_W

    def body(q_ref, k_ref, v_ref, b_ref, do_ref, dq_ref, dk_ref, dv_ref, db_ref):
        i = pl.program_id(1)

        @pl.when(i == 0)
        def _():
            dk_ref[...] = jnp.zeros_like(dk_ref)
            dv_ref[...] = jnp.zeros_like(dv_ref)
            db_ref[...] = jnp.zeros_like(db_ref)

        def row(j, carry):
            start, delta = _na_row_geometry(i * rb + j, rows)
            tok = pl.ds(pl.multiple_of(j * GRID_W, GRID_W), GRID_W)
            win = pl.ds(pl.multiple_of(start * GRID_W, GRID_W), NA_KEYS)
            q = q_ref[tok, :]
            kw = k_ref[win, :]
            vw = v_ref[win, :]
            do = do_ref[tok, :]
            p = _na_probs(q, kw, b_ref[delta])
            dp = _dot(do, vw, NT)
            ds = p * (dp - jnp.sum(p * dp, axis=-1, keepdims=True))
            db_ref[delta] += ds
            dss = (ds * ATTN_SCALE).astype(BF16)
            dq_ref[tok, :] = _dot(dss, kw, NN)
            dk_ref[win, :] += _dot(dss, q, TN)
            dv_ref[win, :] += _dot(p, do, TN)
            return carry

        lax.fori_loop(0, rb, row, 0, unroll=True)

    kv_spec = pl.BlockSpec((S, HEAD_DIM), lambda h, i: (0, h))
    q_spec = pl.BlockSpec((tq, HEAD_DIM), lambda h, i: (i, h))
    b_spec = pl.BlockSpec((None, NA_WIN_R, GRID_W, NA_KEYS), lambda h, i: (h, 0, 0, 0))
    return pl.pallas_call(
        body, name="na_bwd", grid=(H, rows // rb),
        in_specs=[q_spec, kv_spec, kv_spec, b_spec, q_spec],
        out_specs=[q_spec, kv_spec, kv_spec, b_spec],
        out_shape=[jax.ShapeDtypeStruct((S, W), F32), jax.ShapeDtypeStruct((S, W), F32),
                   jax.ShapeDtypeStruct((S, W), F32), jax.ShapeDtypeStruct(btab.shape, F32)],
        compiler_params=_params(24 * S * HEAD_DIM + (12 << 20), ("parallel", "arbitrary")),
    )(qa, ka, va, btab, doa)


WA_KEYS = 3 * WA_BLOCK
WA_BLOCKS_PER_STEP = 4


def _wa_scores(qs, kw, n, start, g):
    s = _dot(qs, kw, NT) * ATTN_SCALE
    qpos = n * WA_BLOCK + lax.broadcasted_iota(jnp.int32, (WA_BLOCK, WA_KEYS), 0)
    kpos = start + lax.broadcasted_iota(jnp.int32, (WA_BLOCK, WA_KEYS), 1)
    valid = jnp.abs(kpos - qpos) <= WA_WINDOW
    valid = jnp.concatenate([valid] * g, axis=0)
    return jnp.where(valid, s, NEG)


def _wa_probs(s, sink):
    m = jnp.maximum(jnp.max(s, axis=-1, keepdims=True), sink)
    e = jnp.exp(s - m)
    es = jnp.exp(sink - m)
    den = jnp.sum(e, axis=-1, keepdims=True) + es
    return e / den, es / den


def _wa_stack(ref, tok, g):
    return jnp.concatenate([ref[tok, t * HEAD_DIM:(t + 1) * HEAD_DIM] for t in range(g)], axis=0)


def _wa_fwd(qb, kb, vb, sink_col):
    S, W = qb.shape
    hkv = kb.shape[1] // HEAD_DIM
    g = W // HEAD_DIM // hkv
    nb = S // WA_BLOCK
    assert S >= WA_KEYS
    qb_step = _tile(nb, WA_BLOCKS_PER_STEP, 1)
    tq = qb_step * WA_BLOCK

    def body(q_ref, k_ref, v_ref, s_ref, o_ref):
        i = pl.program_id(1)

        def blk(j, carry):
            n = i * qb_step + j
            tok = pl.ds(pl.multiple_of(j * WA_BLOCK, WA_BLOCK), WA_BLOCK)
            start = pl.multiple_of(jnp.clip((n - 1) * WA_BLOCK, 0, S - WA_KEYS), WA_BLOCK)
            win = pl.ds(start, WA_KEYS)
            p, _ = _wa_probs(_wa_scores(_wa_stack(q_ref, tok, g), k_ref[win, :], n, start, g), s_ref[...])
            o = _dot(p, v_ref[win, :], NN)
            for t in range(g):
                o_ref[tok, t * HEAD_DIM:(t + 1) * HEAD_DIM] = o[t * WA_BLOCK:(t + 1) * WA_BLOCK]
            return carry

        lax.fori_loop(0, qb_step, blk, 0, unroll=True)

    kv_spec = pl.BlockSpec((S, HEAD_DIM), lambda h, i: (0, h))
    q_spec = pl.BlockSpec((tq, g * HEAD_DIM), lambda h, i: (i, h))
    return pl.pallas_call(
        body, name="wa_fwd", grid=(hkv, nb // qb_step),
        in_specs=[q_spec, kv_spec, kv_spec, pl.BlockSpec((None, g * WA_BLOCK, 1), lambda h, i: (h, 0, 0))],
        out_specs=q_spec, out_shape=jax.ShapeDtypeStruct((S, W), F32),
        compiler_params=_params(8 * S * HEAD_DIM + (12 << 20), ("parallel", "arbitrary")),
    )(qb, kb, vb, sink_col)


def _wa_bwd(qb, kb, vb, sink_col, dob):
    S, W = qb.shape
    KW = kb.shape[1]
    hkv = KW // HEAD_DIM
    g = W // HEAD_DIM // hkv
    nb = S // WA_BLOCK
    qb_step = _tile(nb, WA_BLOCKS_PER_STEP, 1)
    tq = qb_step * WA_BLOCK

    def body(q_ref, k_ref, v_ref, s_ref, do_ref, dq_ref, dk_ref, dv_ref, dsink_ref):
        i = pl.program_id(1)

        @pl.when(i == 0)
        def _():
            dk_ref[...] = jnp.zeros_like(dk_ref)
            dv_ref[...] = jnp.zeros_like(dv_ref)
            dsink_ref[...] = jnp.zeros_like(dsink_ref)

        def blk(j, carry):
            n = i * qb_step + j
            tok = pl.ds(pl.multiple_of(j * WA_BLOCK, WA_BLOCK), WA_BLOCK)
            start = pl.multiple_of(jnp.clip((n - 1) * WA_BLOCK, 0, S - WA_KEYS), WA_BLOCK)
            win = pl.ds(start, WA_KEYS)
            qs = _wa_stack(q_ref, tok, g)
            dos = _wa_stack(do_ref, tok, g)
            kw = k_ref[win, :]
            vw = v_ref[win, :]
            p, ps = _wa_probs(_wa_scores(qs, kw, n, start, g), s_ref[...])
            dp = _dot(dos, vw, NT)
            dsum = jnp.sum(p * dp, axis=-1, keepdims=True)
            ds = p * (dp - dsum)
            dsink_ref[...] -= ps * dsum
            dss = (ds * ATTN_SCALE).astype(BF16)
            dq = _dot(dss, kw, NN)
            for t in range(g):
                dq_ref[tok, t * HEAD_DIM:(t + 1) * HEAD_DIM] = dq[t * WA_BLOCK:(t + 1) * WA_BLOCK]
            dk_ref[win, :] += _dot(dss, qs, TN)
            dv_ref[win, :] += _dot(p, dos, TN)
            return carry

        lax.fori_loop(0, qb_step, blk, 0, unroll=True)

    kv_spec = pl.BlockSpec((S, HEAD_DIM), lambda h, i: (0, h))
    q_spec = pl.BlockSpec((tq, g * HEAD_DIM), lambda h, i: (i, h))
    s_spec = pl.BlockSpec((None, g * WA_BLOCK, 1), lambda h, i: (h, 0, 0))
    return pl.pallas_call(
        body, name="wa_bwd", grid=(hkv, nb // qb_step),
        in_specs=[q_spec, kv_spec, kv_spec, s_spec, q_spec],
        out_specs=[q_spec, kv_spec, kv_spec, s_spec],
        out_shape=[jax.ShapeDtypeStruct((S, W), F32), jax.ShapeDtypeStruct((S, KW), F32),
                   jax.ShapeDtypeStruct((S, KW), F32), jax.ShapeDtypeStruct(sink_col.shape, F32)],
        compiler_params=_params(24 * S * HEAD_DIM + (16 << 20), ("parallel", "arbitrary")),
    )(qb, kb, vb, sink_col, dob)


CONV_HALO = SUBLANES


def _conv_specs(tr, width, nblk, rows_inner):
    per = tr // CONV_HALO

    def spec(shape, row_block):
        if rows_inner:
            return pl.BlockSpec(shape, lambda j, i: (row_block(i), j))
        return pl.BlockSpec(shape, lambda i, j: (row_block(i), j))

    cur = spec((tr, width), lambda i: i)
    prev = spec((CONV_HALO, width), lambda i: jnp.maximum(i * per - 1, 0))
    nxt = spec((CONV_HALO, width), lambda i: jnp.minimum((i + 1) * per, nblk * per - 1))
    return prev, cur, nxt


def _extend(prev_ref, cur_ref, next_ref, i, nblk):
    p = jnp.where(i > 0, prev_ref[...].astype(F32), 0.0)
    n = jnp.where(i < nblk - 1, next_ref[...].astype(F32), 0.0)
    return jnp.concatenate([p, cur_ref[...].astype(F32), n], axis=0)


def _shift_down(x):
    return pltpu.roll(x, 1, axis=0)


def _shift_up(x):
    return pltpu.roll(x, x.shape[0] - 1, axis=0)


def _conv(ext, w_ref, b_ref):
    return _shift_down(ext) * w_ref[0:1, :] + ext * w_ref[1:2, :] + _shift_up(ext) * w_ref[2:3, :] + b_ref[...]


def _sigmoid(x):
    return 1.0 / (1.0 + jnp.exp(-x))


def _convgate_fwd(up_pre, cw, cb, tc):
    S, C2 = up_pre.shape
    tr = _tile(S, 256, 16)
    nblk = S // tr
    prev, cur, nxt = _conv_specs(tr, 2 * tc, nblk, False)

    def body(p_ref, c_ref, n_ref, w_ref, b_ref, a_ref):
        i = pl.program_id(0)
        u = _conv(_extend(p_ref, c_ref, n_ref, i, nblk), w_ref, b_ref)[CONV_HALO:CONV_HALO + tr]
        gate, up = u[:, :tc], u[:, tc:]
        a_ref[...] = (gate * _sigmoid(gate) * up).astype(BF16)

    return pl.pallas_call(
        body, name="convgate_fwd", grid=(nblk, C2 // (2 * tc)),
        in_specs=[prev, cur, nxt, pl.BlockSpec((3, 2 * tc), lambda i, j: (0, j)),
                  pl.BlockSpec((1, 2 * tc), lambda i, j: (0, j))],
        out_specs=pl.BlockSpec((tr, tc), lambda i, j: (i, j)),
        out_shape=jax.ShapeDtypeStruct((S, C2 // 2), BF16),
        compiler_params=_params(48 * tr * tc, ("parallel", "parallel")),
    )(up_pre, up_pre, up_pre, cw, cb)


def _convgate_bwd(up_pre, dact, cw, cb, tc):
    S, C2 = up_pre.shape
    tr = _tile(S, 128, 16)
    nblk = S // tr
    prev, cur, nxt = _conv_specs(tr, 2 * tc, nblk, True)
    dprev, dcur, dnxt = _conv_specs(tr, tc, nblk, True)
    ext_rows = tr + 2 * CONV_HALO

    def body(p_ref, c_ref, n_ref, dp_ref, dc_ref, dn_ref, w_ref, b_ref, dx_ref, dw_ref, db_ref):
        i = pl.program_id(1)
        ext = _extend(p_ref, c_ref, n_ref, i, nblk)
        da = _extend(dp_ref, dc_ref, dn_ref, i, nblk)
        u = _conv(ext, w_ref, b_ref)
        gate, up = u[:, :tc], u[:, tc:]
        sg = _sigmoid(gate)
        silu = gate * sg
        du = jnp.concatenate([da * up * (sg + silu * (1.0 - sg)), da * silu], axis=1)
        dx = _shift_up(du) * w_ref[0:1, :] + du * w_ref[1:2, :] + _shift_down(du) * w_ref[2:3, :]
        dx_ref[...] = dx[CONV_HALO:CONV_HALO + tr].astype(BF16)
        row = lax.broadcasted_iota(jnp.int32, (ext_rows, 1), 0)
        duc = jnp.where(jnp.logical_and(row >= CONV_HALO, row < CONV_HALO + tr), du, 0.0)
        dw = jnp.concatenate([jnp.sum(duc * _shift_down(ext), axis=0, keepdims=True),
                              jnp.sum(duc * ext, axis=0, keepdims=True),
                              jnp.sum(duc * _shift_up(ext), axis=0, keepdims=True)], axis=0)
        db = jnp.sum(duc, axis=0, keepdims=True)

        @pl.when(i == 0)
        def _():
            dw_ref[...] = dw
            db_ref[...] = db

        @pl.when(i > 0)
        def _():
            dw_ref[...] += dw
            db_ref[...] += db

    return pl.pallas_call(
        body, name="convgate_bwd", grid=(C2 // (2 * tc), nblk),
        in_specs=[prev, cur, nxt, dprev, dcur, dnxt,
                  pl.BlockSpec((3, 2 * tc), lambda j, i: (0, j)), pl.BlockSpec((1, 2 * tc), lambda j, i: (0, j))],
        out_specs=[pl.BlockSpec((tr, 2 * tc), lambda j, i: (i, j)),
                   pl.BlockSpec((3, 2 * tc), lambda j, i: (0, j)), pl.BlockSpec((1, 2 * tc), lambda j, i: (0, j))],
        out_shape=[jax.ShapeDtypeStruct((S, C2), BF16), jax.ShapeDtypeStruct((3, C2), F32),
                   jax.ShapeDtypeStruct((1, C2), F32)],
        compiler_params=_params(160 * tr * tc, ("parallel", "arbitrary")),
    )(up_pre, up_pre, up_pre, dact, dact, dact, cw, cb)


def _rope_tables(positions):
    inv = ROPE_THETA ** (-jnp.arange(0, HEAD_DIM, 2, dtype=F32) / HEAD_DIM)
    ang = positions.astype(F32)[:, None] * inv[None, :]
    cos, sin = jnp.cos(ang), jnp.sin(ang)
    return jnp.concatenate([cos, cos], axis=1), jnp.concatenate([-sin, sin], axis=1)


def _sink_col(sink_l, hkv):
    g = sink_l.shape[0] // hkv
    return jnp.broadcast_to(sink_l.reshape(hkv, g, 1), (hkv, g, WA_BLOCK)).reshape(hkv, g * WA_BLOCK, 1)


def _sink_col_t(dcol):
    hkv, rows, _ = dcol.shape
    return jnp.sum(dcol.reshape(hkv, rows // WA_BLOCK, WA_BLOCK), axis=-1).reshape(-1)


def _to_paired(a, sh):
    return jnp.concatenate([a[:, sh.nat(t) * sh.tn:(sh.nat(t) + 1) * sh.tn] for t in range(sh.ntiles)], axis=1)


def _from_paired(a, sh):
    pos = {sh.nat(t): t for t in range(sh.ntiles)}
    return jnp.concatenate([a[:, pos[n] * sh.tn:(pos[n] + 1) * sh.tn] for n in range(sh.ntiles)], axis=1)


def _pack(arrs):
    flat = jnp.concatenate([a.reshape(-1).astype(F32) for a in arrs])
    unit = SUBLANES * LANES
    total = -(-flat.shape[0] // unit) * unit
    return jnp.pad(flat, (0, total - flat.shape[0])).reshape(-1, LANES)


def _unpack(buf, shapes):
    flat = buf.reshape(-1)
    out, off = [], 0
    for s in shapes:
        n = math.prod(s)
        out.append(flat[off:off + n].reshape(s))
        off += n
    return out


HBM_SPEC = pl.BlockSpec(memory_space=pltpu.HBM)
DMA_CHUNK_BYTES = 512 * 1024


def _row_chunks(rows, row_bytes, align):
    want = max(1, rows * row_bytes // DMA_CHUNK_BYTES)
    count = max(k for k in range(1, rows + 1) if rows % k == 0 and (rows // k) % align == 0 and (k <= want or k == 1))
    size = rows // count
    return [(j * size, size) for j in range(count)]


def _mesh_pos():
    return lax.axis_index("x"), lax.axis_index("y"), lax.axis_index("c")


def _other_chips(x, y):
    return [(1 - x, y), (x, 1 - y), (1 - x, 1 - y)]


def _remote(src, dst, send_sems, recv_sems, k, dev):
    return pltpu.make_async_remote_copy(src_ref=src, dst_ref=dst, send_sem=send_sems.at[k], recv_sem=recv_sems.at[k],
                                        device_id=dev, device_id_type=MESH)


def _all_gather_layer(shards, small):
    n = len(shards)
    halves = [s.shape[0] // 2 for s in shards]
    plan = sorted(((j, t, k, s0, sz) for t in range(n) for k in range(3)
                   for j, (s0, sz) in enumerate(_row_chunks(halves[t], shards[t].shape[1] * 2, 16))))
    n_ici = len(plan)
    n_sems = 2 * n_ici + 3

    def body(*refs):
        ins, sm_in = refs[:n], refs[n]
        outs, sm_out = refs[n + 1:2 * n + 1], refs[2 * n + 1]
        send_sems, recv_sems, loc_sems = refs[2 * n + 2:]
        x, y, c = _mesh_pos()
        q_me = 2 * x + y
        chips = _other_chips(x, y)
        sib = (x, y, 1 - c)
        rc = functools.partial(_remote, send_sems=send_sems, recv_sems=recv_sems)

        for t in range(n):
            for s0, sz in _row_chunks(2 * halves[t], shards[t].shape[1] * 2, 16):
                pltpu.make_async_copy(ins[t].at[pl.ds(s0, sz)], outs[t].at[q_me, pl.ds(s0, sz)], loc_sems.at[t]).start()
        pltpu.make_async_copy(sm_in, sm_out.at[q_me], loc_sems.at[n]).start()
        sends = []
        for i, (_, t, k, s0, sz) in enumerate(plan):
            rows = pl.ds(c * halves[t] + s0, sz)
            sends.append(rc(ins[t].at[rows], outs[t].at[q_me, rows], k=i, dev=(*chips[k], c)))
        for k, chip in enumerate(chips):
            sends.append(rc(sm_in, sm_out.at[q_me], k=2 * n_ici + k, dev=(*chip, c)))
        for cp in sends:
            cp.start()
        passed = []
        for i, (_, t, k, s0, sz) in enumerate(plan):
            cx, cy = chips[k]
            blk = outs[t].at[2 * cx + cy, pl.ds(c * halves[t] + s0, sz)]
            rc(blk, blk, k=i, dev=sib).wait_recv()
            fwd = rc(blk, blk, k=n_ici + i, dev=sib)
            fwd.start()
            passed.append(fwd)
        for i, (_, t, k, s0, sz) in enumerate(plan):
            cx, cy = chips[k]
            blk = outs[t].at[2 * cx + cy, pl.ds((1 - c) * halves[t] + s0, sz)]
            rc(blk, blk, k=n_ici + i, dev=sib).wait_recv()
        for k, (cx, cy) in enumerate(chips):
            blk = sm_out.at[2 * cx + cy]
            rc(blk, blk, k=2 * n_ici + k, dev=sib).wait_recv()
        for cp in sends + passed:
            cp.wait_send()
        for t in range(n):
            pltpu.make_async_copy(ins[t], outs[t].at[q_me], loc_sems.at[t]).wait()
        pltpu.make_async_copy(sm_in, sm_out.at[q_me], loc_sems.at[n]).wait()

    out_shape = [jax.ShapeDtypeStruct((N_CHIPS,) + s.shape, s.dtype) for s in shards]
    out_shape.append(jax.ShapeDtypeStruct((N_CHIPS,) + small.shape, small.dtype))
    return pl.pallas_call(
        body, name="all_gather_layer",
        in_specs=[HBM_SPEC] * (n + 1), out_specs=[HBM_SPEC] * (n + 1), out_shape=out_shape,
        scratch_shapes=[pltpu.SemaphoreType.DMA((n_sems,)), pltpu.SemaphoreType.DMA((n_sems,)),
                        pltpu.SemaphoreType.DMA((n + 1,))],
    )(*shards, small)


def _rs_sibling_exchange(grads):
    n = len(grads)
    halves = [g.shape[1] // 2 for g in grads]

    def body(*refs):
        ins, recvs = refs[:n], refs[n:2 * n]
        send_sems, recv_sems = refs[2 * n:]
        x, y, c = _mesh_pos()
        sib = (x, y, 1 - c)
        for t in range(n):
            for q in range(N_CHIPS):
                for s0, sz in _row_chunks(halves[t], grads[t].shape[2] * 4, SUBLANES):
                    _remote(ins[t].at[q, pl.ds((1 - c) * halves[t] + s0, sz)], recvs[t].at[q, pl.ds(s0, sz)],
                            send_sems, recv_sems, t, sib).start()
        for t in range(n):
            _remote(recvs[t], recvs[t], send_sems, recv_sems, t, sib).wait()

    return pl.pallas_call(
        body, name="rs_sibling_exchange", in_specs=[HBM_SPEC] * n, out_specs=[HBM_SPEC] * n,
        out_shape=[jax.ShapeDtypeStruct((N_CHIPS, h, g.shape[2]), F32) for g, h in zip(grads, halves)],
        scratch_shapes=[pltpu.SemaphoreType.DMA((n,)), pltpu.SemaphoreType.DMA((n,))],
    )(*grads)


def _rs_chip_exchange(pbs):
    n = len(pbs)

    def body(*refs):
        pb, recvs = refs[:n], refs[n:2 * n]
        send_sems, recv_sems = refs[2 * n:]
        x, y, c = _mesh_pos()
        for t in range(n):
            for s0, sz in _row_chunks(pbs[t].shape[1], pbs[t].shape[2] * 2, 16):
                for k, (cx, cy) in enumerate(_other_chips(x, y)):
                    _remote(pb[t].at[2 * cx + cy, pl.ds(s0, sz)], recvs[t].at[k, pl.ds(s0, sz)],
                            send_sems, recv_sems, t, (cx, cy, c)).start()
        for t in range(n):
            _remote(recvs[t], recvs[t], send_sems, recv_sems, t, (x, y, 1 - c)).wait()

    return pl.pallas_call(
        body, name="rs_chip_exchange", in_specs=[HBM_SPEC] * n, out_specs=[HBM_SPEC] * n,
        out_shape=[jax.ShapeDtypeStruct((3,) + p.shape[1:], BF16) for p in pbs],
        scratch_shapes=[pltpu.SemaphoreType.DMA((n,)), pltpu.SemaphoreType.DMA((n,))],
    )(*pbs)


def _rs_sibling_share(gs):
    n = len(gs)

    def body(*refs):
        outs = refs[n:2 * n]
        send_sems, recv_sems = refs[2 * n:]
        x, y, c = _mesh_pos()
        sib = (x, y, 1 - c)
        for t in range(n):
            h = gs[t].shape[0] // 2
            for s0, sz in _row_chunks(h, gs[t].shape[1] * 4, SUBLANES):
                rows = outs[t].at[pl.ds(c * h + s0, sz)]
                _remote(rows, rows, send_sems, recv_sems, t, sib).start()
        for t in range(n):
            half = outs[t].at[pl.ds(0, gs[t].shape[0] // 2)]
            _remote(half, half, send_sems, recv_sems, t, sib).wait()

    return pl.pallas_call(
        body, name="rs_sibling_share", in_specs=[HBM_SPEC] * n, out_specs=[HBM_SPEC] * n,
        out_shape=[jax.ShapeDtypeStruct(g.shape, F32) for g in gs],
        input_output_aliases={t: t for t in range(n)},
        scratch_shapes=[pltpu.SemaphoreType.DMA((n,)), pltpu.SemaphoreType.DMA((n,))],
    )(*gs)


def _gather_small(buf):
    def body(in_ref, out_ref, send_sems, recv_sems, loc_sem):
        x, y, c = _mesh_pos()
        me = 4 * x + 2 * y + c
        local = pltpu.make_async_copy(in_ref, out_ref.at[me], loc_sem)
        local.start()
        sends, peers = [], []
        for k in range(1, N_DEV):
            bx, by, bc = (k >> 2) & 1, (k >> 1) & 1, k & 1
            peer = (x + bx - 2 * x * bx, y + by - 2 * y * by, c + bc - 2 * c * bc)
            peers.append(peer)
            sends.append(_remote(in_ref, out_ref.at[me], send_sems, recv_sems, k - 1, peer))
        for cp in sends:
            cp.start()
        for k, (px, py, pc) in enumerate(peers):
            slot = out_ref.at[4 * px + 2 * py + pc]
            _remote(slot, slot, send_sems, recv_sems, k, (px, py, pc)).wait_recv()
        for cp in sends:
            cp.wait_send()
        local.wait()

    return pl.pallas_call(
        body, name="gather_small", in_specs=[HBM_SPEC], out_specs=HBM_SPEC,
        out_shape=jax.ShapeDtypeStruct((N_DEV,) + buf.shape, F32),
        scratch_shapes=[pltpu.SemaphoreType.DMA((N_DEV - 1,)), pltpu.SemaphoreType.DMA((N_DEV - 1,)),
                        pltpu.SemaphoreType.DMA(())],
    )(buf)


def _ew_rows(rows, cols, bytes_per_elem):
    budget = 20 * 1024 * 1024
    return _tile(rows, max(16, budget // (2 * bytes_per_elem * cols) // 16 * 16), 16)


def _sum_pair(grad, recv):
    _, h, cols = recv.shape
    tr = _ew_rows(h, cols, 14)
    nb = h // tr
    half = pl.BlockSpec((None, tr, cols), lambda q, i: (q, i, 0))

    def body(a_ref, b_ref, f_ref, h_ref):
        s = a_ref[...] + b_ref[...]
        f_ref[...] = s
        h_ref[...] = s.astype(BF16)

    return pl.pallas_call(
        body, name="rs_sum_pair", grid=(N_CHIPS, nb),
        in_specs=[pl.BlockSpec((None, tr, cols), lambda q, i: (q, lax.axis_index("c") * nb + i, 0)), half],
        out_specs=[half, half],
        out_shape=[jax.ShapeDtypeStruct(recv.shape, F32), jax.ShapeDtypeStruct(recv.shape, BF16)],
        compiler_params=_params(28 * tr * cols, ("parallel", "parallel")),
    )(grad, recv)


def _sum_four(pf, recv):
    _, h, cols = pf.shape
    tr = _ew_rows(h, cols, 14)
    nb = h // tr

    def body(a_ref, r_ref, o_ref):
        o_ref[...] = ((a_ref[...] + r_ref[0].astype(F32)) + r_ref[1].astype(F32)) + r_ref[2].astype(F32)

    return pl.pallas_call(
        body, name="rs_sum_four", grid=(nb,),
        in_specs=[pl.BlockSpec((None, tr, cols), lambda i: (2 * lax.axis_index("x") + lax.axis_index("y"), i, 0)),
                  pl.BlockSpec((3, tr, cols), lambda i: (0, i, 0))],
        out_specs=pl.BlockSpec((tr, cols), lambda i: (lax.axis_index("c") * nb + i, 0)),
        out_shape=jax.ShapeDtypeStruct((2 * h, cols), F32),
        compiler_params=_params(28 * tr * cols, ("parallel",)),
    )(pf, recv)


def _sum_eight(gathered):
    _, rows, cols = gathered.shape
    tr = _tile(rows, 512, SUBLANES)

    def body(g_ref, o_ref):
        s = g_ref[0]
        for d in range(1, N_DEV):
            s = s + g_ref[d]
        o_ref[...] = s

    return pl.pallas_call(
        body, name="sum_eight", grid=(rows // tr,),
        in_specs=[pl.BlockSpec((N_DEV, tr, cols), lambda i: (0, i, 0))],
        out_specs=_row_spec(tr, cols), out_shape=jax.ShapeDtypeStruct((rows, cols), F32),
        compiler_params=_params(None, ("parallel",)),
    )(gathered)


def _cast_layer(w, l):
    _, rows, cols = w.shape
    tr = _ew_rows(rows, cols, 6)

    def body(w_ref, o_ref):
        o_ref[...] = w_ref[...].astype(BF16)

    return pl.pallas_call(
        body, name="cast_bf16", grid=(rows // tr,),
        in_specs=[pl.BlockSpec((None, tr, cols), lambda i: (l, i, 0))],
        out_specs=_row_spec(tr, cols), out_shape=jax.ShapeDtypeStruct((rows, cols), BF16),
        compiler_params=_params(12 * tr * cols, ("parallel",)),
    )(w)


def _adamw(g, w, m, v, l, prev):
    L, rows, cols = w.shape
    tr = _ew_rows(rows, cols, 32)
    layer = pl.BlockSpec((None, tr, cols), lambda i: (l, i, 0))

    def body(g_ref, w_ref, m_ref, v_ref, *rest):
        og_ref, od_ref, om_ref, ov_ref = rest[-4:]
        gv = g_ref[...]
        mn = ADAM_B1 * m_ref[...] + (1.0 - ADAM_B1) * gv
        vn = ADAM_B2 * v_ref[...] + (1.0 - ADAM_B2) * (gv * gv)
        m_hat = mn / ADAM_C1
        v_hat = vn / ADAM_C2
        od_ref[...] = -ADAM_LR * (m_hat / (jnp.sqrt(v_hat) + ADAM_EPS) + ADAM_WD * w_ref[...])
        og_ref[...] = gv
        om_ref[...] = mn
        ov_ref[...] = vn

    in_specs = [_row_spec(tr, cols), layer, layer, layer]
    args = [g, w, m, v]
    aliases = {}
    if prev is not None:
        in_specs += [HBM_SPEC] * 4
        args += list(prev)
        aliases = {4 + i: i for i in range(4)}
    return pl.pallas_call(
        body, name="adamw", grid=(rows // tr,), in_specs=in_specs, out_specs=[layer] * 4,
        out_shape=[jax.ShapeDtypeStruct((L, rows, cols), F32)] * 4, input_output_aliases=aliases,
        compiler_params=_params(64 * tr * cols, ("parallel",)),
    )(*args)


WEIGHT_ORDER = ("ln1_g", "w_in", "qn_a", "kn_a", "rpb", "qn_b", "kn_b", "sink", "on_a", "on_b", "w_out", "ln2_g",
                "w_up", "conv_w", "conv_b", "w_down")
BIG = ("w_in", "w_out", "w_up", "w_down")
SMALL = tuple(n for n in WEIGHT_ORDER if n not in BIG)


def _reduce_scatter_layer(grads):
    recvs = _rs_sibling_exchange(grads)
    pairs = [_sum_pair(g, r) for g, r in zip(grads, recvs)]
    gots = _rs_chip_exchange([p[1] for p in pairs])
    return _rs_sibling_share([_sum_four(p[0], r) for p, r in zip(pairs, gots)])


def _train_step(x, positions, target, w, m, v):
    S, D = x.shape
    L = w["w_in"].shape[0]
    naw = w["rpb"].shape[1] * HEAD_DIM
    waw = w["sink"].shape[1] * HEAD_DIM
    cin = w["w_in"].shape[2]
    kvw = (N_CHIPS * cin - 3 * naw - waw) // 2
    hkv = kvw // HEAD_DIM
    dims = (naw, waw, kvw)
    assert naw == waw, "the two head groups are normalised by one kernel and must be equally wide"
    cup = w["w_up"].shape[2]
    c2 = N_CHIPS * cup
    dff = c2 // 2
    mix = naw + waw
    sh_in = _ColShards(cin, _tile(cin, 1152, LANES), False)
    sh_up = _ColShards(cup, cup // 2 if (cup // 2) % LANES == 0 else cup, True)
    tc = sh_up.tn
    cos2, sgn_sin = _rope_tables(positions)
    q_me = 2 * lax.axis_index("x") + lax.axis_index("y")
    row = lambda a: a[None]

    saved = []
    for l in range(L):
        g_in, g_out, g_up, g_dn, g_cw = _all_gather_layer(
            [_cast_layer(w[n], l) for n in BIG], w["conv_w"][l])
        g_out = g_out.reshape(mix, D)
        g_dn = g_dn.reshape(dff, D)
        cw_p = _to_paired(jnp.transpose(g_cw, (1, 0, 2)).reshape(3, c2), sh_up)
        cb_p = _to_paired(row(w["conv_b"][l]), sh_up)
        gains = jnp.concatenate([row(w["qn_a"][l]), row(w["kn_a"][l]), row(w["qn_b"][l]), row(w["kn_b"][l]),
                                 jnp.zeros((SUBLANES - 4, HEAD_DIM), F32)], axis=0)
        btab = _na_bias_table(w["rpb"][l])
        sink_col = _sink_col(w["sink"][l], hkv)

        h = _rms_fwd(x, row(w["ln1_g"][l]))
        proj = _mm_cols_fwd("mm_proj", h, g_in, sh_in, F32, tm=1024)
        qa, ka, va, qb, kb, vb = _qk_prep_fwd(proj, cos2, sgn_sin, gains, dims)
        oa = _na_fwd(qa, ka, va, btab)
        ob = _wa_fwd(qb, kb, vb, sink_col)
        o = _out_norm_fwd(oa, ob, row(w["on_a"][l]), row(w["on_b"][l]))
        x1 = _mm_plain("mm_attn_out", o, g_out, NN, F32, tm=1024, tn=1024, tk=2048, res=x)
        h2 = _rms_fwd(x1, row(w["ln2_g"][l]))
        up_pre = _mm_cols_fwd("mm_up", h2, g_up, sh_up, F32, tm=1024)
        act = _convgate_fwd(up_pre, cw_p, cb_p, tc)
        x2 = _mm_plain("mm_down", act, g_dn, NN, F32, tm=1024, tn=1024, tk=1408, res=x1)
        saved.append(dict(g_in=g_in, g_out=g_out, g_up=g_up, g_dn=g_dn, cw_p=cw_p, cb_p=cb_p, gains=gains, btab=btab,
                          sink_col=sink_col, x=x, h=h, proj=proj, qa=qa, ka=ka, va=va, qb=qb, kb=kb, vb=vb, oa=oa,
                          ob=ob, o=o, x1=x1, h2=h2, up_pre=up_pre, act=act))
        x = x2

    dx, dxb, loss_cols = _loss_grad(x, target)

    small_grads = [None] * L
    stacked = {n: None for n in BIG}
    for l in reversed(range(L)):
        s = saved[l]
        dact = _mm_plain("mm_down_dx", dxb, s["g_dn"], NT, F32, tm=1024, tn=1408, tk=2048)
        d_dn = _mm_plain("mm_down_dw", s["act"], dxb, TN, F32, tm=1408, tn=1024, tk=2048)
        dup_pre, dcw_p, dcb_p = _convgate_bwd(s["up_pre"], dact, s["cw_p"], s["cb_p"], tc)
        dh2 = _mm_cols_bwd_x("mm_up_dx", dup_pre, s["g_up"], sh_up, F32, tm=1024, tn=2048)
        d_up = _mm_cols_bwd_w("mm_up_dw", s["h2"], dup_pre, sh_up, tm=1024, tk=2048)
        dx1, dx1b, d_ln2 = _rms_bwd(dh2, s["x1"], row(w["ln2_g"][l]), dx)
        do = _mm_plain("mm_attn_out_dx", dx1b, s["g_out"], NT, F32, tm=1024, tn=1024, tk=2048)
        d_out = _mm_plain("mm_attn_out_dw", s["o"], dx1b, TN, F32, tm=1024, tn=1024, tk=2048)
        doa, dob, d_on_a, d_on_b = _out_norm_bwd(do, s["oa"], s["ob"], row(w["on_a"][l]), row(w["on_b"][l]))
        dqa, dka, dva, dbtab = _na_bwd(s["qa"], s["ka"], s["va"], s["btab"], doa)
        dqb, dkb, dvb, dsink_col = _wa_bwd(s["qb"], s["kb"], s["vb"], s["sink_col"], dob)
        dproj, dgains = _qk_prep_bwd(s["proj"], dqa, dka, dva, dqb, dkb, dvb, cos2, sgn_sin, s["gains"], dims)
        dh = _mm_cols_bwd_x("mm_proj_dx", dproj, s["g_in"], sh_in, F32, tm=1024, tn=2048)
        d_in = _mm_cols_bwd_w("mm_proj_dw", s["h"], dproj, sh_in, tm=1024, tk=2048)
        dx, dxb, d_ln1 = _rms_bwd(dh, s["x"], row(w["ln1_g"][l]), dx1)

        small_grads[l] = dict(
            ln1_g=d_ln1[0], qn_a=dgains[0], kn_a=dgains[1], rpb=_na_bias_table_t(dbtab), qn_b=dgains[2],
            kn_b=dgains[3], sink=_sink_col_t(dsink_col), on_a=d_on_a[0], on_b=d_on_b[0], ln2_g=d_ln2[0],
            conv_w=_from_paired(dcw_p, sh_up), conv_b=_from_paired(dcb_p, sh_up)[0])
        shards = _reduce_scatter_layer([d_in, d_out.reshape(N_CHIPS, mix // N_CHIPS, D), d_up,
                                        d_dn.reshape(N_CHIPS, dff // N_CHIPS, D)])
        for n, g in zip(BIG, shards):
            stacked[n] = _adamw(g, w[n], m[n], v[n], l, stacked[n])
    grad_x = dx[None]

    full_shapes = {n: (w[n].shape[1:] if n != "conv_w" else (3, c2)) for n in SMALL}
    packed = _pack([loss_cols] + [small_grads[l][n] for l in range(L) for n in SMALL])
    total = _sum_eight(_gather_small(packed))
    parts = _unpack(total, [(D,)] + [full_shapes[n] for _ in range(L) for n in SMALL])
    loss = 0.5 * jnp.sum(parts[0]) / D
    per_layer = [dict(zip(SMALL, parts[1 + l * len(SMALL):1 + (l + 1) * len(SMALL)])) for l in range(L)]
    for l in range(L):
        per_layer[l]["conv_w"] = lax.dynamic_slice_in_dim(per_layer[l]["conv_w"], q_me * cup, cup, axis=1)
    small_g = {n: jnp.stack([per_layer[l][n] for l in range(L)]) for n in SMALL}
    outs = _adamw(_pack([small_g[n] for n in SMALL]), _pack([w[n] for n in SMALL])[None],
                  _pack([m[n] for n in SMALL])[None], _pack([v[n] for n in SMALL])[None], 0, None)
    small_out = [dict(zip(SMALL, _unpack(o[0], [w[n].shape for n in SMALL]))) for o in outs]

    result = [loss, grad_x]
    for i in range(4):
        result += [stacked[n][i] if n in BIG else small_out[i][n] for n in WEIGHT_ORDER]
    return tuple(result)


def kernel(x, positions, ln1_g, w_in, qn_a, kn_a, rpb, qn_b, kn_b, sink, on_a, on_b, w_out, ln2_g, w_up, conv_w, conv_b, w_down, loss_target, m_ln1_g, m_w_in, m_qn_a, m_kn_a, m_rpb, m_qn_b, m_kn_b, m_sink, m_on_a, m_on_b, m_w_out, m_ln2_g, m_w_up, m_conv_w, m_conv_b, m_w_down, v_ln1_g, v_w_in, v_qn_a, v_kn_a, v_rpb, v_qn_b, v_kn_b, v_sink, v_on_a, v_on_b, v_w_out, v_ln2_g, v_w_up, v_conv_w, v_conv_b, v_w_down):
    given = dict(locals())
    w = {n: given[n] for n in WEIGHT_ORDER}
    m = {n: given["m_" + n] for n in WEIGHT_ORDER}
    v = {n: given["v_" + n] for n in WEIGHT_ORDER}
    return _train_step(x[0], positions, loss_target[0], w, m, v)
```

```python
import functools
import math

import jax
import jax.numpy as jnp
import numpy as np
from jax import lax
from jax.experimental import pallas as pl
from jax.experimental.pallas import tpu as pltpu

F32 = jnp.float32
BF16 = jnp.bfloat16

HEAD_DIM = 128
GRID_W = 64
NA_WIN_R = 8
NA_WIN_C = 16
WA_WINDOW = 128
WA_BLOCK = 128
ROPE_THETA = 10000.0
EPS = 1e-6
NEG = -1e30
ATTN_SCALE = 1.0 / math.sqrt(HEAD_DIM)

ADAM_LR = 0.001
ADAM_B1 = 0.9
ADAM_B2 = 0.999
ADAM_EPS = 1e-08
ADAM_WD = 0.01
ADAM_STEP = 10
ADAM_C1 = 1.0 - ADAM_B1 ** ADAM_STEP
ADAM_C2 = 1.0 - ADAM_B2 ** ADAM_STEP

V7X_VMEM_BYTES = 64 * 1024 * 1024
V7X_VMEM_CAP = V7X_VMEM_BYTES - 6 * 1024 * 1024
LANES = 128
SUBLANES = 8
N_CHIPS = 4
N_DEV = 8
MESH = pl.DeviceIdType.MESH

NN = ((1,), (0,))
NT = ((1,), (1,))
TN = ((0,), (0,))


def _tile(n, pref, mult):
    best = None
    d = mult
    while d <= min(n, pref):
        if n % d == 0:
            best = d
        d += mult
    return n if best is None else best


def _nbytes(shape, dtype):
    n = 1
    for s in shape:
        if s is not None:
            n *= s
    return n * jnp.dtype(dtype).itemsize


def _params(est_bytes=None, sem=None, **kw):
    if est_bytes is not None:
        kw["vmem_limit_bytes"] = int(min(V7X_VMEM_CAP, max(32 * 1024 * 1024, est_bytes * 5 // 4 + (4 << 20))))
    if sem is not None:
        kw["dimension_semantics"] = sem
    return pltpu.CompilerParams(**kw)


def _dot(a, b, contract):
    return lax.dot_general(a.astype(BF16), b.astype(BF16), (contract, ((), ())), preferred_element_type=F32)


def _mm(name, a, b, *, grid, a_spec, b_spec, o_spec, out_shape, contract, res=None, res_spec=None, carry=None):
    nk = grid[2]
    acc_shape = tuple(s for s in o_spec.block_shape if s is not None)
    n_in = 2 if res is None else 3
    x_in = 0 if carry is None else len(carry.operands)
    x_out = 0 if carry is None else carry.n_sems

    def body(*refs):
        a_ref, b_ref = refs[:2]
        r_ref = None if res is None else refs[2]
        x_ins = refs[n_in:n_in + x_in]
        o_ref = refs[n_in + x_in]
        x_outs = refs[n_in + x_in + 1:n_in + x_in + 1 + x_out]
        scr = refs[n_in + x_in + 1 + x_out:]
        if carry is not None:
            sems, scr = scr[-2:], scr[:-2]
            ids = [pl.program_id(d) for d in range(3)]
            first = functools.reduce(jnp.logical_and, [i == 0 for i in ids])
            last = functools.reduce(jnp.logical_and, [i == g - 1 for i, g in zip(ids, grid)])

            @pl.when(first)
            def _():
                carry.start(x_ins, x_outs, *sems)

        p = _dot(a_ref[...], b_ref[...], contract)

        def finish(acc):
            if r_ref is not None:
                acc = acc + r_ref[...]
            o_ref[...] = acc.astype(o_ref.dtype)

        if nk == 1:
            finish(p)
        else:
            acc_ref = scr[0]
            k = pl.program_id(2)

            @pl.when(k == 0)
            def _():
                acc_ref[...] = p

            @pl.when(jnp.logical_and(k > 0, k < nk - 1))
            def _():
                acc_ref[...] += p

            @pl.when(k == nk - 1)
            def _():
                finish(acc_ref[...] + p)

        if carry is not None:
            @pl.when(last)
            def _():
                carry.wait(x_ins, x_outs, *sems)

    in_specs = [a_spec, b_spec]
    args = [a, b]
    est = 2 * (_nbytes(a_spec.block_shape, a.dtype) + _nbytes(b_spec.block_shape, b.dtype)
               + _nbytes(o_spec.block_shape, out_shape.dtype)) + 2 * _nbytes(acc_shape, F32)
    if res is not None:
        in_specs.append(res_spec)
        args.append(res)
        est += 2 * _nbytes(res_spec.block_shape, res.dtype)
    scratch = [] if nk == 1 else [pltpu.VMEM(acc_shape, F32)]
    if carry is None:
        return pl.pallas_call(
            body, name=name, grid=grid, in_specs=in_specs, out_specs=o_spec, out_shape=out_shape,
            scratch_shapes=scratch,
            compiler_params=_params(est, ("parallel", "parallel", "arbitrary")),
        )(*args)
    outs = pl.pallas_call(
        body, name=name, grid=grid, in_specs=in_specs + [HBM_SPEC] * x_in,
        out_specs=[o_spec] + [HBM_SPEC] * x_out, out_shape=[out_shape] + carry.out_shapes,
        input_output_aliases={n_in + i: 1 + o for i, o in carry.aliases.items()},
        scratch_shapes=scratch + carry.scratch(),
        compiler_params=_params(est, ("arbitrary", "arbitrary", "arbitrary")),
    )(*args, *carry.operands)
    return outs[0], list(outs[1:])


def _mm_plain(name, a, b, contract, out_dtype, *, tm, tn, tk, res=None, carry=None):
    if contract == NN:
        (M, K), N = a.shape, b.shape[1]
    elif contract == NT:
        (M, K), N = a.shape, b.shape[0]
    else:
        (K, M), N = a.shape, b.shape[1]
    tm, tn, tk = _tile(M, tm, LANES), _tile(N, tn, LANES), _tile(K, tk, LANES)
    grid = (M // tm, N // tn, K // tk)
    if contract == TN:
        a_spec = pl.BlockSpec((tk, tm), lambda i, j, k: (k, i))
    else:
        a_spec = pl.BlockSpec((tm, tk), lambda i, j, k: (i, k))
    if contract == NT:
        b_spec = pl.BlockSpec((tn, tk), lambda i, j, k: (j, k))
    else:
        b_spec = pl.BlockSpec((tk, tn), lambda i, j, k: (k, j))
    o_spec = pl.BlockSpec((tm, tn), lambda i, j, k: (i, j))
    return _mm(name, a, b, grid=grid, a_spec=a_spec, b_spec=b_spec, o_spec=o_spec,
               out_shape=jax.ShapeDtypeStruct((M, N), out_dtype), contract=contract,
               res=res, res_spec=None if res is None else pl.BlockSpec((tm, tn), lambda i, j, k: (i, j)), carry=carry)


class _ColShards:
    def __init__(self, cols_per_chip, tn, paired):
        self.c = cols_per_chip
        self.tn = tn
        self.tps = cols_per_chip // tn
        self.ntiles = N_CHIPS * self.tps
        self.paired = paired

    def nat(self, t):
        if not self.paired:
            return t
        return (t % 2) * (self.ntiles // 2) + t // 2

    def chip(self, t):
        return self.nat(t) // self.tps

    def within(self, t):
        return self.nat(t) % self.tps


def _mm_cols_fwd(name, a, wg, sh, out_dtype, *, tm, carry=None):
    S, K = a.shape
    tm = _tile(S, tm, LANES)
    grid = (S // tm, sh.ntiles, 1)
    return _mm(name, a, wg, grid=grid,
               a_spec=pl.BlockSpec((tm, K), lambda i, j, k: (i, 0)),
               b_spec=pl.BlockSpec((None, K, sh.tn), lambda i, j, k: (sh.chip(j), 0, sh.within(j))),
               o_spec=pl.BlockSpec((tm, sh.tn), lambda i, j, k: (i, j)),
               out_shape=jax.ShapeDtypeStruct((S, sh.ntiles * sh.tn), out_dtype), contract=NN, carry=carry)


def _mm_cols_bwd_x(name, dy, wg, sh, out_dtype, *, tm, tn, carry=None):
    S = dy.shape[0]
    K = wg.shape[1]
    tm, tn = _tile(S, tm, LANES), _tile(K, tn, LANES)
    grid = (S // tm, K // tn, sh.ntiles)
    return _mm(name, dy, wg, grid=grid,
               a_spec=pl.BlockSpec((tm, sh.tn), lambda i, j, k: (i, k)),
               b_spec=pl.BlockSpec((None, tn, sh.tn), lambda i, j, k: (sh.chip(k), j, sh.within(k))),
               o_spec=pl.BlockSpec((tm, tn), lambda i, j, k: (i, j)),
               out_shape=jax.ShapeDtypeStruct((S, K), out_dtype), contract=NT, carry=carry)


def _mm_cols_bwd_w(name, a, dy, sh, *, tm, tk):
    S, K = a.shape
    tm, tk = _tile(K, tm, LANES), _tile(S, tk, LANES)
    grid = (K // tm, sh.ntiles, S // tk)
    return _mm(name, a, dy, grid=grid,
               a_spec=pl.BlockSpec((tk, tm), lambda i, j, k: (k, i)),
               b_spec=pl.BlockSpec((tk, sh.tn), lambda i, j, k: (k, j)),
               o_spec=pl.BlockSpec((None, tm, sh.tn), lambda i, j, k: (sh.chip(j), i, sh.within(j))),
               out_shape=jax.ShapeDtypeStruct((N_CHIPS, K, sh.c), F32), contract=TN)


def _row_spec(tr, width):
    return pl.BlockSpec((tr, width), lambda i: (i, 0))


def _full_spec(shape):
    nd = len(shape)
    return pl.BlockSpec(shape, lambda i: (0,) * nd)


def _rms_fwd(x, g):
    S, D = x.shape
    tr = _tile(S, 512, 16)

    def body(x_ref, g_ref, h_ref):
        xv = x_ref[...]
        r = lax.rsqrt(jnp.mean(xv * xv, axis=-1, keepdims=True) + EPS)
        h_ref[...] = (xv * r * g_ref[...]).astype(BF16)

    return pl.pallas_call(
        body, name="rms_fwd", grid=(S // tr,),
        in_specs=[_row_spec(tr, D), _full_spec((1, D))], out_specs=_row_spec(tr, D),
        out_shape=jax.ShapeDtypeStruct((S, D), BF16),
        compiler_params=_params(12 * tr * D, ("parallel",)),
    )(x, g)


def _rms_bwd(dh, x, g, dres):
    S, D = x.shape
    tr = _tile(S, 256, 16)

    def body(dh_ref, x_ref, g_ref, dres_ref, dx_ref, dxb_ref, dg_ref):
        xv = x_ref[...]
        dhv = dh_ref[...]
        r = lax.rsqrt(jnp.mean(xv * xv, axis=-1, keepdims=True) + EPS)
        gy = dhv * g_ref[...]
        dot = jnp.mean(xv * gy, axis=-1, keepdims=True)
        dx = dres_ref[...] + (r * gy - xv * (r * r * r * dot))
        dx_ref[...] = dx
        dxb_ref[...] = dx.astype(BF16)
        part = jnp.sum(dhv * (xv * r), axis=0, keepdims=True)

        @pl.when(pl.program_id(0) == 0)
        def _():
            dg_ref[...] = part

        @pl.when(pl.program_id(0) > 0)
        def _():
            dg_ref[...] += part

    return pl.pallas_call(
        body, name="rms_bwd", grid=(S // tr,),
        in_specs=[_row_spec(tr, D), _row_spec(tr, D), _full_spec((1, D)), _row_spec(tr, D)],
        out_specs=[_row_spec(tr, D), _row_spec(tr, D), _full_spec((1, D))],
        out_shape=[jax.ShapeDtypeStruct((S, D), F32), jax.ShapeDtypeStruct((S, D), BF16),
                   jax.ShapeDtypeStruct((1, D), F32)],
        compiler_params=_params(40 * tr * D, ("arbitrary",)),
    )(dh, x, g, dres)


def _out_norm_fwd(oa, ob, ga, gb):
    S, W = oa.shape
    tr = _tile(S, 512, 16)

    def body(oa_ref, ob_ref, ga_ref, gb_ref, o_ref):
        for src, g_ref, off in ((oa_ref, ga_ref, 0), (ob_ref, gb_ref, W)):
            v = src[...]
            r = lax.rsqrt(jnp.mean(v * v, axis=-1, keepdims=True) + EPS)
            o_ref[:, off:off + W] = (v * r * g_ref[...]).astype(BF16)

    return pl.pallas_call(
        body, name="out_norm_fwd", grid=(S // tr,),
        in_specs=[_row_spec(tr, W), _row_spec(tr, W), _full_spec((1, W)), _full_spec((1, W))],
        out_specs=_row_spec(tr, 2 * W), out_shape=jax.ShapeDtypeStruct((S, 2 * W), BF16),
        compiler_params=_params(24 * tr * W, ("parallel",)),
    )(oa, ob, ga, gb)


def _out_norm_bwd(do, oa, ob, ga, gb):
    S, W = oa.shape
    tr = _tile(S, 256, 16)

    def body(do_ref, oa_ref, ob_ref, ga_ref, gb_ref, doa_ref, dob_ref, dga_ref, dgb_ref):
        first = pl.program_id(0) == 0
        for src, g_ref, off, d_ref, dg_ref in ((oa_ref, ga_ref, 0, doa_ref, dga_ref),
                                               (ob_ref, gb_ref, W, dob_ref, dgb_ref)):
            v = src[...]
            dv = do_ref[:, off:off + W]
            r = lax.rsqrt(jnp.mean(v * v, axis=-1, keepdims=True) + EPS)
            gy = dv * g_ref[...]
            dot = jnp.mean(v * gy, axis=-1, keepdims=True)
            d_ref[...] = (r * gy - v * (r * r * r * dot)).astype(BF16)
            part = jnp.sum(dv * (v * r), axis=0, keepdims=True)

            @pl.when(first)
            def _():
                dg_ref[...] = part

            @pl.when(jnp.logical_not(first))
            def _():
                dg_ref[...] += part

    return pl.pallas_call(
        body, name="out_norm_bwd", grid=(S // tr,),
        in_specs=[_row_spec(tr, 2 * W), _row_spec(tr, W), _row_spec(tr, W), _full_spec((1, W)), _full_spec((1, W))],
        out_specs=[_row_spec(tr, W), _row_spec(tr, W), _full_spec((1, W)), _full_spec((1, W))],
        out_shape=[jax.ShapeDtypeStruct((S, W), BF16), jax.ShapeDtypeStruct((S, W), BF16),
                   jax.ShapeDtypeStruct((1, W), F32), jax.ShapeDtypeStruct((1, W), F32)],
        compiler_params=_params(48 * tr * W, ("arbitrary",)),
    )(do, oa, ob, ga, gb)


def _loss_grad(y, t):
    S, D = y.shape
    tr = _tile(S, 256, 16)

    def body(y_ref, t_ref, dy_ref, dyb_ref, ls_ref):
        e = y_ref[...] - t_ref[...]
        dy = e * (1.0 / D)
        dy_ref[...] = dy
        dyb_ref[...] = dy.astype(BF16)
        part = jnp.sum(e * e, axis=0, keepdims=True)

        @pl.when(pl.program_id(0) == 0)
        def _():
            ls_ref[...] = part

        @pl.when(pl.program_id(0) > 0)
        def _():
            ls_ref[...] += part

    return pl.pallas_call(
        body, name="loss_grad", grid=(S // tr,),
        in_specs=[_row_spec(tr, D), _row_spec(tr, D)],
        out_specs=[_row_spec(tr, D), _row_spec(tr, D), _full_spec((1, D))],
        out_shape=[jax.ShapeDtypeStruct((S, D), F32), jax.ShapeDtypeStruct((S, D), BF16),
                   jax.ShapeDtypeStruct((1, D), F32)],
        compiler_params=_params(32 * tr * D, ("arbitrary",)),
    )(y, t)


def _head_rms(x, g):
    r = lax.rsqrt(jnp.mean(x * x, axis=-1, keepdims=True) + EPS)
    return x * r * g


def _head_rms_bwd(x, dy, g):
    r = lax.rsqrt(jnp.mean(x * x, axis=-1, keepdims=True) + EPS)
    gy = dy * g
    dot = jnp.mean(x * gy, axis=-1, keepdims=True)
    return r * gy - x * (r * r * r * dot), dy * (x * r)


def _rope(y, cos2, sgn_sin):
    return y * cos2 + pltpu.roll(y, HEAD_DIM // 2, axis=1) * sgn_sin


def _rope_t(dy, cos2, sgn_sin):
    return dy * cos2 + pltpu.roll(dy * sgn_sin, HEAD_DIM // 2, axis=1)


def _qk_prep_fwd(proj, cos2, sgn_sin, gains, dims):
    S, P = proj.shape
    naw, waw, kvw = dims
    tr = _tile(S, 256, 16)
    hd = HEAD_DIM

    def body(p_ref, c_ref, s_ref, g_ref, qa_ref, ka_ref, va_ref, qb_ref, kb_ref, vb_ref):
        c2 = c_ref[...]
        ss = s_ref[...]
        off = 0
        for dst, width, gi, rot in ((qa_ref, naw, 0, False), (ka_ref, naw, 1, False), (va_ref, naw, None, False),
                                    (qb_ref, waw, 2, True), (kb_ref, kvw, 3, True), (vb_ref, kvw, None, False)):
            for h in range(width // hd):
                xh = p_ref[:, off + h * hd:off + (h + 1) * hd]
                if gi is not None:
                    xh = _head_rms(xh, g_ref[gi:gi + 1, :])
                    if rot:
                        xh = _rope(xh, c2, ss)
                dst[:, h * hd:(h + 1) * hd] = xh.astype(BF16)
            off += width

    widths = (naw, naw, naw, waw, kvw, kvw)
    return pl.pallas_call(
        body, name="qk_prep_fwd", grid=(S // tr,),
        in_specs=[_row_spec(tr, P), _row_spec(tr, hd), _row_spec(tr, hd), _full_spec((SUBLANES, hd))],
        out_specs=[_row_spec(tr, w) for w in widths],
        out_shape=[jax.ShapeDtypeStruct((S, w), BF16) for w in widths],
        compiler_params=_params(16 * tr * P, ("parallel",)),
    )(proj, cos2, sgn_sin, gains)


def _qk_prep_bwd(proj, dqa, dka, dva, dqb, dkb, dvb, cos2, sgn_sin, gains, dims):
    S, P = proj.shape
    naw, waw, kvw = dims
    tr = _tile(S, 128, 16)
    hd = HEAD_DIM

    def body(p_ref, dqa_ref, dka_ref, dva_ref, dqb_ref, dkb_ref, dvb_ref, c_ref, s_ref, g_ref, dp_ref, dg_ref):
        c2 = c_ref[...]
        ss = s_ref[...]
        off = 0
        dgs = [jnp.zeros((1, hd), F32) for _ in range(4)]
        for src, width, gi, rot in ((dqa_ref, naw, 0, False), (dka_ref, naw, 1, False), (dva_ref, naw, None, False),
                                    (dqb_ref, waw, 2, True), (dkb_ref, kvw, 3, True), (dvb_ref, kvw, None, False)):
            for h in range(width // hd):
                dy = src[:, h * hd:(h + 1) * hd].astype(F32)
                if gi is not None:
                    if rot:
                        dy = _rope_t(dy, c2, ss)
                    xh = p_ref[:, off + h * hd:off + (h + 1) * hd]
                    dy, dgt = _head_rms_bwd(xh, dy, g_ref[gi:gi + 1, :])
                    dgs[gi] = dgs[gi] + jnp.sum(dgt, axis=0, keepdims=True)
                dp_ref[:, off + h * hd:off + (h + 1) * hd] = dy.astype(BF16)
            off += width
        part = jnp.concatenate(dgs + [jnp.zeros((SUBLANES - 4, hd), F32)], axis=0)

        @pl.when(pl.program_id(0) == 0)
        def _():
            dg_ref[...] = part

        @pl.when(pl.program_id(0) > 0)
        def _():
            dg_ref[...] += part

    return pl.pallas_call(
        body, name="qk_prep_bwd", grid=(S // tr,),
        in_specs=[_row_spec(tr, P), _row_spec(tr, naw), _row_spec(tr, naw), _row_spec(tr, naw),
                  _row_spec(tr, waw), _row_spec(tr, kvw), _row_spec(tr, kvw),
                  _row_spec(tr, hd), _row_spec(tr, hd), _full_spec((SUBLANES, hd))],
        out_specs=[_row_spec(tr, P), _full_spec((SUBLANES, hd))],
        out_shape=[jax.ShapeDtypeStruct((S, P), BF16), jax.ShapeDtypeStruct((SUBLANES, hd), F32)],
        compiler_params=_params(24 * tr * P, ("arbitrary",)),
    )(proj, dqa, dka, dva, dqb, dkb, dvb, cos2, sgn_sin, gains)


NA_KEYS = NA_WIN_R * GRID_W
NA_ROWS_PER_STEP = 8


def _na_col_geometry():
    c = np.arange(GRID_W)
    col_start = np.clip(c - NA_WIN_C // 2, 0, GRID_W - NA_WIN_C)
    mask = (c[None, :] >= col_start[:, None]) & (c[None, :] < col_start[:, None] + NA_WIN_C)
    dc = np.clip(c[None, :] - c[:, None], -(NA_WIN_C - 1), NA_WIN_C - 1) + (NA_WIN_C - 1)
    onehot = (dc[:, :, None] == np.arange(2 * NA_WIN_C - 1)[None, None, :]) & mask[:, :, None]
    return mask, onehot


def _na_bias_table(rpb_l):
    H = rpb_l.shape[0]
    mask, onehot = _na_col_geometry()
    t = jnp.sum(jnp.where(onehot[None, None], rpb_l[:, :, None, None, :], 0.0), axis=-1)
    t = jnp.where(mask[None, None], t, NEG)
    per_delta = [jnp.transpose(t[:, d:d + NA_WIN_R], (0, 2, 1, 3)).reshape(H, GRID_W, NA_KEYS) for d in range(NA_WIN_R)]
    return jnp.stack(per_delta, axis=1)


def _na_bias_table_t(db):
    H = db.shape[0]
    _, onehot = _na_col_geometry()
    d5 = db.reshape(H, NA_WIN_R, GRID_W, NA_WIN_R, GRID_W)
    folded = jnp.einsum("hdqwk,qkc->hdwc", d5, onehot.astype(np.float32), precision=lax.Precision.HIGHEST)
    return sum(jnp.pad(folded[:, d], ((0, 0), (d, NA_WIN_R - 1 - d), (0, 0))) for d in range(NA_WIN_R))


def _na_row_geometry(r, rows):
    start = jnp.clip(r - NA_WIN_R // 2, 0, rows - NA_WIN_R)
    return start, start - r + (NA_WIN_R - 1)


def _na_probs(q, kw, bias):
    s = _dot(q, kw, NT) * ATTN_SCALE + bias
    m = jnp.max(s, axis=-1, keepdims=True)
    e = jnp.exp(s - m)
    return e / jnp.sum(e, axis=-1, keepdims=True)


def _na_fwd(qa, ka, va, btab):
    S, W = qa.shape
    H = W // HEAD_DIM
    rows = S // GRID_W
    assert rows >= NA_WIN_R
    rb = _tile(rows, NA_ROWS_PER_STEP, 1)
    tq = rb * GRID_W

    def body(q_ref, k_ref, v_ref, b_ref, o_ref):
        i = pl.program_id(1)

        def row(j, carry):
            start, delta = _na_row_geometry(i * rb + j, rows)
            tok = pl.ds(pl.multiple_of(j * GRID_W, GRID_W), GRID_W)
            win = pl.ds(pl.multiple_of(start * GRID_W, GRID_W), NA_KEYS)
            p = _na_probs(q_ref[tok, :], k_ref[win, :], b_ref[delta])
            o_ref[tok, :] = _dot(p, v_ref[win, :], NN)
            return carry

        lax.fori_loop(0, rb, row, 0, unroll=True)

    kv_spec = pl.BlockSpec((S, HEAD_DIM), lambda h, i: (0, h))
    return pl.pallas_call(
        body, name="na_fwd", grid=(H, rows // rb),
        in_specs=[pl.BlockSpec((tq, HEAD_DIM), lambda h, i: (i, h)), kv_spec, kv_spec,
                  pl.BlockSpec((None, NA_WIN_R, GRID_W, NA_KEYS), lambda h, i: (h, 0, 0, 0))],
        out_specs=pl.BlockSpec((tq, HEAD_DIM), lambda h, i: (i, h)),
        out_shape=jax.ShapeDtypeStruct((S, W), F32),
        compiler_params=_params(8 * S * HEAD_DIM + (8 << 20), ("parallel", "arbitrary")),
    )(qa, ka, va, btab)


def _na_bwd(qa, ka, va, btab, doa):
    S, W = qa.shape
    H = W // HEAD_DIM
    rows = S // GRID_W
    rb = _tile(rows, NA_ROWS_PER_STEP, 1)
    tq = rb * GRID_W

    def body(q_ref, k_ref, v_ref, b_ref, do_ref, dq_ref, dk_ref, dv_ref, db_ref):
        i = pl.program_id(1)

        @pl.when(i == 0)
        def _():
            dk_ref[...] = jnp.zeros_like(dk_ref)
            dv_ref[...] = jnp.zeros_like(dv_ref)
            db_ref[...] = jnp.zeros_like(db_ref)

        def row(j, carry):
            start, delta = _na_row_geometry(i * rb + j, rows)
            tok = pl.ds(pl.multiple_of(j * GRID_W, GRID_W), GRID_W)
            win = pl.ds(pl.multiple_of(start * GRID_W, GRID_W), NA_KEYS)
            q = q_ref[tok, :]
            kw = k_ref[win, :]
            vw = v_ref[win, :]
            do = do_ref[tok, :]
            p = _na_probs(q, kw, b_ref[delta])
            dp = _dot(do, vw, NT)
            ds = p * (dp - jnp.sum(p * dp, axis=-1, keepdims=True))
            db_ref[delta] += ds
            dss = (ds * ATTN_SCALE).astype(BF16)
            dq_ref[tok, :] = _dot(dss, kw, NN)
            dk_ref[win, :] += _dot(dss, q, TN)
            dv_ref[win, :] += _dot(p, do, TN)
            return carry

        lax.fori_loop(0, rb, row, 0, unroll=True)

    kv_spec = pl.BlockSpec((S, HEAD_DIM), lambda h, i: (0, h))
    q_spec = pl.BlockSpec((tq, HEAD_DIM), lambda h, i: (i, h))
    b_spec = pl.BlockSpec((None, NA_WIN_R, GRID_W, NA_KEYS), lambda h, i: (h, 0, 0, 0))
    return pl.pallas_call(
        body, name="na_bwd", grid=(H, rows // rb),
        in_specs=[q_spec, kv_spec, kv_spec, b_spec, q_spec],
        out_specs=[q_spec, kv_spec, kv_spec, b_spec],
        out_shape=[jax.ShapeDtypeStruct((S, W), F32), jax.ShapeDtypeStruct((S, W), F32),
                   jax.ShapeDtypeStruct((S, W), F32), jax.ShapeDtypeStruct(btab.shape, F32)],
        compiler_params=_params(24 * S * HEAD_DIM + (12 << 20), ("parallel", "arbitrary")),
    )(qa, ka, va, btab, doa)


WA_KEYS = 3 * WA_BLOCK
WA_BLOCKS_PER_STEP = 4


def _wa_scores(qs, kw, n, start, g):
    s = _dot(qs, kw, NT) * ATTN_SCALE
    qpos = n * WA_BLOCK + lax.broadcasted_iota(jnp.int32, (WA_BLOCK, WA_KEYS), 0)
    kpos = start + lax.broadcasted_iota(jnp.int32, (WA_BLOCK, WA_KEYS), 1)
    valid = jnp.abs(kpos - qpos) <= WA_WINDOW
    valid = jnp.concatenate([valid] * g, axis=0)
    return jnp.where(valid, s, NEG)


def _wa_probs(s, sink):
    m = jnp.maximum(jnp.max(s, axis=-1, keepdims=True), sink)
    e = jnp.exp(s - m)
    es = jnp.exp(sink - m)
    den = jnp.sum(e, axis=-1, keepdims=True) + es
    return e / den, es / den


def _wa_stack(ref, tok, g):
    return jnp.concatenate([ref[tok, t * HEAD_DIM:(t + 1) * HEAD_DIM] for t in range(g)], axis=0)


def _wa_fwd(qb, kb, vb, sink_col):
    S, W = qb.shape
    hkv = kb.shape[1] // HEAD_DIM
    g = W // HEAD_DIM // hkv
    nb = S // WA_BLOCK
    assert S >= WA_KEYS
    qb_step = _tile(nb, WA_BLOCKS_PER_STEP, 1)
    tq = qb_step * WA_BLOCK

    def body(q_ref, k_ref, v_ref, s_ref, o_ref):
        i = pl.program_id(1)

        def blk(j, carry):
            n = i * qb_step + j
            tok = pl.ds(pl.multiple_of(j * WA_BLOCK, WA_BLOCK), WA_BLOCK)
            start = pl.multiple_of(jnp.clip((n - 1) * WA_BLOCK, 0, S - WA_KEYS), WA_BLOCK)
            win = pl.ds(start, WA_KEYS)
            p, _ = _wa_probs(_wa_scores(_wa_stack(q_ref, tok, g), k_ref[win, :], n, start, g), s_ref[...])
            o = _dot(p, v_ref[win, :], NN)
            for t in range(g):
                o_ref[tok, t * HEAD_DIM:(t + 1) * HEAD_DIM] = o[t * WA_BLOCK:(t + 1) * WA_BLOCK]
            return carry

        lax.fori_loop(0, qb_step, blk, 0, unroll=True)

    kv_spec = pl.BlockSpec((S, HEAD_DIM), lambda h, i: (0, h))
    q_spec = pl.BlockSpec((tq, g * HEAD_DIM), lambda h, i: (i, h))
    return pl.pallas_call(
        body, name="wa_fwd", grid=(hkv, nb // qb_step),
        in_specs=[q_spec, kv_spec, kv_spec, pl.BlockSpec((None, g * WA_BLOCK, 1), lambda h, i: (h, 0, 0))],
        out_specs=q_spec, out_shape=jax.ShapeDtypeStruct((S, W), F32),
        compiler_params=_params(8 * S * HEAD_DIM + (12 << 20), ("parallel", "arbitrary")),
    )(qb, kb, vb, sink_col)


def _wa_bwd(qb, kb, vb, sink_col, dob):
    S, W = qb.shape
    KW = kb.shape[1]
    hkv = KW // HEAD_DIM
    g = W // HEAD_DIM // hkv
    nb = S // WA_BLOCK
    qb_step = _tile(nb, WA_BLOCKS_PER_STEP, 1)
    tq = qb_step * WA_BLOCK

    def body(q_ref, k_ref, v_ref, s_ref, do_ref, dq_ref, dk_ref, dv_ref, dsink_ref):
        i = pl.program_id(1)

        @pl.when(i == 0)
        def _():
            dk_ref[...] = jnp.zeros_like(dk_ref)
            dv_ref[...] = jnp.zeros_like(dv_ref)
            dsink_ref[...] = jnp.zeros_like(dsink_ref)

        def blk(j, carry):
            n = i * qb_step + j
            tok = pl.ds(pl.multiple_of(j * WA_BLOCK, WA_BLOCK), WA_BLOCK)
            start = pl.multiple_of(jnp.clip((n - 1) * WA_BLOCK, 0, S - WA_KEYS), WA_BLOCK)
            win = pl.ds(start, WA_KEYS)
            qs = _wa_stack(q_ref, tok, g)
            dos = _wa_stack(do_ref, tok, g)
            kw = k_ref[win, :]
            vw = v_ref[win, :]
            p, ps = _wa_probs(_wa_scores(qs, kw, n, start, g), s_ref[...])
            dp = _dot(dos, vw, NT)
            dsum = jnp.sum(p * dp, axis=-1, keepdims=True)
            ds = p * (dp - dsum)
            dsink_ref[...] -= ps * dsum
            dss = (ds * ATTN_SCALE).astype(BF16)
            dq = _dot(dss, kw, NN)
            for t in range(g):
                dq_ref[tok, t * HEAD_DIM:(t + 1) * HEAD_DIM] = dq[t * WA_BLOCK:(t + 1) * WA_BLOCK]
            dk_ref[win, :] += _dot(dss, qs, TN)
            dv_ref[win, :] += _dot(p, dos, TN)
            return carry

        lax.fori_loop(0, qb_step, blk, 0, unroll=True)

    kv_spec = pl.BlockSpec((S, HEAD_DIM), lambda h, i: (0, h))
    q_spec = pl.BlockSpec((tq, g * HEAD_DIM), lambda h, i: (i, h))
    s_spec = pl.BlockSpec((None, g * WA_BLOCK, 1), lambda h, i: (h, 0, 0))
    return pl.pallas_call(
        body, name="wa_bwd", grid=(hkv, nb // qb_step),
        in_specs=[q_spec, kv_spec, kv_spec, s_spec, q_spec],
        out_specs=[q_spec, kv_spec, kv_spec, s_spec],
        out_shape=[jax.ShapeDtypeStruct((S, W), F32), jax.ShapeDtypeStruct((S, KW), F32),
                   jax.ShapeDtypeStruct((S, KW), F32), jax.ShapeDtypeStruct(sink_col.shape, F32)],
        compiler_params=_params(24 * S * HEAD_DIM + (16 << 20), ("parallel", "arbitrary")),
    )(qb, kb, vb, sink_col, dob)


CONV_HALO = SUBLANES


def _conv_specs(tr, width, nblk, rows_inner):
    per = tr // CONV_HALO

    def spec(shape, row_block):
        if rows_inner:
            return pl.BlockSpec(shape, lambda j, i: (row_block(i), j))
        return pl.BlockSpec(shape, lambda i, j: (row_block(i), j))

    cur = spec((tr, width), lambda i: i)
    prev = spec((CONV_HALO, width), lambda i: jnp.maximum(i * per - 1, 0))
    nxt = spec((CONV_HALO, width), lambda i: jnp.minimum((i + 1) * per, nblk * per - 1))
    return prev, cur, nxt


def _extend(prev_ref, cur_ref, next_ref, i, nblk):
    p = jnp.where(i > 0, prev_ref[...].astype(F32), 0.0)
    n = jnp.where(i < nblk - 1, next_ref[...].astype(F32), 0.0)
    return jnp.concatenate([p, cur_ref[...].astype(F32), n], axis=0)


def _shift_down(x):
    return pltpu.roll(x, 1, axis=0)


def _shift_up(x):
    return pltpu.roll(x, x.shape[0] - 1, axis=0)


def _conv(ext, w_ref, b_ref):
    return _shift_down(ext) * w_ref[0:1, :] + ext * w_ref[1:2, :] + _shift_up(ext) * w_ref[2:3, :] + b_ref[...]


def _sigmoid(x):
    return 1.0 / (1.0 + jnp.exp(-x))


def _convgate_fwd(up_pre, cw, cb, tc):
    S, C2 = up_pre.shape
    tr = _tile(S, 256, 16)
    nblk = S // tr
    prev, cur, nxt = _conv_specs(tr, 2 * tc, nblk, False)

    def body(p_ref, c_ref, n_ref, w_ref, b_ref, a_ref):
        i = pl.program_id(0)
        u = _conv(_extend(p_ref, c_ref, n_ref, i, nblk), w_ref, b_ref)[CONV_HALO:CONV_HALO + tr]
        gate, up = u[:, :tc], u[:, tc:]
        a_ref[...] = (gate * _sigmoid(gate) * up).astype(BF16)

    return pl.pallas_call(
        body, name="convgate_fwd", grid=(nblk, C2 // (2 * tc)),
        in_specs=[prev, cur, nxt, pl.BlockSpec((3, 2 * tc), lambda i, j: (0, j)),
                  pl.BlockSpec((1, 2 * tc), lambda i, j: (0, j))],
        out_specs=pl.BlockSpec((tr, tc), lambda i, j: (i, j)),
        out_shape=jax.ShapeDtypeStruct((S, C2 // 2), BF16),
        compiler_params=_params(48 * tr * tc, ("parallel", "parallel")),
    )(up_pre, up_pre, up_pre, cw, cb)


def _convgate_bwd(up_pre, dact, cw, cb, tc):
    S, C2 = up_pre.shape
    tr = _tile(S, 128, 16)
    nblk = S // tr
    prev, cur, nxt = _conv_specs(tr, 2 * tc, nblk, True)
    dprev, dcur, dnxt = _conv_specs(tr, tc, nblk, True)
    ext_rows = tr + 2 * CONV_HALO

    def body(p_ref, c_ref, n_ref, dp_ref, dc_ref, dn_ref, w_ref, b_ref, dx_ref, dw_ref, db_ref):
        i = pl.program_id(1)
        ext = _extend(p_ref, c_ref, n_ref, i, nblk)
        da = _extend(dp_ref, dc_ref, dn_ref, i, nblk)
        u = _conv(ext, w_ref, b_ref)
        gate, up = u[:, :tc], u[:, tc:]
        sg = _sigmoid(gate)
        silu = gate * sg
        du = jnp.concatenate([da * up * (sg + silu * (1.0 - sg)), da * silu], axis=1)
        dx = _shift_up(du) * w_ref[0:1, :] + du * w_ref[1:2, :] + _shift_down(du) * w_ref[2:3, :]
        dx_ref[...] = dx[CONV_HALO:CONV_HALO + tr].astype(BF16)
        row = lax.broadcasted_iota(jnp.int32, (ext_rows, 1), 0)
        duc = jnp.where(jnp.logical_and(row >= CONV_HALO, row < CONV_HALO + tr), du, 0.0)
        dw = jnp.concatenate([jnp.sum(duc * _shift_down(ext), axis=0, keepdims=True),
                              jnp.sum(duc * ext, axis=0, keepdims=True),
                              jnp.sum(duc * _shift_up(ext), axis=0, keepdims=True)], axis=0)
        db = jnp.sum(duc, axis=0, keepdims=True)

        @pl.when(i == 0)
        def _():
            dw_ref[...] = dw
            db_ref[...] = db

        @pl.when(i > 0)
        def _():
            dw_ref[...] += dw
            db_ref[...] += db

    return pl.pallas_call(
        body, name="convgate_bwd", grid=(C2 // (2 * tc), nblk),
        in_specs=[prev, cur, nxt, dprev, dcur, dnxt,
                  pl.BlockSpec((3, 2 * tc), lambda j, i: (0, j)), pl.BlockSpec((1, 2 * tc), lambda j, i: (0, j))],
        out_specs=[pl.BlockSpec((tr, 2 * tc), lambda j, i: (i, j)),
                   pl.BlockSpec((3, 2 * tc), lambda j, i: (0, j)), pl.BlockSpec((1, 2 * tc), lambda j, i: (0, j))],
        out_shape=[jax.ShapeDtypeStruct((S, C2), BF16), jax.ShapeDtypeStruct((3, C2), F32),
                   jax.ShapeDtypeStruct((1, C2), F32)],
        compiler_params=_params(160 * tr * tc, ("parallel", "arbitrary")),
    )(up_pre, up_pre, up_pre, dact, dact, dact, cw, cb)


def _rope_tables(positions):
    inv = ROPE_THETA ** (-jnp.arange(0, HEAD_DIM, 2, dtype=F32) / HEAD_DIM)
    ang = positions.astype(F32)[:, None] * inv[None, :]
    cos, sin = jnp.cos(ang), jnp.sin(ang)
    return jnp.concatenate([cos, cos], axis=1), jnp.concatenate([-sin, sin], axis=1)


def _sink_col(sink_l, hkv):
    g = sink_l.shape[0] // hkv
    return jnp.broadcast_to(sink_l.reshape(hkv, g, 1), (hkv, g, WA_BLOCK)).reshape(hkv, g * WA_BLOCK, 1)


def _sink_col_t(dcol):
    hkv, rows, _ = dcol.shape
    return jnp.sum(dcol.reshape(hkv, rows // WA_BLOCK, WA_BLOCK), axis=-1).reshape(-1)


def _to_paired(a, sh):
    return jnp.concatenate([a[:, sh.nat(t) * sh.tn:(sh.nat(t) + 1) * sh.tn] for t in range(sh.ntiles)], axis=1)


def _from_paired(a, sh):
    pos = {sh.nat(t): t for t in range(sh.ntiles)}
    return jnp.concatenate([a[:, pos[n] * sh.tn:(pos[n] + 1) * sh.tn] for n in range(sh.ntiles)], axis=1)


def _pack(arrs):
    flat = jnp.concatenate([a.reshape(-1).astype(F32) for a in arrs])
    unit = SUBLANES * LANES
    total = -(-flat.shape[0] // unit) * unit
    return jnp.pad(flat, (0, total - flat.shape[0])).reshape(-1, LANES)


def _unpack(buf, shapes):
    flat = buf.reshape(-1)
    out, off = [], 0
    for s in shapes:
        n = math.prod(s)
        out.append(flat[off:off + n].reshape(s))
        off += n
    return out


HBM_SPEC = pl.BlockSpec(memory_space=pltpu.HBM)
DMA_CHUNK_BYTES = 512 * 1024


def _row_chunks(rows, row_bytes, align):
    want = max(1, rows * row_bytes // DMA_CHUNK_BYTES)
    count = max(k for k in range(1, rows + 1) if rows % k == 0 and (rows // k) % align == 0 and (k <= want or k == 1))
    size = rows // count
    return [(j * size, size) for j in range(count)]


def _mesh_pos():
    return lax.axis_index("x"), lax.axis_index("y"), lax.axis_index("c")


def _other_chips(x, y):
    return [(1 - x, y), (x, 1 - y), (1 - x, 1 - y)]


def _remote(src, dst, send_sems, recv_sems, k, dev):
    return pltpu.make_async_remote_copy(src_ref=src, dst_ref=dst, send_sem=send_sems.at[k], recv_sem=recv_sems.at[k],
                                        device_id=dev, device_id_type=MESH)


class _Exchange:
    def __init__(self, operands, out_shapes, aliases, start, wait):
        self.operands, self.out_shapes, self.aliases = list(operands), list(out_shapes), dict(aliases)
        self.start, self.wait = start, wait
        self.n_sems = len(out_shapes)

    def scratch(self):
        return [pltpu.SemaphoreType.DMA((self.n_sems,)), pltpu.SemaphoreType.DMA((self.n_sems,))]


def _exchange_alone(name, ex):
    n_in = len(ex.operands)

    def body(*refs):
        ins, outs = refs[:n_in], refs[n_in:n_in + ex.n_sems]
        send_sems, recv_sems = refs[n_in + ex.n_sems:]
        ex.start(ins, outs, send_sems, recv_sems)
        ex.wait(ins, outs, send_sems, recv_sems)

    return pl.pallas_call(
        body, name=name, in_specs=[HBM_SPEC] * n_in, out_specs=[HBM_SPEC] * ex.n_sems, out_shape=ex.out_shapes,
        input_output_aliases=ex.aliases, scratch_shapes=ex.scratch(),
    )(*ex.operands)


def _ag_chip_exchange(gathered, small):
    n = len(gathered)
    arrays = list(gathered) + [small]

    def start(ins, outs, send_sems, recv_sems):
        x, y, c = _mesh_pos()
        q_me = 2 * x + y
        chips = _other_chips(x, y)
        for t in range(n):
            h = gathered[t].shape[1] // 2
            for s0, sz in _row_chunks(h, gathered[t].shape[2] * 2, 16):
                blk = outs[t].at[q_me, pl.ds(c * h + s0, sz)]
                for chip in chips:
                    _remote(blk, blk, send_sems, recv_sems, t, (*chip, c)).start()
        for chip in chips:
            _remote(outs[n].at[q_me], outs[n].at[q_me], send_sems, recv_sems, n, (*chip, c)).start()

    def wait(ins, outs, send_sems, recv_sems):
        x, y, c = _mesh_pos()
        for t in range(n):
            three = outs[t].at[pl.ds(0, 3), pl.ds(0, gathered[t].shape[1] // 2)]
            _remote(three, three, send_sems, recv_sems, t, (x, y, 1 - c)).wait()
        three = outs[n].at[pl.ds(0, 3)]
        _remote(three, three, send_sems, recv_sems, n, (x, y, 1 - c)).wait()

    return _Exchange(arrays, [jax.ShapeDtypeStruct(a.shape, a.dtype) for a in arrays],
                     {i: i for i in range(n + 1)}, start, wait)


def _ag_sibling_pass(gathered):
    n = len(gathered)

    def start(ins, outs, send_sems, recv_sems):
        x, y, c = _mesh_pos()
        for t in range(n):
            h = gathered[t].shape[1] // 2
            for s0, sz in _row_chunks(h, gathered[t].shape[2] * 2, 16):
                for cx, cy in _other_chips(x, y):
                    blk = outs[t].at[2 * cx + cy, pl.ds(c * h + s0, sz)]
                    _remote(blk, blk, send_sems, recv_sems, t, (x, y, 1 - c)).start()

    def wait(ins, outs, send_sems, recv_sems):
        x, y, c = _mesh_pos()
        for t in range(n):
            three = outs[t].at[pl.ds(0, 3), pl.ds(0, gathered[t].shape[1] // 2)]
            _remote(three, three, send_sems, recv_sems, t, (x, y, 1 - c)).wait()

    return _Exchange(gathered, [jax.ShapeDtypeStruct(a.shape, a.dtype) for a in gathered],
                     {i: i for i in range(n)}, start, wait)


def _rs_sibling_exchange(grads):
    n = len(grads)
    halves = [g.shape[1] // 2 for g in grads]

    def start(ins, outs, send_sems, recv_sems):
        x, y, c = _mesh_pos()
        for t in range(n):
            for q in range(N_CHIPS):
                for s0, sz in _row_chunks(halves[t], grads[t].shape[2] * 4, SUBLANES):
                    _remote(ins[t].at[q, pl.ds((1 - c) * halves[t] + s0, sz)], outs[t].at[q, pl.ds(s0, sz)],
                            send_sems, recv_sems, t, (x, y, 1 - c)).start()

    def wait(ins, outs, send_sems, recv_sems):
        x, y, c = _mesh_pos()
        for t in range(n):
            _remote(outs[t], outs[t], send_sems, recv_sems, t, (x, y, 1 - c)).wait()

    return _Exchange(grads, [jax.ShapeDtypeStruct((N_CHIPS, h, g.shape[2]), F32) for g, h in zip(grads, halves)],
                     {}, start, wait)


def _rs_chip_exchange(pbs):
    n = len(pbs)

    def start(ins, outs, send_sems, recv_sems):
        x, y, c = _mesh_pos()
        for t in range(n):
            for s0, sz in _row_chunks(pbs[t].shape[1], pbs[t].shape[2] * 2, 16):
                for k, (cx, cy) in enumerate(_other_chips(x, y)):
                    _remote(ins[t].at[2 * cx + cy, pl.ds(s0, sz)], outs[t].at[k, pl.ds(s0, sz)],
                            send_sems, recv_sems, t, (cx, cy, c)).start()

    def wait(ins, outs, send_sems, recv_sems):
        x, y, c = _mesh_pos()
        for t in range(n):
            _remote(outs[t], outs[t], send_sems, recv_sems, t, (x, y, 1 - c)).wait()

    return _Exchange(pbs, [jax.ShapeDtypeStruct((3,) + p.shape[1:], BF16) for p in pbs], {}, start, wait)


def _rs_sibling_share(gs):
    n = len(gs)

    def start(ins, outs, send_sems, recv_sems):
        x, y, c = _mesh_pos()
        for t in range(n):
            h = gs[t].shape[0] // 2
            for s0, sz in _row_chunks(h, gs[t].shape[1] * 4, SUBLANES):
                rows = outs[t].at[pl.ds(c * h + s0, sz)]
                _remote(rows, rows, send_sems, recv_sems, t, (x, y, 1 - c)).start()

    def wait(ins, outs, send_sems, recv_sems):
        x, y, c = _mesh_pos()
        for t in range(n):
            half = outs[t].at[pl.ds(0, gs[t].shape[0] // 2)]
            _remote(half, half, send_sems, recv_sems, t, (x, y, 1 - c)).wait()

    return _Exchange(gs, [jax.ShapeDtypeStruct(g.shape, F32) for g in gs], {i: i for i in range(n)}, start, wait)


def _gather_small(buf):
    def body(in_ref, out_ref, send_sems, recv_sems, loc_sem):
        x, y, c = _mesh_pos()
        me = 4 * x + 2 * y + c
        local = pltpu.make_async_copy(in_ref, out_ref.at[me], loc_sem)
        local.start()
        sends, peers = [], []
        for k in range(1, N_DEV):
            bx, by, bc = (k >> 2) & 1, (k >> 1) & 1, k & 1
            peer = (x + bx - 2 * x * bx, y + by - 2 * y * by, c + bc - 2 * c * bc)
            peers.append(peer)
            sends.append(_remote(in_ref, out_ref.at[me], send_sems, recv_sems, k - 1, peer))
        for cp in sends:
            cp.start()
        for k, (px, py, pc) in enumerate(peers):
            slot = out_ref.at[4 * px + 2 * py + pc]
            _remote(slot, slot, send_sems, recv_sems, k, (px, py, pc)).wait_recv()
        for cp in sends:
            cp.wait_send()
        local.wait()

    return pl.pallas_call(
        body, name="gather_small", in_specs=[HBM_SPEC], out_specs=HBM_SPEC,
        out_shape=jax.ShapeDtypeStruct((N_DEV,) + buf.shape, F32),
        scratch_shapes=[pltpu.SemaphoreType.DMA((N_DEV - 1,)), pltpu.SemaphoreType.DMA((N_DEV - 1,)),
                        pltpu.SemaphoreType.DMA(())],
    )(buf)


def _ew_rows(rows, cols, bytes_per_elem):
    budget = 20 * 1024 * 1024
    return _tile(rows, max(16, budget // (2 * bytes_per_elem * cols) // 16 * 16), 16)


def _sum_pair(grad, recv):
    _, h, cols = recv.shape
    tr = _ew_rows(h, cols, 14)
    nb = h // tr
    half = pl.BlockSpec((None, tr, cols), lambda q, i: (q, i, 0))

    def body(a_ref, b_ref, f_ref, h_ref):
        s = a_ref[...] + b_ref[...]
        f_ref[...] = s
        h_ref[...] = s.astype(BF16)

    return pl.pallas_call(
        body, name="rs_sum_pair", grid=(N_CHIPS, nb),
        in_specs=[pl.BlockSpec((None, tr, cols), lambda q, i: (q, lax.axis_index("c") * nb + i, 0)), half],
        out_specs=[half, half],
        out_shape=[jax.ShapeDtypeStruct(recv.shape, F32), jax.ShapeDtypeStruct(recv.shape, BF16)],
        compiler_params=_params(28 * tr * cols, ("parallel", "parallel")),
    )(grad, recv)


def _sum_four(pf, recv):
    _, h, cols = pf.shape
    tr = _ew_rows(h, cols, 14)
    nb = h // tr

    def body(a_ref, r_ref, o_ref):
        o_ref[...] = ((a_ref[...] + r_ref[0].astype(F32)) + r_ref[1].astype(F32)) + r_ref[2].astype(F32)

    return pl.pallas_call(
        body, name="rs_sum_four", grid=(nb,),
        in_specs=[pl.BlockSpec((None, tr, cols), lambda i: (2 * lax.axis_index("x") + lax.axis_index("y"), i, 0)),
                  pl.BlockSpec((3, tr, cols), lambda i: (0, i, 0))],
        out_specs=pl.BlockSpec((tr, cols), lambda i: (lax.axis_index("c") * nb + i, 0)),
        out_shape=jax.ShapeDtypeStruct((2 * h, cols), F32),
        compiler_params=_params(28 * tr * cols, ("parallel",)),
    )(pf, recv)


def _sum_eight(gathered):
    _, rows, cols = gathered.shape
    tr = _tile(rows, 512, SUBLANES)

    def body(g_ref, o_ref):
        s = g_ref[0]
        for d in range(1, N_DEV):
            s = s + g_ref[d]
        o_ref[...] = s

    return pl.pallas_call(
        body, name="sum_eight", grid=(rows // tr,),
        in_specs=[pl.BlockSpec((N_DEV, tr, cols), lambda i: (0, i, 0))],
        out_specs=_row_spec(tr, cols), out_shape=jax.ShapeDtypeStruct((rows, cols), F32),
        compiler_params=_params(None, ("parallel",)),
    )(gathered)


def _cast_layer(w, l):
    _, rows, cols = w.shape
    tr = _ew_rows(rows, cols, 6)

    def body(w_ref, o_ref):
        o_ref[...] = w_ref[...].astype(BF16)

    return pl.pallas_call(
        body, name="cast_bf16", grid=(rows // tr,),
        in_specs=[pl.BlockSpec((None, tr, cols), lambda i: (l, i, 0))],
        out_specs=pl.BlockSpec((None, tr, cols), lambda i: (2 * lax.axis_index("x") + lax.axis_index("y"), i, 0)),
        out_shape=jax.ShapeDtypeStruct((N_CHIPS, rows, cols), BF16),
        compiler_params=_params(12 * tr * cols, ("parallel",)),
    )(w)


def _adamw(g, w, m, v, l, prev):
    L, rows, cols = w.shape
    tr = _ew_rows(rows, cols, 32)
    layer = pl.BlockSpec((None, tr, cols), lambda i: (l, i, 0))

    def body(g_ref, w_ref, m_ref, v_ref, *rest):
        og_ref, od_ref, om_ref, ov_ref = rest[-4:]
        gv = g_ref[...]
        mn = ADAM_B1 * m_ref[...] + (1.0 - ADAM_B1) * gv
        vn = ADAM_B2 * v_ref[...] + (1.0 - ADAM_B2) * (gv * gv)
        m_hat = mn / ADAM_C1
        v_hat = vn / ADAM_C2
        od_ref[...] = -ADAM_LR * (m_hat / (jnp.sqrt(v_hat) + ADAM_EPS) + ADAM_WD * w_ref[...])
        og_ref[...] = gv
        om_ref[...] = mn
        ov_ref[...] = vn

    in_specs = [_row_spec(tr, cols), layer, layer, layer]
    args = [g, w, m, v]
    aliases = {}
    if prev is not None:
        in_specs += [HBM_SPEC] * 4
        args += list(prev)
        aliases = {4 + i: i for i in range(4)}
    return pl.pallas_call(
        body, name="adamw", grid=(rows // tr,), in_specs=in_specs, out_specs=[layer] * 4,
        out_shape=[jax.ShapeDtypeStruct((L, rows, cols), F32)] * 4, input_output_aliases=aliases,
        compiler_params=_params(64 * tr * cols, ("parallel",)),
    )(*args)


WEIGHT_ORDER = ("ln1_g", "w_in", "qn_a", "kn_a", "rpb", "qn_b", "kn_b", "sink", "on_a", "on_b", "w_out", "ln2_g",
                "w_up", "conv_w", "conv_b", "w_down")
BIG = ("w_in", "w_out", "w_up", "w_down")
SMALL = tuple(n for n in WEIGHT_ORDER if n not in BIG)


def _reduce_scatter_alone(grads):
    recvs = _exchange_alone("rs_sibling_exchange", _rs_sibling_exchange(grads))
    pairs = [_sum_pair(g, r) for g, r in zip(grads, recvs)]
    gots = _exchange_alone("rs_chip_exchange", _rs_chip_exchange([p[1] for p in pairs]))
    return _exchange_alone("rs_sibling_share", _rs_sibling_share([_sum_four(p[0], r) for p, r in zip(pairs, gots)]))


def _train_step(x, positions, target, w, m, v):
    S, D = x.shape
    L = w["w_in"].shape[0]
    naw = w["rpb"].shape[1] * HEAD_DIM
    waw = w["sink"].shape[1] * HEAD_DIM
    cin = w["w_in"].shape[2]
    kvw = (N_CHIPS * cin - 3 * naw - waw) // 2
    hkv = kvw // HEAD_DIM
    dims = (naw, waw, kvw)
    assert naw == waw, "the two head groups are normalised by one kernel and must be equally wide"
    cup = w["w_up"].shape[2]
    c2 = N_CHIPS * cup
    dff = c2 // 2
    mix = naw + waw
    sh_in = _ColShards(cin, _tile(cin, 1152, LANES), False)
    sh_up = _ColShards(cup, cup // 2 if (cup // 2) % LANES == 0 else cup, True)
    tc = sh_up.tn
    cos2, sgn_sin = _rope_tables(positions)
    q_me = 2 * lax.axis_index("x") + lax.axis_index("y")
    row = lambda a: a[None]

    def own_slots(l):
        small = lax.dynamic_update_slice_in_dim(jnp.zeros((N_CHIPS,) + w["conv_w"].shape[1:], F32),
                                                w["conv_w"][l][None], q_me, axis=0)
        return [_cast_layer(w[n], l) for n in BIG], small

    bufs, small = own_slots(0)
    *bufs, small = _exchange_alone("ag_chip_exchange", _ag_chip_exchange(bufs, small))
    gathered = (_exchange_alone("ag_sibling_pass", _ag_sibling_pass(bufs)), small)
    saved = []
    for l in range(L):
        (g_in, g_out, g_up, g_dn), g_cw = gathered
        g_out = g_out.reshape(mix, D)
        g_dn = g_dn.reshape(dff, D)
        cw_p = _to_paired(jnp.transpose(g_cw, (1, 0, 2)).reshape(3, c2), sh_up)
        cb_p = _to_paired(row(w["conv_b"][l]), sh_up)
        gains = jnp.concatenate([row(w["qn_a"][l]), row(w["kn_a"][l]), row(w["qn_b"][l]), row(w["kn_b"][l]),
                                 jnp.zeros((SUBLANES - 4, HEAD_DIM), F32)], axis=0)
        btab = _na_bias_table(w["rpb"][l])
        sink_col = _sink_col(w["sink"][l], hkv)

        h = _rms_fwd(x, row(w["ln1_g"][l]))
        proj = _mm_cols_fwd("mm_proj", h, g_in, sh_in, F32, tm=1024)
        qa, ka, va, qb, kb, vb = _qk_prep_fwd(proj, cos2, sgn_sin, gains, dims)
        oa = _na_fwd(qa, ka, va, btab)
        ob = _wa_fwd(qb, kb, vb, sink_col)
        o = _out_norm_fwd(oa, ob, row(w["on_a"][l]), row(w["on_b"][l]))
        x1 = _mm_plain("mm_attn_out", o, g_out, NN, F32, tm=1024, tn=1024, tk=2048, res=x)
        h2 = _rms_fwd(x1, row(w["ln2_g"][l]))
        if l + 1 < L:
            bufs, small = own_slots(l + 1)
            up_pre, (*bufs, small) = _mm_cols_fwd("mm_up_ag", h2, g_up, sh_up, F32, tm=1024,
                                                  carry=_ag_chip_exchange(bufs, small))
            act = _convgate_fwd(up_pre, cw_p, cb_p, tc)
            x2, bufs = _mm_plain("mm_down_ag", act, g_dn, NN, F32, tm=1024, tn=1024, tk=1408, res=x1,
                                 carry=_ag_sibling_pass(bufs))
            gathered = (bufs, small)
        else:
            up_pre = _mm_cols_fwd("mm_up", h2, g_up, sh_up, F32, tm=1024)
            act = _convgate_fwd(up_pre, cw_p, cb_p, tc)
            x2 = _mm_plain("mm_down", act, g_dn, NN, F32, tm=1024, tn=1024, tk=1408, res=x1)
        saved.append(dict(g_in=g_in, g_out=g_out, g_up=g_up, g_dn=g_dn, cw_p=cw_p, cb_p=cb_p, gains=gains, btab=btab,
                          sink_col=sink_col, x=x, h=h, proj=proj, qa=qa, ka=ka, va=va, qb=qb, kb=kb, vb=vb, oa=oa,
                          ob=ob, o=o, x1=x1, h2=h2, up_pre=up_pre, act=act))
        x = x2

    dx, dxb, loss_cols = _loss_grad(x, target)

    small_grads = [None] * L
    stacked = {n: None for n in BIG}

    def update(layer, shards):
        for n, g in zip(BIG, shards):
            stacked[n] = _adamw(g, w[n], m[n], v[n], layer, stacked[n])

    above = None
    for l in reversed(range(L)):
        s = saved[l]
        if above is None:
            dact = _mm_plain("mm_down_dx", dxb, s["g_dn"], NT, F32, tm=1024, tn=1408, tk=2048)
        else:
            dact, recvs = _mm_plain("mm_down_dx_rs", dxb, s["g_dn"], NT, F32, tm=1024, tn=1408, tk=2048,
                                    carry=_rs_sibling_exchange(above[1]))
            pairs = [_sum_pair(g, r) for g, r in zip(above[1], recvs)]
        d_dn = _mm_plain("mm_down_dw", s["act"], dxb, TN, F32, tm=1408, tn=1024, tk=2048)
        dup_pre, dcw_p, dcb_p = _convgate_bwd(s["up_pre"], dact, s["cw_p"], s["cb_p"], tc)
        if above is None:
            dh2 = _mm_cols_bwd_x("mm_up_dx", dup_pre, s["g_up"], sh_up, F32, tm=1024, tn=2048)
        else:
            dh2, gots = _mm_cols_bwd_x("mm_up_dx_rs", dup_pre, s["g_up"], sh_up, F32, tm=1024, tn=2048,
                                       carry=_rs_chip_exchange([p[1] for p in pairs]))
            halves = [_sum_four(p[0], r) for p, r in zip(pairs, gots)]
        d_up = _mm_cols_bwd_w("mm_up_dw", s["h2"], dup_pre, sh_up, tm=1024, tk=2048)
        dx1, dx1b, d_ln2 = _rms_bwd(dh2, s["x1"], row(w["ln2_g"][l]), dx)
        if above is None:
            do = _mm_plain("mm_attn_out_dx", dx1b, s["g_out"], NT, F32, tm=1024, tn=1024, tk=2048)
        else:
            do, shards = _mm_plain("mm_attn_out_dx_rs", dx1b, s["g_out"], NT, F32, tm=1024, tn=1024, tk=2048,
                                   carry=_rs_sibling_share(halves))
            update(above[0], shards)
        d_out = _mm_plain("mm_attn_out_dw", s["o"], dx1b, TN, F32, tm=1024, tn=1024, tk=2048)
        doa, dob, d_on_a, d_on_b = _out_norm_bwd(do, s["oa"], s["ob"], row(w["on_a"][l]), row(w["on_b"][l]))
        dqa, dka, dva, dbtab = _na_bwd(s["qa"], s["ka"], s["va"], s["btab"], doa)
        dqb, dkb, dvb, dsink_col = _wa_bwd(s["qb"], s["kb"], s["vb"], s["sink_col"], dob)
        dproj, dgains = _qk_prep_bwd(s["proj"], dqa, dka, dva, dqb, dkb, dvb, cos2, sgn_sin, s["gains"], dims)
        dh = _mm_cols_bwd_x("mm_proj_dx", dproj, s["g_in"], sh_in, F32, tm=1024, tn=2048)
        d_in = _mm_cols_bwd_w("mm_proj_dw", s["h"], dproj, sh_in, tm=1024, tk=2048)
        dx, dxb, d_ln1 = _rms_bwd(dh, s["x"], row(w["ln1_g"][l]), dx1)

        small_grads[l] = dict(
            ln1_g=d_ln1[0], qn_a=dgains[0], kn_a=dgains[1], rpb=_na_bias_table_t(dbtab), qn_b=dgains[2],
            kn_b=dgains[3], sink=_sink_col_t(dsink_col), on_a=d_on_a[0], on_b=d_on_b[0], ln2_g=d_ln2[0],
            conv_w=_from_paired(dcw_p, sh_up), conv_b=_from_paired(dcb_p, sh_up)[0])
        above = (l, [d_in, d_out.reshape(N_CHIPS, mix // N_CHIPS, D), d_up, d_dn.reshape(N_CHIPS, dff // N_CHIPS, D)])
    update(above[0], _reduce_scatter_alone(above[1]))
    grad_x = dx[None]

    full_shapes = {n: (w[n].shape[1:] if n != "conv_w" else (3, c2)) for n in SMALL}
    packed = _pack([loss_cols] + [small_grads[l][n] for l in range(L) for n in SMALL])
    total = _sum_eight(_gather_small(packed))
    parts = _unpack(total, [(D,)] + [full_shapes[n] for _ in range(L) for n in SMALL])
    loss = 0.5 * jnp.sum(parts[0]) / D
    per_layer = [dict(zip(SMALL, parts[1 + l * len(SMALL):1 + (l + 1) * len(SMALL)])) for l in range(L)]
    for l in range(L):
        per_layer[l]["conv_w"] = lax.dynamic_slice_in_dim(per_layer[l]["conv_w"], q_me * cup, cup, axis=1)
    small_g = {n: jnp.stack([per_layer[l][n] for l in range(L)]) for n in SMALL}
    outs = _adamw(_pack([small_g[n] for n in SMALL]), _pack([w[n] for n in SMALL])[None],
                  _pack([m[n] for n in SMALL])[None], _pack([v[n] for n in SMALL])[None], 0, None)
    small_out = [dict(zip(SMALL, _unpack(o[0], [w[n].shape for n in SMALL]))) for o in outs]

    result = [loss, grad_x]
    for i in range(4):
        result += [stacked[n][i] if n in BIG else small_out[i][n] for n in WEIGHT_ORDER]
    return tuple(result)


def kernel(x, positions, ln1_g, w_in, qn_a, kn_a, rpb, qn_b, kn_b, sink, on_a, on_b, w_out, ln2_g, w_up, conv_w, conv_b, w_down, loss_target, m_ln1_g, m_w_in, m_qn_a, m_kn_a, m_rpb, m_qn_b, m_kn_b, m_sink, m_on_a, m_on_b, m_w_out, m_ln2_g, m_w_up, m_conv_w, m_conv_b, m_w_down, v_ln1_g, v_w_in, v_qn_a, v_kn_a, v_rpb, v_qn_b, v_kn_b, v_sink, v_on_a, v_on_b, v_w_out, v_ln2_g, v_w_up, v_conv_w, v_conv_b, v_w_down):
    given = dict(locals())
    w = {n: given[n] for n in WEIGHT_ORDER}
    m = {n: given["m_" + n] for n in WEIGHT_ORDER}
    v = {n: given["v_" + n] for n in WEIGHT_ORDER}
    return _train_step(x[0], positions, loss_target[0], w, m, v)
```

```python
import functools
import math

import jax
import jax.numpy as jnp
import numpy as np
from jax import lax
from jax.experimental import pallas as pl
from jax.experimental.pallas import tpu as pltpu

F32 = jnp.float32
BF16 = jnp.bfloat16

HEAD_DIM = 128
GRID_W = 64
NA_WIN_R = 8
NA_WIN_C = 16
WA_WINDOW = 128
WA_BLOCK = 128
ROPE_THETA = 10000.0
EPS = 1e-6
NEG = -1e30
ATTN_SCALE = 1.0 / math.sqrt(HEAD_DIM)

ADAM_LR = 0.001
ADAM_B1 = 0.9
ADAM_B2 = 0.999
ADAM_EPS = 1e-08
ADAM_WD = 0.01
ADAM_STEP = 10
ADAM_C1 = 1.0 - ADAM_B1 ** ADAM_STEP
ADAM_C2 = 1.0 - ADAM_B2 ** ADAM_STEP

V7X_VMEM_BYTES = 64 * 1024 * 1024
V7X_VMEM_CAP = V7X_VMEM_BYTES - 6 * 1024 * 1024
LANES = 128
SUBLANES = 8
N_CHIPS = 4
N_DEV = 8
MESH = pl.DeviceIdType.MESH

NN = ((1,), (0,))
NT = ((1,), (1,))
TN = ((0,), (0,))


def _tile(n, pref, mult):
    best = None
    d = mult
    while d <= min(n, pref):
        if n % d == 0:
            best = d
        d += mult
    return n if best is None else best


def _nbytes(shape, dtype):
    n = 1
    for s in shape:
        if s is not None:
            n *= s
    return n * jnp.dtype(dtype).itemsize


def _params(est_bytes=None, sem=None, **kw):
    if est_bytes is not None:
        kw["vmem_limit_bytes"] = int(min(V7X_VMEM_CAP, max(32 * 1024 * 1024, est_bytes * 5 // 4 + (4 << 20))))
    if sem is not None:
        kw["dimension_semantics"] = sem
    return pltpu.CompilerParams(**kw)


def _dot(a, b, contract):
    return lax.dot_general(a.astype(BF16), b.astype(BF16), (contract, ((), ())), preferred_element_type=F32)


def _mm(name, a, b, *, grid, a_spec, b_spec, o_spec, out_shape, contract, res=None, res_spec=None, carry=None):
    nk = grid[2]
    acc_shape = tuple(s for s in o_spec.block_shape if s is not None)
    n_in = 2 if res is None else 3
    x_in = 0 if carry is None else len(carry.operands)
    x_out = 0 if carry is None else carry.n_sems

    def body(*refs):
        a_ref, b_ref = refs[:2]
        r_ref = None if res is None else refs[2]
        x_ins = refs[n_in:n_in + x_in]
        o_ref = refs[n_in + x_in]
        x_outs = refs[n_in + x_in + 1:n_in + x_in + 1 + x_out]
        scr = refs[n_in + x_in + 1 + x_out:]
        if carry is not None:
            sems, scr = scr[-2:], scr[:-2]
            ids = [pl.program_id(d) for d in range(3)]
            first = functools.reduce(jnp.logical_and, [i == 0 for i in ids])
            last = functools.reduce(jnp.logical_and, [i == g - 1 for i, g in zip(ids, grid)])

            @pl.when(first)
            def _():
                carry.start(x_ins, x_outs, *sems)

        p = _dot(a_ref[...], b_ref[...], contract)

        def finish(acc):
            if r_ref is not None:
                acc = acc + r_ref[...]
            o_ref[...] = acc.astype(o_ref.dtype)

        if nk == 1:
            finish(p)
        else:
            acc_ref = scr[0]
            k = pl.program_id(2)

            @pl.when(k == 0)
            def _():
                acc_ref[...] = p

            @pl.when(jnp.logical_and(k > 0, k < nk - 1))
            def _():
                acc_ref[...] += p

            @pl.when(k == nk - 1)
            def _():
                finish(acc_ref[...] + p)

        if carry is not None:
            @pl.when(last)
            def _():
                carry.wait(x_ins, x_outs, *sems)

    in_specs = [a_spec, b_spec]
    args = [a, b]
    est = 2 * (_nbytes(a_spec.block_shape, a.dtype) + _nbytes(b_spec.block_shape, b.dtype)
               + _nbytes(o_spec.block_shape, out_shape.dtype)) + 2 * _nbytes(acc_shape, F32)
    if res is not None:
        in_specs.append(res_spec)
        args.append(res)
        est += 2 * _nbytes(res_spec.block_shape, res.dtype)
    scratch = [] if nk == 1 else [pltpu.VMEM(acc_shape, F32)]
    if carry is None:
        return pl.pallas_call(
            body, name=name, grid=grid, in_specs=in_specs, out_specs=o_spec, out_shape=out_shape,
            scratch_shapes=scratch,
            compiler_params=_params(est, ("parallel", "parallel", "arbitrary")),
        )(*args)
    outs = pl.pallas_call(
        body, name=name, grid=grid, in_specs=in_specs + [HBM_SPEC] * x_in,
        out_specs=[o_spec] + [HBM_SPEC] * x_out, out_shape=[out_shape] + carry.out_shapes,
        input_output_aliases={n_in + i: 1 + o for i, o in carry.aliases.items()},
        scratch_shapes=scratch + carry.scratch(),
        compiler_params=_params(est, ("arbitrary", "arbitrary", "arbitrary")),
    )(*args, *carry.operands)
    return outs[0], list(outs[1:])


def _mm_plain(name, a, b, contract, out_dtype, *, tm, tn, tk, res=None, carry=None):
    if contract == NN:
        (M, K), N = a.shape, b.shape[1]
    elif contract == NT:
        (M, K), N = a.shape, b.shape[0]
    else:
        (K, M), N = a.shape, b.shape[1]
    tm, tn, tk = _tile(M, tm, LANES), _tile(N, tn, LANES), _tile(K, tk, LANES)
    grid = (M // tm, N // tn, K // tk)
    if contract == TN:
        a_spec = pl.BlockSpec((tk, tm), lambda i, j, k: (k, i))
    else:
        a_spec = pl.BlockSpec((tm, tk), lambda i, j, k: (i, k))
    if contract == NT:
        b_spec = pl.BlockSpec((tn, tk), lambda i, j, k: (j, k))
    else:
        b_spec = pl.BlockSpec((tk, tn), lambda i, j, k: (k, j))
    o_spec = pl.BlockSpec((tm, tn), lambda i, j, k: (i, j))
    return _mm(name, a, b, grid=grid, a_spec=a_spec, b_spec=b_spec, o_spec=o_spec,
               out_shape=jax.ShapeDtypeStruct((M, N), out_dtype), contract=contract,
               res=res, res_spec=None if res is None else pl.BlockSpec((tm, tn), lambda i, j, k: (i, j)), carry=carry)


class _ColShards:
    def __init__(self, cols_per_chip, tn, paired):
        self.c = cols_per_chip
        self.tn = tn
        self.tps = cols_per_chip // tn
        self.ntiles = N_CHIPS * self.tps
        self.paired = paired

    def nat(self, t):
        if not self.paired:
            return t
        return (t % 2) * (self.ntiles // 2) + t // 2

    def chip(self, t):
        return self.nat(t) // self.tps

    def within(self, t):
        return self.nat(t) % self.tps


def _mm_cols_fwd(name, a, wg, sh, out_dtype, *, tm, carry=None):
    S, K = a.shape
    tm = _tile(S, tm, LANES)
    grid = (S // tm, sh.ntiles, 1)
    return _mm(name, a, wg, grid=grid,
               a_spec=pl.BlockSpec((tm, K), lambda i, j, k: (i, 0)),
               b_spec=pl.BlockSpec((None, K, sh.tn), lambda i, j, k: (sh.chip(j), 0, sh.within(j))),
               o_spec=pl.BlockSpec((tm, sh.tn), lambda i, j, k: (i, j)),
               out_shape=jax.ShapeDtypeStruct((S, sh.ntiles * sh.tn), out_dtype), contract=NN, carry=carry)


def _mm_cols_bwd_x(name, dy, wg, sh, out_dtype, *, tm, tn, carry=None):
    S = dy.shape[0]
    K = wg.shape[1]
    tm, tn = _tile(S, tm, LANES), _tile(K, tn, LANES)
    grid = (S // tm, K // tn, sh.ntiles)
    return _mm(name, dy, wg, grid=grid,
               a_spec=pl.BlockSpec((tm, sh.tn), lambda i, j, k: (i, k)),
               b_spec=pl.BlockSpec((None, tn, sh.tn), lambda i, j, k: (sh.chip(k), j, sh.within(k))),
               o_spec=pl.BlockSpec((tm, tn), lambda i, j, k: (i, j)),
               out_shape=jax.ShapeDtypeStruct((S, K), out_dtype), contract=NT, carry=carry)


def _mm_cols_bwd_w(name, a, dy, sh, *, tm, tk, carry=None):
    S, K = a.shape
    tm, tk = _tile(K, tm, LANES), _tile(S, tk, LANES)
    grid = (K // tm, sh.ntiles, S // tk)
    return _mm(name, a, dy, grid=grid,
               a_spec=pl.BlockSpec((tk, tm), lambda i, j, k: (k, i)),
               b_spec=pl.BlockSpec((tk, sh.tn), lambda i, j, k: (k, j)),
               o_spec=pl.BlockSpec((None, tm, sh.tn), lambda i, j, k: (sh.chip(j), i, sh.within(j))),
               out_shape=jax.ShapeDtypeStruct((N_CHIPS, K, sh.c), F32), contract=TN, carry=carry)


def _row_spec(tr, width):
    return pl.BlockSpec((tr, width), lambda i: (i, 0))


def _full_spec(shape):
    nd = len(shape)
    return pl.BlockSpec(shape, lambda i: (0,) * nd)


def _rms_fwd(x, g):
    S, D = x.shape
    tr = _tile(S, 512, 16)

    def body(x_ref, g_ref, h_ref):
        xv = x_ref[...]
        r = lax.rsqrt(jnp.mean(xv * xv, axis=-1, keepdims=True) + EPS)
        h_ref[...] = (xv * r * g_ref[...]).astype(BF16)

    return pl.pallas_call(
        body, name="rms_fwd", grid=(S // tr,),
        in_specs=[_row_spec(tr, D), _full_spec((1, D))], out_specs=_row_spec(tr, D),
        out_shape=jax.ShapeDtypeStruct((S, D), BF16),
        compiler_params=_params(12 * tr * D, ("parallel",)),
    )(x, g)


def _rms_bwd(dh, x, g, dres):
    S, D = x.shape
    tr = _tile(S, 256, 16)

    def body(dh_ref, x_ref, g_ref, dres_ref, dx_ref, dxb_ref, dg_ref):
        xv = x_ref[...]
        dhv = dh_ref[...]
        r = lax.rsqrt(jnp.mean(xv * xv, axis=-1, keepdims=True) + EPS)
        gy = dhv * g_ref[...]
        dot = jnp.mean(xv * gy, axis=-1, keepdims=True)
        dx = dres_ref[...] + (r * gy - xv * (r * r * r * dot))
        dx_ref[...] = dx
        dxb_ref[...] = dx.astype(BF16)
        part = jnp.sum(dhv * (xv * r), axis=0, keepdims=True)

        @pl.when(pl.program_id(0) == 0)
        def _():
            dg_ref[...] = part

        @pl.when(pl.program_id(0) > 0)
        def _():
            dg_ref[...] += part

    return pl.pallas_call(
        body, name="rms_bwd", grid=(S // tr,),
        in_specs=[_row_spec(tr, D), _row_spec(tr, D), _full_spec((1, D)), _row_spec(tr, D)],
        out_specs=[_row_spec(tr, D), _row_spec(tr, D), _full_spec((1, D))],
        out_shape=[jax.ShapeDtypeStruct((S, D), F32), jax.ShapeDtypeStruct((S, D), BF16),
                   jax.ShapeDtypeStruct((1, D), F32)],
        compiler_params=_params(40 * tr * D, ("arbitrary",)),
    )(dh, x, g, dres)


def _out_norm_fwd(oa, ob, ga, gb):
    S, W = oa.shape
    tr = _tile(S, 512, 16)

    def body(oa_ref, ob_ref, ga_ref, gb_ref, o_ref):
        for src, g_ref, off in ((oa_ref, ga_ref, 0), (ob_ref, gb_ref, W)):
            v = src[...]
            r = lax.rsqrt(jnp.mean(v * v, axis=-1, keepdims=True) + EPS)
            o_ref[:, off:off + W] = (v * r * g_ref[...]).astype(BF16)

    return pl.pallas_call(
        body, name="out_norm_fwd", grid=(S // tr,),
        in_specs=[_row_spec(tr, W), _row_spec(tr, W), _full_spec((1, W)), _full_spec((1, W))],
        out_specs=_row_spec(tr, 2 * W), out_shape=jax.ShapeDtypeStruct((S, 2 * W), BF16),
        compiler_params=_params(24 * tr * W, ("parallel",)),
    )(oa, ob, ga, gb)


def _out_norm_bwd(do, oa, ob, ga, gb):
    S, W = oa.shape
    tr = _tile(S, 256, 16)

    def body(do_ref, oa_ref, ob_ref, ga_ref, gb_ref, doa_ref, dob_ref, dga_ref, dgb_ref):
        first = pl.program_id(0) == 0
        for src, g_ref, off, d_ref, dg_ref in ((oa_ref, ga_ref, 0, doa_ref, dga_ref),
                                               (ob_ref, gb_ref, W, dob_ref, dgb_ref)):
            v = src[...]
            dv = do_ref[:, off:off + W]
            r = lax.rsqrt(jnp.mean(v * v, axis=-1, keepdims=True) + EPS)
            gy = dv * g_ref[...]
            dot = jnp.mean(v * gy, axis=-1, keepdims=True)
            d_ref[...] = (r * gy - v * (r * r * r * dot)).astype(BF16)
            part = jnp.sum(dv * (v * r), axis=0, keepdims=True)

            @pl.when(first)
            def _():
                dg_ref[...] = part

            @pl.when(jnp.logical_not(first))
            def _():
                dg_ref[...] += part

    return pl.pallas_call(
        body, name="out_norm_bwd", grid=(S // tr,),
        in_specs=[_row_spec(tr, 2 * W), _row_spec(tr, W), _row_spec(tr, W), _full_spec((1, W)), _full_spec((1, W))],
        out_specs=[_row_spec(tr, W), _row_spec(tr, W), _full_spec((1, W)), _full_spec((1, W))],
        out_shape=[jax.ShapeDtypeStruct((S, W), BF16), jax.ShapeDtypeStruct((S, W), BF16),
                   jax.ShapeDtypeStruct((1, W), F32), jax.ShapeDtypeStruct((1, W), F32)],
        compiler_params=_params(48 * tr * W, ("arbitrary",)),
    )(do, oa, ob, ga, gb)


def _loss_grad(y, t):
    S, D = y.shape
    tr = _tile(S, 256, 16)

    def body(y_ref, t_ref, dy_ref, dyb_ref, ls_ref):
        e = y_ref[...] - t_ref[...]
        dy = e * (1.0 / D)
        dy_ref[...] = dy
        dyb_ref[...] = dy.astype(BF16)
        part = jnp.sum(e * e, axis=0, keepdims=True)

        @pl.when(pl.program_id(0) == 0)
        def _():
            ls_ref[...] = part

        @pl.when(pl.program_id(0) > 0)
        def _():
            ls_ref[...] += part

    return pl.pallas_call(
        body, name="loss_grad", grid=(S // tr,),
        in_specs=[_row_spec(tr, D), _row_spec(tr, D)],
        out_specs=[_row_spec(tr, D), _row_spec(tr, D), _full_spec((1, D))],
        out_shape=[jax.ShapeDtypeStruct((S, D), F32), jax.ShapeDtypeStruct((S, D), BF16),
                   jax.ShapeDtypeStruct((1, D), F32)],
        compiler_params=_params(32 * tr * D, ("arbitrary",)),
    )(y, t)


def _head_rms(x, g):
    r = lax.rsqrt(jnp.mean(x * x, axis=-1, keepdims=True) + EPS)
    return x * r * g


def _head_rms_bwd(x, dy, g):
    r = lax.rsqrt(jnp.mean(x * x, axis=-1, keepdims=True) + EPS)
    gy = dy * g
    dot = jnp.mean(x * gy, axis=-1, keepdims=True)
    return r * gy - x * (r * r * r * dot), dy * (x * r)


def _rope(y, cos2, sgn_sin):
    return y * cos2 + pltpu.roll(y, HEAD_DIM // 2, axis=1) * sgn_sin


def _rope_t(dy, cos2, sgn_sin):
    return dy * cos2 + pltpu.roll(dy * sgn_sin, HEAD_DIM // 2, axis=1)


def _qk_prep_fwd(proj, cos2, sgn_sin, gains, dims):
    S, P = proj.shape
    naw, waw, kvw = dims
    tr = _tile(S, 256, 16)
    hd = HEAD_DIM

    def body(p_ref, c_ref, s_ref, g_ref, qa_ref, ka_ref, va_ref, qb_ref, kb_ref, vb_ref):
        c2 = c_ref[...]
        ss = s_ref[...]
        off = 0
        for dst, width, gi, rot in ((qa_ref, naw, 0, False), (ka_ref, naw, 1, False), (va_ref, naw, None, False),
                                    (qb_ref, waw, 2, True), (kb_ref, kvw, 3, True), (vb_ref, kvw, None, False)):
            for h in range(width // hd):
                xh = p_ref[:, off + h * hd:off + (h + 1) * hd]
                if gi is not None:
                    xh = _head_rms(xh, g_ref[gi:gi + 1, :])
                    if rot:
                        xh = _rope(xh, c2, ss)
                dst[:, h * hd:(h + 1) * hd] = xh.astype(BF16)
            off += width

    widths = (naw, naw, naw, waw, kvw, kvw)
    return pl.pallas_call(
        body, name="qk_prep_fwd", grid=(S // tr,),
        in_specs=[_row_spec(tr, P), _row_spec(tr, hd), _row_spec(tr, hd), _full_spec((SUBLANES, hd))],
        out_specs=[_row_spec(tr, w) for w in widths],
        out_shape=[jax.ShapeDtypeStruct((S, w), BF16) for w in widths],
        compiler_params=_params(16 * tr * P, ("parallel",)),
    )(proj, cos2, sgn_sin, gains)


def _qk_prep_bwd(proj, dqa, dka, dva, dqb, dkb, dvb, cos2, sgn_sin, gains, dims):
    S, P = proj.shape
    naw, waw, kvw = dims
    tr = _tile(S, 128, 16)
    hd = HEAD_DIM

    def body(p_ref, dqa_ref, dka_ref, dva_ref, dqb_ref, dkb_ref, dvb_ref, c_ref, s_ref, g_ref, dp_ref, dg_ref):
        c2 = c_ref[...]
        ss = s_ref[...]
        off = 0
        dgs = [jnp.zeros((1, hd), F32) for _ in range(4)]
        for src, width, gi, rot in ((dqa_ref, naw, 0, False), (dka_ref, naw, 1, False), (dva_ref, naw, None, False),
                                    (dqb_ref, waw, 2, True), (dkb_ref, kvw, 3, True), (dvb_ref, kvw, None, False)):
            for h in range(width // hd):
                dy = src[:, h * hd:(h + 1) * hd].astype(F32)
                if gi is not None:
                    if rot:
                        dy = _rope_t(dy, c2, ss)
                    xh = p_ref[:, off + h * hd:off + (h + 1) * hd]
                    dy, dgt = _head_rms_bwd(xh, dy, g_ref[gi:gi + 1, :])
                    dgs[gi] = dgs[gi] + jnp.sum(dgt, axis=0, keepdims=True)
                dp_ref[:, off + h * hd:off + (h + 1) * hd] = dy.astype(BF16)
            off += width
        part = jnp.concatenate(dgs + [jnp.zeros((SUBLANES - 4, hd), F32)], axis=0)

        @pl.when(pl.program_id(0) == 0)
        def _():
            dg_ref[...] = part

        @pl.when(pl.program_id(0) > 0)
        def _():
            dg_ref[...] += part

    return pl.pallas_call(
        body, name="qk_prep_bwd", grid=(S // tr,),
        in_specs=[_row_spec(tr, P), _row_spec(tr, naw), _row_spec(tr, naw), _row_spec(tr, naw),
                  _row_spec(tr, waw), _row_spec(tr, kvw), _row_spec(tr, kvw),
                  _row_spec(tr, hd), _row_spec(tr, hd), _full_spec((SUBLANES, hd))],
        out_specs=[_row_spec(tr, P), _full_spec((SUBLANES, hd))],
        out_shape=[jax.ShapeDtypeStruct((S, P), BF16), jax.ShapeDtypeStruct((SUBLANES, hd), F32)],
        compiler_params=_params(24 * tr * P, ("arbitrary",)),
    )(proj, dqa, dka, dva, dqb, dkb, dvb, cos2, sgn_sin, gains)


NA_KEYS = NA_WIN_R * GRID_W
NA_ROWS_PER_STEP = 8


def _na_col_geometry():
    c = np.arange(GRID_W)
    col_start = np.clip(c - NA_WIN_C // 2, 0, GRID_W - NA_WIN_C)
    mask = (c[None, :] >= col_start[:, None]) & (c[None, :] < col_start[:, None] + NA_WIN_C)
    dc = np.clip(c[None, :] - c[:, None], -(NA_WIN_C - 1), NA_WIN_C - 1) + (NA_WIN_C - 1)
    onehot = (dc[:, :, None] == np.arange(2 * NA_WIN_C - 1)[None, None, :]) & mask[:, :, None]
    return mask, onehot


def _na_bias_table(rpb_l):
    H = rpb_l.shape[0]
    mask, onehot = _na_col_geometry()
    t = jnp.sum(jnp.where(onehot[None, None], rpb_l[:, :, None, None, :], 0.0), axis=-1)
    t = jnp.where(mask[None, None], t, NEG)
    per_delta = [jnp.transpose(t[:, d:d + NA_WIN_R], (0, 2, 1, 3)).reshape(H, GRID_W, NA_KEYS) for d in range(NA_WIN_R)]
    return jnp.stack(per_delta, axis=1)


def _na_bias_table_t(db):
    H = db.shape[0]
    _, onehot = _na_col_geometry()
    d5 = db.reshape(H, NA_WIN_R, GRID_W, NA_WIN_R, GRID_W)
    folded = jnp.einsum("hdqwk,qkc->hdwc", d5, onehot.astype(np.float32), precision=lax.Precision.HIGHEST)
    return sum(jnp.pad(folded[:, d], ((0, 0), (d, NA_WIN_R - 1 - d), (0, 0))) for d in range(NA_WIN_R))


def _na_row_geometry(r, rows):
    start = jnp.clip(r - NA_WIN_R // 2, 0, rows - NA_WIN_R)
    return start, start - r + (NA_WIN_R - 1)


def _softmax_rows(s):
    m = jnp.max(s, axis=-1, keepdims=True)
    e = jnp.exp(s - m)
    return e / jnp.sum(e, axis=-1, keepdims=True)


def _na_fwd(qa, ka, va, btab):
    S, W = qa.shape
    H = W // HEAD_DIM
    rows = S // GRID_W
    assert rows >= NA_WIN_R
    rb = _tile(rows, NA_ROWS_PER_STEP, 1)
    tq = rb * GRID_W

    def body(q_ref, k_ref, v_ref, b_ref, o_ref):
        i = pl.program_id(1)

        geo = [_na_row_geometry(i * rb + j, rows) for j in range(rb)]
        toks = [pl.ds(j * GRID_W, GRID_W) for j in range(rb)]
        wins = [pl.ds(pl.multiple_of(start * GRID_W, GRID_W), NA_KEYS) for start, _ in geo]
        scores = [_dot(q_ref[toks[j], :], k_ref[wins[j], :], NT) for j in range(rb)]
        probs = [_softmax_rows(scores[j] * ATTN_SCALE + b_ref[geo[j][1]]) for j in range(rb)]
        for j in range(rb):
            o_ref[toks[j], :] = _dot(probs[j], v_ref[wins[j], :], NN)

    kv_spec = pl.BlockSpec((S, HEAD_DIM), lambda h, i: (0, h))
    return pl.pallas_call(
        body, name="na_fwd", grid=(H, rows // rb),
        in_specs=[pl.BlockSpec((tq, HEAD_DIM), lambda h, i: (i, h)), kv_spec, kv_spec,
                  pl.BlockSpec((None, NA_WIN_R, GRID_W, NA_KEYS), lambda h, i: (h, 0, 0, 0))],
        out_specs=pl.BlockSpec((tq, HEAD_DIM), lambda h, i: (i, h)),
        out_shape=jax.ShapeDtypeStruct((S, W), F32),
        compiler_params=_params(8 * S * HEAD_DIM + (8 << 20), ("parallel", "arbitrary")),
    )(qa, ka, va, btab)


def _na_bwd(qa, ka, va, btab, doa):
    S, W = qa.shape
    H = W // HEAD_DIM
    rows = S // GRID_W
    rb = _tile(rows, NA_ROWS_PER_STEP, 1)
    tq = rb * GRID_W

    def body(q_ref, k_ref, v_ref, b_ref, do_ref, dq_ref, dk_ref, dv_ref, db_ref):
        i = pl.program_id(1)

        @pl.when(i == 0)
        def _():
            dk_ref[...] = jnp.zeros_like(dk_ref)
            dv_ref[...] = jnp.zeros_like(dv_ref)
            db_ref[...] = jnp.zeros_like(db_ref)

        steps = range(rb)
        geo = [_na_row_geometry(i * rb + j, rows) for j in steps]
        toks = [pl.ds(j * GRID_W, GRID_W) for j in steps]
        wins = [pl.ds(pl.multiple_of(start * GRID_W, GRID_W), NA_KEYS) for start, _ in geo]
        scores = [_dot(q_ref[toks[j], :], k_ref[wins[j], :], NT) for j in steps]
        dps = [_dot(do_ref[toks[j], :], v_ref[wins[j], :], NT) for j in steps]
        probs = [_softmax_rows(scores[j] * ATTN_SCALE + b_ref[geo[j][1]]) for j in steps]
        dss = [probs[j] * (dps[j] - jnp.sum(probs[j] * dps[j], axis=-1, keepdims=True)) for j in steps]
        for j in steps:
            db_ref[geo[j][1]] += dss[j]
        dsb = [(dss[j] * ATTN_SCALE).astype(BF16) for j in steps]
        for j in steps:
            dq_ref[toks[j], :] = _dot(dsb[j], k_ref[wins[j], :], NN)
        dks = [_dot(dsb[j], q_ref[toks[j], :], TN) for j in steps]
        dvs = [_dot(probs[j], do_ref[toks[j], :], TN) for j in steps]
        for j in steps:
            dk_ref[wins[j], :] += dks[j]
            dv_ref[wins[j], :] += dvs[j]

    kv_spec = pl.BlockSpec((S, HEAD_DIM), lambda h, i: (0, h))
    q_spec = pl.BlockSpec((tq, HEAD_DIM), lambda h, i: (i, h))
    b_spec = pl.BlockSpec((None, NA_WIN_R, GRID_W, NA_KEYS), lambda h, i: (h, 0, 0, 0))
    return pl.pallas_call(
        body, name="na_bwd", grid=(H, rows // rb),
        in_specs=[q_spec, kv_spec, kv_spec, b_spec, q_spec],
        out_specs=[q_spec, kv_spec, kv_spec, b_spec],
        out_shape=[jax.ShapeDtypeStruct((S, W), F32), jax.ShapeDtypeStruct((S, W), F32),
                   jax.ShapeDtypeStruct((S, W), F32), jax.ShapeDtypeStruct(btab.shape, F32)],
        compiler_params=_params(24 * S * HEAD_DIM + (12 << 20), ("parallel", "arbitrary")),
    )(qa, ka, va, btab, doa)


WA_KEYS = 3 * WA_BLOCK
WA_BLOCKS_PER_STEP = 4


def _wa_mask(qk, n, start, g):
    s = qk * ATTN_SCALE
    qpos = n * WA_BLOCK + lax.broadcasted_iota(jnp.int32, (WA_BLOCK, WA_KEYS), 0)
    kpos = start + lax.broadcasted_iota(jnp.int32, (WA_BLOCK, WA_KEYS), 1)
    valid = jnp.abs(kpos - qpos) <= WA_WINDOW
    valid = jnp.concatenate([valid] * g, axis=0)
    return jnp.where(valid, s, NEG)


def _wa_probs(s, sink):
    m = jnp.maximum(jnp.max(s, axis=-1, keepdims=True), sink)
    e = jnp.exp(s - m)
    es = jnp.exp(sink - m)
    den = jnp.sum(e, axis=-1, keepdims=True) + es
    return e / den, es / den


def _wa_stack(ref, tok, g):
    return jnp.concatenate([ref[tok, t * HEAD_DIM:(t + 1) * HEAD_DIM] for t in range(g)], axis=0)


def _wa_fwd(qb, kb, vb, sink_col):
    S, W = qb.shape
    hkv = kb.shape[1] // HEAD_DIM
    g = W // HEAD_DIM // hkv
    nb = S // WA_BLOCK
    assert S >= WA_KEYS
    qb_step = _tile(nb, WA_BLOCKS_PER_STEP, 1)
    tq = qb_step * WA_BLOCK

    def body(q_ref, k_ref, v_ref, s_ref, o_ref):
        i = pl.program_id(1)

        steps = range(qb_step)
        ns = [i * qb_step + j for j in steps]
        toks = [pl.ds(j * WA_BLOCK, WA_BLOCK) for j in steps]
        starts = [pl.multiple_of(jnp.clip((n - 1) * WA_BLOCK, 0, S - WA_KEYS), WA_BLOCK) for n in ns]
        wins = [pl.ds(start, WA_KEYS) for start in starts]
        scores = [_dot(_wa_stack(q_ref, toks[j], g), k_ref[wins[j], :], NT) for j in steps]
        probs = [_wa_probs(_wa_mask(scores[j], ns[j], starts[j], g), s_ref[...])[0] for j in steps]
        outs = [_dot(probs[j], v_ref[wins[j], :], NN) for j in steps]
        for j in steps:
            for t in range(g):
                o_ref[toks[j], t * HEAD_DIM:(t + 1) * HEAD_DIM] = outs[j][t * WA_BLOCK:(t + 1) * WA_BLOCK]

    kv_spec = pl.BlockSpec((S, HEAD_DIM), lambda h, i: (0, h))
    q_spec = pl.BlockSpec((tq, g * HEAD_DIM), lambda h, i: (i, h))
    return pl.pallas_call(
        body, name="wa_fwd", grid=(hkv, nb // qb_step),
        in_specs=[q_spec, kv_spec, kv_spec, pl.BlockSpec((None, g * WA_BLOCK, 1), lambda h, i: (h, 0, 0))],
        out_specs=q_spec, out_shape=jax.ShapeDtypeStruct((S, W), F32),
        compiler_params=_params(8 * S * HEAD_DIM + (12 << 20), ("parallel", "arbitrary")),
    )(qb, kb, vb, sink_col)


def _wa_bwd(qb, kb, vb, sink_col, dob):
    S, W = qb.shape
    KW = kb.shape[1]
    hkv = KW // HEAD_DIM
    g = W // HEAD_DIM // hkv
    nb = S // WA_BLOCK
    qb_step = _tile(nb, WA_BLOCKS_PER_STEP, 1)
    tq = qb_step * WA_BLOCK

    def body(q_ref, k_ref, v_ref, s_ref, do_ref, dq_ref, dk_ref, dv_ref, dsink_ref):
        i = pl.program_id(1)

        @pl.when(i == 0)
        def _():
            dk_ref[...] = jnp.zeros_like(dk_ref)
            dv_ref[...] = jnp.zeros_like(dv_ref)
            dsink_ref[...] = jnp.zeros_like(dsink_ref)

        steps = range(qb_step)
        ns = [i * qb_step + j for j in steps]
        toks = [pl.ds(j * WA_BLOCK, WA_BLOCK) for j in steps]
        starts = [pl.multiple_of(jnp.clip((n - 1) * WA_BLOCK, 0, S - WA_KEYS), WA_BLOCK) for n in ns]
        wins = [pl.ds(start, WA_KEYS) for start in starts]
        qss = [_wa_stack(q_ref, toks[j], g) for j in steps]
        doss = [_wa_stack(do_ref, toks[j], g) for j in steps]
        scores = [_dot(qss[j], k_ref[wins[j], :], NT) for j in steps]
        dps = [_dot(doss[j], v_ref[wins[j], :], NT) for j in steps]
        pp = [_wa_probs(_wa_mask(scores[j], ns[j], starts[j], g), s_ref[...]) for j in steps]
        dsums = [jnp.sum(pp[j][0] * dps[j], axis=-1, keepdims=True) for j in steps]
        dsb = [(pp[j][0] * (dps[j] - dsums[j]) * ATTN_SCALE).astype(BF16) for j in steps]
        dsink_ref[...] -= sum(pp[j][1] * dsums[j] for j in steps)
        dqs = [_dot(dsb[j], k_ref[wins[j], :], NN) for j in steps]
        dks = [_dot(dsb[j], qss[j], TN) for j in steps]
        dvs = [_dot(pp[j][0], doss[j], TN) for j in steps]
        for j in steps:
            for t in range(g):
                dq_ref[toks[j], t * HEAD_DIM:(t + 1) * HEAD_DIM] = dqs[j][t * WA_BLOCK:(t + 1) * WA_BLOCK]
            dk_ref[wins[j], :] += dks[j]
            dv_ref[wins[j], :] += dvs[j]

    kv_spec = pl.BlockSpec((S, HEAD_DIM), lambda h, i: (0, h))
    q_spec = pl.BlockSpec((tq, g * HEAD_DIM), lambda h, i: (i, h))
    s_spec = pl.BlockSpec((None, g * WA_BLOCK, 1), lambda h, i: (h, 0, 0))
    return pl.pallas_call(
        body, name="wa_bwd", grid=(hkv, nb // qb_step),
        in_specs=[q_spec, kv_spec, kv_spec, s_spec, q_spec],
        out_specs=[q_spec, kv_spec, kv_spec, s_spec],
        out_shape=[jax.ShapeDtypeStruct((S, W), F32), jax.ShapeDtypeStruct((S, KW), F32),
                   jax.ShapeDtypeStruct((S, KW), F32), jax.ShapeDtypeStruct(sink_col.shape, F32)],
        compiler_params=_params(24 * S * HEAD_DIM + (16 << 20), ("parallel", "arbitrary")),
    )(qb, kb, vb, sink_col, dob)


CONV_HALO = SUBLANES


def _conv_specs(tr, width, nblk, rows_inner):
    per = tr // CONV_HALO

    def spec(shape, row_block):
        if rows_inner:
            return pl.BlockSpec(shape, lambda j, i: (row_block(i), j))
        return pl.BlockSpec(shape, lambda i, j: (row_block(i), j))

    cur = spec((tr, width), lambda i: i)
    prev = spec((CONV_HALO, width), lambda i: jnp.maximum(i * per - 1, 0))
    nxt = spec((CONV_HALO, width), lambda i: jnp.minimum((i + 1) * per, nblk * per - 1))
    return prev, cur, nxt


def _extend(prev_ref, cur_ref, next_ref, i, nblk):
    p = jnp.where(i > 0, prev_ref[...].astype(F32), 0.0)
    n = jnp.where(i < nblk - 1, next_ref[...].astype(F32), 0.0)
    return jnp.concatenate([p, cur_ref[...].astype(F32), n], axis=0)


def _shift_down(x):
    return pltpu.roll(x, 1, axis=0)


def _shift_up(x):
    return pltpu.roll(x, x.shape[0] - 1, axis=0)


def _conv(ext, w_ref, b_ref):
    return _shift_down(ext) * w_ref[0:1, :] + ext * w_ref[1:2, :] + _shift_up(ext) * w_ref[2:3, :] + b_ref[...]


def _sigmoid(x):
    return 1.0 / (1.0 + jnp.exp(-x))


def _convgate_fwd(up_pre, cw, cb, tc):
    S, C2 = up_pre.shape
    tr = _tile(S, 256, 16)
    nblk = S // tr
    prev, cur, nxt = _conv_specs(tr, 2 * tc, nblk, False)

    def body(p_ref, c_ref, n_ref, w_ref, b_ref, a_ref):
        i = pl.program_id(0)
        u = _conv(_extend(p_ref, c_ref, n_ref, i, nblk), w_ref, b_ref)[CONV_HALO:CONV_HALO + tr]
        gate, up = u[:, :tc], u[:, tc:]
        a_ref[...] = (gate * _sigmoid(gate) * up).astype(BF16)

    return pl.pallas_call(
        body, name="convgate_fwd", grid=(nblk, C2 // (2 * tc)),
        in_specs=[prev, cur, nxt, pl.BlockSpec((3, 2 * tc), lambda i, j: (0, j)),
                  pl.BlockSpec((1, 2 * tc), lambda i, j: (0, j))],
        out_specs=pl.BlockSpec((tr, tc), lambda i, j: (i, j)),
        out_shape=jax.ShapeDtypeStruct((S, C2 // 2), BF16),
        compiler_params=_params(48 * tr * tc, ("parallel", "parallel")),
    )(up_pre, up_pre, up_pre, cw, cb)


def _convgate_bwd(up_pre, dact, cw, cb, tc):
    S, C2 = up_pre.shape
    tr = _tile(S, 128, 16)
    nblk = S // tr
    prev, cur, nxt = _conv_specs(tr, 2 * tc, nblk, True)
    dprev, dcur, dnxt = _conv_specs(tr, tc, nblk, True)

    def body(p_ref, c_ref, n_ref, dp_ref, dc_ref, dn_ref, w_ref, b_ref, dx_ref, dw_ref, db_ref):
        i = pl.program_id(1)
        ext = _extend(p_ref, c_ref, n_ref, i, nblk)
        da = _extend(dp_ref, dc_ref, dn_ref, i, nblk)
        ext_dn, ext_up = _shift_down(ext), _shift_up(ext)
        u = ext_dn * w_ref[0:1, :] + ext * w_ref[1:2, :] + ext_up * w_ref[2:3, :] + b_ref[...]
        gate, up = u[:, :tc], u[:, tc:]
        sg = _sigmoid(gate)
        silu = gate * sg
        du = jnp.concatenate([da * up * (sg + silu * (1.0 - sg)), da * silu], axis=1)
        dx = _shift_up(du) * w_ref[0:1, :] + du * w_ref[1:2, :] + _shift_down(du) * w_ref[2:3, :]
        mid = slice(CONV_HALO, CONV_HALO + tr)
        dx_ref[...] = dx[mid].astype(BF16)
        duc = du[mid]
        dw = jnp.concatenate([jnp.sum(duc * ext_dn[mid], axis=0, keepdims=True),
                              jnp.sum(duc * ext[mid], axis=0, keepdims=True),
                              jnp.sum(duc * ext_up[mid], axis=0, keepdims=True)], axis=0)
        db = jnp.sum(duc, axis=0, keepdims=True)

        @pl.when(i == 0)
        def _():
            dw_ref[...] = dw
            db_ref[...] = db

        @pl.when(i > 0)
        def _():
            dw_ref[...] += dw
            db_ref[...] += db

    return pl.pallas_call(
        body, name="convgate_bwd", grid=(C2 // (2 * tc), nblk),
        in_specs=[prev, cur, nxt, dprev, dcur, dnxt,
                  pl.BlockSpec((3, 2 * tc), lambda j, i: (0, j)), pl.BlockSpec((1, 2 * tc), lambda j, i: (0, j))],
        out_specs=[pl.BlockSpec((tr, 2 * tc), lambda j, i: (i, j)),
                   pl.BlockSpec((3, 2 * tc), lambda j, i: (0, j)), pl.BlockSpec((1, 2 * tc), lambda j, i: (0, j))],
        out_shape=[jax.ShapeDtypeStruct((S, C2), BF16), jax.ShapeDtypeStruct((3, C2), F32),
                   jax.ShapeDtypeStruct((1, C2), F32)],
        compiler_params=_params(160 * tr * tc, ("parallel", "arbitrary")),
    )(up_pre, up_pre, up_pre, dact, dact, dact, cw, cb)


def _rope_tables(positions):
    inv = ROPE_THETA ** (-jnp.arange(0, HEAD_DIM, 2, dtype=F32) / HEAD_DIM)
    ang = positions.astype(F32)[:, None] * inv[None, :]
    cos, sin = jnp.cos(ang), jnp.sin(ang)
    return jnp.concatenate([cos, cos], axis=1), jnp.concatenate([-sin, sin], axis=1)


def _sink_col(sink_l, hkv):
    g = sink_l.shape[0] // hkv
    return jnp.broadcast_to(sink_l.reshape(hkv, g, 1), (hkv, g, WA_BLOCK)).reshape(hkv, g * WA_BLOCK, 1)


def _sink_col_t(dcol):
    hkv, rows, _ = dcol.shape
    return jnp.sum(dcol.reshape(hkv, rows // WA_BLOCK, WA_BLOCK), axis=-1).reshape(-1)


def _to_paired(a, sh):
    return jnp.concatenate([a[:, sh.nat(t) * sh.tn:(sh.nat(t) + 1) * sh.tn] for t in range(sh.ntiles)], axis=1)


def _from_paired(a, sh):
    pos = {sh.nat(t): t for t in range(sh.ntiles)}
    return jnp.concatenate([a[:, pos[n] * sh.tn:(pos[n] + 1) * sh.tn] for n in range(sh.ntiles)], axis=1)


def _pack(arrs):
    flat = jnp.concatenate([a.reshape(-1).astype(F32) for a in arrs])
    unit = SUBLANES * LANES
    total = -(-flat.shape[0] // unit) * unit
    return jnp.pad(flat, (0, total - flat.shape[0])).reshape(-1, LANES)


def _unpack(buf, shapes):
    flat = buf.reshape(-1)
    out, off = [], 0
    for s in shapes:
        n = math.prod(s)
        out.append(flat[off:off + n].reshape(s))
        off += n
    return out


HBM_SPEC = pl.BlockSpec(memory_space=pltpu.HBM)
DMA_CHUNK_BYTES = 512 * 1024


def _row_chunks(rows, row_bytes, align):
    want = max(1, rows * row_bytes // DMA_CHUNK_BYTES)
    count = max(k for k in range(1, rows + 1) if rows % k == 0 and (rows // k) % align == 0 and (k <= want or k == 1))
    size = rows // count
    return [(j * size, size) for j in range(count)]


def _mesh_pos():
    return lax.axis_index("x"), lax.axis_index("y"), lax.axis_index("c")


def _other_chips(x, y):
    return [(1 - x, y), (x, 1 - y), (1 - x, 1 - y)]


def _remote(src, dst, send_sems, recv_sems, k, dev):
    return pltpu.make_async_remote_copy(src_ref=src, dst_ref=dst, send_sem=send_sems.at[k], recv_sem=recv_sems.at[k],
                                        device_id=dev, device_id_type=MESH)


class _Exchange:
    def __init__(self, operands, out_shapes, aliases, start, wait):
        self.operands, self.out_shapes, self.aliases = list(operands), list(out_shapes), dict(aliases)
        self.start, self.wait = start, wait
        self.n_sems = len(out_shapes)

    def scratch(self):
        return [pltpu.SemaphoreType.DMA((self.n_sems,)), pltpu.SemaphoreType.DMA((self.n_sems,))]


class _SemSlice:
    def __init__(self, sems, offset):
        self._sems, self._offset = sems, offset

    @property
    def at(self):
        return self

    def __getitem__(self, k):
        return self._sems.at[self._offset + k]


def _both(a, b):
    ia, oa = len(a.operands), a.n_sems

    def split(ins, outs, send_sems, recv_sems):
        return ((ins[:ia], outs[:oa], send_sems, recv_sems),
                (ins[ia:], outs[oa:], _SemSlice(send_sems, oa), _SemSlice(recv_sems, oa)))

    def start(*refs):
        first, second = split(*refs)
        a.start(*first)
        b.start(*second)

    def wait(*refs):
        first, second = split(*refs)
        a.wait(*first)
        b.wait(*second)

    aliases = dict(a.aliases)
    aliases.update({ia + i: oa + o for i, o in b.aliases.items()})
    return _Exchange(a.operands + b.operands, a.out_shapes + b.out_shapes, aliases, start, wait)


def _exchange_alone(name, ex):
    n_in = len(ex.operands)

    def body(*refs):
        ins, outs = refs[:n_in], refs[n_in:n_in + ex.n_sems]
        send_sems, recv_sems = refs[n_in + ex.n_sems:]
        ex.start(ins, outs, send_sems, recv_sems)
        ex.wait(ins, outs, send_sems, recv_sems)

    return pl.pallas_call(
        body, name=name, in_specs=[HBM_SPEC] * n_in, out_specs=[HBM_SPEC] * ex.n_sems, out_shape=ex.out_shapes,
        input_output_aliases=ex.aliases, scratch_shapes=ex.scratch(),
    )(*ex.operands)


def _ag_chip_exchange(gathered, small):
    n = len(gathered)
    arrays = list(gathered) + ([] if small is None else [small])

    def start(ins, outs, send_sems, recv_sems):
        x, y, c = _mesh_pos()
        q_me = 2 * x + y
        chips = _other_chips(x, y)
        for t in range(n):
            h = gathered[t].shape[1] // 2
            for s0, sz in _row_chunks(h, gathered[t].shape[2] * 2, 16):
                blk = outs[t].at[q_me, pl.ds(c * h + s0, sz)]
                for chip in chips:
                    _remote(blk, blk, send_sems, recv_sems, t, (*chip, c)).start()
        if small is not None:
            for chip in chips:
                _remote(outs[n].at[q_me], outs[n].at[q_me], send_sems, recv_sems, n, (*chip, c)).start()

    def wait(ins, outs, send_sems, recv_sems):
        x, y, c = _mesh_pos()
        for t in range(n):
            three = outs[t].at[pl.ds(0, 3), pl.ds(0, gathered[t].shape[1] // 2)]
            _remote(three, three, send_sems, recv_sems, t, (x, y, 1 - c)).wait()
        if small is not None:
            three = outs[n].at[pl.ds(0, 3)]
            _remote(three, three, send_sems, recv_sems, n, (x, y, 1 - c)).wait()

    return _Exchange(arrays, [jax.ShapeDtypeStruct(a.shape, a.dtype) for a in arrays],
                     {i: i for i in range(len(arrays))}, start, wait)


def _ag_sibling_pass(gathered):
    n = len(gathered)

    def start(ins, outs, send_sems, recv_sems):
        x, y, c = _mesh_pos()
        for t in range(n):
            h = gathered[t].shape[1] // 2
            for s0, sz in _row_chunks(h, gathered[t].shape[2] * 2, 16):
                for cx, cy in _other_chips(x, y):
                    blk = outs[t].at[2 * cx + cy, pl.ds(c * h + s0, sz)]
                    _remote(blk, blk, send_sems, recv_sems, t, (x, y, 1 - c)).start()

    def wait(ins, outs, send_sems, recv_sems):
        x, y, c = _mesh_pos()
        for t in range(n):
            three = outs[t].at[pl.ds(0, 3), pl.ds(0, gathered[t].shape[1] // 2)]
            _remote(three, three, send_sems, recv_sems, t, (x, y, 1 - c)).wait()

    return _Exchange(gathered, [jax.ShapeDtypeStruct(a.shape, a.dtype) for a in gathered],
                     {i: i for i in range(n)}, start, wait)


def _rs_sibling_exchange(grads):
    n = len(grads)
    halves = [g.shape[1] // 2 for g in grads]

    def start(ins, outs, send_sems, recv_sems):
        x, y, c = _mesh_pos()
        for t in range(n):
            for q in range(N_CHIPS):
                for s0, sz in _row_chunks(halves[t], grads[t].shape[2] * 4, SUBLANES):
                    _remote(ins[t].at[q, pl.ds((1 - c) * halves[t] + s0, sz)], outs[t].at[q, pl.ds(s0, sz)],
                            send_sems, recv_sems, t, (x, y, 1 - c)).start()

    def wait(ins, outs, send_sems, recv_sems):
        x, y, c = _mesh_pos()
        for t in range(n):
            _remote(outs[t], outs[t], send_sems, recv_sems, t, (x, y, 1 - c)).wait()

    return _Exchange(grads, [jax.ShapeDtypeStruct((N_CHIPS, h, g.shape[2]), F32) for g, h in zip(grads, halves)],
                     {}, start, wait)


def _rs_chip_exchange(pbs):
    n = len(pbs)

    def start(ins, outs, send_sems, recv_sems):
        x, y, c = _mesh_pos()
        for t in range(n):
            for s0, sz in _row_chunks(pbs[t].shape[1], pbs[t].shape[2] * 2, 16):
                for k, (cx, cy) in enumerate(_other_chips(x, y)):
                    _remote(ins[t].at[2 * cx + cy, pl.ds(s0, sz)], outs[t].at[k, pl.ds(s0, sz)],
                            send_sems, recv_sems, t, (cx, cy, c)).start()

    def wait(ins, outs, send_sems, recv_sems):
        x, y, c = _mesh_pos()
        for t in range(n):
            _remote(outs[t], outs[t], send_sems, recv_sems, t, (x, y, 1 - c)).wait()

    return _Exchange(pbs, [jax.ShapeDtypeStruct((3,) + p.shape[1:], BF16) for p in pbs], {}, start, wait)


def _rs_sibling_share(gs):
    n = len(gs)

    def start(ins, outs, send_sems, recv_sems):
        x, y, c = _mesh_pos()
        for t in range(n):
            h = gs[t].shape[0] // 2
            for s0, sz in _row_chunks(h, gs[t].shape[1] * 4, SUBLANES):
                rows = outs[t].at[pl.ds(c * h + s0, sz)]
                _remote(rows, rows, send_sems, recv_sems, t, (x, y, 1 - c)).start()

    def wait(ins, outs, send_sems, recv_sems):
        x, y, c = _mesh_pos()
        for t in range(n):
            half = outs[t].at[pl.ds(0, gs[t].shape[0] // 2)]
            _remote(half, half, send_sems, recv_sems, t, (x, y, 1 - c)).wait()

    return _Exchange(gs, [jax.ShapeDtypeStruct(g.shape, F32) for g in gs], {i: i for i in range(n)}, start, wait)


def _gather_small(buf):
    def body(in_ref, out_ref, send_sems, recv_sems, loc_sem):
        x, y, c = _mesh_pos()
        me = 4 * x + 2 * y + c
        local = pltpu.make_async_copy(in_ref, out_ref.at[me], loc_sem)
        local.start()
        sends, peers = [], []
        for k in range(1, N_DEV):
            bx, by, bc = (k >> 2) & 1, (k >> 1) & 1, k & 1
            peer = (x + bx - 2 * x * bx, y + by - 2 * y * by, c + bc - 2 * c * bc)
            peers.append(peer)
            sends.append(_remote(in_ref, out_ref.at[me], send_sems, recv_sems, k - 1, peer))
        for cp in sends:
            cp.start()
        for k, (px, py, pc) in enumerate(peers):
            slot = out_ref.at[4 * px + 2 * py + pc]
            _remote(slot, slot, send_sems, recv_sems, k, (px, py, pc)).wait_recv()
        for cp in sends:
            cp.wait_send()
        local.wait()

    return pl.pallas_call(
        body, name="gather_small", in_specs=[HBM_SPEC], out_specs=HBM_SPEC,
        out_shape=jax.ShapeDtypeStruct((N_DEV,) + buf.shape, F32),
        scratch_shapes=[pltpu.SemaphoreType.DMA((N_DEV - 1,)), pltpu.SemaphoreType.DMA((N_DEV - 1,)),
                        pltpu.SemaphoreType.DMA(())],
    )(buf)


def _ew_rows(rows, cols, bytes_per_elem):
    budget = 20 * 1024 * 1024
    return _tile(rows, max(16, budget // (2 * bytes_per_elem * cols) // 16 * 16), 16)


def _sum_pair(grad, recv):
    _, h, cols = recv.shape
    tr = _ew_rows(h, cols, 14)
    nb = h // tr
    half = pl.BlockSpec((None, tr, cols), lambda q, i: (q, i, 0))

    def body(a_ref, b_ref, f_ref, h_ref):
        s = a_ref[...] + b_ref[...]
        f_ref[...] = s
        h_ref[...] = s.astype(BF16)

    return pl.pallas_call(
        body, name="rs_sum_pair", grid=(N_CHIPS, nb),
        in_specs=[pl.BlockSpec((None, tr, cols), lambda q, i: (q, lax.axis_index("c") * nb + i, 0)), half],
        out_specs=[half, half],
        out_shape=[jax.ShapeDtypeStruct(recv.shape, F32), jax.ShapeDtypeStruct(recv.shape, BF16)],
        compiler_params=_params(28 * tr * cols, ("parallel", "parallel")),
    )(grad, recv)


def _sum_four(pf, recv):
    _, h, cols = pf.shape
    tr = _ew_rows(h, cols, 14)
    nb = h // tr

    def body(a_ref, r_ref, o_ref):
        o_ref[...] = ((a_ref[...] + r_ref[0].astype(F32)) + r_ref[1].astype(F32)) + r_ref[2].astype(F32)

    return pl.pallas_call(
        body, name="rs_sum_four", grid=(nb,),
        in_specs=[pl.BlockSpec((None, tr, cols), lambda i: (2 * lax.axis_index("x") + lax.axis_index("y"), i, 0)),
                  pl.BlockSpec((3, tr, cols), lambda i: (0, i, 0))],
        out_specs=pl.BlockSpec((tr, cols), lambda i: (lax.axis_index("c") * nb + i, 0)),
        out_shape=jax.ShapeDtypeStruct((2 * h, cols), F32),
        compiler_params=_params(28 * tr * cols, ("parallel",)),
    )(pf, recv)


def _sum_eight(gathered):
    _, rows, cols = gathered.shape
    tr = _tile(rows, 512, SUBLANES)

    def body(g_ref, o_ref):
        s = g_ref[0]
        for d in range(1, N_DEV):
            s = s + g_ref[d]
        o_ref[...] = s

    return pl.pallas_call(
        body, name="sum_eight", grid=(rows // tr,),
        in_specs=[pl.BlockSpec((N_DEV, tr, cols), lambda i: (0, i, 0))],
        out_specs=_row_spec(tr, cols), out_shape=jax.ShapeDtypeStruct((rows, cols), F32),
        compiler_params=_params(None, ("parallel",)),
    )(gathered)


def _cast_layer(w, l):
    _, rows, cols = w.shape
    tr = _ew_rows(rows, cols, 6)

    def body(w_ref, o_ref):
        o_ref[...] = w_ref[...].astype(BF16)

    return pl.pallas_call(
        body, name="cast_bf16", grid=(rows // tr,),
        in_specs=[pl.BlockSpec((None, tr, cols), lambda i: (l, i, 0))],
        out_specs=pl.BlockSpec((None, tr, cols), lambda i: (2 * lax.axis_index("x") + lax.axis_index("y"), i, 0)),
        out_shape=jax.ShapeDtypeStruct((N_CHIPS, rows, cols), BF16),
        compiler_params=_params(12 * tr * cols, ("parallel",)),
    )(w)


def _adamw(g, w, m, v, l, prev):
    L, rows, cols = w.shape
    tr = _ew_rows(rows, cols, 32)
    layer = pl.BlockSpec((None, tr, cols), lambda i: (l, i, 0))

    def body(g_ref, w_ref, m_ref, v_ref, *rest):
        og_ref, od_ref, om_ref, ov_ref = rest[-4:]
        gv = g_ref[...]
        mn = ADAM_B1 * m_ref[...] + (1.0 - ADAM_B1) * gv
        vn = ADAM_B2 * v_ref[...] + (1.0 - ADAM_B2) * (gv * gv)
        m_hat = mn / ADAM_C1
        v_hat = vn / ADAM_C2
        od_ref[...] = -ADAM_LR * (m_hat / (jnp.sqrt(v_hat) + ADAM_EPS) + ADAM_WD * w_ref[...])
        og_ref[...] = gv
        om_ref[...] = mn
        ov_ref[...] = vn

    in_specs = [_row_spec(tr, cols), layer, layer, layer]
    args = [g, w, m, v]
    aliases = {}
    if prev is not None:
        in_specs += [HBM_SPEC] * 4
        args += list(prev)
        aliases = {4 + i: i for i in range(4)}
    return pl.pallas_call(
        body, name="adamw", grid=(rows // tr,), in_specs=in_specs, out_specs=[layer] * 4,
        out_shape=[jax.ShapeDtypeStruct((L, rows, cols), F32)] * 4, input_output_aliases=aliases,
        compiler_params=_params(64 * tr * cols, ("parallel",)),
    )(*args)


WEIGHT_ORDER = ("ln1_g", "w_in", "qn_a", "kn_a", "rpb", "qn_b", "kn_b", "sink", "on_a", "on_b", "w_out", "ln2_g",
                "w_up", "conv_w", "conv_b", "w_down")
BIG = ("w_in", "w_out", "w_up", "w_down")
SMALL = tuple(n for n in WEIGHT_ORDER if n not in BIG)


def _train_step(x, positions, target, w, m, v):
    S, D = x.shape
    L = w["w_in"].shape[0]
    naw = w["rpb"].shape[1] * HEAD_DIM
    waw = w["sink"].shape[1] * HEAD_DIM
    cin = w["w_in"].shape[2]
    kvw = (N_CHIPS * cin - 3 * naw - waw) // 2
    hkv = kvw // HEAD_DIM
    dims = (naw, waw, kvw)
    assert naw == waw, "the two head groups are normalised by one kernel and must be equally wide"
    cup = w["w_up"].shape[2]
    c2 = N_CHIPS * cup
    dff = c2 // 2
    mix = naw + waw
    sh_in = _ColShards(cin, _tile(cin, 1152, LANES), False)
    sh_up = _ColShards(cup, cup // 2 if (cup // 2) % LANES == 0 else cup, True)
    tc = sh_up.tn
    cos2, sgn_sin = _rope_tables(positions)
    q_me = 2 * lax.axis_index("x") + lax.axis_index("y")
    row = lambda a: a[None]

    def own_slots(l):
        small = lax.dynamic_update_slice_in_dim(jnp.zeros((N_CHIPS,) + w["conv_w"].shape[1:], F32),
                                                w["conv_w"][l][None], q_me, axis=0)
        return [_cast_layer(w[n], l) for n in BIG], small

    gath = {}

    def prepare(l):
        bufs, small = own_slots(l)
        gath.update({(l, n): b for n, b in zip(BIG, bufs)})
        gath[l, "conv_w"] = small

    def over_ici(l, names, small=False):
        keys = [(l, n) for n in names]
        ex = _ag_chip_exchange([gath[k] for k in keys], gath[l, "conv_w"] if small else None)
        return ex, keys + ([(l, "conv_w")] if small else [])

    def to_sibling(l, names):
        keys = [(l, n) for n in names]
        return _ag_sibling_pass([gath[k] for k in keys]), keys

    def together(parts):
        ex, keys = parts[0]
        for e, k in parts[1:]:
            ex, keys = _both(ex, e), keys + k
        return ex, keys

    def carried(call, parts):
        if not parts:
            return call(None)
        ex, keys = together(parts)
        out, results = call(ex)
        gath.update(dict(zip(keys, results)))
        return out

    prepare(0)
    ex, keys = over_ici(0, ("w_in", "w_out"), True)
    gath.update(dict(zip(keys, _exchange_alone("ag_first_over_ici", ex))))
    ex, keys = to_sibling(0, ("w_in", "w_out"))
    gath.update(dict(zip(keys, _exchange_alone("ag_first_to_sibling", ex))))
    saved = []
    for l in range(L):
        more = l + 1 < L
        if more:
            prepare(l + 1)
        g_cw = gath[l, "conv_w"]
        cw_p = _to_paired(jnp.transpose(g_cw, (1, 0, 2)).reshape(3, c2), sh_up)
        cb_p = _to_paired(row(w["conv_b"][l]), sh_up)
        gains = jnp.concatenate([row(w["qn_a"][l]), row(w["kn_a"][l]), row(w["qn_b"][l]), row(w["kn_b"][l]),
                                 jnp.zeros((SUBLANES - 4, HEAD_DIM), F32)], axis=0)
        btab = _na_bias_table(w["rpb"][l])
        sink_col = _sink_col(w["sink"][l], hkv)

        h = _rms_fwd(x, row(w["ln1_g"][l]))
        g_in = gath[l, "w_in"]
        proj = carried(lambda ex: _mm_cols_fwd("mm_proj", h, g_in, sh_in, F32, tm=1024, carry=ex),
                       [over_ici(0, ("w_up",))] if l == 0 else [to_sibling(l, ("w_down",))])
        qa, ka, va, qb, kb, vb = _qk_prep_fwd(proj, cos2, sgn_sin, gains, dims)
        oa = _na_fwd(qa, ka, va, btab)
        ob = _wa_fwd(qb, kb, vb, sink_col)
        o = _out_norm_fwd(oa, ob, row(w["on_a"][l]), row(w["on_b"][l]))
        g_out = gath[l, "w_out"].reshape(mix, D)
        x1 = carried(lambda ex: _mm_plain("mm_attn_out", o, g_out, NN, F32, tm=1024, tn=1024, tk=2048, res=x, carry=ex),
                     [to_sibling(0, ("w_up",)), over_ici(0, ("w_down",))] if l == 0 else [])
        h2 = _rms_fwd(x1, row(w["ln2_g"][l]))
        g_up = gath[l, "w_up"]
        up_pre = carried(lambda ex: _mm_cols_fwd("mm_up", h2, g_up, sh_up, F32, tm=1024, carry=ex),
                         ([to_sibling(0, ("w_down",))] if l == 0 else [])
                         + ([over_ici(l + 1, ("w_in", "w_out", "w_up"), True)] if more else []))
        act = _convgate_fwd(up_pre, cw_p, cb_p, tc)
        g_dn = gath[l, "w_down"].reshape(dff, D)
        x2 = carried(lambda ex: _mm_plain("mm_down", act, g_dn, NN, F32, tm=1024, tn=1024, tk=1408, res=x1, carry=ex),
                     [to_sibling(l + 1, ("w_in", "w_out", "w_up")), over_ici(l + 1, ("w_down",))] if more else [])
        saved.append(dict(g_in=g_in, g_out=g_out, g_up=g_up, g_dn=g_dn, cw_p=cw_p, cb_p=cb_p, gains=gains, btab=btab,
                          sink_col=sink_col, x=x, h=h, proj=proj, qa=qa, ka=ka, va=va, qb=qb, kb=kb, vb=vb, oa=oa,
                          ob=ob, o=o, x1=x1, h2=h2, up_pre=up_pre, act=act))
        x = x2

    dx, dxb, loss_cols = _loss_grad(x, target)

    small_grads = [None] * L
    stacked = {n: None for n in BIG}

    def update(layer, shards):
        for n, g in zip(BIG, shards):
            stacked[n] = _adamw(g, w[n], m[n], v[n], layer, stacked[n])

    above = None
    for l in reversed(range(L)):
        s = saved[l]
        if above is None:
            dact = _mm_plain("mm_down_dx", dxb, s["g_dn"], NT, F32, tm=1024, tn=1408, tk=2048)
        else:
            dact, recvs = _mm_plain("mm_down_dx_rs", dxb, s["g_dn"], NT, F32, tm=1024, tn=1408, tk=2048,
                                    carry=_rs_sibling_exchange(above[1]))
            pairs = [_sum_pair(g, r) for g, r in zip(above[1], recvs)]
        d_dn = _mm_plain("mm_down_dw", s["act"], dxb, TN, F32, tm=1408, tn=1024, tk=2048)
        dup_pre, dcw_p, dcb_p = _convgate_bwd(s["up_pre"], dact, s["cw_p"], s["cb_p"], tc)
        if above is None:
            dh2 = _mm_cols_bwd_x("mm_up_dx", dup_pre, s["g_up"], sh_up, F32, tm=1024, tn=2048)
        else:
            dh2, gots = _mm_cols_bwd_x("mm_up_dx_rs", dup_pre, s["g_up"], sh_up, F32, tm=1024, tn=2048,
                                       carry=_rs_chip_exchange([p[1] for p in pairs]))
            halves = [_sum_four(p[0], r) for p, r in zip(pairs, gots)]
        d_up = _mm_cols_bwd_w("mm_up_dw", s["h2"], dup_pre, sh_up, tm=1024, tk=2048)
        dx1, dx1b, d_ln2 = _rms_bwd(dh2, s["x1"], row(w["ln2_g"][l]), dx)
        if above is None:
            do = _mm_plain("mm_attn_out_dx", dx1b, s["g_out"], NT, F32, tm=1024, tn=1024, tk=2048)
        else:
            do, shards = _mm_plain("mm_attn_out_dx_rs", dx1b, s["g_out"], NT, F32, tm=1024, tn=1024, tk=2048,
                                   carry=_rs_sibling_share(halves))
            update(above[0], shards)
        d_dn = d_dn.reshape(N_CHIPS, dff // N_CHIPS, D)
        last = l == 0
        if last:
            d_out, recvs = _mm_plain("mm_attn_out_dw", s["o"], dx1b, TN, F32, tm=1024, tn=1024, tk=2048,
                                     carry=_rs_sibling_exchange([d_up, d_dn]))
            pair_up, pair_dn = _sum_pair(d_up, recvs[0]), _sum_pair(d_dn, recvs[1])
        else:
            d_out = _mm_plain("mm_attn_out_dw", s["o"], dx1b, TN, F32, tm=1024, tn=1024, tk=2048)
        d_out = d_out.reshape(N_CHIPS, mix // N_CHIPS, D)
        doa, dob, d_on_a, d_on_b = _out_norm_bwd(do, s["oa"], s["ob"], row(w["on_a"][l]), row(w["on_b"][l]))
        dqa, dka, dva, dbtab = _na_bwd(s["qa"], s["ka"], s["va"], s["btab"], doa)
        dqb, dkb, dvb, dsink_col = _wa_bwd(s["qb"], s["kb"], s["vb"], s["sink_col"], dob)
        dproj, dgains = _qk_prep_bwd(s["proj"], dqa, dka, dva, dqb, dkb, dvb, cos2, sgn_sin, s["gains"], dims)
        if last:
            dh, (got_dn,) = _mm_cols_bwd_x("mm_proj_dx", dproj, s["g_in"], sh_in, F32, tm=1024, tn=2048,
                                           carry=_rs_chip_exchange([pair_dn[1]]))
            d_in, (got_up,) = _mm_cols_bwd_w("mm_proj_dw", s["h"], dproj, sh_in, tm=1024, tk=2048,
                                             carry=_rs_chip_exchange([pair_up[1]]))
        else:
            dh = _mm_cols_bwd_x("mm_proj_dx", dproj, s["g_in"], sh_in, F32, tm=1024, tn=2048)
            d_in = _mm_cols_bwd_w("mm_proj_dw", s["h"], dproj, sh_in, tm=1024, tk=2048)
        dx, dxb, d_ln1 = _rms_bwd(dh, s["x"], row(w["ln1_g"][l]), dx1)

        small_grads[l] = dict(
            ln1_g=d_ln1[0], qn_a=dgains[0], kn_a=dgains[1], rpb=_na_bias_table_t(dbtab), qn_b=dgains[2],
            kn_b=dgains[3], sink=_sink_col_t(dsink_col), on_a=d_on_a[0], on_b=d_on_b[0], ln2_g=d_ln2[0],
            conv_w=_from_paired(dcw_p, sh_up), conv_b=_from_paired(dcb_p, sh_up)[0])
        above = (l, [d_in, d_out, d_up, d_dn])
    recvs = _exchange_alone("rs_sibling_exchange", _rs_sibling_exchange([d_in, d_out]))
    pair_in, pair_out = _sum_pair(d_in, recvs[0]), _sum_pair(d_out, recvs[1])
    got_in, got_out = _exchange_alone("rs_chip_exchange", _rs_chip_exchange([pair_in[1], pair_out[1]]))
    halves = [_sum_four(p[0], g) for p, g in ((pair_in, got_in), (pair_out, got_out), (pair_up, got_up), (pair_dn, got_dn))]
    update(0, _exchange_alone("rs_sibling_share", _rs_sibling_share(halves)))
    grad_x = dx[None]

    full_shapes = {n: (w[n].shape[1:] if n != "conv_w" else (3, c2)) for n in SMALL}
    packed = _pack([loss_cols] + [small_grads[l][n] for l in range(L) for n in SMALL])
    total = _sum_eight(_gather_small(packed))
    parts = _unpack(total, [(D,)] + [full_shapes[n] for _ in range(L) for n in SMALL])
    loss = 0.5 * jnp.sum(parts[0]) / D
    per_layer = [dict(zip(SMALL, parts[1 + l * len(SMALL):1 + (l + 1) * len(SMALL)])) for l in range(L)]
    for l in range(L):
        per_layer[l]["conv_w"] = lax.dynamic_slice_in_dim(per_layer[l]["conv_w"], q_me * cup, cup, axis=1)
    small_g = {n: jnp.stack([per_layer[l][n] for l in range(L)]) for n in SMALL}
    outs = _adamw(_pack([small_g[n] for n in SMALL]), _pack([w[n] for n in SMALL])[None],
                  _pack([m[n] for n in SMALL])[None], _pack([v[n] for n in SMALL])[None], 0, None)
    small_out = [dict(zip(SMALL, _unpack(o[0], [w[n].shape for n in SMALL]))) for o in outs]

    result = [loss, grad_x]
    for i in range(4):
        result += [stacked[n][i] if n in BIG else small_out[i][n] for n in WEIGHT_ORDER]
    return tuple(result)


def kernel(x, positions, ln1_g, w_in, qn_a, kn_a, rpb, qn_b, kn_b, sink, on_a, on_b, w_out, ln2_g, w_up, conv_w, conv_b, w_down, loss_target, m_ln1_g, m_w_in, m_qn_a, m_kn_a, m_rpb, m_qn_b, m_kn_b, m_sink, m_on_a, m_on_b, m_w_out, m_ln2_g, m_w_up, m_conv_w, m_conv_b, m_w_down, v_ln1_g, v_w_in, v_qn_a, v_kn_a, v_rpb, v_qn_b, v_kn_b, v_sink, v_on_a, v_on_b, v_w_out, v_ln2_g, v_w_up, v_conv_w, v_conv_b, v_w_down):
    given = dict(locals())
    w = {n: given[n] for n in WEIGHT_ORDER}
    m = {n: given["m_" + n] for n in WEIGHT_ORDER}
    v = {n: given["v_" + n] for n in WEIGHT_ORDER}
    return _train_step(x[0], positions, loss_target[0], w, m, v)
```

```python
import functools
import math

import jax
import jax.numpy as jnp
import numpy as np
from jax import lax
from jax.experimental import pallas as pl
from jax.experimental.pallas import tpu as pltpu

F32 = jnp.float32
BF16 = jnp.bfloat16

HEAD_DIM = 128
GRID_W = 64
NA_WIN_R = 8
NA_WIN_C = 16
WA_WINDOW = 128
WA_BLOCK = 128
ROPE_THETA = 10000.0
EPS = 1e-6
NEG = -1e30
ATTN_SCALE = 1.0 / math.sqrt(HEAD_DIM)

ADAM_LR = 0.001
ADAM_B1 = 0.9
ADAM_B2 = 0.999
ADAM_EPS = 1e-08
ADAM_WD = 0.01
ADAM_STEP = 10
ADAM_C1 = 1.0 - ADAM_B1 ** ADAM_STEP
ADAM_C2 = 1.0 - ADAM_B2 ** ADAM_STEP

V7X_VMEM_BYTES = 64 * 1024 * 1024
V7X_VMEM_CAP = V7X_VMEM_BYTES - 6 * 1024 * 1024
LANES = 128
SUBLANES = 8
N_CHIPS = 4
N_DEV = 8
MESH = pl.DeviceIdType.MESH

NN = ((1,), (0,))
NT = ((1,), (1,))
TN = ((0,), (0,))


def _tile(n, pref, mult):
    best = None
    d = mult
    while d <= min(n, pref):
        if n % d == 0:
            best = d
        d += mult
    return n if best is None else best


def _nbytes(shape, dtype):
    n = 1
    for s in shape:
        if s is not None:
            n *= s
    return n * jnp.dtype(dtype).itemsize


def _params(est_bytes=None, sem=None, **kw):
    if est_bytes is not None:
        kw["vmem_limit_bytes"] = int(min(V7X_VMEM_CAP, max(32 * 1024 * 1024, est_bytes * 5 // 4 + (4 << 20))))
    if sem is not None:
        kw["dimension_semantics"] = sem
    return pltpu.CompilerParams(**kw)


def _dot(a, b, contract):
    return lax.dot_general(a.astype(BF16), b.astype(BF16), (contract, ((), ())), preferred_element_type=F32)


def _mm(name, a, b, *, grid, a_spec, b_spec, o_spec, out_shape, contract, res=None, res_spec=None, carry=None,
        rider=None):
    nk = grid[2]
    n_steps = grid[0] * grid[1] * grid[2]
    acc_shape = tuple(s for s in o_spec.block_shape if s is not None)
    n_in = 2 if res is None else 3
    r_in = 0 if rider is None else len(rider.operands)
    r_out = 0 if rider is None else len(rider.out_shapes)
    x_in = 0 if carry is None else len(carry.operands)
    x_out = 0 if carry is None else carry.n_sems

    def body(*refs):
        a_ref, b_ref = refs[:2]
        r_ref = None if res is None else refs[2]
        pos = n_in
        r_ins, pos = refs[pos:pos + r_in], pos + r_in
        x_ins, pos = refs[pos:pos + x_in], pos + x_in
        o_ref, pos = refs[pos], pos + 1
        r_outs, pos = refs[pos:pos + r_out], pos + r_out
        x_outs, pos = refs[pos:pos + x_out], pos + x_out
        scr = refs[pos:]
        step = (pl.program_id(0) * grid[1] + pl.program_id(1)) * grid[2] + pl.program_id(2)
        if carry is not None:
            sems, scr = scr[-2:], scr[:-2]

            @pl.when(step == 0)
            def _():
                carry.start(x_ins, x_outs, *sems)

        p = _dot(a_ref[...], b_ref[...], contract)
        if rider is not None:
            rider.body(step, r_ins, r_outs)

        def finish(acc):
            if r_ref is not None:
                acc = acc + r_ref[...]
            o_ref[...] = acc.astype(o_ref.dtype)

        if nk == 1:
            finish(p)
        else:
            acc_ref = scr[0]
            k = pl.program_id(2)

            @pl.when(k == 0)
            def _():
                acc_ref[...] = p

            @pl.when(jnp.logical_and(k > 0, k < nk - 1))
            def _():
                acc_ref[...] += p

            @pl.when(k == nk - 1)
            def _():
                finish(acc_ref[...] + p)

        if carry is not None:
            @pl.when(step == n_steps - 1)
            def _():
                carry.wait(x_ins, x_outs, *sems)

    in_specs = [a_spec, b_spec]
    args = [a, b]
    est = 2 * (_nbytes(a_spec.block_shape, a.dtype) + _nbytes(b_spec.block_shape, b.dtype)
               + _nbytes(o_spec.block_shape, out_shape.dtype)) + 2 * _nbytes(acc_shape, F32)
    if res is not None:
        in_specs.append(res_spec)
        args.append(res)
        est += 2 * _nbytes(res_spec.block_shape, res.dtype)
    scratch = [] if nk == 1 else [pltpu.VMEM(acc_shape, F32)]
    if carry is None and rider is None:
        return pl.pallas_call(
            body, name=name, grid=grid, in_specs=in_specs, out_specs=o_spec, out_shape=out_shape,
            scratch_shapes=scratch,
            compiler_params=_params(est, ("parallel", "parallel", "arbitrary")),
        )(*args)

    def by_step(spec):
        return pl.BlockSpec(spec.block_shape, lambda i, j, k: spec.index_map((i * grid[1] + j) * grid[2] + k))

    out_specs, out_shapes, aliases = [o_spec], [out_shape], {}
    if rider is not None:
        in_specs += [by_step(s) for s in rider.in_specs]
        args += rider.operands
        out_specs += [by_step(s) for s in rider.out_specs]
        out_shapes += rider.out_shapes
        est += rider.vmem_bytes
    if carry is not None:
        aliases = {len(args) + i: len(out_shapes) + o for i, o in carry.aliases.items()}
        in_specs += [HBM_SPEC] * x_in
        args += carry.operands
        out_specs += [HBM_SPEC] * x_out
        out_shapes += carry.out_shapes
        scratch += carry.scratch()
    outs = pl.pallas_call(
        body, name=name, grid=grid, in_specs=in_specs, out_specs=out_specs, out_shape=out_shapes,
        input_output_aliases=aliases, scratch_shapes=scratch,
        compiler_params=_params(est, ("arbitrary", "arbitrary", "arbitrary")),
    )(*args)
    results = [outs[0]]
    if rider is not None:
        results.append(list(outs[1:1 + r_out]))
    if carry is not None:
        results.append(list(outs[1 + r_out:]))
    return tuple(results)


def _mm_plain(name, a, b, contract, out_dtype, *, tm, tn, tk, res=None, carry=None, rider=None):
    if contract == NN:
        (M, K), N = a.shape, b.shape[1]
    elif contract == NT:
        (M, K), N = a.shape, b.shape[0]
    else:
        (K, M), N = a.shape, b.shape[1]
    tm, tn, tk = _tile(M, tm, LANES), _tile(N, tn, LANES), _tile(K, tk, LANES)
    grid = (M // tm, N // tn, K // tk)
    if contract == TN:
        a_spec = pl.BlockSpec((tk, tm), lambda i, j, k: (k, i))
    else:
        a_spec = pl.BlockSpec((tm, tk), lambda i, j, k: (i, k))
    if contract == NT:
        b_spec = pl.BlockSpec((tn, tk), lambda i, j, k: (j, k))
    else:
        b_spec = pl.BlockSpec((tk, tn), lambda i, j, k: (k, j))
    o_spec = pl.BlockSpec((tm, tn), lambda i, j, k: (i, j))
    return _mm(name, a, b, grid=grid, a_spec=a_spec, b_spec=b_spec, o_spec=o_spec,
               out_shape=jax.ShapeDtypeStruct((M, N), out_dtype), contract=contract,
               res=res, res_spec=None if res is None else pl.BlockSpec((tm, tn), lambda i, j, k: (i, j)), carry=carry,
               rider=None if rider is None else rider(math.prod(grid)))


class _ColShards:
    def __init__(self, cols_per_chip, tn, paired):
        self.c = cols_per_chip
        self.tn = tn
        self.tps = cols_per_chip // tn
        self.ntiles = N_CHIPS * self.tps
        self.paired = paired

    def nat(self, t):
        if not self.paired:
            return t
        return (t % 2) * (self.ntiles // 2) + t // 2

    def chip(self, t):
        return self.nat(t) // self.tps

    def within(self, t):
        return self.nat(t) % self.tps


def _mm_cols_fwd(name, a, wg, sh, out_dtype, *, tm, carry=None):
    S, K = a.shape
    tm = _tile(S, tm, LANES)
    grid = (S // tm, sh.ntiles, 1)
    return _mm(name, a, wg, grid=grid,
               a_spec=pl.BlockSpec((tm, K), lambda i, j, k: (i, 0)),
               b_spec=pl.BlockSpec((None, K, sh.tn), lambda i, j, k: (sh.chip(j), 0, sh.within(j))),
               o_spec=pl.BlockSpec((tm, sh.tn), lambda i, j, k: (i, j)),
               out_shape=jax.ShapeDtypeStruct((S, sh.ntiles * sh.tn), out_dtype), contract=NN, carry=carry)


def _mm_cols_bwd_x(name, dy, wg, sh, out_dtype, *, tm, tn, carry=None):
    S = dy.shape[0]
    K = wg.shape[1]
    tm, tn = _tile(S, tm, LANES), _tile(K, tn, LANES)
    grid = (S // tm, K // tn, sh.ntiles)
    return _mm(name, dy, wg, grid=grid,
               a_spec=pl.BlockSpec((tm, sh.tn), lambda i, j, k: (i, k)),
               b_spec=pl.BlockSpec((None, tn, sh.tn), lambda i, j, k: (sh.chip(k), j, sh.within(k))),
               o_spec=pl.BlockSpec((tm, tn), lambda i, j, k: (i, j)),
               out_shape=jax.ShapeDtypeStruct((S, K), out_dtype), contract=NT, carry=carry)


def _mm_cols_bwd_w(name, a, dy, sh, *, tm, tk, carry=None, rider=None):
    S, K = a.shape
    tm, tk = _tile(K, tm, LANES), _tile(S, tk, LANES)
    grid = (K // tm, sh.ntiles, S // tk)
    return _mm(name, a, dy, grid=grid,
               a_spec=pl.BlockSpec((tk, tm), lambda i, j, k: (k, i)),
               b_spec=pl.BlockSpec((tk, sh.tn), lambda i, j, k: (k, j)),
               o_spec=pl.BlockSpec((None, tm, sh.tn), lambda i, j, k: (sh.chip(j), i, sh.within(j))),
               out_shape=jax.ShapeDtypeStruct((N_CHIPS, K, sh.c), F32), contract=TN, carry=carry,
               rider=None if rider is None else rider(math.prod(grid)))


def _row_spec(tr, width):
    return pl.BlockSpec((tr, width), lambda i: (i, 0))


def _full_spec(shape):
    nd = len(shape)
    return pl.BlockSpec(shape, lambda i: (0,) * nd)


def _rms_fwd(x, g):
    S, D = x.shape
    tr = _tile(S, 512, 16)

    def body(x_ref, g_ref, h_ref):
        xv = x_ref[...]
        r = lax.rsqrt(jnp.mean(xv * xv, axis=-1, keepdims=True) + EPS)
        h_ref[...] = (xv * r * g_ref[...]).astype(BF16)

    return pl.pallas_call(
        body, name="rms_fwd", grid=(S // tr,),
        in_specs=[_row_spec(tr, D), _full_spec((1, D))], out_specs=_row_spec(tr, D),
        out_shape=jax.ShapeDtypeStruct((S, D), BF16),
        compiler_params=_params(12 * tr * D, ("parallel",)),
    )(x, g)


class _Rider:
    def __init__(self, operands, in_specs, out_specs, out_shapes, body, vmem_bytes):
        self.operands, self.in_specs, self.out_specs = list(operands), list(in_specs), list(out_specs)
        self.out_shapes, self.body, self.vmem_bytes = list(out_shapes), body, vmem_bytes


def _accumulate(step, ref, part):
    @pl.when(step == 0)
    def _():
        ref[...] = part

    @pl.when(step > 0)
    def _():
        ref[...] += part


def _rms_bwd(dh, x, g, dres, steps):
    S, D = x.shape
    tr = S // steps

    def body(step, ins, outs):
        dh_ref, x_ref, g_ref, dres_ref = ins
        dx_ref, dxb_ref, dg_ref = outs
        xv = x_ref[...]
        dhv = dh_ref[...]
        r = lax.rsqrt(jnp.mean(xv * xv, axis=-1, keepdims=True) + EPS)
        gy = dhv * g_ref[...]
        dot = jnp.mean(xv * gy, axis=-1, keepdims=True)
        dx = dres_ref[...] + (r * gy - xv * (r * r * r * dot))
        dx_ref[...] = dx
        dxb_ref[...] = dx.astype(BF16)
        _accumulate(step, dg_ref, jnp.sum(dhv * (xv * r), axis=0, keepdims=True))

    return _Rider([dh, x, g, dres],
                  [_row_spec(tr, D), _row_spec(tr, D), _full_spec((1, D)), _row_spec(tr, D)],
                  [_row_spec(tr, D), _row_spec(tr, D), _full_spec((1, D))],
                  [jax.ShapeDtypeStruct((S, D), F32), jax.ShapeDtypeStruct((S, D), BF16),
                   jax.ShapeDtypeStruct((1, D), F32)], body, 40 * tr * D)


def _out_norm_fwd(oa, ob, ga, gb):
    S, W = oa.shape
    tr = _tile(S, 512, 16)

    def body(oa_ref, ob_ref, ga_ref, gb_ref, o_ref):
        for src, g_ref, off in ((oa_ref, ga_ref, 0), (ob_ref, gb_ref, W)):
            v = src[...]
            r = lax.rsqrt(jnp.mean(v * v, axis=-1, keepdims=True) + EPS)
            o_ref[:, off:off + W] = (v * r * g_ref[...]).astype(BF16)

    return pl.pallas_call(
        body, name="out_norm_fwd", grid=(S // tr,),
        in_specs=[_row_spec(tr, W), _row_spec(tr, W), _full_spec((1, W)), _full_spec((1, W))],
        out_specs=_row_spec(tr, 2 * W), out_shape=jax.ShapeDtypeStruct((S, 2 * W), BF16),
        compiler_params=_params(24 * tr * W, ("parallel",)),
    )(oa, ob, ga, gb)


def _out_norm_bwd(do, oa, ob, ga, gb, steps):
    S, W = oa.shape
    tr = S // steps

    def body(step, ins, outs):
        do_ref, oa_ref, ob_ref, ga_ref, gb_ref = ins
        doa_ref, dob_ref, dga_ref, dgb_ref = outs
        parts = []
        for src, g_ref, off, d_ref in ((oa_ref, ga_ref, 0, doa_ref), (ob_ref, gb_ref, W, dob_ref)):
            v = src[...]
            dv = do_ref[:, off:off + W]
            r = lax.rsqrt(jnp.mean(v * v, axis=-1, keepdims=True) + EPS)
            gy = dv * g_ref[...]
            dot = jnp.mean(v * gy, axis=-1, keepdims=True)
            d_ref[...] = (r * gy - v * (r * r * r * dot)).astype(BF16)
            parts.append(jnp.sum(dv * (v * r), axis=0, keepdims=True))
        _accumulate(step, dga_ref, parts[0])
        _accumulate(step, dgb_ref, parts[1])

    return _Rider([do, oa, ob, ga, gb],
                  [_row_spec(tr, 2 * W), _row_spec(tr, W), _row_spec(tr, W), _full_spec((1, W)), _full_spec((1, W))],
                  [_row_spec(tr, W), _row_spec(tr, W), _full_spec((1, W)), _full_spec((1, W))],
                  [jax.ShapeDtypeStruct((S, W), BF16), jax.ShapeDtypeStruct((S, W), BF16),
                   jax.ShapeDtypeStruct((1, W), F32), jax.ShapeDtypeStruct((1, W), F32)], body, 48 * tr * W)


def _loss_grad(y, t):
    S, D = y.shape
    tr = _tile(S, 256, 16)

    def body(y_ref, t_ref, dy_ref, dyb_ref, ls_ref):
        e = y_ref[...] - t_ref[...]
        dy = e * (1.0 / D)
        dy_ref[...] = dy
        dyb_ref[...] = dy.astype(BF16)
        part = jnp.sum(e * e, axis=0, keepdims=True)

        @pl.when(pl.program_id(0) == 0)
        def _():
            ls_ref[...] = part

        @pl.when(pl.program_id(0) > 0)
        def _():
            ls_ref[...] += part

    return pl.pallas_call(
        body, name="loss_grad", grid=(S // tr,),
        in_specs=[_row_spec(tr, D), _row_spec(tr, D)],
        out_specs=[_row_spec(tr, D), _row_spec(tr, D), _full_spec((1, D))],
        out_shape=[jax.ShapeDtypeStruct((S, D), F32), jax.ShapeDtypeStruct((S, D), BF16),
                   jax.ShapeDtypeStruct((1, D), F32)],
        compiler_params=_params(32 * tr * D, ("arbitrary",)),
    )(y, t)


def _head_rms(x, g):
    r = lax.rsqrt(jnp.mean(x * x, axis=-1, keepdims=True) + EPS)
    return x * r * g


def _head_rms_bwd(x, dy, g):
    r = lax.rsqrt(jnp.mean(x * x, axis=-1, keepdims=True) + EPS)
    gy = dy * g
    dot = jnp.mean(x * gy, axis=-1, keepdims=True)
    return r * gy - x * (r * r * r * dot), dy * (x * r)


def _rope(y, cos2, sgn_sin):
    return y * cos2 + pltpu.roll(y, HEAD_DIM // 2, axis=1) * sgn_sin


def _rope_t(dy, cos2, sgn_sin):
    return dy * cos2 + pltpu.roll(dy * sgn_sin, HEAD_DIM // 2, axis=1)


def _qk_prep_fwd(proj, cos2, sgn_sin, gains, dims):
    S, P = proj.shape
    naw, waw, kvw = dims
    tr = _tile(S, 256, 16)
    hd = HEAD_DIM

    def body(p_ref, c_ref, s_ref, g_ref, qa_ref, ka_ref, va_ref, qb_ref, kb_ref, vb_ref):
        c2 = c_ref[...]
        ss = s_ref[...]
        off = 0
        for dst, width, gi, rot in ((qa_ref, naw, 0, False), (ka_ref, naw, 1, False), (va_ref, naw, None, False),
                                    (qb_ref, waw, 2, True), (kb_ref, kvw, 3, True), (vb_ref, kvw, None, False)):
            for h in range(width // hd):
                xh = p_ref[:, off + h * hd:off + (h + 1) * hd]
                if gi is not None:
                    xh = _head_rms(xh, g_ref[gi:gi + 1, :])
                    if rot:
                        xh = _rope(xh, c2, ss)
                dst[:, h * hd:(h + 1) * hd] = xh.astype(BF16)
            off += width

    widths = (naw, naw, naw, waw, kvw, kvw)
    return pl.pallas_call(
        body, name="qk_prep_fwd", grid=(S // tr,),
        in_specs=[_row_spec(tr, P), _row_spec(tr, hd), _row_spec(tr, hd), _full_spec((SUBLANES, hd))],
        out_specs=[_row_spec(tr, w) for w in widths],
        out_shape=[jax.ShapeDtypeStruct((S, w), BF16) for w in widths],
        compiler_params=_params(16 * tr * P, ("parallel",)),
    )(proj, cos2, sgn_sin, gains)


def _qk_prep_bwd(proj, dqa, dka, dva, dqb, dkb, dvb, cos2, sgn_sin, gains, dims, steps):
    S, P = proj.shape
    naw, waw, kvw = dims
    tr = S // steps
    hd = HEAD_DIM

    def body(step, ins, outs):
        p_ref, dqa_ref, dka_ref, dva_ref, dqb_ref, dkb_ref, dvb_ref, c_ref, s_ref, g_ref = ins
        dp_ref, dg_ref = outs
        c2 = c_ref[...]
        ss = s_ref[...]
        off = 0
        dgs = [jnp.zeros((1, hd), F32) for _ in range(4)]
        for src, width, gi, rot in ((dqa_ref, naw, 0, False), (dka_ref, naw, 1, False), (dva_ref, naw, None, False),
                                    (dqb_ref, waw, 2, True), (dkb_ref, kvw, 3, True), (dvb_ref, kvw, None, False)):
            for h in range(width // hd):
                dy = src[:, h * hd:(h + 1) * hd].astype(F32)
                if gi is not None:
                    if rot:
                        dy = _rope_t(dy, c2, ss)
                    xh = p_ref[:, off + h * hd:off + (h + 1) * hd]
                    dy, dgt = _head_rms_bwd(xh, dy, g_ref[gi:gi + 1, :])
                    dgs[gi] = dgs[gi] + jnp.sum(dgt, axis=0, keepdims=True)
                dp_ref[:, off + h * hd:off + (h + 1) * hd] = dy.astype(BF16)
            off += width
        _accumulate(step, dg_ref, jnp.concatenate(dgs + [jnp.zeros((SUBLANES - 4, hd), F32)], axis=0))

    return _Rider([proj, dqa, dka, dva, dqb, dkb, dvb, cos2, sgn_sin, gains],
                  [_row_spec(tr, P), _row_spec(tr, naw), _row_spec(tr, naw), _row_spec(tr, naw),
                   _row_spec(tr, waw), _row_spec(tr, kvw), _row_spec(tr, kvw),
                   _row_spec(tr, hd), _row_spec(tr, hd), _full_spec((SUBLANES, hd))],
                  [_row_spec(tr, P), _full_spec((SUBLANES, hd))],
                  [jax.ShapeDtypeStruct((S, P), BF16), jax.ShapeDtypeStruct((SUBLANES, hd), F32)], body, 24 * tr * P)


NA_KEYS = NA_WIN_R * GRID_W
NA_ROWS_PER_STEP = 8


def _na_col_geometry():
    c = np.arange(GRID_W)
    col_start = np.clip(c - NA_WIN_C // 2, 0, GRID_W - NA_WIN_C)
    mask = (c[None, :] >= col_start[:, None]) & (c[None, :] < col_start[:, None] + NA_WIN_C)
    dc = np.clip(c[None, :] - c[:, None], -(NA_WIN_C - 1), NA_WIN_C - 1) + (NA_WIN_C - 1)
    onehot = (dc[:, :, None] == np.arange(2 * NA_WIN_C - 1)[None, None, :]) & mask[:, :, None]
    return mask, onehot


def _na_bias_table(rpb_l):
    H = rpb_l.shape[0]
    mask, onehot = _na_col_geometry()
    t = jnp.sum(jnp.where(onehot[None, None], rpb_l[:, :, None, None, :], 0.0), axis=-1)
    t = jnp.where(mask[None, None], t, NEG)
    per_delta = [jnp.transpose(t[:, d:d + NA_WIN_R], (0, 2, 1, 3)).reshape(H, GRID_W, NA_KEYS) for d in range(NA_WIN_R)]
    return jnp.stack(per_delta, axis=1)


def _na_bias_table_t(db):
    H = db.shape[0]
    _, onehot = _na_col_geometry()
    d5 = db.reshape(H, NA_WIN_R, GRID_W, NA_WIN_R, GRID_W)
    folded = jnp.einsum("hdqwk,qkc->hdwc", d5, onehot.astype(np.float32), precision=lax.Precision.HIGHEST)
    return sum(jnp.pad(folded[:, d], ((0, 0), (d, NA_WIN_R - 1 - d), (0, 0))) for d in range(NA_WIN_R))


def _na_row_geometry(r, rows):
    start = jnp.clip(r - NA_WIN_R // 2, 0, rows - NA_WIN_R)
    return start, start - r + (NA_WIN_R - 1)


def _softmax_rows(s):
    m = jnp.max(s, axis=-1, keepdims=True)
    e = jnp.exp(s - m)
    return e / jnp.sum(e, axis=-1, keepdims=True)


def _na_fwd(qa, ka, va, btab):
    S, W = qa.shape
    H = W // HEAD_DIM
    rows = S // GRID_W
    assert rows >= NA_WIN_R
    rb = _tile(rows, NA_ROWS_PER_STEP, 1)
    tq = rb * GRID_W

    def body(q_ref, k_ref, v_ref, b_ref, o_ref):
        i = pl.program_id(1)

        geo = [_na_row_geometry(i * rb + j, rows) for j in range(rb)]
        toks = [pl.ds(j * GRID_W, GRID_W) for j in range(rb)]
        wins = [pl.ds(pl.multiple_of(start * GRID_W, GRID_W), NA_KEYS) for start, _ in geo]
        scores = [_dot(q_ref[toks[j], :], k_ref[wins[j], :], NT) for j in range(rb)]
        probs = [_softmax_rows(scores[j] * ATTN_SCALE + b_ref[geo[j][1]]) for j in range(rb)]
        for j in range(rb):
            o_ref[toks[j], :] = _dot(probs[j], v_ref[wins[j], :], NN)

    kv_spec = pl.BlockSpec((S, HEAD_DIM), lambda h, i: (0, h))
    return pl.pallas_call(
        body, name="na_fwd", grid=(H, rows // rb),
        in_specs=[pl.BlockSpec((tq, HEAD_DIM), lambda h, i: (i, h)), kv_spec, kv_spec,
                  pl.BlockSpec((None, NA_WIN_R, GRID_W, NA_KEYS), lambda h, i: (h, 0, 0, 0))],
        out_specs=pl.BlockSpec((tq, HEAD_DIM), lambda h, i: (i, h)),
        out_shape=jax.ShapeDtypeStruct((S, W), F32),
        compiler_params=_params(8 * S * HEAD_DIM + (8 << 20), ("parallel", "arbitrary")),
    )(qa, ka, va, btab)


def _na_bwd(qa, ka, va, btab, doa):
    S, W = qa.shape
    H = W // HEAD_DIM
    rows = S // GRID_W
    rb = _tile(rows, NA_ROWS_PER_STEP, 1)
    tq = rb * GRID_W

    def body(q_ref, k_ref, v_ref, b_ref, do_ref, dq_ref, dk_ref, dv_ref, db_ref):
        i = pl.program_id(1)

        @pl.when(i == 0)
        def _():
            dk_ref[...] = jnp.zeros_like(dk_ref)
            dv_ref[...] = jnp.zeros_like(dv_ref)
            db_ref[...] = jnp.zeros_like(db_ref)

        steps = range(rb)
        geo = [_na_row_geometry(i * rb + j, rows) for j in steps]
        toks = [pl.ds(j * GRID_W, GRID_W) for j in steps]
        wins = [pl.ds(pl.multiple_of(start * GRID_W, GRID_W), NA_KEYS) for start, _ in geo]
        scores = [_dot(q_ref[toks[j], :], k_ref[wins[j], :], NT) for j in steps]
        dps = [_dot(do_ref[toks[j], :], v_ref[wins[j], :], NT) for j in steps]
        probs = [_softmax_rows(scores[j] * ATTN_SCALE + b_ref[geo[j][1]]) for j in steps]
        dss = [probs[j] * (dps[j] - jnp.sum(probs[j] * dps[j], axis=-1, keepdims=True)) for j in steps]
        for j in steps:
            db_ref[geo[j][1]] += dss[j]
        dsb = [(dss[j] * ATTN_SCALE).astype(BF16) for j in steps]
        for j in steps:
            dq_ref[toks[j], :] = _dot(dsb[j], k_ref[wins[j], :], NN)
        dks = [_dot(dsb[j], q_ref[toks[j], :], TN) for j in steps]
        dvs = [_dot(probs[j], do_ref[toks[j], :], TN) for j in steps]
        for j in steps:
            dk_ref[wins[j], :] += dks[j]
            dv_ref[wins[j], :] += dvs[j]

    kv_spec = pl.BlockSpec((S, HEAD_DIM), lambda h, i: (0, h))
    q_spec = pl.BlockSpec((tq, HEAD_DIM), lambda h, i: (i, h))
    b_spec = pl.BlockSpec((None, NA_WIN_R, GRID_W, NA_KEYS), lambda h, i: (h, 0, 0, 0))
    return pl.pallas_call(
        body, name="na_bwd", grid=(H, rows // rb),
        in_specs=[q_spec, kv_spec, kv_spec, b_spec, q_spec],
        out_specs=[q_spec, kv_spec, kv_spec, b_spec],
        out_shape=[jax.ShapeDtypeStruct((S, W), F32), jax.ShapeDtypeStruct((S, W), F32),
                   jax.ShapeDtypeStruct((S, W), F32), jax.ShapeDtypeStruct(btab.shape, F32)],
        compiler_params=_params(24 * S * HEAD_DIM + (12 << 20), ("parallel", "arbitrary")),
    )(qa, ka, va, btab, doa)


WA_KEYS = 3 * WA_BLOCK
WA_BLOCKS_PER_STEP = 4


def _wa_mask(qk, n, start, g):
    s = qk * ATTN_SCALE
    qpos = n * WA_BLOCK + lax.broadcasted_iota(jnp.int32, (WA_BLOCK, WA_KEYS), 0)
    kpos = start + lax.broadcasted_iota(jnp.int32, (WA_BLOCK, WA_KEYS), 1)
    valid = jnp.abs(kpos - qpos) <= WA_WINDOW
    valid = jnp.concatenate([valid] * g, axis=0)
    return jnp.where(valid, s, NEG)


def _wa_probs(s, sink):
    m = jnp.maximum(jnp.max(s, axis=-1, keepdims=True), sink)
    e = jnp.exp(s - m)
    es = jnp.exp(sink - m)
    den = jnp.sum(e, axis=-1, keepdims=True) + es
    return e / den, es / den


def _wa_stack(ref, tok, g):
    return jnp.concatenate([ref[tok, t * HEAD_DIM:(t + 1) * HEAD_DIM] for t in range(g)], axis=0)


def _wa_fwd(qb, kb, vb, sink_col):
    S, W = qb.shape
    hkv = kb.shape[1] // HEAD_DIM
    g = W // HEAD_DIM // hkv
    nb = S // WA_BLOCK
    assert S >= WA_KEYS
    qb_step = _tile(nb, WA_BLOCKS_PER_STEP, 1)
    tq = qb_step * WA_BLOCK

    def body(q_ref, k_ref, v_ref, s_ref, o_ref):
        i = pl.program_id(1)

        steps = range(qb_step)
        ns = [i * qb_step + j for j in steps]
        toks = [pl.ds(j * WA_BLOCK, WA_BLOCK) for j in steps]
        starts = [pl.multiple_of(jnp.clip((n - 1) * WA_BLOCK, 0, S - WA_KEYS), WA_BLOCK) for n in ns]
        wins = [pl.ds(start, WA_KEYS) for start in starts]
        scores = [_dot(_wa_stack(q_ref, toks[j], g), k_ref[wins[j], :], NT) for j in steps]
        probs = [_wa_probs(_wa_mask(scores[j], ns[j], starts[j], g), s_ref[...])[0] for j in steps]
        outs = [_dot(probs[j], v_ref[wins[j], :], NN) for j in steps]
        for j in steps:
            for t in range(g):
                o_ref[toks[j], t * HEAD_DIM:(t + 1) * HEAD_DIM] = outs[j][t * WA_BLOCK:(t + 1) * WA_BLOCK]

    kv_spec = pl.BlockSpec((S, HEAD_DIM), lambda h, i: (0, h))
    q_spec = pl.BlockSpec((tq, g * HEAD_DIM), lambda h, i: (i, h))
    return pl.pallas_call(
        body, name="wa_fwd", grid=(hkv, nb // qb_step),
        in_specs=[q_spec, kv_spec, kv_spec, pl.BlockSpec((None, g * WA_BLOCK, 1), lambda h, i: (h, 0, 0))],
        out_specs=q_spec, out_shape=jax.ShapeDtypeStruct((S, W), F32),
        compiler_params=_params(8 * S * HEAD_DIM + (12 << 20), ("parallel", "arbitrary")),
    )(qb, kb, vb, sink_col)


def _wa_bwd(qb, kb, vb, sink_col, dob):
    S, W = qb.shape
    KW = kb.shape[1]
    hkv = KW // HEAD_DIM
    g = W // HEAD_DIM // hkv
    nb = S // WA_BLOCK
    qb_step = _tile(nb, WA_BLOCKS_PER_STEP, 1)
    tq = qb_step * WA_BLOCK

    def body(q_ref, k_ref, v_ref, s_ref, do_ref, dq_ref, dk_ref, dv_ref, dsink_ref):
        i = pl.program_id(1)

        @pl.when(i == 0)
        def _():
            dk_ref[...] = jnp.zeros_like(dk_ref)
            dv_ref[...] = jnp.zeros_like(dv_ref)
            dsink_ref[...] = jnp.zeros_like(dsink_ref)

        steps = range(qb_step)
        ns = [i * qb_step + j for j in steps]
        toks = [pl.ds(j * WA_BLOCK, WA_BLOCK) for j in steps]
        starts = [pl.multiple_of(jnp.clip((n - 1) * WA_BLOCK, 0, S - WA_KEYS), WA_BLOCK) for n in ns]
        wins = [pl.ds(start, WA_KEYS) for start in starts]
        qss = [_wa_stack(q_ref, toks[j], g) for j in steps]
        doss = [_wa_stack(do_ref, toks[j], g) for j in steps]
        scores = [_dot(qss[j], k_ref[wins[j], :], NT) for j in steps]
        dps = [_dot(doss[j], v_ref[wins[j], :], NT) for j in steps]
        pp = [_wa_probs(_wa_mask(scores[j], ns[j], starts[j], g), s_ref[...]) for j in steps]
        dsums = [jnp.sum(pp[j][0] * dps[j], axis=-1, keepdims=True) for j in steps]
        dsb = [(pp[j][0] * (dps[j] - dsums[j]) * ATTN_SCALE).astype(BF16) for j in steps]
        dsink_ref[...] -= sum(pp[j][1] * dsums[j] for j in steps)
        dqs = [_dot(dsb[j], k_ref[wins[j], :], NN) for j in steps]
        dks = [_dot(dsb[j], qss[j], TN) for j in steps]
        dvs = [_dot(pp[j][0], doss[j], TN) for j in steps]
        for j in steps:
            for t in range(g):
                dq_ref[toks[j], t * HEAD_DIM:(t + 1) * HEAD_DIM] = dqs[j][t * WA_BLOCK:(t + 1) * WA_BLOCK]
            dk_ref[wins[j], :] += dks[j]
            dv_ref[wins[j], :] += dvs[j]

    kv_spec = pl.BlockSpec((S, HEAD_DIM), lambda h, i: (0, h))
    q_spec = pl.BlockSpec((tq, g * HEAD_DIM), lambda h, i: (i, h))
    s_spec = pl.BlockSpec((None, g * WA_BLOCK, 1), lambda h, i: (h, 0, 0))
    return pl.pallas_call(
        body, name="wa_bwd", grid=(hkv, nb // qb_step),
        in_specs=[q_spec, kv_spec, kv_spec, s_spec, q_spec],
        out_specs=[q_spec, kv_spec, kv_spec, s_spec],
        out_shape=[jax.ShapeDtypeStruct((S, W), F32), jax.ShapeDtypeStruct((S, KW), F32),
                   jax.ShapeDtypeStruct((S, KW), F32), jax.ShapeDtypeStruct(sink_col.shape, F32)],
        compiler_params=_params(24 * S * HEAD_DIM + (16 << 20), ("parallel", "arbitrary")),
    )(qb, kb, vb, sink_col, dob)


CONV_HALO = SUBLANES


def _conv_specs(tr, width, nblk, rows_inner):
    per = tr // CONV_HALO

    def spec(shape, row_block):
        if rows_inner:
            return pl.BlockSpec(shape, lambda j, i: (row_block(i), j))
        return pl.BlockSpec(shape, lambda i, j: (row_block(i), j))

    cur = spec((tr, width), lambda i: i)
    prev = spec((CONV_HALO, width), lambda i: jnp.maximum(i * per - 1, 0))
    nxt = spec((CONV_HALO, width), lambda i: jnp.minimum((i + 1) * per, nblk * per - 1))
    return prev, cur, nxt


def _extend(prev_ref, cur_ref, next_ref, i, nblk):
    p = jnp.where(i > 0, prev_ref[...].astype(F32), 0.0)
    n = jnp.where(i < nblk - 1, next_ref[...].astype(F32), 0.0)
    return jnp.concatenate([p, cur_ref[...].astype(F32), n], axis=0)


def _shift_down(x):
    return pltpu.roll(x, 1, axis=0)


def _shift_up(x):
    return pltpu.roll(x, x.shape[0] - 1, axis=0)


def _conv(ext, w_ref, b_ref):
    return _shift_down(ext) * w_ref[0:1, :] + ext * w_ref[1:2, :] + _shift_up(ext) * w_ref[2:3, :] + b_ref[...]


def _sigmoid(x):
    return 1.0 / (1.0 + jnp.exp(-x))


def _convgate_fwd(up_pre, cw, cb, tc):
    S, C2 = up_pre.shape
    tr = _tile(S, 256, 16)
    nblk = S // tr
    prev, cur, nxt = _conv_specs(tr, 2 * tc, nblk, False)

    def body(p_ref, c_ref, n_ref, w_ref, b_ref, a_ref):
        i = pl.program_id(0)
        u = _conv(_extend(p_ref, c_ref, n_ref, i, nblk), w_ref, b_ref)[CONV_HALO:CONV_HALO + tr]
        gate, up = u[:, :tc], u[:, tc:]
        a_ref[...] = (gate * _sigmoid(gate) * up).astype(BF16)

    return pl.pallas_call(
        body, name="convgate_fwd", grid=(nblk, C2 // (2 * tc)),
        in_specs=[prev, cur, nxt, pl.BlockSpec((3, 2 * tc), lambda i, j: (0, j)),
                  pl.BlockSpec((1, 2 * tc), lambda i, j: (0, j))],
        out_specs=pl.BlockSpec((tr, tc), lambda i, j: (i, j)),
        out_shape=jax.ShapeDtypeStruct((S, C2 // 2), BF16),
        compiler_params=_params(48 * tr * tc, ("parallel", "parallel")),
    )(up_pre, up_pre, up_pre, cw, cb)


def _convgate_bwd(up_pre, dact, cw, cb, tc):
    S, C2 = up_pre.shape
    tr = _tile(S, 128, 16)
    nblk = S // tr
    prev, cur, nxt = _conv_specs(tr, 2 * tc, nblk, True)
    dprev, dcur, dnxt = _conv_specs(tr, tc, nblk, True)

    def body(p_ref, c_ref, n_ref, dp_ref, dc_ref, dn_ref, w_ref, b_ref, dx_ref, dw_ref, db_ref):
        i = pl.program_id(1)
        ext = _extend(p_ref, c_ref, n_ref, i, nblk)
        da = _extend(dp_ref, dc_ref, dn_ref, i, nblk)
        ext_dn, ext_up = _shift_down(ext), _shift_up(ext)
        u = ext_dn * w_ref[0:1, :] + ext * w_ref[1:2, :] + ext_up * w_ref[2:3, :] + b_ref[...]
        gate, up = u[:, :tc], u[:, tc:]
        sg = _sigmoid(gate)
        silu = gate * sg
        du = jnp.concatenate([da * up * (sg + silu * (1.0 - sg)), da * silu], axis=1)
        dx = _shift_up(du) * w_ref[0:1, :] + du * w_ref[1:2, :] + _shift_down(du) * w_ref[2:3, :]
        mid = slice(CONV_HALO, CONV_HALO + tr)
        dx_ref[...] = dx[mid].astype(BF16)
        duc = du[mid]
        dw = jnp.concatenate([jnp.sum(duc * ext_dn[mid], axis=0, keepdims=True),
                              jnp.sum(duc * ext[mid], axis=0, keepdims=True),
                              jnp.sum(duc * ext_up[mid], axis=0, keepdims=True)], axis=0)
        db = jnp.sum(duc, axis=0, keepdims=True)

        @pl.when(i == 0)
        def _():
            dw_ref[...] = dw
            db_ref[...] = db

        @pl.when(i > 0)
        def _():
            dw_ref[...] += dw
            db_ref[...] += db

    return pl.pallas_call(
        body, name="convgate_bwd", grid=(C2 // (2 * tc), nblk),
        in_specs=[prev, cur, nxt, dprev, dcur, dnxt,
                  pl.BlockSpec((3, 2 * tc), lambda j, i: (0, j)), pl.BlockSpec((1, 2 * tc), lambda j, i: (0, j))],
        out_specs=[pl.BlockSpec((tr, 2 * tc), lambda j, i: (i, j)),
                   pl.BlockSpec((3, 2 * tc), lambda j, i: (0, j)), pl.BlockSpec((1, 2 * tc), lambda j, i: (0, j))],
        out_shape=[jax.ShapeDtypeStruct((S, C2), BF16), jax.ShapeDtypeStruct((3, C2), F32),
                   jax.ShapeDtypeStruct((1, C2), F32)],
        compiler_params=_params(160 * tr * tc, ("parallel", "arbitrary")),
    )(up_pre, up_pre, up_pre, dact, dact, dact, cw, cb)


def _rope_tables(positions):
    inv = ROPE_THETA ** (-jnp.arange(0, HEAD_DIM, 2, dtype=F32) / HEAD_DIM)
    ang = positions.astype(F32)[:, None] * inv[None, :]
    cos, sin = jnp.cos(ang), jnp.sin(ang)
    return jnp.concatenate([cos, cos], axis=1), jnp.concatenate([-sin, sin], axis=1)


def _sink_col(sink_l, hkv):
    g = sink_l.shape[0] // hkv
    return jnp.broadcast_to(sink_l.reshape(hkv, g, 1), (hkv, g, WA_BLOCK)).reshape(hkv, g * WA_BLOCK, 1)


def _sink_col_t(dcol):
    hkv, rows, _ = dcol.shape
    return jnp.sum(dcol.reshape(hkv, rows // WA_BLOCK, WA_BLOCK), axis=-1).reshape(-1)


def _to_paired(a, sh):
    return jnp.concatenate([a[:, sh.nat(t) * sh.tn:(sh.nat(t) + 1) * sh.tn] for t in range(sh.ntiles)], axis=1)


def _from_paired(a, sh):
    pos = {sh.nat(t): t for t in range(sh.ntiles)}
    return jnp.concatenate([a[:, pos[n] * sh.tn:(pos[n] + 1) * sh.tn] for n in range(sh.ntiles)], axis=1)


def _pack(arrs):
    flat = jnp.concatenate([a.reshape(-1).astype(F32) for a in arrs])
    unit = SUBLANES * LANES
    total = -(-flat.shape[0] // unit) * unit
    return jnp.pad(flat, (0, total - flat.shape[0])).reshape(-1, LANES)


def _unpack(buf, shapes):
    flat = buf.reshape(-1)
    out, off = [], 0
    for s in shapes:
        n = math.prod(s)
        out.append(flat[off:off + n].reshape(s))
        off += n
    return out


HBM_SPEC = pl.BlockSpec(memory_space=pltpu.HBM)
DMA_CHUNK_BYTES = 512 * 1024


def _row_chunks(rows, row_bytes, align):
    want = max(1, rows * row_bytes // DMA_CHUNK_BYTES)
    count = max(k for k in range(1, rows + 1) if rows % k == 0 and (rows // k) % align == 0 and (k <= want or k == 1))
    size = rows // count
    return [(j * size, size) for j in range(count)]


def _mesh_pos():
    return lax.axis_index("x"), lax.axis_index("y"), lax.axis_index("c")


def _other_chips(x, y):
    return [(1 - x, y), (x, 1 - y), (1 - x, 1 - y)]


def _remote(src, dst, send_sems, recv_sems, k, dev):
    return pltpu.make_async_remote_copy(src_ref=src, dst_ref=dst, send_sem=send_sems.at[k], recv_sem=recv_sems.at[k],
                                        device_id=dev, device_id_type=MESH)


class _Exchange:
    def __init__(self, operands, out_shapes, aliases, start, wait):
        self.operands, self.out_shapes, self.aliases = list(operands), list(out_shapes), dict(aliases)
        self.start, self.wait = start, wait
        self.n_sems = len(out_shapes)

    def scratch(self):
        return [pltpu.SemaphoreType.DMA((self.n_sems,)), pltpu.SemaphoreType.DMA((self.n_sems,))]


class _SemSlice:
    def __init__(self, sems, offset):
        self._sems, self._offset = sems, offset

    @property
    def at(self):
        return self

    def __getitem__(self, k):
        return self._sems.at[self._offset + k]


def _both(a, b):
    ia, oa = len(a.operands), a.n_sems

    def split(ins, outs, send_sems, recv_sems):
        return ((ins[:ia], outs[:oa], send_sems, recv_sems),
                (ins[ia:], outs[oa:], _SemSlice(send_sems, oa), _SemSlice(recv_sems, oa)))

    def start(*refs):
        first, second = split(*refs)
        a.start(*first)
        b.start(*second)

    def wait(*refs):
        first, second = split(*refs)
        a.wait(*first)
        b.wait(*second)

    aliases = dict(a.aliases)
    aliases.update({ia + i: oa + o for i, o in b.aliases.items()})
    return _Exchange(a.operands + b.operands, a.out_shapes + b.out_shapes, aliases, start, wait)


def _exchange_alone(name, ex):
    n_in = len(ex.operands)

    def body(*refs):
        ins, outs = refs[:n_in], refs[n_in:n_in + ex.n_sems]
        send_sems, recv_sems = refs[n_in + ex.n_sems:]
        ex.start(ins, outs, send_sems, recv_sems)
        ex.wait(ins, outs, send_sems, recv_sems)

    return pl.pallas_call(
        body, name=name, in_specs=[HBM_SPEC] * n_in, out_specs=[HBM_SPEC] * ex.n_sems, out_shape=ex.out_shapes,
        input_output_aliases=ex.aliases, scratch_shapes=ex.scratch(),
    )(*ex.operands)


def _ag_chip_exchange(gathered, small):
    n = len(gathered)
    arrays = list(gathered) + ([] if small is None else [small])

    def start(ins, outs, send_sems, recv_sems):
        x, y, c = _mesh_pos()
        q_me = 2 * x + y
        chips = _other_chips(x, y)
        for t in range(n):
            h = gathered[t].shape[1] // 2
            for s0, sz in _row_chunks(h, gathered[t].shape[2] * 2, 16):
                blk = outs[t].at[q_me, pl.ds(c * h + s0, sz)]
                for chip in chips:
                    _remote(blk, blk, send_sems, recv_sems, t, (*chip, c)).start()
        if small is not None:
            for chip in chips:
                _remote(outs[n].at[q_me], outs[n].at[q_me], send_sems, recv_sems, n, (*chip, c)).start()

    def wait(ins, outs, send_sems, recv_sems):
        x, y, c = _mesh_pos()
        for t in range(n):
            three = outs[t].at[pl.ds(0, 3), pl.ds(0, gathered[t].shape[1] // 2)]
            _remote(three, three, send_sems, recv_sems, t, (x, y, 1 - c)).wait()
        if small is not None:
            three = outs[n].at[pl.ds(0, 3)]
            _remote(three, three, send_sems, recv_sems, n, (x, y, 1 - c)).wait()

    return _Exchange(arrays, [jax.ShapeDtypeStruct(a.shape, a.dtype) for a in arrays],
                     {i: i for i in range(len(arrays))}, start, wait)


def _ag_sibling_pass(gathered):
    n = len(gathered)

    def start(ins, outs, send_sems, recv_sems):
        x, y, c = _mesh_pos()
        for t in range(n):
            h = gathered[t].shape[1] // 2
            for s0, sz in _row_chunks(h, gathered[t].shape[2] * 2, 16):
                for cx, cy in _other_chips(x, y):
                    blk = outs[t].at[2 * cx + cy, pl.ds(c * h + s0, sz)]
                    _remote(blk, blk, send_sems, recv_sems, t, (x, y, 1 - c)).start()

    def wait(ins, outs, send_sems, recv_sems):
        x, y, c = _mesh_pos()
        for t in range(n):
            three = outs[t].at[pl.ds(0, 3), pl.ds(0, gathered[t].shape[1] // 2)]
            _remote(three, three, send_sems, recv_sems, t, (x, y, 1 - c)).wait()

    return _Exchange(gathered, [jax.ShapeDtypeStruct(a.shape, a.dtype) for a in gathered],
                     {i: i for i in range(n)}, start, wait)


def _rs_sibling_exchange(grads):
    n = len(grads)
    halves = [g.shape[1] // 2 for g in grads]

    def start(ins, outs, send_sems, recv_sems):
        x, y, c = _mesh_pos()
        for t in range(n):
            for q in range(N_CHIPS):
                for s0, sz in _row_chunks(halves[t], grads[t].shape[2] * 4, SUBLANES):
                    _remote(ins[t].at[q, pl.ds((1 - c) * halves[t] + s0, sz)], outs[t].at[q, pl.ds(s0, sz)],
                            send_sems, recv_sems, t, (x, y, 1 - c)).start()

    def wait(ins, outs, send_sems, recv_sems):
        x, y, c = _mesh_pos()
        for t in range(n):
            _remote(outs[t], outs[t], send_sems, recv_sems, t, (x, y, 1 - c)).wait()

    return _Exchange(grads, [jax.ShapeDtypeStruct((N_CHIPS, h, g.shape[2]), F32) for g, h in zip(grads, halves)],
                     {}, start, wait)


def _rs_chip_exchange(pbs):
    n = len(pbs)

    def start(ins, outs, send_sems, recv_sems):
        x, y, c = _mesh_pos()
        for t in range(n):
            for s0, sz in _row_chunks(pbs[t].shape[1], pbs[t].shape[2] * 2, 16):
                for k, (cx, cy) in enumerate(_other_chips(x, y)):
                    _remote(ins[t].at[2 * cx + cy, pl.ds(s0, sz)], outs[t].at[k, pl.ds(s0, sz)],
                            send_sems, recv_sems, t, (cx, cy, c)).start()

    def wait(ins, outs, send_sems, recv_sems):
        x, y, c = _mesh_pos()
        for t in range(n):
            _remote(outs[t], outs[t], send_sems, recv_sems, t, (x, y, 1 - c)).wait()

    return _Exchange(pbs, [jax.ShapeDtypeStruct((3,) + p.shape[1:], BF16) for p in pbs], {}, start, wait)


def _rs_sibling_share(gs):
    n = len(gs)

    def start(ins, outs, send_sems, recv_sems):
        x, y, c = _mesh_pos()
        for t in range(n):
            h = gs[t].shape[0] // 2
            for s0, sz in _row_chunks(h, gs[t].shape[1] * 4, SUBLANES):
                rows = outs[t].at[pl.ds(c * h + s0, sz)]
                _remote(rows, rows, send_sems, recv_sems, t, (x, y, 1 - c)).start()

    def wait(ins, outs, send_sems, recv_sems):
        x, y, c = _mesh_pos()
        for t in range(n):
            half = outs[t].at[pl.ds(0, gs[t].shape[0] // 2)]
            _remote(half, half, send_sems, recv_sems, t, (x, y, 1 - c)).wait()

    return _Exchange(gs, [jax.ShapeDtypeStruct(g.shape, F32) for g in gs], {i: i for i in range(n)}, start, wait)


def _gather_small(buf):
    def body(in_ref, out_ref, send_sems, recv_sems, loc_sem):
        x, y, c = _mesh_pos()
        me = 4 * x + 2 * y + c
        local = pltpu.make_async_copy(in_ref, out_ref.at[me], loc_sem)
        local.start()
        sends, peers = [], []
        for k in range(1, N_DEV):
            bx, by, bc = (k >> 2) & 1, (k >> 1) & 1, k & 1
            peer = (x + bx - 2 * x * bx, y + by - 2 * y * by, c + bc - 2 * c * bc)
            peers.append(peer)
            sends.append(_remote(in_ref, out_ref.at[me], send_sems, recv_sems, k - 1, peer))
        for cp in sends:
            cp.start()
        for k, (px, py, pc) in enumerate(peers):
            slot = out_ref.at[4 * px + 2 * py + pc]
            _remote(slot, slot, send_sems, recv_sems, k, (px, py, pc)).wait_recv()
        for cp in sends:
            cp.wait_send()
        local.wait()

    return pl.pallas_call(
        body, name="gather_small", in_specs=[HBM_SPEC], out_specs=HBM_SPEC,
        out_shape=jax.ShapeDtypeStruct((N_DEV,) + buf.shape, F32),
        scratch_shapes=[pltpu.SemaphoreType.DMA((N_DEV - 1,)), pltpu.SemaphoreType.DMA((N_DEV - 1,)),
                        pltpu.SemaphoreType.DMA(())],
    )(buf)


def _ew_rows(rows, cols, bytes_per_elem):
    budget = 20 * 1024 * 1024
    return _tile(rows, max(16, budget // (2 * bytes_per_elem * cols) // 16 * 16), 16)


def _sum_pair(grad, recv):
    _, h, cols = recv.shape
    tr = _ew_rows(h, cols, 14)
    nb = h // tr
    half = pl.BlockSpec((None, tr, cols), lambda q, i: (q, i, 0))

    def body(a_ref, b_ref, f_ref, h_ref):
        s = a_ref[...] + b_ref[...]
        f_ref[...] = s
        h_ref[...] = s.astype(BF16)

    return pl.pallas_call(
        body, name="rs_sum_pair", grid=(N_CHIPS, nb),
        in_specs=[pl.BlockSpec((None, tr, cols), lambda q, i: (q, lax.axis_index("c") * nb + i, 0)), half],
        out_specs=[half, half],
        out_shape=[jax.ShapeDtypeStruct(recv.shape, F32), jax.ShapeDtypeStruct(recv.shape, BF16)],
        compiler_params=_params(28 * tr * cols, ("parallel", "parallel")),
    )(grad, recv)


def _sum_four(pf, recv):
    _, h, cols = pf.shape
    tr = _ew_rows(h, cols, 14)
    nb = h // tr

    def body(a_ref, r_ref, o_ref):
        o_ref[...] = ((a_ref[...] + r_ref[0].astype(F32)) + r_ref[1].astype(F32)) + r_ref[2].astype(F32)

    return pl.pallas_call(
        body, name="rs_sum_four", grid=(nb,),
        in_specs=[pl.BlockSpec((None, tr, cols), lambda i: (2 * lax.axis_index("x") + lax.axis_index("y"), i, 0)),
                  pl.BlockSpec((3, tr, cols), lambda i: (0, i, 0))],
        out_specs=pl.BlockSpec((tr, cols), lambda i: (lax.axis_index("c") * nb + i, 0)),
        out_shape=jax.ShapeDtypeStruct((2 * h, cols), F32),
        compiler_params=_params(28 * tr * cols, ("parallel",)),
    )(pf, recv)


def _sum_eight(gathered):
    _, rows, cols = gathered.shape
    tr = _tile(rows, 512, SUBLANES)

    def body(g_ref, o_ref):
        s = g_ref[0]
        for d in range(1, N_DEV):
            s = s + g_ref[d]
        o_ref[...] = s

    return pl.pallas_call(
        body, name="sum_eight", grid=(rows // tr,),
        in_specs=[pl.BlockSpec((N_DEV, tr, cols), lambda i: (0, i, 0))],
        out_specs=_row_spec(tr, cols), out_shape=jax.ShapeDtypeStruct((rows, cols), F32),
        compiler_params=_params(None, ("parallel",)),
    )(gathered)


def _cast_layer(w, l):
    _, rows, cols = w.shape
    tr = _ew_rows(rows, cols, 6)

    def body(w_ref, o_ref):
        o_ref[...] = w_ref[...].astype(BF16)

    return pl.pallas_call(
        body, name="cast_bf16", grid=(rows // tr,),
        in_specs=[pl.BlockSpec((None, tr, cols), lambda i: (l, i, 0))],
        out_specs=pl.BlockSpec((None, tr, cols), lambda i: (2 * lax.axis_index("x") + lax.axis_index("y"), i, 0)),
        out_shape=jax.ShapeDtypeStruct((N_CHIPS, rows, cols), BF16),
        compiler_params=_params(12 * tr * cols, ("parallel",)),
    )(w)


def _adamw(g, w, m, v, l, prev):
    L, rows, cols = w.shape
    tr = _ew_rows(rows, cols, 32)
    layer = pl.BlockSpec((None, tr, cols), lambda i: (l, i, 0))

    def body(g_ref, w_ref, m_ref, v_ref, *rest):
        og_ref, od_ref, om_ref, ov_ref = rest[-4:]
        gv = g_ref[...]
        mn = ADAM_B1 * m_ref[...] + (1.0 - ADAM_B1) * gv
        vn = ADAM_B2 * v_ref[...] + (1.0 - ADAM_B2) * (gv * gv)
        m_hat = mn / ADAM_C1
        v_hat = vn / ADAM_C2
        od_ref[...] = -ADAM_LR * (m_hat / (jnp.sqrt(v_hat) + ADAM_EPS) + ADAM_WD * w_ref[...])
        og_ref[...] = gv
        om_ref[...] = mn
        ov_ref[...] = vn

    in_specs = [_row_spec(tr, cols), layer, layer, layer]
    args = [g, w, m, v]
    aliases = {}
    if prev is not None:
        in_specs += [HBM_SPEC] * 4
        args += list(prev)
        aliases = {4 + i: i for i in range(4)}
    return pl.pallas_call(
        body, name="adamw", grid=(rows // tr,), in_specs=in_specs, out_specs=[layer] * 4,
        out_shape=[jax.ShapeDtypeStruct((L, rows, cols), F32)] * 4, input_output_aliases=aliases,
        compiler_params=_params(64 * tr * cols, ("parallel",)),
    )(*args)


WEIGHT_ORDER = ("ln1_g", "w_in", "qn_a", "kn_a", "rpb", "qn_b", "kn_b", "sink", "on_a", "on_b", "w_out", "ln2_g",
                "w_up", "conv_w", "conv_b", "w_down")
BIG = ("w_in", "w_out", "w_up", "w_down")
SMALL = tuple(n for n in WEIGHT_ORDER if n not in BIG)


def _train_step(x, positions, target, w, m, v):
    S, D = x.shape
    L = w["w_in"].shape[0]
    naw = w["rpb"].shape[1] * HEAD_DIM
    waw = w["sink"].shape[1] * HEAD_DIM
    cin = w["w_in"].shape[2]
    kvw = (N_CHIPS * cin - 3 * naw - waw) // 2
    hkv = kvw // HEAD_DIM
    dims = (naw, waw, kvw)
    assert naw == waw, "the two head groups are normalised by one kernel and must be equally wide"
    cup = w["w_up"].shape[2]
    c2 = N_CHIPS * cup
    dff = c2 // 2
    mix = naw + waw
    sh_in = _ColShards(cin, _tile(cin, 1152, LANES), False)
    sh_up = _ColShards(cup, cup // 2 if (cup // 2) % LANES == 0 else cup, True)
    tc = sh_up.tn
    cos2, sgn_sin = _rope_tables(positions)
    q_me = 2 * lax.axis_index("x") + lax.axis_index("y")
    row = lambda a: a[None]

    def own_slots(l):
        small = lax.dynamic_update_slice_in_dim(jnp.zeros((N_CHIPS,) + w["conv_w"].shape[1:], F32),
                                                w["conv_w"][l][None], q_me, axis=0)
        return [_cast_layer(w[n], l) for n in BIG], small

    gath = {}

    def prepare(l):
        bufs, small = own_slots(l)
        gath.update({(l, n): b for n, b in zip(BIG, bufs)})
        gath[l, "conv_w"] = small

    def over_ici(l, names, small=False):
        keys = [(l, n) for n in names]
        ex = _ag_chip_exchange([gath[k] for k in keys], gath[l, "conv_w"] if small else None)
        return ex, keys + ([(l, "conv_w")] if small else [])

    def to_sibling(l, names):
        keys = [(l, n) for n in names]
        return _ag_sibling_pass([gath[k] for k in keys]), keys

    def together(parts):
        ex, keys = parts[0]
        for e, k in parts[1:]:
            ex, keys = _both(ex, e), keys + k
        return ex, keys

    def carried(call, parts):
        if not parts:
            return call(None)
        ex, keys = together(parts)
        out, results = call(ex)
        gath.update(dict(zip(keys, results)))
        return out

    prepare(0)
    ex, keys = over_ici(0, ("w_in", "w_out"), True)
    gath.update(dict(zip(keys, _exchange_alone("ag_first_over_ici", ex))))
    ex, keys = to_sibling(0, ("w_in", "w_out"))
    gath.update(dict(zip(keys, _exchange_alone("ag_first_to_sibling", ex))))
    saved = []
    for l in range(L):
        more = l + 1 < L
        if more:
            prepare(l + 1)
        g_cw = gath[l, "conv_w"]
        cw_p = _to_paired(jnp.transpose(g_cw, (1, 0, 2)).reshape(3, c2), sh_up)
        cb_p = _to_paired(row(w["conv_b"][l]), sh_up)
        gains = jnp.concatenate([row(w["qn_a"][l]), row(w["kn_a"][l]), row(w["qn_b"][l]), row(w["kn_b"][l]),
                                 jnp.zeros((SUBLANES - 4, HEAD_DIM), F32)], axis=0)
        btab = _na_bias_table(w["rpb"][l])
        sink_col = _sink_col(w["sink"][l], hkv)

        h = _rms_fwd(x, row(w["ln1_g"][l]))
        g_in = gath[l, "w_in"]
        proj = carried(lambda ex: _mm_cols_fwd("mm_proj", h, g_in, sh_in, F32, tm=1024, carry=ex),
                       [over_ici(0, ("w_up",))] if l == 0 else [to_sibling(l, ("w_down",))])
        qa, ka, va, qb, kb, vb = _qk_prep_fwd(proj, cos2, sgn_sin, gains, dims)
        oa = _na_fwd(qa, ka, va, btab)
        ob = _wa_fwd(qb, kb, vb, sink_col)
        o = _out_norm_fwd(oa, ob, row(w["on_a"][l]), row(w["on_b"][l]))
        g_out = gath[l, "w_out"].reshape(mix, D)
        x1 = carried(lambda ex: _mm_plain("mm_attn_out", o, g_out, NN, F32, tm=1024, tn=1024, tk=2048, res=x, carry=ex),
                     [to_sibling(0, ("w_up",)), over_ici(0, ("w_down",))] if l == 0 else [])
        h2 = _rms_fwd(x1, row(w["ln2_g"][l]))
        g_up = gath[l, "w_up"]
        up_pre = carried(lambda ex: _mm_cols_fwd("mm_up", h2, g_up, sh_up, F32, tm=1024, carry=ex),
                         ([to_sibling(0, ("w_down",))] if l == 0 else [])
                         + ([over_ici(l + 1, ("w_in", "w_out", "w_up"), True)] if more else []))
        act = _convgate_fwd(up_pre, cw_p, cb_p, tc)
        g_dn = gath[l, "w_down"].reshape(dff, D)
        x2 = carried(lambda ex: _mm_plain("mm_down", act, g_dn, NN, F32, tm=1024, tn=1024, tk=1408, res=x1, carry=ex),
                     [to_sibling(l + 1, ("w_in", "w_out", "w_up")), over_ici(l + 1, ("w_down",))] if more else [])
        saved.append(dict(g_in=g_in, g_out=g_out, g_up=g_up, g_dn=g_dn, cw_p=cw_p, cb_p=cb_p, gains=gains, btab=btab,
                          sink_col=sink_col, x=x, h=h, proj=proj, qa=qa, ka=ka, va=va, qb=qb, kb=kb, vb=vb, oa=oa,
                          ob=ob, o=o, x1=x1, h2=h2, up_pre=up_pre, act=act))
        x = x2

    dx, dxb, loss_cols = _loss_grad(x, target)

    small_grads = [None] * L
    stacked = {n: None for n in BIG}

    def update(layer, shards):
        for n, g in zip(BIG, shards):
            stacked[n] = _adamw(g, w[n], m[n], v[n], layer, stacked[n])

    above = None
    for l in reversed(range(L)):
        s = saved[l]
        if above is None:
            dact = _mm_plain("mm_down_dx", dxb, s["g_dn"], NT, F32, tm=1024, tn=1408, tk=2048)
        else:
            dact, recvs = _mm_plain("mm_down_dx_rs", dxb, s["g_dn"], NT, F32, tm=1024, tn=1408, tk=2048,
                                    carry=_rs_sibling_exchange(above[1]))
            pairs = [_sum_pair(g, r) for g, r in zip(above[1], recvs)]
        dup_pre, dcw_p, dcb_p = _convgate_bwd(s["up_pre"], dact, s["cw_p"], s["cb_p"], tc)
        if above is None:
            dh2 = _mm_cols_bwd_x("mm_up_dx", dup_pre, s["g_up"], sh_up, F32, tm=1024, tn=2048)
        else:
            dh2, gots = _mm_cols_bwd_x("mm_up_dx_rs", dup_pre, s["g_up"], sh_up, F32, tm=1024, tn=2048,
                                       carry=_rs_chip_exchange([p[1] for p in pairs]))
            halves = [_sum_four(p[0], r) for p, r in zip(pairs, gots)]
        d_dn, (dx1, dx1b, d_ln2) = _mm_plain(
            "mm_down_dw", s["act"], dxb, TN, F32, tm=1408, tn=512, tk=2048,
            rider=lambda steps: _rms_bwd(dh2, s["x1"], row(w["ln2_g"][l]), dx, steps))
        d_dn = d_dn.reshape(N_CHIPS, dff // N_CHIPS, D)
        if above is None:
            do = _mm_plain("mm_attn_out_dx", dx1b, s["g_out"], NT, F32, tm=1024, tn=1024, tk=2048)
        else:
            do, shards = _mm_plain("mm_attn_out_dx_rs", dx1b, s["g_out"], NT, F32, tm=1024, tn=1024, tk=2048,
                                   carry=_rs_sibling_share(halves))
            update(above[0], shards)
        last = l == 0
        d_out, (doa, dob, d_on_a, d_on_b), *rest = _mm_plain(
            "mm_attn_out_dw", s["o"], dx1b, TN, F32, tm=1024, tn=512, tk=2048,
            carry=_rs_sibling_exchange([d_dn]) if last else None,
            rider=lambda steps: _out_norm_bwd(do, s["oa"], s["ob"], row(w["on_a"][l]), row(w["on_b"][l]), steps))
        d_out = d_out.reshape(N_CHIPS, mix // N_CHIPS, D)
        if last:
            pair_dn = _sum_pair(d_dn, rest[0][0])
        dqa, dka, dva, dbtab = _na_bwd(s["qa"], s["ka"], s["va"], s["btab"], doa)
        dqb, dkb, dvb, dsink_col = _wa_bwd(s["qb"], s["kb"], s["vb"], s["sink_col"], dob)
        d_up, (dproj, dgains) = _mm_cols_bwd_w(
            "mm_up_dw", s["h2"], dup_pre, sh_up, tm=1024, tk=2048,
            rider=lambda steps: _qk_prep_bwd(s["proj"], dqa, dka, dva, dqb, dkb, dvb, cos2, sgn_sin, s["gains"], dims, steps))
        if last:
            dh, (recv_up, got_dn) = _mm_cols_bwd_x(
                "mm_proj_dx", dproj, s["g_in"], sh_in, F32, tm=1024, tn=2048,
                carry=_both(_rs_sibling_exchange([d_up]), _rs_chip_exchange([pair_dn[1]])))
            pair_up = _sum_pair(d_up, recv_up)
        else:
            dh = _mm_cols_bwd_x("mm_proj_dx", dproj, s["g_in"], sh_in, F32, tm=1024, tn=2048)
        d_in, (dx, dxb, d_ln1), *rest = _mm_cols_bwd_w(
            "mm_proj_dw", s["h"], dproj, sh_in, tm=512, tk=2048,
            carry=_rs_chip_exchange([pair_up[1]]) if last else None,
            rider=lambda steps: _rms_bwd(dh, s["x"], row(w["ln1_g"][l]), dx1, steps))
        if last:
            got_up = rest[0][0]

        small_grads[l] = dict(
            ln1_g=d_ln1[0], qn_a=dgains[0], kn_a=dgains[1], rpb=_na_bias_table_t(dbtab), qn_b=dgains[2],
            kn_b=dgains[3], sink=_sink_col_t(dsink_col), on_a=d_on_a[0], on_b=d_on_b[0], ln2_g=d_ln2[0],
            conv_w=_from_paired(dcw_p, sh_up), conv_b=_from_paired(dcb_p, sh_up)[0])
        above = (l, [d_in, d_out, d_up, d_dn])
    recvs = _exchange_alone("rs_sibling_exchange", _rs_sibling_exchange([d_in, d_out]))
    pair_in, pair_out = _sum_pair(d_in, recvs[0]), _sum_pair(d_out, recvs[1])
    got_in, got_out = _exchange_alone("rs_chip_exchange", _rs_chip_exchange([pair_in[1], pair_out[1]]))
    halves = [_sum_four(p[0], g) for p, g in ((pair_in, got_in), (pair_out, got_out), (pair_up, got_up), (pair_dn, got_dn))]
    update(0, _exchange_alone("rs_sibling_share", _rs_sibling_share(halves)))
    grad_x = dx[None]

    full_shapes = {n: (w[n].shape[1:] if n != "conv_w" else (3, c2)) for n in SMALL}
    packed = _pack([loss_cols] + [small_grads[l][n] for l in range(L) for n in SMALL])
    total = _sum_eight(_gather_small(packed))
    parts = _unpack(total, [(D,)] + [full_shapes[n] for _ in range(L) for n in SMALL])
    loss = 0.5 * jnp.sum(parts[0]) / D
    per_layer = [dict(zip(SMALL, parts[1 + l * len(SMALL):1 + (l + 1) * len(SMALL)])) for l in range(L)]
    for l in range(L):
        per_layer[l]["conv_w"] = lax.dynamic_slice_in_dim(per_layer[l]["conv_w"], q_me * cup, cup, axis=1)
    small_g = {n: jnp.stack([per_layer[l][n] for l in range(L)]) for n in SMALL}
    outs = _adamw(_pack([small_g[n] for n in SMALL]), _pack([w[n] for n in SMALL])[None],
                  _pack([m[n] for n in SMALL])[None], _pack([v[n] for n in SMALL])[None], 0, None)
    small_out = [dict(zip(SMALL, _unpack(o[0], [w[n].shape for n in SMALL]))) for o in outs]

    result = [loss, grad_x]
    for i in range(4):
        result += [stacked[n][i] if n in BIG else small_out[i][n] for n in WEIGHT_ORDER]
    return tuple(result)


def kernel(x, positions, ln1_g, w_in, qn_a, kn_a, rpb, qn_b, kn_b, sink, on_a, on_b, w_out, ln2_g, w_up, conv_w, conv_b, w_down, loss_target, m_ln1_g, m_w_in, m_qn_a, m_kn_a, m_rpb, m_qn_b, m_kn_b, m_sink, m_on_a, m_on_b, m_w_out, m_ln2_g, m_w_up, m_conv_w, m_conv_b, m_w_down, v_ln1_g, v_w_in, v_qn_a, v_kn_a, v_rpb, v_qn_b, v_kn_b, v_sink, v_on_a, v_on_b, v_w_out, v_ln2_g, v_w_up, v_conv_w, v_conv_b, v_w_down):
    given = dict(locals())
    w = {n: given[n] for n in WEIGHT_ORDER}
    m = {n: given["m_" + n] for n in WEIGHT_ORDER}
    v = {n: given["v_" + n] for n in WEIGHT_ORDER}
    return _train_step(x[0], positions, loss_target[0], w, m, v)
```

```python
import functools
import math

import jax
import jax.numpy as jnp
import numpy as np
from jax import lax
from jax.experimental import pallas as pl
from jax.experimental.pallas import tpu as pltpu

F32 = jnp.float32
BF16 = jnp.bfloat16

HEAD_DIM = 128
GRID_W = 64
NA_WIN_R = 8
NA_WIN_C = 16
WA_WINDOW = 128
WA_BLOCK = 128
ROPE_THETA = 10000.0
EPS = 1e-6
NEG = -1e30
ATTN_SCALE = 1.0 / math.sqrt(HEAD_DIM)

ADAM_LR = 0.001
ADAM_B1 = 0.9
ADAM_B2 = 0.999
ADAM_EPS = 1e-08
ADAM_WD = 0.01
ADAM_STEP = 10
ADAM_C1 = 1.0 - ADAM_B1 ** ADAM_STEP
ADAM_C2 = 1.0 - ADAM_B2 ** ADAM_STEP

V7X_VMEM_BYTES = 64 * 1024 * 1024
V7X_VMEM_CAP = V7X_VMEM_BYTES - 6 * 1024 * 1024
LANES = 128
SUBLANES = 8
N_CHIPS = 4
N_DEV = 8
MESH = pl.DeviceIdType.MESH

NN = ((1,), (0,))
NT = ((1,), (1,))
TN = ((0,), (0,))


def _tile(n, pref, mult):
    best = None
    d = mult
    while d <= min(n, pref):
        if n % d == 0:
            best = d
        d += mult
    return n if best is None else best


def _nbytes(shape, dtype):
    n = 1
    for s in shape:
        if s is not None:
            n *= s
    return n * jnp.dtype(dtype).itemsize


def _params(est_bytes=None, sem=None, **kw):
    if est_bytes is not None:
        kw["vmem_limit_bytes"] = int(min(V7X_VMEM_CAP, max(32 * 1024 * 1024, est_bytes * 5 // 4 + (4 << 20))))
    if sem is not None:
        kw["dimension_semantics"] = sem
    return pltpu.CompilerParams(**kw)


def _dot(a, b, contract):
    return lax.dot_general(a.astype(BF16), b.astype(BF16), (contract, ((), ())), preferred_element_type=F32)


def _mm(name, a, b, *, grid, a_spec, b_spec, o_spec, out_shape, contract, res=None, res_spec=None, carry=None,
        rider=None):
    nk = grid[2]
    n_steps = grid[0] * grid[1] * grid[2]
    acc_shape = tuple(s for s in o_spec.block_shape if s is not None)
    n_in = 2 if res is None else 3
    r_in = 0 if rider is None else len(rider.operands)
    r_out = 0 if rider is None else len(rider.out_shapes)
    x_in = 0 if carry is None else len(carry.operands)
    x_out = 0 if carry is None else carry.n_sems

    def body(*refs):
        a_ref, b_ref = refs[:2]
        r_ref = None if res is None else refs[2]
        pos = n_in
        r_ins, pos = refs[pos:pos + r_in], pos + r_in
        x_ins, pos = refs[pos:pos + x_in], pos + x_in
        o_ref, pos = refs[pos], pos + 1
        r_outs, pos = refs[pos:pos + r_out], pos + r_out
        x_outs, pos = refs[pos:pos + x_out], pos + x_out
        scr = refs[pos:]
        step = (pl.program_id(0) * grid[1] + pl.program_id(1)) * grid[2] + pl.program_id(2)
        if carry is not None:
            sems, scr = scr[-2:], scr[:-2]

            @pl.when(step == 0)
            def _():
                carry.start(x_ins, x_outs, *sems)

        p = _dot(a_ref[...], b_ref[...], contract)
        if rider is not None:
            rider.body(step, r_ins, r_outs)

        def finish(acc):
            if r_ref is not None:
                acc = acc + r_ref[...]
            o_ref[...] = acc.astype(o_ref.dtype)

        if nk == 1:
            finish(p)
        else:
            acc_ref = scr[0]
            k = pl.program_id(2)

            @pl.when(k == 0)
            def _():
                acc_ref[...] = p

            @pl.when(jnp.logical_and(k > 0, k < nk - 1))
            def _():
                acc_ref[...] += p

            @pl.when(k == nk - 1)
            def _():
                finish(acc_ref[...] + p)

        if carry is not None:
            @pl.when(step == n_steps - 1)
            def _():
                carry.wait(x_ins, x_outs, *sems)

    in_specs = [a_spec, b_spec]
    args = [a, b]
    est = 2 * (_nbytes(a_spec.block_shape, a.dtype) + _nbytes(b_spec.block_shape, b.dtype)
               + _nbytes(o_spec.block_shape, out_shape.dtype)) + 2 * _nbytes(acc_shape, F32)
    if res is not None:
        in_specs.append(res_spec)
        args.append(res)
        est += 2 * _nbytes(res_spec.block_shape, res.dtype)
    scratch = [] if nk == 1 else [pltpu.VMEM(acc_shape, F32)]
    if carry is None and rider is None:
        return pl.pallas_call(
            body, name=name, grid=grid, in_specs=in_specs, out_specs=o_spec, out_shape=out_shape,
            scratch_shapes=scratch,
            compiler_params=_params(est, ("parallel", "parallel", "arbitrary")),
        )(*args)

    def by_step(spec):
        return pl.BlockSpec(spec.block_shape, lambda i, j, k: spec.index_map((i * grid[1] + j) * grid[2] + k))

    out_specs, out_shapes, aliases = [o_spec], [out_shape], {}
    if rider is not None:
        in_specs += [by_step(s) for s in rider.in_specs]
        args += rider.operands
        out_specs += [by_step(s) for s in rider.out_specs]
        out_shapes += rider.out_shapes
        est += rider.vmem_bytes
    if carry is not None:
        aliases = {len(args) + i: len(out_shapes) + o for i, o in carry.aliases.items()}
        in_specs += [HBM_SPEC] * x_in
        args += carry.operands
        out_specs += [HBM_SPEC] * x_out
        out_shapes += carry.out_shapes
        scratch += carry.scratch()
    outs = pl.pallas_call(
        body, name=name, grid=grid, in_specs=in_specs, out_specs=out_specs, out_shape=out_shapes,
        input_output_aliases=aliases, scratch_shapes=scratch,
        compiler_params=_params(est, ("arbitrary", "arbitrary", "arbitrary")),
    )(*args)
    results = [outs[0]]
    if rider is not None:
        results.append(list(outs[1:1 + r_out]))
    if carry is not None:
        results.append(list(outs[1 + r_out:]))
    return tuple(results)


def _mm_plain(name, a, b, contract, out_dtype, *, tm, tn, tk, res=None, carry=None, rider=None):
    if contract == NN:
        (M, K), N = a.shape, b.shape[1]
    elif contract == NT:
        (M, K), N = a.shape, b.shape[0]
    else:
        (K, M), N = a.shape, b.shape[1]
    tm, tn, tk = _tile(M, tm, LANES), _tile(N, tn, LANES), _tile(K, tk, LANES)
    grid = (M // tm, N // tn, K // tk)
    if contract == TN:
        a_spec = pl.BlockSpec((tk, tm), lambda i, j, k: (k, i))
    else:
        a_spec = pl.BlockSpec((tm, tk), lambda i, j, k: (i, k))
    if contract == NT:
        b_spec = pl.BlockSpec((tn, tk), lambda i, j, k: (j, k))
    else:
        b_spec = pl.BlockSpec((tk, tn), lambda i, j, k: (k, j))
    o_spec = pl.BlockSpec((tm, tn), lambda i, j, k: (i, j))
    return _mm(name, a, b, grid=grid, a_spec=a_spec, b_spec=b_spec, o_spec=o_spec,
               out_shape=jax.ShapeDtypeStruct((M, N), out_dtype), contract=contract,
               res=res, res_spec=None if res is None else pl.BlockSpec((tm, tn), lambda i, j, k: (i, j)), carry=carry,
               rider=None if rider is None else rider(math.prod(grid)))


class _ColShards:
    def __init__(self, cols_per_chip, tn, paired):
        self.c = cols_per_chip
        self.tn = tn
        self.tps = cols_per_chip // tn
        self.ntiles = N_CHIPS * self.tps
        self.paired = paired

    def nat(self, t):
        if not self.paired:
            return t
        return (t % 2) * (self.ntiles // 2) + t // 2

    def chip(self, t):
        return self.nat(t) // self.tps

    def within(self, t):
        return self.nat(t) % self.tps


def _mm_cols_fwd(name, a, wg, sh, out_dtype, *, tm, carry=None):
    S, K = a.shape
    tm = _tile(S, tm, LANES)
    grid = (S // tm, sh.ntiles, 1)
    return _mm(name, a, wg, grid=grid,
               a_spec=pl.BlockSpec((tm, K), lambda i, j, k: (i, 0)),
               b_spec=pl.BlockSpec((None, K, sh.tn), lambda i, j, k: (sh.chip(j), 0, sh.within(j))),
               o_spec=pl.BlockSpec((tm, sh.tn), lambda i, j, k: (i, j)),
               out_shape=jax.ShapeDtypeStruct((S, sh.ntiles * sh.tn), out_dtype), contract=NN, carry=carry)


def _mm_cols_bwd_x(name, dy, wg, sh, out_dtype, *, tm, tn, carry=None):
    S = dy.shape[0]
    K = wg.shape[1]
    tm, tn = _tile(S, tm, LANES), _tile(K, tn, LANES)
    grid = (S // tm, K // tn, sh.ntiles)
    return _mm(name, dy, wg, grid=grid,
               a_spec=pl.BlockSpec((tm, sh.tn), lambda i, j, k: (i, k)),
               b_spec=pl.BlockSpec((None, tn, sh.tn), lambda i, j, k: (sh.chip(k), j, sh.within(k))),
               o_spec=pl.BlockSpec((tm, tn), lambda i, j, k: (i, j)),
               out_shape=jax.ShapeDtypeStruct((S, K), out_dtype), contract=NT, carry=carry)


def _mm_cols_bwd_w(name, a, dy, sh, *, tm, tk, carry=None, rider=None):
    S, K = a.shape
    tm, tk = _tile(K, tm, LANES), _tile(S, tk, LANES)
    grid = (K // tm, sh.ntiles, S // tk)
    return _mm(name, a, dy, grid=grid,
               a_spec=pl.BlockSpec((tk, tm), lambda i, j, k: (k, i)),
               b_spec=pl.BlockSpec((tk, sh.tn), lambda i, j, k: (k, j)),
               o_spec=pl.BlockSpec((None, tm, sh.tn), lambda i, j, k: (sh.chip(j), i, sh.within(j))),
               out_shape=jax.ShapeDtypeStruct((N_CHIPS, K, sh.c), F32), contract=TN, carry=carry,
               rider=None if rider is None else rider(math.prod(grid)))


def _row_spec(tr, width):
    return pl.BlockSpec((tr, width), lambda i: (i, 0))


def _full_spec(shape):
    nd = len(shape)
    return pl.BlockSpec(shape, lambda i: (0,) * nd)


def _rms_fwd(x, g):
    S, D = x.shape
    tr = _tile(S, 512, 16)

    def body(x_ref, g_ref, h_ref):
        xv = x_ref[...]
        r = lax.rsqrt(jnp.mean(xv * xv, axis=-1, keepdims=True) + EPS)
        h_ref[...] = (xv * r * g_ref[...]).astype(BF16)

    return pl.pallas_call(
        body, name="rms_fwd", grid=(S // tr,),
        in_specs=[_row_spec(tr, D), _full_spec((1, D))], out_specs=_row_spec(tr, D),
        out_shape=jax.ShapeDtypeStruct((S, D), BF16),
        compiler_params=_params(12 * tr * D, ("parallel",)),
    )(x, g)


class _Rider:
    def __init__(self, operands, in_specs, out_specs, out_shapes, body, vmem_bytes):
        self.operands, self.in_specs, self.out_specs = list(operands), list(in_specs), list(out_specs)
        self.out_shapes, self.body, self.vmem_bytes = list(out_shapes), body, vmem_bytes


def _accumulate(step, ref, part):
    @pl.when(step == 0)
    def _():
        ref[...] = part

    @pl.when(step > 0)
    def _():
        ref[...] += part


def _rms_bwd(dh, x, g, dres, steps):
    S, D = x.shape
    tr = S // steps

    def body(step, ins, outs):
        dh_ref, x_ref, g_ref, dres_ref = ins
        dx_ref, dxb_ref, dg_ref = outs
        xv = x_ref[...]
        dhv = dh_ref[...]
        r = lax.rsqrt(jnp.mean(xv * xv, axis=-1, keepdims=True) + EPS)
        gy = dhv * g_ref[...]
        dot = jnp.mean(xv * gy, axis=-1, keepdims=True)
        dx = dres_ref[...] + (r * gy - xv * (r * r * r * dot))
        dx_ref[...] = dx
        dxb_ref[...] = dx.astype(BF16)
        _accumulate(step, dg_ref, jnp.sum(dhv * (xv * r), axis=0, keepdims=True))

    return _Rider([dh, x, g, dres],
                  [_row_spec(tr, D), _row_spec(tr, D), _full_spec((1, D)), _row_spec(tr, D)],
                  [_row_spec(tr, D), _row_spec(tr, D), _full_spec((1, D))],
                  [jax.ShapeDtypeStruct((S, D), F32), jax.ShapeDtypeStruct((S, D), BF16),
                   jax.ShapeDtypeStruct((1, D), F32)], body, 40 * tr * D)


def _out_norm_fwd(oa, ob, ga, gb):
    S, W = oa.shape
    tr = _tile(S, 512, 16)

    def body(oa_ref, ob_ref, ga_ref, gb_ref, o_ref):
        for src, g_ref, off in ((oa_ref, ga_ref, 0), (ob_ref, gb_ref, W)):
            v = src[...]
            r = lax.rsqrt(jnp.mean(v * v, axis=-1, keepdims=True) + EPS)
            o_ref[:, off:off + W] = (v * r * g_ref[...]).astype(BF16)

    return pl.pallas_call(
        body, name="out_norm_fwd", grid=(S // tr,),
        in_specs=[_row_spec(tr, W), _row_spec(tr, W), _full_spec((1, W)), _full_spec((1, W))],
        out_specs=_row_spec(tr, 2 * W), out_shape=jax.ShapeDtypeStruct((S, 2 * W), BF16),
        compiler_params=_params(24 * tr * W, ("parallel",)),
    )(oa, ob, ga, gb)


def _out_norm_bwd(do, oa, ob, ga, gb, steps):
    S, W = oa.shape
    tr = S // steps

    def body(step, ins, outs):
        do_ref, oa_ref, ob_ref, ga_ref, gb_ref = ins
        doa_ref, dob_ref, dga_ref, dgb_ref = outs
        parts = []
        for src, g_ref, off, d_ref in ((oa_ref, ga_ref, 0, doa_ref), (ob_ref, gb_ref, W, dob_ref)):
            v = src[...]
            dv = do_ref[:, off:off + W]
            r = lax.rsqrt(jnp.mean(v * v, axis=-1, keepdims=True) + EPS)
            gy = dv * g_ref[...]
            dot = jnp.mean(v * gy, axis=-1, keepdims=True)
            d_ref[...] = (r * gy - v * (r * r * r * dot)).astype(BF16)
            parts.append(jnp.sum(dv * (v * r), axis=0, keepdims=True))
        _accumulate(step, dga_ref, parts[0])
        _accumulate(step, dgb_ref, parts[1])

    return _Rider([do, oa, ob, ga, gb],
                  [_row_spec(tr, 2 * W), _row_spec(tr, W), _row_spec(tr, W), _full_spec((1, W)), _full_spec((1, W))],
                  [_row_spec(tr, W), _row_spec(tr, W), _full_spec((1, W)), _full_spec((1, W))],
                  [jax.ShapeDtypeStruct((S, W), BF16), jax.ShapeDtypeStruct((S, W), BF16),
                   jax.ShapeDtypeStruct((1, W), F32), jax.ShapeDtypeStruct((1, W), F32)], body, 48 * tr * W)


def _loss_grad(y, t):
    S, D = y.shape
    tr = _tile(S, 256, 16)

    def body(y_ref, t_ref, dy_ref, dyb_ref, ls_ref):
        e = y_ref[...] - t_ref[...]
        dy = e * (1.0 / D)
        dy_ref[...] = dy
        dyb_ref[...] = dy.astype(BF16)
        part = jnp.sum(e * e, axis=0, keepdims=True)

        @pl.when(pl.program_id(0) == 0)
        def _():
            ls_ref[...] = part

        @pl.when(pl.program_id(0) > 0)
        def _():
            ls_ref[...] += part

    return pl.pallas_call(
        body, name="loss_grad", grid=(S // tr,),
        in_specs=[_row_spec(tr, D), _row_spec(tr, D)],
        out_specs=[_row_spec(tr, D), _row_spec(tr, D), _full_spec((1, D))],
        out_shape=[jax.ShapeDtypeStruct((S, D), F32), jax.ShapeDtypeStruct((S, D), BF16),
                   jax.ShapeDtypeStruct((1, D), F32)],
        compiler_params=_params(32 * tr * D, ("arbitrary",)),
    )(y, t)


def _head_rms(x, g):
    r = lax.rsqrt(jnp.mean(x * x, axis=-1, keepdims=True) + EPS)
    return x * r * g


def _head_rms_bwd(x, dy, g):
    r = lax.rsqrt(jnp.mean(x * x, axis=-1, keepdims=True) + EPS)
    gy = dy * g
    dot = jnp.mean(x * gy, axis=-1, keepdims=True)
    return r * gy - x * (r * r * r * dot), dy * (x * r)


def _rope(y, cos2, sgn_sin):
    return y * cos2 + pltpu.roll(y, HEAD_DIM // 2, axis=1) * sgn_sin


def _rope_t(dy, cos2, sgn_sin):
    return dy * cos2 + pltpu.roll(dy * sgn_sin, HEAD_DIM // 2, axis=1)


def _qk_prep_fwd(proj, cos2, sgn_sin, gains, dims):
    S, P = proj.shape
    naw, waw, kvw = dims
    tr = _tile(S, 256, 16)
    hd = HEAD_DIM

    def body(p_ref, c_ref, s_ref, g_ref, qa_ref, ka_ref, va_ref, qb_ref, kb_ref, vb_ref):
        c2 = c_ref[...]
        ss = s_ref[...]
        off = 0
        for dst, width, gi, rot in ((qa_ref, naw, 0, False), (ka_ref, naw, 1, False), (va_ref, naw, None, False),
                                    (qb_ref, waw, 2, True), (kb_ref, kvw, 3, True), (vb_ref, kvw, None, False)):
            for h in range(width // hd):
                xh = p_ref[:, off + h * hd:off + (h + 1) * hd]
                if gi is not None:
                    xh = _head_rms(xh, g_ref[gi:gi + 1, :])
                    if rot:
                        xh = _rope(xh, c2, ss)
                dst[:, h * hd:(h + 1) * hd] = xh.astype(BF16)
            off += width

    widths = (naw, naw, naw, waw, kvw, kvw)
    return pl.pallas_call(
        body, name="qk_prep_fwd", grid=(S // tr,),
        in_specs=[_row_spec(tr, P), _row_spec(tr, hd), _row_spec(tr, hd), _full_spec((SUBLANES, hd))],
        out_specs=[_row_spec(tr, w) for w in widths],
        out_shape=[jax.ShapeDtypeStruct((S, w), BF16) for w in widths],
        compiler_params=_params(16 * tr * P, ("parallel",)),
    )(proj, cos2, sgn_sin, gains)


def _qk_prep_bwd(proj, dqa, dka, dva, dqb, dkb, dvb, cos2, sgn_sin, gains, dims, steps):
    S, P = proj.shape
    naw, waw, kvw = dims
    tr = S // steps
    hd = HEAD_DIM

    def body(step, ins, outs):
        p_ref, dqa_ref, dka_ref, dva_ref, dqb_ref, dkb_ref, dvb_ref, c_ref, s_ref, g_ref = ins
        dp_ref, dg_ref = outs
        c2 = c_ref[...]
        ss = s_ref[...]
        off = 0
        dgs = [jnp.zeros((1, hd), F32) for _ in range(4)]
        for src, width, gi, rot in ((dqa_ref, naw, 0, False), (dka_ref, naw, 1, False), (dva_ref, naw, None, False),
                                    (dqb_ref, waw, 2, True), (dkb_ref, kvw, 3, True), (dvb_ref, kvw, None, False)):
            for h in range(width // hd):
                dy = src[:, h * hd:(h + 1) * hd].astype(F32)
                if gi is not None:
                    if rot:
                        dy = _rope_t(dy, c2, ss)
                    xh = p_ref[:, off + h * hd:off + (h + 1) * hd]
                    dy, dgt = _head_rms_bwd(xh, dy, g_ref[gi:gi + 1, :])
                    dgs[gi] = dgs[gi] + jnp.sum(dgt, axis=0, keepdims=True)
                dp_ref[:, off + h * hd:off + (h + 1) * hd] = dy.astype(BF16)
            off += width
        _accumulate(step, dg_ref, jnp.concatenate(dgs + [jnp.zeros((SUBLANES - 4, hd), F32)], axis=0))

    return _Rider([proj, dqa, dka, dva, dqb, dkb, dvb, cos2, sgn_sin, gains],
                  [_row_spec(tr, P), _row_spec(tr, naw), _row_spec(tr, naw), _row_spec(tr, naw),
                   _row_spec(tr, waw), _row_spec(tr, kvw), _row_spec(tr, kvw),
                   _row_spec(tr, hd), _row_spec(tr, hd), _full_spec((SUBLANES, hd))],
                  [_row_spec(tr, P), _full_spec((SUBLANES, hd))],
                  [jax.ShapeDtypeStruct((S, P), BF16), jax.ShapeDtypeStruct((SUBLANES, hd), F32)], body, 24 * tr * P)


NA_KEYS = NA_WIN_R * GRID_W
NA_ROWS_PER_STEP = 32


def _na_col_geometry():
    c = np.arange(GRID_W)
    col_start = np.clip(c - NA_WIN_C // 2, 0, GRID_W - NA_WIN_C)
    mask = (c[None, :] >= col_start[:, None]) & (c[None, :] < col_start[:, None] + NA_WIN_C)
    dc = np.clip(c[None, :] - c[:, None], -(NA_WIN_C - 1), NA_WIN_C - 1) + (NA_WIN_C - 1)
    onehot = (dc[:, :, None] == np.arange(2 * NA_WIN_C - 1)[None, None, :]) & mask[:, :, None]
    return mask, onehot


def _na_bias_table(rpb_l):
    H = rpb_l.shape[0]
    mask, onehot = _na_col_geometry()
    t = jnp.sum(jnp.where(onehot[None, None], rpb_l[:, :, None, None, :], 0.0), axis=-1)
    t = jnp.where(mask[None, None], t, NEG)
    per_delta = [jnp.transpose(t[:, d:d + NA_WIN_R], (0, 2, 1, 3)).reshape(H, GRID_W, NA_KEYS) for d in range(NA_WIN_R)]
    return jnp.stack(per_delta, axis=1)


def _na_bias_table_t(db):
    H = db.shape[0]
    _, onehot = _na_col_geometry()
    d5 = db.reshape(H, NA_WIN_R, GRID_W, NA_WIN_R, GRID_W)
    folded = jnp.einsum("hdqwk,qkc->hdwc", d5, onehot.astype(np.float32), precision=lax.Precision.HIGHEST)
    return sum(jnp.pad(folded[:, d], ((0, 0), (d, NA_WIN_R - 1 - d), (0, 0))) for d in range(NA_WIN_R))


def _na_row_geometry(r, rows):
    start = jnp.clip(r - NA_WIN_R // 2, 0, rows - NA_WIN_R)
    return start, start - r + (NA_WIN_R - 1)


def _softmax_rows(s):
    m = jnp.max(s, axis=-1, keepdims=True)
    e = jnp.exp(s - m)
    return e / jnp.sum(e, axis=-1, keepdims=True)


def _na_fwd(qa, ka, va, btab):
    S, W = qa.shape
    H = W // HEAD_DIM
    rows = S // GRID_W
    assert rows >= NA_WIN_R
    rb = _tile(rows, NA_ROWS_PER_STEP, 1)
    tq = rb * GRID_W

    def body(q_ref, k_ref, v_ref, b_ref, o_ref):
        i = pl.program_id(1)

        geo = [_na_row_geometry(i * rb + j, rows) for j in range(rb)]
        toks = [pl.ds(j * GRID_W, GRID_W) for j in range(rb)]
        wins = [pl.ds(pl.multiple_of(start * GRID_W, GRID_W), NA_KEYS) for start, _ in geo]
        scores = [_dot(q_ref[toks[j], :], k_ref[wins[j], :], NT) for j in range(rb)]
        probs = [_softmax_rows(scores[j] * ATTN_SCALE + b_ref[geo[j][1]]) for j in range(rb)]
        for j in range(rb):
            o_ref[toks[j], :] = _dot(probs[j], v_ref[wins[j], :], NN)

    kv_spec = pl.BlockSpec((S, HEAD_DIM), lambda h, i: (0, h))
    return pl.pallas_call(
        body, name="na_fwd", grid=(H, rows // rb),
        in_specs=[pl.BlockSpec((tq, HEAD_DIM), lambda h, i: (i, h)), kv_spec, kv_spec,
                  pl.BlockSpec((None, NA_WIN_R, GRID_W, NA_KEYS), lambda h, i: (h, 0, 0, 0))],
        out_specs=pl.BlockSpec((tq, HEAD_DIM), lambda h, i: (i, h)),
        out_shape=jax.ShapeDtypeStruct((S, W), F32),
        compiler_params=_params(8 * S * HEAD_DIM + (8 << 20), ("parallel", "arbitrary")),
    )(qa, ka, va, btab)


def _na_bwd(qa, ka, va, btab, doa):
    S, W = qa.shape
    H = W // HEAD_DIM
    rows = S // GRID_W
    rb = _tile(rows, NA_ROWS_PER_STEP, 1)
    tq = rb * GRID_W

    def body(q_ref, k_ref, v_ref, b_ref, do_ref, dq_ref, dk_ref, dv_ref, db_ref):
        i = pl.program_id(1)

        @pl.when(i == 0)
        def _():
            dk_ref[...] = jnp.zeros_like(dk_ref)
            dv_ref[...] = jnp.zeros_like(dv_ref)
            db_ref[...] = jnp.zeros_like(db_ref)

        steps = range(rb)
        geo = [_na_row_geometry(i * rb + j, rows) for j in steps]
        toks = [pl.ds(j * GRID_W, GRID_W) for j in steps]
        wins = [pl.ds(pl.multiple_of(start * GRID_W, GRID_W), NA_KEYS) for start, _ in geo]
        scores = [_dot(q_ref[toks[j], :], k_ref[wins[j], :], NT) for j in steps]
        dps = [_dot(do_ref[toks[j], :], v_ref[wins[j], :], NT) for j in steps]
        probs = [_softmax_rows(scores[j] * ATTN_SCALE + b_ref[geo[j][1]]) for j in steps]
        dss = [probs[j] * (dps[j] - jnp.sum(probs[j] * dps[j], axis=-1, keepdims=True)) for j in steps]
        for j in steps:
            db_ref[geo[j][1]] += dss[j]
        dsb = [(dss[j] * ATTN_SCALE).astype(BF16) for j in steps]
        for j in steps:
            dq_ref[toks[j], :] = _dot(dsb[j], k_ref[wins[j], :], NN)
        dks = [_dot(dsb[j], q_ref[toks[j], :], TN) for j in steps]
        dvs = [_dot(probs[j], do_ref[toks[j], :], TN) for j in steps]
        for j in steps:
            dk_ref[wins[j], :] += dks[j]
            dv_ref[wins[j], :] += dvs[j]

    kv_spec = pl.BlockSpec((S, HEAD_DIM), lambda h, i: (0, h))
    q_spec = pl.BlockSpec((tq, HEAD_DIM), lambda h, i: (i, h))
    b_spec = pl.BlockSpec((None, NA_WIN_R, GRID_W, NA_KEYS), lambda h, i: (h, 0, 0, 0))
    return pl.pallas_call(
        body, name="na_bwd", grid=(H, rows // rb),
        in_specs=[q_spec, kv_spec, kv_spec, b_spec, q_spec],
        out_specs=[q_spec, kv_spec, kv_spec, b_spec],
        out_shape=[jax.ShapeDtypeStruct((S, W), F32), jax.ShapeDtypeStruct((S, W), F32),
                   jax.ShapeDtypeStruct((S, W), F32), jax.ShapeDtypeStruct(btab.shape, F32)],
        compiler_params=_params(24 * S * HEAD_DIM + (12 << 20), ("parallel", "arbitrary")),
    )(qa, ka, va, btab, doa)


WA_KEYS = 3 * WA_BLOCK
WA_BLOCKS_PER_STEP = 8


def _wa_mask(qk, n, start, g):
    s = qk * ATTN_SCALE
    qpos = n * WA_BLOCK + lax.broadcasted_iota(jnp.int32, (WA_BLOCK, WA_KEYS), 0)
    kpos = start + lax.broadcasted_iota(jnp.int32, (WA_BLOCK, WA_KEYS), 1)
    valid = jnp.abs(kpos - qpos) <= WA_WINDOW
    valid = jnp.concatenate([valid] * g, axis=0)
    return jnp.where(valid, s, NEG)


def _wa_probs(s, sink):
    m = jnp.maximum(jnp.max(s, axis=-1, keepdims=True), sink)
    e = jnp.exp(s - m)
    es = jnp.exp(sink - m)
    den = jnp.sum(e, axis=-1, keepdims=True) + es
    return e / den, es / den


def _wa_stack(ref, tok, g):
    return jnp.concatenate([ref[tok, t * HEAD_DIM:(t + 1) * HEAD_DIM] for t in range(g)], axis=0)


def _wa_fwd(qb, kb, vb, sink_col):
    S, W = qb.shape
    hkv = kb.shape[1] // HEAD_DIM
    g = W // HEAD_DIM // hkv
    nb = S // WA_BLOCK
    assert S >= WA_KEYS
    qb_step = _tile(nb, WA_BLOCKS_PER_STEP, 1)
    tq = qb_step * WA_BLOCK

    def body(q_ref, k_ref, v_ref, s_ref, o_ref):
        i = pl.program_id(1)

        steps = range(qb_step)
        ns = [i * qb_step + j for j in steps]
        toks = [pl.ds(j * WA_BLOCK, WA_BLOCK) for j in steps]
        starts = [pl.multiple_of(jnp.clip((n - 1) * WA_BLOCK, 0, S - WA_KEYS), WA_BLOCK) for n in ns]
        wins = [pl.ds(start, WA_KEYS) for start in starts]
        scores = [_dot(_wa_stack(q_ref, toks[j], g), k_ref[wins[j], :], NT) for j in steps]
        probs = [_wa_probs(_wa_mask(scores[j], ns[j], starts[j], g), s_ref[...])[0] for j in steps]
        outs = [_dot(probs[j], v_ref[wins[j], :], NN) for j in steps]
        for j in steps:
            for t in range(g):
                o_ref[toks[j], t * HEAD_DIM:(t + 1) * HEAD_DIM] = outs[j][t * WA_BLOCK:(t + 1) * WA_BLOCK]

    kv_spec = pl.BlockSpec((S, HEAD_DIM), lambda h, i: (0, h))
    q_spec = pl.BlockSpec((tq, g * HEAD_DIM), lambda h, i: (i, h))
    return pl.pallas_call(
        body, name="wa_fwd", grid=(hkv, nb // qb_step),
        in_specs=[q_spec, kv_spec, kv_spec, pl.BlockSpec((None, g * WA_BLOCK, 1), lambda h, i: (h, 0, 0))],
        out_specs=q_spec, out_shape=jax.ShapeDtypeStruct((S, W), F32),
        compiler_params=_params(8 * S * HEAD_DIM + (12 << 20), ("parallel", "arbitrary")),
    )(qb, kb, vb, sink_col)


def _wa_bwd(qb, kb, vb, sink_col, dob):
    S, W = qb.shape
    KW = kb.shape[1]
    hkv = KW // HEAD_DIM
    g = W // HEAD_DIM // hkv
    nb = S // WA_BLOCK
    qb_step = _tile(nb, WA_BLOCKS_PER_STEP, 1)
    tq = qb_step * WA_BLOCK

    def body(q_ref, k_ref, v_ref, s_ref, do_ref, dq_ref, dk_ref, dv_ref, dsink_ref):
        i = pl.program_id(1)

        @pl.when(i == 0)
        def _():
            dk_ref[...] = jnp.zeros_like(dk_ref)
            dv_ref[...] = jnp.zeros_like(dv_ref)
            dsink_ref[...] = jnp.zeros_like(dsink_ref)

        steps = range(qb_step)
        ns = [i * qb_step + j for j in steps]
        toks = [pl.ds(j * WA_BLOCK, WA_BLOCK) for j in steps]
        starts = [pl.multiple_of(jnp.clip((n - 1) * WA_BLOCK, 0, S - WA_KEYS), WA_BLOCK) for n in ns]
        wins = [pl.ds(start, WA_KEYS) for start in starts]
        qss = [_wa_stack(q_ref, toks[j], g) for j in steps]
        doss = [_wa_stack(do_ref, toks[j], g) for j in steps]
        scores = [_dot(qss[j], k_ref[wins[j], :], NT) for j in steps]
        dps = [_dot(doss[j], v_ref[wins[j], :], NT) for j in steps]
        pp = [_wa_probs(_wa_mask(scores[j], ns[j], starts[j], g), s_ref[...]) for j in steps]
        dsums = [jnp.sum(pp[j][0] * dps[j], axis=-1, keepdims=True) for j in steps]
        dsb = [(pp[j][0] * (dps[j] - dsums[j]) * ATTN_SCALE).astype(BF16) for j in steps]
        dsink_ref[...] -= sum(pp[j][1] * dsums[j] for j in steps)
        dqs = [_dot(dsb[j], k_ref[wins[j], :], NN) for j in steps]
        dks = [_dot(dsb[j], qss[j], TN) for j in steps]
        dvs = [_dot(pp[j][0], doss[j], TN) for j in steps]
        for j in steps:
            for t in range(g):
                dq_ref[toks[j], t * HEAD_DIM:(t + 1) * HEAD_DIM] = dqs[j][t * WA_BLOCK:(t + 1) * WA_BLOCK]
            dk_ref[wins[j], :] += dks[j]
            dv_ref[wins[j], :] += dvs[j]

    kv_spec = pl.BlockSpec((S, HEAD_DIM), lambda h, i: (0, h))
    q_spec = pl.BlockSpec((tq, g * HEAD_DIM), lambda h, i: (i, h))
    s_spec = pl.BlockSpec((None, g * WA_BLOCK, 1), lambda h, i: (h, 0, 0))
    return pl.pallas_call(
        body, name="wa_bwd", grid=(hkv, nb // qb_step),
        in_specs=[q_spec, kv_spec, kv_spec, s_spec, q_spec],
        out_specs=[q_spec, kv_spec, kv_spec, s_spec],
        out_shape=[jax.ShapeDtypeStruct((S, W), F32), jax.ShapeDtypeStruct((S, KW), F32),
                   jax.ShapeDtypeStruct((S, KW), F32), jax.ShapeDtypeStruct(sink_col.shape, F32)],
        compiler_params=_params(24 * S * HEAD_DIM + (16 << 20), ("parallel", "arbitrary")),
    )(qb, kb, vb, sink_col, dob)


CONV_HALO = SUBLANES


def _conv_specs(tr, width, nblk, rows_inner):
    per = tr // CONV_HALO

    def spec(shape, row_block):
        if rows_inner:
            return pl.BlockSpec(shape, lambda j, i: (row_block(i), j))
        return pl.BlockSpec(shape, lambda i, j: (row_block(i), j))

    cur = spec((tr, width), lambda i: i)
    prev = spec((CONV_HALO, width), lambda i: jnp.maximum(i * per - 1, 0))
    nxt = spec((CONV_HALO, width), lambda i: jnp.minimum((i + 1) * per, nblk * per - 1))
    return prev, cur, nxt


def _extend(prev_ref, cur_ref, next_ref, i, nblk):
    p = jnp.where(i > 0, prev_ref[...].astype(F32), 0.0)
    n = jnp.where(i < nblk - 1, next_ref[...].astype(F32), 0.0)
    return jnp.concatenate([p, cur_ref[...].astype(F32), n], axis=0)


def _shift_down(x):
    return pltpu.roll(x, 1, axis=0)


def _shift_up(x):
    return pltpu.roll(x, x.shape[0] - 1, axis=0)


def _conv(ext, w_ref, b_ref):
    return _shift_down(ext) * w_ref[0:1, :] + ext * w_ref[1:2, :] + _shift_up(ext) * w_ref[2:3, :] + b_ref[...]


def _sigmoid(x):
    return 1.0 / (1.0 + jnp.exp(-x))


def _convgate_fwd(up_pre, cw, cb, tc):
    S, C2 = up_pre.shape
    tr = _tile(S, 256, 16)
    nblk = S // tr
    prev, cur, nxt = _conv_specs(tr, 2 * tc, nblk, False)

    def body(p_ref, c_ref, n_ref, w_ref, b_ref, a_ref):
        i = pl.program_id(0)
        u = _conv(_extend(p_ref, c_ref, n_ref, i, nblk), w_ref, b_ref)[CONV_HALO:CONV_HALO + tr]
        gate, up = u[:, :tc], u[:, tc:]
        a_ref[...] = (gate * _sigmoid(gate) * up).astype(BF16)

    return pl.pallas_call(
        body, name="convgate_fwd", grid=(nblk, C2 // (2 * tc)),
        in_specs=[prev, cur, nxt, pl.BlockSpec((3, 2 * tc), lambda i, j: (0, j)),
                  pl.BlockSpec((1, 2 * tc), lambda i, j: (0, j))],
        out_specs=pl.BlockSpec((tr, tc), lambda i, j: (i, j)),
        out_shape=jax.ShapeDtypeStruct((S, C2 // 2), BF16),
        compiler_params=_params(48 * tr * tc, ("parallel", "parallel")),
    )(up_pre, up_pre, up_pre, cw, cb)


def _convgate_bwd(up_pre, dact, cw, cb, tc):
    S, C2 = up_pre.shape
    tr = _tile(S, 128, 16)
    nblk = S // tr
    prev, cur, nxt = _conv_specs(tr, 2 * tc, nblk, True)
    dprev, dcur, dnxt = _conv_specs(tr, tc, nblk, True)

    def body(p_ref, c_ref, n_ref, dp_ref, dc_ref, dn_ref, w_ref, b_ref, dx_ref, dw_ref, db_ref):
        i = pl.program_id(1)
        ext = _extend(p_ref, c_ref, n_ref, i, nblk)
        da = _extend(dp_ref, dc_ref, dn_ref, i, nblk)
        ext_dn, ext_up = _shift_down(ext), _shift_up(ext)
        u = ext_dn * w_ref[0:1, :] + ext * w_ref[1:2, :] + ext_up * w_ref[2:3, :] + b_ref[...]
        gate, up = u[:, :tc], u[:, tc:]
        sg = _sigmoid(gate)
        silu = gate * sg
        du = jnp.concatenate([da * up * (sg + silu * (1.0 - sg)), da * silu], axis=1)
        dx = _shift_up(du) * w_ref[0:1, :] + du * w_ref[1:2, :] + _shift_down(du) * w_ref[2:3, :]
        mid = slice(CONV_HALO, CONV_HALO + tr)
        dx_ref[...] = dx[mid].astype(BF16)
        duc = du[mid]
        dw = jnp.concatenate([jnp.sum(duc * ext_dn[mid], axis=0, keepdims=True),
                              jnp.sum(duc * ext[mid], axis=0, keepdims=True),
                              jnp.sum(duc * ext_up[mid], axis=0, keepdims=True)], axis=0)
        db = jnp.sum(duc, axis=0, keepdims=True)

        @pl.when(i == 0)
        def _():
            dw_ref[...] = dw
            db_ref[...] = db

        @pl.when(i > 0)
        def _():
            dw_ref[...] += dw
            db_ref[...] += db

    return pl.pallas_call(
        body, name="convgate_bwd", grid=(C2 // (2 * tc), nblk),
        in_specs=[prev, cur, nxt, dprev, dcur, dnxt,
                  pl.BlockSpec((3, 2 * tc), lambda j, i: (0, j)), pl.BlockSpec((1, 2 * tc), lambda j, i: (0, j))],
        out_specs=[pl.BlockSpec((tr, 2 * tc), lambda j, i: (i, j)),
                   pl.BlockSpec((3, 2 * tc), lambda j, i: (0, j)), pl.BlockSpec((1, 2 * tc), lambda j, i: (0, j))],
        out_shape=[jax.ShapeDtypeStruct((S, C2), BF16), jax.ShapeDtypeStruct((3, C2), F32),
                   jax.ShapeDtypeStruct((1, C2), F32)],
        compiler_params=_params(160 * tr * tc, ("parallel", "arbitrary")),
    )(up_pre, up_pre, up_pre, dact, dact, dact, cw, cb)


def _rope_tables(positions):
    inv = ROPE_THETA ** (-jnp.arange(0, HEAD_DIM, 2, dtype=F32) / HEAD_DIM)
    ang = positions.astype(F32)[:, None] * inv[None, :]
    cos, sin = jnp.cos(ang), jnp.sin(ang)
    return jnp.concatenate([cos, cos], axis=1), jnp.concatenate([-sin, sin], axis=1)


def _sink_col(sink_l, hkv):
    g = sink_l.shape[0] // hkv
    return jnp.broadcast_to(sink_l.reshape(hkv, g, 1), (hkv, g, WA_BLOCK)).reshape(hkv, g * WA_BLOCK, 1)


def _sink_col_t(dcol):
    hkv, rows, _ = dcol.shape
    return jnp.sum(dcol.reshape(hkv, rows // WA_BLOCK, WA_BLOCK), axis=-1).reshape(-1)


def _to_paired(a, sh):
    return jnp.concatenate([a[:, sh.nat(t) * sh.tn:(sh.nat(t) + 1) * sh.tn] for t in range(sh.ntiles)], axis=1)


def _from_paired(a, sh):
    pos = {sh.nat(t): t for t in range(sh.ntiles)}
    return jnp.concatenate([a[:, pos[n] * sh.tn:(pos[n] + 1) * sh.tn] for n in range(sh.ntiles)], axis=1)


def _pack(arrs):
    flat = jnp.concatenate([a.reshape(-1).astype(F32) for a in arrs])
    unit = SUBLANES * LANES
    total = -(-flat.shape[0] // unit) * unit
    return jnp.pad(flat, (0, total - flat.shape[0])).reshape(-1, LANES)


def _unpack(buf, shapes):
    flat = buf.reshape(-1)
    out, off = [], 0
    for s in shapes:
        n = math.prod(s)
        out.append(flat[off:off + n].reshape(s))
        off += n
    return out


HBM_SPEC = pl.BlockSpec(memory_space=pltpu.HBM)
DMA_CHUNK_BYTES = 512 * 1024


def _row_chunks(rows, row_bytes, align):
    want = max(1, rows * row_bytes // DMA_CHUNK_BYTES)
    count = max(k for k in range(1, rows + 1) if rows % k == 0 and (rows // k) % align == 0 and (k <= want or k == 1))
    size = rows // count
    return [(j * size, size) for j in range(count)]


def _mesh_pos():
    return lax.axis_index("x"), lax.axis_index("y"), lax.axis_index("c")


def _other_chips(x, y):
    return [(1 - x, y), (x, 1 - y), (1 - x, 1 - y)]


def _remote(src, dst, send_sems, recv_sems, k, dev):
    return pltpu.make_async_remote_copy(src_ref=src, dst_ref=dst, send_sem=send_sems.at[k], recv_sem=recv_sems.at[k],
                                        device_id=dev, device_id_type=MESH)


class _Exchange:
    def __init__(self, operands, out_shapes, aliases, start, wait):
        self.operands, self.out_shapes, self.aliases = list(operands), list(out_shapes), dict(aliases)
        self.start, self.wait = start, wait
        self.n_sems = len(out_shapes)

    def scratch(self):
        return [pltpu.SemaphoreType.DMA((self.n_sems,)), pltpu.SemaphoreType.DMA((self.n_sems,))]


class _SemSlice:
    def __init__(self, sems, offset):
        self._sems, self._offset = sems, offset

    @property
    def at(self):
        return self

    def __getitem__(self, k):
        return self._sems.at[self._offset + k]


def _both(a, b):
    ia, oa = len(a.operands), a.n_sems

    def split(ins, outs, send_sems, recv_sems):
        return ((ins[:ia], outs[:oa], send_sems, recv_sems),
                (ins[ia:], outs[oa:], _SemSlice(send_sems, oa), _SemSlice(recv_sems, oa)))

    def start(*refs):
        first, second = split(*refs)
        a.start(*first)
        b.start(*second)

    def wait(*refs):
        first, second = split(*refs)
        a.wait(*first)
        b.wait(*second)

    aliases = dict(a.aliases)
    aliases.update({ia + i: oa + o for i, o in b.aliases.items()})
    return _Exchange(a.operands + b.operands, a.out_shapes + b.out_shapes, aliases, start, wait)


def _exchange_alone(name, ex):
    n_in = len(ex.operands)

    def body(*refs):
        ins, outs = refs[:n_in], refs[n_in:n_in + ex.n_sems]
        send_sems, recv_sems = refs[n_in + ex.n_sems:]
        ex.start(ins, outs, send_sems, recv_sems)
        ex.wait(ins, outs, send_sems, recv_sems)

    return pl.pallas_call(
        body, name=name, in_specs=[HBM_SPEC] * n_in, out_specs=[HBM_SPEC] * ex.n_sems, out_shape=ex.out_shapes,
        input_output_aliases=ex.aliases, scratch_shapes=ex.scratch(),
    )(*ex.operands)


def _ag_chip_exchange(gathered, small):
    n = len(gathered)
    arrays = list(gathered) + ([] if small is None else [small])

    def start(ins, outs, send_sems, recv_sems):
        x, y, c = _mesh_pos()
        q_me = 2 * x + y
        chips = _other_chips(x, y)
        for t in range(n):
            h = gathered[t].shape[1] // 2
            for s0, sz in _row_chunks(h, gathered[t].shape[2] * 2, 16):
                blk = outs[t].at[q_me, pl.ds(c * h + s0, sz)]
                for chip in chips:
                    _remote(blk, blk, send_sems, recv_sems, t, (*chip, c)).start()
        if small is not None:
            for chip in chips:
                _remote(outs[n].at[q_me], outs[n].at[q_me], send_sems, recv_sems, n, (*chip, c)).start()

    def wait(ins, outs, send_sems, recv_sems):
        x, y, c = _mesh_pos()
        for t in range(n):
            three = outs[t].at[pl.ds(0, 3), pl.ds(0, gathered[t].shape[1] // 2)]
            _remote(three, three, send_sems, recv_sems, t, (x, y, 1 - c)).wait()
        if small is not None:
            three = outs[n].at[pl.ds(0, 3)]
            _remote(three, three, send_sems, recv_sems, n, (x, y, 1 - c)).wait()

    return _Exchange(arrays, [jax.ShapeDtypeStruct(a.shape, a.dtype) for a in arrays],
                     {i: i for i in range(len(arrays))}, start, wait)


def _ag_sibling_pass(gathered):
    n = len(gathered)

    def start(ins, outs, send_sems, recv_sems):
        x, y, c = _mesh_pos()
        for t in range(n):
            h = gathered[t].shape[1] // 2
            for s0, sz in _row_chunks(h, gathered[t].shape[2] * 2, 16):
                for cx, cy in _other_chips(x, y):
                    blk = outs[t].at[2 * cx + cy, pl.ds(c * h + s0, sz)]
                    _remote(blk, blk, send_sems, recv_sems, t, (x, y, 1 - c)).start()

    def wait(ins, outs, send_sems, recv_sems):
        x, y, c = _mesh_pos()
        for t in range(n):
            three = outs[t].at[pl.ds(0, 3), pl.ds(0, gathered[t].shape[1] // 2)]
            _remote(three, three, send_sems, recv_sems, t, (x, y, 1 - c)).wait()

    return _Exchange(gathered, [jax.ShapeDtypeStruct(a.shape, a.dtype) for a in gathered],
                     {i: i for i in range(n)}, start, wait)


def _rs_sibling_exchange(grads):
    n = len(grads)
    halves = [g.shape[1] // 2 for g in grads]

    def start(ins, outs, send_sems, recv_sems):
        x, y, c = _mesh_pos()
        for t in range(n):
            for q in range(N_CHIPS):
                for s0, sz in _row_chunks(halves[t], grads[t].shape[2] * 4, SUBLANES):
                    _remote(ins[t].at[q, pl.ds((1 - c) * halves[t] + s0, sz)], outs[t].at[q, pl.ds(s0, sz)],
                            send_sems, recv_sems, t, (x, y, 1 - c)).start()

    def wait(ins, outs, send_sems, recv_sems):
        x, y, c = _mesh_pos()
        for t in range(n):
            _remote(outs[t], outs[t], send_sems, recv_sems, t, (x, y, 1 - c)).wait()

    return _Exchange(grads, [jax.ShapeDtypeStruct((N_CHIPS, h, g.shape[2]), F32) for g, h in zip(grads, halves)],
                     {}, start, wait)


def _rs_chip_exchange(pbs):
    n = len(pbs)

    def start(ins, outs, send_sems, recv_sems):
        x, y, c = _mesh_pos()
        for t in range(n):
            for s0, sz in _row_chunks(pbs[t].shape[1], pbs[t].shape[2] * 2, 16):
                for k, (cx, cy) in enumerate(_other_chips(x, y)):
                    _remote(ins[t].at[2 * cx + cy, pl.ds(s0, sz)], outs[t].at[k, pl.ds(s0, sz)],
                            send_sems, recv_sems, t, (cx, cy, c)).start()

    def wait(ins, outs, send_sems, recv_sems):
        x, y, c = _mesh_pos()
        for t in range(n):
            _remote(outs[t], outs[t], send_sems, recv_sems, t, (x, y, 1 - c)).wait()

    return _Exchange(pbs, [jax.ShapeDtypeStruct((3,) + p.shape[1:], BF16) for p in pbs], {}, start, wait)


def _rs_sibling_share(gs):
    n = len(gs)

    def start(ins, outs, send_sems, recv_sems):
        x, y, c = _mesh_pos()
        for t in range(n):
            h = gs[t].shape[0] // 2
            for s0, sz in _row_chunks(h, gs[t].shape[1] * 4, SUBLANES):
                rows = outs[t].at[pl.ds(c * h + s0, sz)]
                _remote(rows, rows, send_sems, recv_sems, t, (x, y, 1 - c)).start()

    def wait(ins, outs, send_sems, recv_sems):
        x, y, c = _mesh_pos()
        for t in range(n):
            half = outs[t].at[pl.ds(0, gs[t].shape[0] // 2)]
            _remote(half, half, send_sems, recv_sems, t, (x, y, 1 - c)).wait()

    return _Exchange(gs, [jax.ShapeDtypeStruct(g.shape, F32) for g in gs], {i: i for i in range(n)}, start, wait)


def _gather_small(buf):
    def body(in_ref, out_ref, send_sems, recv_sems, loc_sem):
        x, y, c = _mesh_pos()
        me = 4 * x + 2 * y + c
        local = pltpu.make_async_copy(in_ref, out_ref.at[me], loc_sem)
        local.start()
        sends, peers = [], []
        for k in range(1, N_DEV):
            bx, by, bc = (k >> 2) & 1, (k >> 1) & 1, k & 1
            peer = (x + bx - 2 * x * bx, y + by - 2 * y * by, c + bc - 2 * c * bc)
            peers.append(peer)
            sends.append(_remote(in_ref, out_ref.at[me], send_sems, recv_sems, k - 1, peer))
        for cp in sends:
            cp.start()
        for k, (px, py, pc) in enumerate(peers):
            slot = out_ref.at[4 * px + 2 * py + pc]
            _remote(slot, slot, send_sems, recv_sems, k, (px, py, pc)).wait_recv()
        for cp in sends:
            cp.wait_send()
        local.wait()

    return pl.pallas_call(
        body, name="gather_small", in_specs=[HBM_SPEC], out_specs=HBM_SPEC,
        out_shape=jax.ShapeDtypeStruct((N_DEV,) + buf.shape, F32),
        scratch_shapes=[pltpu.SemaphoreType.DMA((N_DEV - 1,)), pltpu.SemaphoreType.DMA((N_DEV - 1,)),
                        pltpu.SemaphoreType.DMA(())],
    )(buf)


def _ew_rows(rows, cols, bytes_per_elem):
    budget = 20 * 1024 * 1024
    return _tile(rows, max(16, budget // (2 * bytes_per_elem * cols) // 16 * 16), 16)


def _sum_pair(grad, recv):
    _, h, cols = recv.shape
    tr = _ew_rows(h, cols, 14)
    nb = h // tr
    half = pl.BlockSpec((None, tr, cols), lambda q, i: (q, i, 0))

    def body(a_ref, b_ref, f_ref, h_ref):
        s = a_ref[...] + b_ref[...]
        f_ref[...] = s
        h_ref[...] = s.astype(BF16)

    return pl.pallas_call(
        body, name="rs_sum_pair", grid=(N_CHIPS, nb),
        in_specs=[pl.BlockSpec((None, tr, cols), lambda q, i: (q, lax.axis_index("c") * nb + i, 0)), half],
        out_specs=[half, half],
        out_shape=[jax.ShapeDtypeStruct(recv.shape, F32), jax.ShapeDtypeStruct(recv.shape, BF16)],
        compiler_params=_params(28 * tr * cols, ("parallel", "parallel")),
    )(grad, recv)


def _sum_four(pf, recv):
    _, h, cols = pf.shape
    tr = _ew_rows(h, cols, 14)
    nb = h // tr

    def body(a_ref, r_ref, o_ref):
        o_ref[...] = ((a_ref[...] + r_ref[0].astype(F32)) + r_ref[1].astype(F32)) + r_ref[2].astype(F32)

    return pl.pallas_call(
        body, name="rs_sum_four", grid=(nb,),
        in_specs=[pl.BlockSpec((None, tr, cols), lambda i: (2 * lax.axis_index("x") + lax.axis_index("y"), i, 0)),
                  pl.BlockSpec((3, tr, cols), lambda i: (0, i, 0))],
        out_specs=pl.BlockSpec((tr, cols), lambda i: (lax.axis_index("c") * nb + i, 0)),
        out_shape=jax.ShapeDtypeStruct((2 * h, cols), F32),
        compiler_params=_params(28 * tr * cols, ("parallel",)),
    )(pf, recv)


def _sum_eight(gathered):
    _, rows, cols = gathered.shape
    tr = _tile(rows, 512, SUBLANES)

    def body(g_ref, o_ref):
        s = g_ref[0]
        for d in range(1, N_DEV):
            s = s + g_ref[d]
        o_ref[...] = s

    return pl.pallas_call(
        body, name="sum_eight", grid=(rows // tr,),
        in_specs=[pl.BlockSpec((N_DEV, tr, cols), lambda i: (0, i, 0))],
        out_specs=_row_spec(tr, cols), out_shape=jax.ShapeDtypeStruct((rows, cols), F32),
        compiler_params=_params(None, ("parallel",)),
    )(gathered)


def _cast_layer(w, l):
    _, rows, cols = w.shape
    tr = _ew_rows(rows, cols, 6)

    def body(w_ref, o_ref):
        o_ref[...] = w_ref[...].astype(BF16)

    return pl.pallas_call(
        body, name="cast_bf16", grid=(rows // tr,),
        in_specs=[pl.BlockSpec((None, tr, cols), lambda i: (l, i, 0))],
        out_specs=pl.BlockSpec((None, tr, cols), lambda i: (2 * lax.axis_index("x") + lax.axis_index("y"), i, 0)),
        out_shape=jax.ShapeDtypeStruct((N_CHIPS, rows, cols), BF16),
        compiler_params=_params(12 * tr * cols, ("parallel",)),
    )(w)


def _adamw(g, w, m, v, l, prev):
    L, rows, cols = w.shape
    tr = _ew_rows(rows, cols, 32)
    layer = pl.BlockSpec((None, tr, cols), lambda i: (l, i, 0))

    def body(g_ref, w_ref, m_ref, v_ref, *rest):
        og_ref, od_ref, om_ref, ov_ref = rest[-4:]
        gv = g_ref[...]
        mn = ADAM_B1 * m_ref[...] + (1.0 - ADAM_B1) * gv
        vn = ADAM_B2 * v_ref[...] + (1.0 - ADAM_B2) * (gv * gv)
        m_hat = mn / ADAM_C1
        v_hat = vn / ADAM_C2
        od_ref[...] = -ADAM_LR * (m_hat / (jnp.sqrt(v_hat) + ADAM_EPS) + ADAM_WD * w_ref[...])
        og_ref[...] = gv
        om_ref[...] = mn
        ov_ref[...] = vn

    in_specs = [_row_spec(tr, cols), layer, layer, layer]
    args = [g, w, m, v]
    aliases = {}
    if prev is not None:
        in_specs += [HBM_SPEC] * 4
        args += list(prev)
        aliases = {4 + i: i for i in range(4)}
    return pl.pallas_call(
        body, name="adamw", grid=(rows // tr,), in_specs=in_specs, out_specs=[layer] * 4,
        out_shape=[jax.ShapeDtypeStruct((L, rows, cols), F32)] * 4, input_output_aliases=aliases,
        compiler_params=_params(64 * tr * cols, ("parallel",)),
    )(*args)


WEIGHT_ORDER = ("ln1_g", "w_in", "qn_a", "kn_a", "rpb", "qn_b", "kn_b", "sink", "on_a", "on_b", "w_out", "ln2_g",
                "w_up", "conv_w", "conv_b", "w_down")
BIG = ("w_in", "w_out", "w_up", "w_down")
SMALL = tuple(n for n in WEIGHT_ORDER if n not in BIG)


def _train_step(x, positions, target, w, m, v):
    S, D = x.shape
    L = w["w_in"].shape[0]
    naw = w["rpb"].shape[1] * HEAD_DIM
    waw = w["sink"].shape[1] * HEAD_DIM
    cin = w["w_in"].shape[2]
    kvw = (N_CHIPS * cin - 3 * naw - waw) // 2
    hkv = kvw // HEAD_DIM
    dims = (naw, waw, kvw)
    assert naw == waw, "the two head groups are normalised by one kernel and must be equally wide"
    cup = w["w_up"].shape[2]
    c2 = N_CHIPS * cup
    dff = c2 // 2
    mix = naw + waw
    sh_in = _ColShards(cin, _tile(cin, 1152, LANES), False)
    sh_up = _ColShards(cup, cup // 2 if (cup // 2) % LANES == 0 else cup, True)
    tc = sh_up.tn
    cos2, sgn_sin = _rope_tables(positions)
    q_me = 2 * lax.axis_index("x") + lax.axis_index("y")
    row = lambda a: a[None]

    def own_slots(l):
        small = lax.dynamic_update_slice_in_dim(jnp.zeros((N_CHIPS,) + w["conv_w"].shape[1:], F32),
                                                w["conv_w"][l][None], q_me, axis=0)
        return [_cast_layer(w[n], l) for n in BIG], small

    gath = {}

    def prepare(l):
        bufs, small = own_slots(l)
        gath.update({(l, n): b for n, b in zip(BIG, bufs)})
        gath[l, "conv_w"] = small

    def over_ici(l, names, small=False):
        keys = [(l, n) for n in names]
        ex = _ag_chip_exchange([gath[k] for k in keys], gath[l, "conv_w"] if small else None)
        return ex, keys + ([(l, "conv_w")] if small else [])

    def to_sibling(l, names):
        keys = [(l, n) for n in names]
        return _ag_sibling_pass([gath[k] for k in keys]), keys

    def together(parts):
        ex, keys = parts[0]
        for e, k in parts[1:]:
            ex, keys = _both(ex, e), keys + k
        return ex, keys

    def carried(call, parts):
        if not parts:
            return call(None)
        ex, keys = together(parts)
        out, results = call(ex)
        gath.update(dict(zip(keys, results)))
        return out

    prepare(0)
    ex, keys = over_ici(0, ("w_in", "w_out"), True)
    gath.update(dict(zip(keys, _exchange_alone("ag_first_over_ici", ex))))
    ex, keys = to_sibling(0, ("w_in", "w_out"))
    gath.update(dict(zip(keys, _exchange_alone("ag_first_to_sibling", ex))))
    saved = []
    for l in range(L):
        more = l + 1 < L
        if more:
            prepare(l + 1)
        g_cw = gath[l, "conv_w"]
        cw_p = _to_paired(jnp.transpose(g_cw, (1, 0, 2)).reshape(3, c2), sh_up)
        cb_p = _to_paired(row(w["conv_b"][l]), sh_up)
        gains = jnp.concatenate([row(w["qn_a"][l]), row(w["kn_a"][l]), row(w["qn_b"][l]), row(w["kn_b"][l]),
                                 jnp.zeros((SUBLANES - 4, HEAD_DIM), F32)], axis=0)
        btab = _na_bias_table(w["rpb"][l])
        sink_col = _sink_col(w["sink"][l], hkv)

        h = _rms_fwd(x, row(w["ln1_g"][l]))
        g_in = gath[l, "w_in"]
        proj = carried(lambda ex: _mm_cols_fwd("mm_proj", h, g_in, sh_in, F32, tm=1024, carry=ex),
                       [over_ici(0, ("w_up",))] if l == 0 else [to_sibling(l, ("w_down",))])
        qa, ka, va, qb, kb, vb = _qk_prep_fwd(proj, cos2, sgn_sin, gains, dims)
        oa = _na_fwd(qa, ka, va, btab)
        ob = _wa_fwd(qb, kb, vb, sink_col)
        o = _out_norm_fwd(oa, ob, row(w["on_a"][l]), row(w["on_b"][l]))
        g_out = gath[l, "w_out"].reshape(mix, D)
        x1 = carried(lambda ex: _mm_plain("mm_attn_out", o, g_out, NN, F32, tm=1024, tn=1024, tk=2048, res=x, carry=ex),
                     [to_sibling(0, ("w_up",)), over_ici(0, ("w_down",))] if l == 0 else [])
        h2 = _rms_fwd(x1, row(w["ln2_g"][l]))
        g_up = gath[l, "w_up"]
        up_pre = carried(lambda ex: _mm_cols_fwd("mm_up", h2, g_up, sh_up, F32, tm=1024, carry=ex),
                         ([to_sibling(0, ("w_down",))] if l == 0 else [])
                         + ([over_ici(l + 1, ("w_in", "w_out", "w_up"), True)] if more else []))
        act = _convgate_fwd(up_pre, cw_p, cb_p, tc)
        g_dn = gath[l, "w_down"].reshape(dff, D)
        x2 = carried(lambda ex: _mm_plain("mm_down", act, g_dn, NN, F32, tm=1024, tn=512, tk=5632, res=x1, carry=ex),
                     [to_sibling(l + 1, ("w_in", "w_out", "w_up")), over_ici(l + 1, ("w_down",))] if more else [])
        saved.append(dict(g_in=g_in, g_out=g_out, g_up=g_up, g_dn=g_dn, cw_p=cw_p, cb_p=cb_p, gains=gains, btab=btab,
                          sink_col=sink_col, x=x, h=h, proj=proj, qa=qa, ka=ka, va=va, qb=qb, kb=kb, vb=vb, oa=oa,
                          ob=ob, o=o, x1=x1, h2=h2, up_pre=up_pre, act=act))
        x = x2

    dx, dxb, loss_cols = _loss_grad(x, target)

    small_grads = [None] * L
    stacked = {n: None for n in BIG}

    def update(layer, shards):
        for n, g in zip(BIG, shards):
            stacked[n] = _adamw(g, w[n], m[n], v[n], layer, stacked[n])

    above = None
    for l in reversed(range(L)):
        s = saved[l]
        if above is None:
            dact = _mm_plain("mm_down_dx", dxb, s["g_dn"], NT, F32, tm=1024, tn=1408, tk=2048)
        else:
            dact, recvs = _mm_plain("mm_down_dx_rs", dxb, s["g_dn"], NT, F32, tm=1024, tn=1408, tk=2048,
                                    carry=_rs_sibling_exchange(above[1]))
            pairs = [_sum_pair(g, r) for g, r in zip(above[1], recvs)]
        dup_pre, dcw_p, dcb_p = _convgate_bwd(s["up_pre"], dact, s["cw_p"], s["cb_p"], tc)
        if above is None:
            dh2 = _mm_cols_bwd_x("mm_up_dx", dup_pre, s["g_up"], sh_up, F32, tm=1024, tn=2048)
        else:
            dh2, gots = _mm_cols_bwd_x("mm_up_dx_rs", dup_pre, s["g_up"], sh_up, F32, tm=1024, tn=2048,
                                       carry=_rs_chip_exchange([p[1] for p in pairs]))
            halves = [_sum_four(p[0], r) for p, r in zip(pairs, gots)]
        d_dn, (dx1, dx1b, d_ln2) = _mm_plain(
            "mm_down_dw", s["act"], dxb, TN, F32, tm=1408, tn=512, tk=2048,
            rider=lambda steps: _rms_bwd(dh2, s["x1"], row(w["ln2_g"][l]), dx, steps))
        d_dn = d_dn.reshape(N_CHIPS, dff // N_CHIPS, D)
        if above is None:
            do = _mm_plain("mm_attn_out_dx", dx1b, s["g_out"], NT, F32, tm=1024, tn=1024, tk=2048)
        else:
            do, shards = _mm_plain("mm_attn_out_dx_rs", dx1b, s["g_out"], NT, F32, tm=1024, tn=1024, tk=2048,
                                   carry=_rs_sibling_share(halves))
            update(above[0], shards)
        last = l == 0
        d_out, (doa, dob, d_on_a, d_on_b), *rest = _mm_plain(
            "mm_attn_out_dw", s["o"], dx1b, TN, F32, tm=1024, tn=512, tk=2048,
            carry=_rs_sibling_exchange([d_dn]) if last else None,
            rider=lambda steps: _out_norm_bwd(do, s["oa"], s["ob"], row(w["on_a"][l]), row(w["on_b"][l]), steps))
        d_out = d_out.reshape(N_CHIPS, mix // N_CHIPS, D)
        if last:
            pair_dn = _sum_pair(d_dn, rest[0][0])
        dqa, dka, dva, dbtab = _na_bwd(s["qa"], s["ka"], s["va"], s["btab"], doa)
        dqb, dkb, dvb, dsink_col = _wa_bwd(s["qb"], s["kb"], s["vb"], s["sink_col"], dob)
        d_up, (dproj, dgains) = _mm_cols_bwd_w(
            "mm_up_dw", s["h2"], dup_pre, sh_up, tm=1024, tk=2048,
            rider=lambda steps: _qk_prep_bwd(s["proj"], dqa, dka, dva, dqb, dkb, dvb, cos2, sgn_sin, s["gains"], dims, steps))
        if last:
            dh, (recv_up, got_dn) = _mm_cols_bwd_x(
                "mm_proj_dx", dproj, s["g_in"], sh_in, F32, tm=1024, tn=2048,
                carry=_both(_rs_sibling_exchange([d_up]), _rs_chip_exchange([pair_dn[1]])))
            pair_up = _sum_pair(d_up, recv_up)
        else:
            dh = _mm_cols_bwd_x("mm_proj_dx", dproj, s["g_in"], sh_in, F32, tm=1024, tn=2048)
        d_in, (dx, dxb, d_ln1), *rest = _mm_cols_bwd_w(
            "mm_proj_dw", s["h"], dproj, sh_in, tm=512, tk=2048,
            carry=_rs_chip_exchange([pair_up[1]]) if last else None,
            rider=lambda steps: _rms_bwd(dh, s["x"], row(w["ln1_g"][l]), dx1, steps))
        if last:
            got_up = rest[0][0]

        small_grads[l] = dict(
            ln1_g=d_ln1[0], qn_a=dgains[0], kn_a=dgains[1], rpb=_na_bias_table_t(dbtab), qn_b=dgains[2],
            kn_b=dgains[3], sink=_sink_col_t(dsink_col), on_a=d_on_a[0], on_b=d_on_b[0], ln2_g=d_ln2[0],
            conv_w=_from_paired(dcw_p, sh_up), conv_b=_from_paired(dcb_p, sh_up)[0])
        above = (l, [d_in, d_out, d_up, d_dn])
    recvs = _exchange_alone("rs_sibling_exchange", _rs_sibling_exchange([d_in, d_out]))
    pair_in, pair_out = _sum_pair(d_in, recvs[0]), _sum_pair(d_out, recvs[1])
    got_in, got_out = _exchange_alone("rs_chip_exchange", _rs_chip_exchange([pair_in[1], pair_out[1]]))
    halves = [_sum_four(p[0], g) for p, g in ((pair_in, got_in), (pair_out, got_out), (pair_up, got_up), (pair_dn, got_dn))]
    update(0, _exchange_alone("rs_sibling_share", _rs_sibling_share(halves)))
    grad_x = dx[None]

    full_shapes = {n: (w[n].shape[1:] if n != "conv_w" else (3, c2)) for n in SMALL}
    packed = _pack([loss_cols] + [small_grads[l][n] for l in range(L) for n in SMALL])
    total = _sum_eight(_gather_small(packed))
    parts = _unpack(total, [(D,)] + [full_shapes[n] for _ in range(L) for n in SMALL])
    loss = 0.5 * jnp.sum(parts[0]) / D
    per_layer = [dict(zip(SMALL, parts[1 + l * len(SMALL):1 + (l + 1) * len(SMALL)])) for l in range(L)]
    for l in range(L):
        per_layer[l]["conv_w"] = lax.dynamic_slice_in_dim(per_layer[l]["conv_w"], q_me * cup, cup, axis=1)
    small_g = {n: jnp.stack([per_layer[l][n] for l in range(L)]) for n in SMALL}
    outs = _adamw(_pack([small_g[n] for n in SMALL]), _pack([w[n] for n in SMALL])[None],
                  _pack([m[n] for n in SMALL])[None], _pack([v[n] for n in SMALL])[None], 0, None)
    small_out = [dict(zip(SMALL, _unpack(o[0], [w[n].shape for n in SMALL]))) for o in outs]

    result = [loss, grad_x]
    for i in range(4):
        result += [stacked[n][i] if n in BIG else small_out[i][n] for n in WEIGHT_ORDER]
    return tuple(result)


def kernel(x, positions, ln1_g, w_in, qn_a, kn_a, rpb, qn_b, kn_b, sink, on_a, on_b, w_out, ln2_g, w_up, conv_w, conv_b, w_down, loss_target, m_ln1_g, m_w_in, m_qn_a, m_kn_a, m_rpb, m_qn_b, m_kn_b, m_sink, m_on_a, m_on_b, m_w_out, m_ln2_g, m_w_up, m_conv_w, m_conv_b, m_w_down, v_ln1_g, v_w_in, v_qn_a, v_kn_a, v_rpb, v_qn_b, v_kn_b, v_sink, v_on_a, v_on_b, v_w_out, v_ln2_g, v_w_up, v_conv_w, v_conv_b, v_w_down):
    given = dict(locals())
    w = {n: given[n] for n in WEIGHT_ORDER}
    m = {n: given["m_" + n] for n in WEIGHT_ORDER}
    v = {n: given["v_" + n] for n in WEIGHT_ORDER}
    return _train_step(x[0], positions, loss_target[0], w, m, v)
```

```python
import functools
import math

import jax
import jax.numpy as jnp
import numpy as np
from jax import lax
from jax.experimental import pallas as pl
from jax.experimental.pallas import tpu as pltpu

F32 = jnp.float32
BF16 = jnp.bfloat16

HEAD_DIM = 128
GRID_W = 64
NA_WIN_R = 8
NA_WIN_C = 16
WA_WINDOW = 128
WA_BLOCK = 128
ROPE_THETA = 10000.0
EPS = 1e-6
NEG = -1e30
ATTN_SCALE = 1.0 / math.sqrt(HEAD_DIM)

ADAM_LR = 0.001
ADAM_B1 = 0.9
ADAM_B2 = 0.999
ADAM_EPS = 1e-08
ADAM_WD = 0.01
ADAM_STEP = 10
ADAM_C1 = 1.0 - ADAM_B1 ** ADAM_STEP
ADAM_C2 = 1.0 - ADAM_B2 ** ADAM_STEP

V7X_VMEM_BYTES = 64 * 1024 * 1024
V7X_VMEM_CAP = V7X_VMEM_BYTES - 6 * 1024 * 1024
LANES = 128
SUBLANES = 8
N_CHIPS = 4
N_DEV = 8
MESH = pl.DeviceIdType.MESH

NN = ((1,), (0,))
NT = ((1,), (1,))
TN = ((0,), (0,))


def _tile(n, pref, mult):
    best = None
    d = mult
    while d <= min(n, pref):
        if n % d == 0:
            best = d
        d += mult
    return n if best is None else best


def _nbytes(shape, dtype):
    n = 1
    for s in shape:
        if s is not None:
            n *= s
    return n * jnp.dtype(dtype).itemsize


def _params(est_bytes=None, sem=None, **kw):
    if est_bytes is not None:
        kw["vmem_limit_bytes"] = int(min(V7X_VMEM_CAP, max(32 * 1024 * 1024, est_bytes * 5 // 4 + (4 << 20))))
    if sem is not None:
        kw["dimension_semantics"] = sem
    return pltpu.CompilerParams(**kw)


def _dot(a, b, contract):
    return lax.dot_general(a.astype(BF16), b.astype(BF16), (contract, ((), ())), preferred_element_type=F32)


def _mm(name, a, b, *, grid, a_spec, b_spec, o_spec, out_shape, contract, res=None, res_spec=None, carry=None,
        rider=None):
    nk = grid[2]
    n_steps = grid[0] * grid[1] * grid[2]
    acc_shape = tuple(s for s in o_spec.block_shape if s is not None)
    n_in = 2 if res is None else 3
    r_in = 0 if rider is None else len(rider.operands)
    r_out = 0 if rider is None else len(rider.out_shapes)
    x_in = 0 if carry is None else len(carry.operands)
    x_out = 0 if carry is None else carry.n_sems

    def body(*refs):
        a_ref, b_ref = refs[:2]
        r_ref = None if res is None else refs[2]
        pos = n_in
        r_ins, pos = refs[pos:pos + r_in], pos + r_in
        x_ins, pos = refs[pos:pos + x_in], pos + x_in
        o_ref, pos = refs[pos], pos + 1
        r_outs, pos = refs[pos:pos + r_out], pos + r_out
        x_outs, pos = refs[pos:pos + x_out], pos + x_out
        scr = refs[pos:]
        step = (pl.program_id(0) * grid[1] + pl.program_id(1)) * grid[2] + pl.program_id(2)
        if carry is not None:
            sems, scr = scr[-2:], scr[:-2]

            @pl.when(step == 0)
            def _():
                carry.start(x_ins, x_outs, *sems)

        p = _dot(a_ref[...], b_ref[...], contract)
        if rider is not None:
            rider.body(step, r_ins, r_outs)

        def finish(acc):
            if r_ref is not None:
                acc = acc + r_ref[...]
            o_ref[...] = acc.astype(o_ref.dtype)

        if nk == 1:
            finish(p)
        else:
            acc_ref = scr[0]
            k = pl.program_id(2)

            @pl.when(k == 0)
            def _():
                acc_ref[...] = p

            @pl.when(jnp.logical_and(k > 0, k < nk - 1))
            def _():
                acc_ref[...] += p

            @pl.when(k == nk - 1)
            def _():
                finish(acc_ref[...] + p)

        if carry is not None:
            @pl.when(step == n_steps - 1)
            def _():
                carry.wait(x_ins, x_outs, *sems)

    in_specs = [a_spec, b_spec]
    args = [a, b]
    est = 2 * (_nbytes(a_spec.block_shape, a.dtype) + _nbytes(b_spec.block_shape, b.dtype)
               + _nbytes(o_spec.block_shape, out_shape.dtype)) + 2 * _nbytes(acc_shape, F32)
    if res is not None:
        in_specs.append(res_spec)
        args.append(res)
        est += 2 * _nbytes(res_spec.block_shape, res.dtype)
    scratch = [] if nk == 1 else [pltpu.VMEM(acc_shape, F32)]
    if carry is None and rider is None:
        return pl.pallas_call(
            body, name=name, grid=grid, in_specs=in_specs, out_specs=o_spec, out_shape=out_shape,
            scratch_shapes=scratch,
            compiler_params=_params(est, ("parallel", "parallel", "arbitrary")),
        )(*args)

    def by_step(spec):
        return pl.BlockSpec(spec.block_shape, lambda i, j, k: spec.index_map((i * grid[1] + j) * grid[2] + k))

    out_specs, out_shapes, aliases = [o_spec], [out_shape], {}
    if rider is not None:
        in_specs += [by_step(s) for s in rider.in_specs]
        args += rider.operands
        out_specs += [by_step(s) for s in rider.out_specs]
        out_shapes += rider.out_shapes
        est += rider.vmem_bytes
    if carry is not None:
        aliases = {len(args) + i: len(out_shapes) + o for i, o in carry.aliases.items()}
        in_specs += [HBM_SPEC] * x_in
        args += carry.operands
        out_specs += [HBM_SPEC] * x_out
        out_shapes += carry.out_shapes
        scratch += carry.scratch()
    outs = pl.pallas_call(
        body, name=name, grid=grid, in_specs=in_specs, out_specs=out_specs, out_shape=out_shapes,
        input_output_aliases=aliases, scratch_shapes=scratch,
        compiler_params=_params(est, ("arbitrary", "arbitrary", "arbitrary")),
    )(*args)
    results = [outs[0]]
    if rider is not None:
        results.append(list(outs[1:1 + r_out]))
    if carry is not None:
        results.append(list(outs[1 + r_out:]))
    return tuple(results)


def _mm_plain(name, a, b, contract, out_dtype, *, tm, tn, tk, res=None, carry=None, rider=None):
    if contract == NN:
        (M, K), N = a.shape, b.shape[1]
    elif contract == NT:
        (M, K), N = a.shape, b.shape[0]
    else:
        (K, M), N = a.shape, b.shape[1]
    tm, tn, tk = _tile(M, tm, LANES), _tile(N, tn, LANES), _tile(K, tk, LANES)
    grid = (M // tm, N // tn, K // tk)
    if contract == TN:
        a_spec = pl.BlockSpec((tk, tm), lambda i, j, k: (k, i))
    else:
        a_spec = pl.BlockSpec((tm, tk), lambda i, j, k: (i, k))
    if contract == NT:
        b_spec = pl.BlockSpec((tn, tk), lambda i, j, k: (j, k))
    else:
        b_spec = pl.BlockSpec((tk, tn), lambda i, j, k: (k, j))
    o_spec = pl.BlockSpec((tm, tn), lambda i, j, k: (i, j))
    return _mm(name, a, b, grid=grid, a_spec=a_spec, b_spec=b_spec, o_spec=o_spec,
               out_shape=jax.ShapeDtypeStruct((M, N), out_dtype), contract=contract,
               res=res, res_spec=None if res is None else pl.BlockSpec((tm, tn), lambda i, j, k: (i, j)), carry=carry,
               rider=None if rider is None else rider(math.prod(grid)))


class _ColShards:
    def __init__(self, cols_per_chip, tn, paired):
        self.c = cols_per_chip
        self.tn = tn
        self.tps = cols_per_chip // tn
        self.ntiles = N_CHIPS * self.tps
        self.paired = paired

    def nat(self, t):
        if not self.paired:
            return t
        return (t % 2) * (self.ntiles // 2) + t // 2

    def chip(self, t):
        return self.nat(t) // self.tps

    def within(self, t):
        return self.nat(t) % self.tps


def _mm_cols_fwd(name, a, wg, sh, out_dtype, *, tm, carry=None):
    S, K = a.shape
    tm = _tile(S, tm, LANES)
    grid = (S // tm, sh.ntiles, 1)
    return _mm(name, a, wg, grid=grid,
               a_spec=pl.BlockSpec((tm, K), lambda i, j, k: (i, 0)),
               b_spec=pl.BlockSpec((None, K, sh.tn), lambda i, j, k: (sh.chip(j), 0, sh.within(j))),
               o_spec=pl.BlockSpec((tm, sh.tn), lambda i, j, k: (i, j)),
               out_shape=jax.ShapeDtypeStruct((S, sh.ntiles * sh.tn), out_dtype), contract=NN, carry=carry)


def _mm_cols_bwd_x(name, dy, wg, sh, out_dtype, *, tm, tn, carry=None):
    S = dy.shape[0]
    K = wg.shape[1]
    tm, tn = _tile(S, tm, LANES), _tile(K, tn, LANES)
    grid = (S // tm, K // tn, sh.ntiles)
    return _mm(name, dy, wg, grid=grid,
               a_spec=pl.BlockSpec((tm, sh.tn), lambda i, j, k: (i, k)),
               b_spec=pl.BlockSpec((None, tn, sh.tn), lambda i, j, k: (sh.chip(k), j, sh.within(k))),
               o_spec=pl.BlockSpec((tm, tn), lambda i, j, k: (i, j)),
               out_shape=jax.ShapeDtypeStruct((S, K), out_dtype), contract=NT, carry=carry)


def _mm_cols_bwd_w(name, a, dy, sh, *, tm, tk, carry=None, rider=None):
    S, K = a.shape
    tm, tk = _tile(K, tm, LANES), _tile(S, tk, LANES)
    grid = (K // tm, sh.ntiles, S // tk)
    return _mm(name, a, dy, grid=grid,
               a_spec=pl.BlockSpec((tk, tm), lambda i, j, k: (k, i)),
               b_spec=pl.BlockSpec((tk, sh.tn), lambda i, j, k: (k, j)),
               o_spec=pl.BlockSpec((None, tm, sh.tn), lambda i, j, k: (sh.chip(j), i, sh.within(j))),
               out_shape=jax.ShapeDtypeStruct((N_CHIPS, K, sh.c), F32), contract=TN, carry=carry,
               rider=None if rider is None else rider(math.prod(grid)))


def _row_spec(tr, width):
    return pl.BlockSpec((tr, width), lambda i: (i, 0))


def _full_spec(shape):
    nd = len(shape)
    return pl.BlockSpec(shape, lambda i: (0,) * nd)


def _rms_fwd(x, g):
    S, D = x.shape
    tr = _tile(S, 512, 16)

    def body(x_ref, g_ref, h_ref):
        xv = x_ref[...]
        r = lax.rsqrt(jnp.mean(xv * xv, axis=-1, keepdims=True) + EPS)
        h_ref[...] = (xv * r * g_ref[...]).astype(BF16)

    return pl.pallas_call(
        body, name="rms_fwd", grid=(S // tr,),
        in_specs=[_row_spec(tr, D), _full_spec((1, D))], out_specs=_row_spec(tr, D),
        out_shape=jax.ShapeDtypeStruct((S, D), BF16),
        compiler_params=_params(12 * tr * D, ("parallel",)),
    )(x, g)


class _Rider:
    def __init__(self, operands, in_specs, out_specs, out_shapes, body, vmem_bytes):
        self.operands, self.in_specs, self.out_specs = list(operands), list(in_specs), list(out_specs)
        self.out_shapes, self.body, self.vmem_bytes = list(out_shapes), body, vmem_bytes


def _accumulate(step, ref, part):
    @pl.when(step == 0)
    def _():
        ref[...] = part

    @pl.when(step > 0)
    def _():
        ref[...] += part


def _rms_bwd(dh, x, g, dres, steps):
    S, D = x.shape
    tr = S // steps

    def body(step, ins, outs):
        dh_ref, x_ref, g_ref, dres_ref = ins
        dx_ref, dxb_ref, dg_ref = outs
        xv = x_ref[...]
        dhv = dh_ref[...]
        r = lax.rsqrt(jnp.mean(xv * xv, axis=-1, keepdims=True) + EPS)
        gy = dhv * g_ref[...]
        dot = jnp.mean(xv * gy, axis=-1, keepdims=True)
        dx = dres_ref[...] + (r * gy - xv * (r * r * r * dot))
        dx_ref[...] = dx
        dxb_ref[...] = dx.astype(BF16)
        _accumulate(step, dg_ref, jnp.sum(dhv * (xv * r), axis=0, keepdims=True))

    return _Rider([dh, x, g, dres],
                  [_row_spec(tr, D), _row_spec(tr, D), _full_spec((1, D)), _row_spec(tr, D)],
                  [_row_spec(tr, D), _row_spec(tr, D), _full_spec((1, D))],
                  [jax.ShapeDtypeStruct((S, D), F32), jax.ShapeDtypeStruct((S, D), BF16),
                   jax.ShapeDtypeStruct((1, D), F32)], body, 40 * tr * D)


def _out_norm_fwd(oa, ob, ga, gb):
    S, W = oa.shape
    tr = _tile(S, 512, 16)

    def body(oa_ref, ob_ref, ga_ref, gb_ref, o_ref):
        for src, g_ref, off in ((oa_ref, ga_ref, 0), (ob_ref, gb_ref, W)):
            v = src[...]
            r = lax.rsqrt(jnp.mean(v * v, axis=-1, keepdims=True) + EPS)
            o_ref[:, off:off + W] = (v * r * g_ref[...]).astype(BF16)

    return pl.pallas_call(
        body, name="out_norm_fwd", grid=(S // tr,),
        in_specs=[_row_spec(tr, W), _row_spec(tr, W), _full_spec((1, W)), _full_spec((1, W))],
        out_specs=_row_spec(tr, 2 * W), out_shape=jax.ShapeDtypeStruct((S, 2 * W), BF16),
        compiler_params=_params(24 * tr * W, ("parallel",)),
    )(oa, ob, ga, gb)


def _out_norm_bwd(do, oa, ob, ga, gb, steps):
    S, W = oa.shape
    tr = S // steps

    def body(step, ins, outs):
        do_ref, oa_ref, ob_ref, ga_ref, gb_ref = ins
        doa_ref, dob_ref, dga_ref, dgb_ref = outs
        parts = []
        for src, g_ref, off, d_ref in ((oa_ref, ga_ref, 0, doa_ref), (ob_ref, gb_ref, W, dob_ref)):
            v = src[...]
            dv = do_ref[:, off:off + W]
            r = lax.rsqrt(jnp.mean(v * v, axis=-1, keepdims=True) + EPS)
            gy = dv * g_ref[...]
            dot = jnp.mean(v * gy, axis=-1, keepdims=True)
            d_ref[...] = (r * gy - v * (r * r * r * dot)).astype(BF16)
            parts.append(jnp.sum(dv * (v * r), axis=0, keepdims=True))
        _accumulate(step, dga_ref, parts[0])
        _accumulate(step, dgb_ref, parts[1])

    return _Rider([do, oa, ob, ga, gb],
                  [_row_spec(tr, 2 * W), _row_spec(tr, W), _row_spec(tr, W), _full_spec((1, W)), _full_spec((1, W))],
                  [_row_spec(tr, W), _row_spec(tr, W), _full_spec((1, W)), _full_spec((1, W))],
                  [jax.ShapeDtypeStruct((S, W), BF16), jax.ShapeDtypeStruct((S, W), BF16),
                   jax.ShapeDtypeStruct((1, W), F32), jax.ShapeDtypeStruct((1, W), F32)], body, 48 * tr * W)


def _loss_grad(y, t):
    S, D = y.shape
    tr = _tile(S, 256, 16)

    def body(y_ref, t_ref, dy_ref, dyb_ref, ls_ref):
        e = y_ref[...] - t_ref[...]
        dy = e * (1.0 / D)
        dy_ref[...] = dy
        dyb_ref[...] = dy.astype(BF16)
        part = jnp.sum(e * e, axis=0, keepdims=True)

        @pl.when(pl.program_id(0) == 0)
        def _():
            ls_ref[...] = part

        @pl.when(pl.program_id(0) > 0)
        def _():
            ls_ref[...] += part

    return pl.pallas_call(
        body, name="loss_grad", grid=(S // tr,),
        in_specs=[_row_spec(tr, D), _row_spec(tr, D)],
        out_specs=[_row_spec(tr, D), _row_spec(tr, D), _full_spec((1, D))],
        out_shape=[jax.ShapeDtypeStruct((S, D), F32), jax.ShapeDtypeStruct((S, D), BF16),
                   jax.ShapeDtypeStruct((1, D), F32)],
        compiler_params=_params(32 * tr * D, ("arbitrary",)),
    )(y, t)


def _head_rms(x, g):
    r = lax.rsqrt(jnp.mean(x * x, axis=-1, keepdims=True) + EPS)
    return x * r * g


def _head_rms_bwd(x, dy, g):
    r = lax.rsqrt(jnp.mean(x * x, axis=-1, keepdims=True) + EPS)
    gy = dy * g
    dot = jnp.mean(x * gy, axis=-1, keepdims=True)
    return r * gy - x * (r * r * r * dot), dy * (x * r)


def _rope(y, cos2, sgn_sin):
    return y * cos2 + pltpu.roll(y, HEAD_DIM // 2, axis=1) * sgn_sin


def _rope_t(dy, cos2, sgn_sin):
    return dy * cos2 + pltpu.roll(dy * sgn_sin, HEAD_DIM // 2, axis=1)


def _qk_prep_fwd(proj, cos2, sgn_sin, gains, dims):
    S, P = proj.shape
    naw, waw, kvw = dims
    tr = _tile(S, 256, 16)
    hd = HEAD_DIM

    def body(p_ref, c_ref, s_ref, g_ref, qa_ref, ka_ref, va_ref, qb_ref, kb_ref, vb_ref):
        c2 = c_ref[...]
        ss = s_ref[...]
        off = 0
        for dst, width, gi, rot in ((qa_ref, naw, 0, False), (ka_ref, naw, 1, False), (va_ref, naw, None, False),
                                    (qb_ref, waw, 2, True), (kb_ref, kvw, 3, True), (vb_ref, kvw, None, False)):
            for h in range(width // hd):
                xh = p_ref[:, off + h * hd:off + (h + 1) * hd]
                if gi is not None:
                    xh = _head_rms(xh, g_ref[gi:gi + 1, :])
                    if rot:
                        xh = _rope(xh, c2, ss)
                dst[:, h * hd:(h + 1) * hd] = xh.astype(BF16)
            off += width

    widths = (naw, naw, naw, waw, kvw, kvw)
    return pl.pallas_call(
        body, name="qk_prep_fwd", grid=(S // tr,),
        in_specs=[_row_spec(tr, P), _row_spec(tr, hd), _row_spec(tr, hd), _full_spec((SUBLANES, hd))],
        out_specs=[_row_spec(tr, w) for w in widths],
        out_shape=[jax.ShapeDtypeStruct((S, w), BF16) for w in widths],
        compiler_params=_params(16 * tr * P, ("parallel",)),
    )(proj, cos2, sgn_sin, gains)


def _qk_prep_bwd(proj, dqa, dka, dva, dqb, dkb, dvb, cos2, sgn_sin, gains, dims, steps):
    S, P = proj.shape
    naw, waw, kvw = dims
    tr = S // steps
    hd = HEAD_DIM

    def body(step, ins, outs):
        p_ref, dqa_ref, dka_ref, dva_ref, dqb_ref, dkb_ref, dvb_ref, c_ref, s_ref, g_ref = ins
        dp_ref, dg_ref = outs
        c2 = c_ref[...]
        ss = s_ref[...]
        off = 0
        dgs = [jnp.zeros((1, hd), F32) for _ in range(4)]
        for src, width, gi, rot in ((dqa_ref, naw, 0, False), (dka_ref, naw, 1, False), (dva_ref, naw, None, False),
                                    (dqb_ref, waw, 2, True), (dkb_ref, kvw, 3, True), (dvb_ref, kvw, None, False)):
            for h in range(width // hd):
                dy = src[:, h * hd:(h + 1) * hd].astype(F32)
                if gi is not None:
                    if rot:
                        dy = _rope_t(dy, c2, ss)
                    xh = p_ref[:, off + h * hd:off + (h + 1) * hd]
                    dy, dgt = _head_rms_bwd(xh, dy, g_ref[gi:gi + 1, :])
                    dgs[gi] = dgs[gi] + jnp.sum(dgt, axis=0, keepdims=True)
                dp_ref[:, off + h * hd:off + (h + 1) * hd] = dy.astype(BF16)
            off += width
        _accumulate(step, dg_ref, jnp.concatenate(dgs + [jnp.zeros((SUBLANES - 4, hd), F32)], axis=0))

    return _Rider([proj, dqa, dka, dva, dqb, dkb, dvb, cos2, sgn_sin, gains],
                  [_row_spec(tr, P), _row_spec(tr, naw), _row_spec(tr, naw), _row_spec(tr, naw),
                   _row_spec(tr, waw), _row_spec(tr, kvw), _row_spec(tr, kvw),
                   _row_spec(tr, hd), _row_spec(tr, hd), _full_spec((SUBLANES, hd))],
                  [_row_spec(tr, P), _full_spec((SUBLANES, hd))],
                  [jax.ShapeDtypeStruct((S, P), BF16), jax.ShapeDtypeStruct((SUBLANES, hd), F32)], body, 24 * tr * P)


NA_KEYS = NA_WIN_R * GRID_W
NA_ROWS_PER_STEP = 32


def _na_col_geometry():
    c = np.arange(GRID_W)
    col_start = np.clip(c - NA_WIN_C // 2, 0, GRID_W - NA_WIN_C)
    mask = (c[None, :] >= col_start[:, None]) & (c[None, :] < col_start[:, None] + NA_WIN_C)
    dc = np.clip(c[None, :] - c[:, None], -(NA_WIN_C - 1), NA_WIN_C - 1) + (NA_WIN_C - 1)
    onehot = (dc[:, :, None] == np.arange(2 * NA_WIN_C - 1)[None, None, :]) & mask[:, :, None]
    return mask, onehot


def _na_bias_table(rpb_l):
    H = rpb_l.shape[0]
    mask, onehot = _na_col_geometry()
    t = jnp.sum(jnp.where(onehot[None, None], rpb_l[:, :, None, None, :], 0.0), axis=-1)
    t = jnp.where(mask[None, None], t, NEG)
    per_delta = [jnp.transpose(t[:, d:d + NA_WIN_R], (0, 2, 1, 3)).reshape(H, GRID_W, NA_KEYS) for d in range(NA_WIN_R)]
    return jnp.stack(per_delta, axis=1)


def _na_bias_table_t(db):
    H = db.shape[0]
    _, onehot = _na_col_geometry()
    d5 = db.reshape(H, NA_WIN_R, GRID_W, NA_WIN_R, GRID_W)
    folded = jnp.einsum("hdqwk,qkc->hdwc", d5, onehot.astype(np.float32), precision=lax.Precision.HIGHEST)
    return sum(jnp.pad(folded[:, d], ((0, 0), (d, NA_WIN_R - 1 - d), (0, 0))) for d in range(NA_WIN_R))


def _na_row_geometry(r, rows):
    start = jnp.clip(r - NA_WIN_R // 2, 0, rows - NA_WIN_R)
    return start, start - r + (NA_WIN_R - 1)


def _softmax_rows(s):
    m = jnp.max(s, axis=-1, keepdims=True)
    e = jnp.exp(s - m)
    return e / jnp.sum(e, axis=-1, keepdims=True)


def _na_fwd(qa, ka, va, btab):
    S, W = qa.shape
    H = W // HEAD_DIM
    rows = S // GRID_W
    assert rows >= NA_WIN_R
    rb = _tile(rows, NA_ROWS_PER_STEP, 1)
    tq = rb * GRID_W

    def body(q_ref, k_ref, v_ref, b_ref, o_ref):
        i = pl.program_id(1)

        geo = [_na_row_geometry(i * rb + j, rows) for j in range(rb)]
        toks = [pl.ds(j * GRID_W, GRID_W) for j in range(rb)]
        wins = [pl.ds(pl.multiple_of(start * GRID_W, GRID_W), NA_KEYS) for start, _ in geo]
        scores = [_dot(q_ref[toks[j], :], k_ref[wins[j], :], NT) for j in range(rb)]
        probs = [_softmax_rows(scores[j] * ATTN_SCALE + b_ref[geo[j][1]]) for j in range(rb)]
        for j in range(rb):
            o_ref[toks[j], :] = _dot(probs[j], v_ref[wins[j], :], NN)

    kv_spec = pl.BlockSpec((S, HEAD_DIM), lambda h, i: (0, h))
    return pl.pallas_call(
        body, name="na_fwd", grid=(H, rows // rb),
        in_specs=[pl.BlockSpec((tq, HEAD_DIM), lambda h, i: (i, h)), kv_spec, kv_spec,
                  pl.BlockSpec((None, NA_WIN_R, GRID_W, NA_KEYS), lambda h, i: (h, 0, 0, 0))],
        out_specs=pl.BlockSpec((tq, HEAD_DIM), lambda h, i: (i, h)),
        out_shape=jax.ShapeDtypeStruct((S, W), F32),
        compiler_params=_params(8 * S * HEAD_DIM + (8 << 20), ("parallel", "arbitrary")),
    )(qa, ka, va, btab)


def _na_bwd(qa, ka, va, btab, doa):
    S, W = qa.shape
    H = W // HEAD_DIM
    rows = S // GRID_W
    rb = _tile(rows, NA_ROWS_PER_STEP, 1)
    tq = rb * GRID_W

    def body(q_ref, k_ref, v_ref, b_ref, do_ref, dq_ref, dk_ref, dv_ref, db_ref):
        i = pl.program_id(1)

        @pl.when(i == 0)
        def _():
            dk_ref[...] = jnp.zeros_like(dk_ref)
            dv_ref[...] = jnp.zeros_like(dv_ref)
            db_ref[...] = jnp.zeros_like(db_ref)

        steps = range(rb)
        geo = [_na_row_geometry(i * rb + j, rows) for j in steps]
        toks = [pl.ds(j * GRID_W, GRID_W) for j in steps]
        wins = [pl.ds(pl.multiple_of(start * GRID_W, GRID_W), NA_KEYS) for start, _ in geo]
        scores = [_dot(q_ref[toks[j], :], k_ref[wins[j], :], NT) for j in steps]
        dps = [_dot(do_ref[toks[j], :], v_ref[wins[j], :], NT) for j in steps]
        probs = [_softmax_rows(scores[j] * ATTN_SCALE + b_ref[geo[j][1]]) for j in steps]
        dss = [probs[j] * (dps[j] - jnp.sum(probs[j] * dps[j], axis=-1, keepdims=True)) for j in steps]
        for j in steps:
            db_ref[geo[j][1]] += dss[j]
        dsb = [(dss[j] * ATTN_SCALE).astype(BF16) for j in steps]
        for j in steps:
            dq_ref[toks[j], :] = _dot(dsb[j], k_ref[wins[j], :], NN)
        dks = [_dot(dsb[j], q_ref[toks[j], :], TN) for j in steps]
        dvs = [_dot(probs[j], do_ref[toks[j], :], TN) for j in steps]
        for j in steps:
            dk_ref[wins[j], :] += dks[j]
            dv_ref[wins[j], :] += dvs[j]

    kv_spec = pl.BlockSpec((S, HEAD_DIM), lambda h, i: (0, h))
    q_spec = pl.BlockSpec((tq, HEAD_DIM), lambda h, i: (i, h))
    b_spec = pl.BlockSpec((None, NA_WIN_R, GRID_W, NA_KEYS), lambda h, i: (h, 0, 0, 0))
    return pl.pallas_call(
        body, name="na_bwd", grid=(H, rows // rb),
        in_specs=[q_spec, kv_spec, kv_spec, b_spec, q_spec],
        out_specs=[q_spec, kv_spec, kv_spec, b_spec],
        out_shape=[jax.ShapeDtypeStruct((S, W), F32), jax.ShapeDtypeStruct((S, W), F32),
                   jax.ShapeDtypeStruct((S, W), F32), jax.ShapeDtypeStruct(btab.shape, F32)],
        compiler_params=_params(24 * S * HEAD_DIM + (12 << 20), ("parallel", "arbitrary")),
    )(qa, ka, va, btab, doa)


WA_KEYS = 3 * WA_BLOCK
WA_BLOCKS_PER_STEP = 8


def _wa_mask(qk, n, start, g):
    s = qk * ATTN_SCALE
    qpos = n * WA_BLOCK + lax.broadcasted_iota(jnp.int32, (WA_BLOCK, WA_KEYS), 0)
    kpos = start + lax.broadcasted_iota(jnp.int32, (WA_BLOCK, WA_KEYS), 1)
    valid = jnp.abs(kpos - qpos) <= WA_WINDOW
    valid = jnp.concatenate([valid] * g, axis=0)
    return jnp.where(valid, s, NEG)


def _wa_probs(s, sink):
    m = jnp.maximum(jnp.max(s, axis=-1, keepdims=True), sink)
    e = jnp.exp(s - m)
    es = jnp.exp(sink - m)
    den = jnp.sum(e, axis=-1, keepdims=True) + es
    return e / den, es / den


def _wa_stack(ref, tok, g):
    return jnp.concatenate([ref[tok, t * HEAD_DIM:(t + 1) * HEAD_DIM] for t in range(g)], axis=0)


def _wa_fwd(qb, kb, vb, sink_col):
    S, W = qb.shape
    hkv = kb.shape[1] // HEAD_DIM
    g = W // HEAD_DIM // hkv
    nb = S // WA_BLOCK
    assert S >= WA_KEYS
    qb_step = _tile(nb, WA_BLOCKS_PER_STEP, 1)
    tq = qb_step * WA_BLOCK

    def body(q_ref, k_ref, v_ref, s_ref, o_ref):
        i = pl.program_id(1)

        steps = range(qb_step)
        ns = [i * qb_step + j for j in steps]
        toks = [pl.ds(j * WA_BLOCK, WA_BLOCK) for j in steps]
        starts = [pl.multiple_of(jnp.clip((n - 1) * WA_BLOCK, 0, S - WA_KEYS), WA_BLOCK) for n in ns]
        wins = [pl.ds(start, WA_KEYS) for start in starts]
        scores = [_dot(_wa_stack(q_ref, toks[j], g), k_ref[wins[j], :], NT) for j in steps]
        probs = [_wa_probs(_wa_mask(scores[j], ns[j], starts[j], g), s_ref[...])[0] for j in steps]
        outs = [_dot(probs[j], v_ref[wins[j], :], NN) for j in steps]
        for j in steps:
            for t in range(g):
                o_ref[toks[j], t * HEAD_DIM:(t + 1) * HEAD_DIM] = outs[j][t * WA_BLOCK:(t + 1) * WA_BLOCK]

    kv_spec = pl.BlockSpec((S, HEAD_DIM), lambda h, i: (0, h))
    q_spec = pl.BlockSpec((tq, g * HEAD_DIM), lambda h, i: (i, h))
    return pl.pallas_call(
        body, name="wa_fwd", grid=(hkv, nb // qb_step),
        in_specs=[q_spec, kv_spec, kv_spec, pl.BlockSpec((None, g * WA_BLOCK, 1), lambda h, i: (h, 0, 0))],
        out_specs=q_spec, out_shape=jax.ShapeDtypeStruct((S, W), F32),
        compiler_params=_params(8 * S * HEAD_DIM + (12 << 20), ("parallel", "arbitrary")),
    )(qb, kb, vb, sink_col)


def _wa_bwd(qb, kb, vb, sink_col, dob):
    S, W = qb.shape
    KW = kb.shape[1]
    hkv = KW // HEAD_DIM
    g = W // HEAD_DIM // hkv
    nb = S // WA_BLOCK
    qb_step = _tile(nb, WA_BLOCKS_PER_STEP, 1)
    tq = qb_step * WA_BLOCK

    def body(q_ref, k_ref, v_ref, s_ref, do_ref, dq_ref, dk_ref, dv_ref, dsink_ref):
        i = pl.program_id(1)

        @pl.when(i == 0)
        def _():
            dk_ref[...] = jnp.zeros_like(dk_ref)
            dv_ref[...] = jnp.zeros_like(dv_ref)
            dsink_ref[...] = jnp.zeros_like(dsink_ref)

        steps = range(qb_step)
        ns = [i * qb_step + j for j in steps]
        toks = [pl.ds(j * WA_BLOCK, WA_BLOCK) for j in steps]
        starts = [pl.multiple_of(jnp.clip((n - 1) * WA_BLOCK, 0, S - WA_KEYS), WA_BLOCK) for n in ns]
        wins = [pl.ds(start, WA_KEYS) for start in starts]
        qss = [_wa_stack(q_ref, toks[j], g) for j in steps]
        doss = [_wa_stack(do_ref, toks[j], g) for j in steps]
        scores = [_dot(qss[j], k_ref[wins[j], :], NT) for j in steps]
        dps = [_dot(doss[j], v_ref[wins[j], :], NT) for j in steps]
        pp = [_wa_probs(_wa_mask(scores[j], ns[j], starts[j], g), s_ref[...]) for j in steps]
        dsums = [jnp.sum(pp[j][0] * dps[j], axis=-1, keepdims=True) for j in steps]
        dsb = [(pp[j][0] * (dps[j] - dsums[j]) * ATTN_SCALE).astype(BF16) for j in steps]
        dsink_ref[...] -= sum(pp[j][1] * dsums[j] for j in steps)
        dqs = [_dot(dsb[j], k_ref[wins[j], :], NN) for j in steps]
        dks = [_dot(dsb[j], qss[j], TN) for j in steps]
        dvs = [_dot(pp[j][0], doss[j], TN) for j in steps]
        for j in steps:
            for t in range(g):
                dq_ref[toks[j], t * HEAD_DIM:(t + 1) * HEAD_DIM] = dqs[j][t * WA_BLOCK:(t + 1) * WA_BLOCK]
            dk_ref[wins[j], :] += dks[j]
            dv_ref[wins[j], :] += dvs[j]

    kv_spec = pl.BlockSpec((S, HEAD_DIM), lambda h, i: (0, h))
    q_spec = pl.BlockSpec((tq, g * HEAD_DIM), lambda h, i: (i, h))
    s_spec = pl.BlockSpec((None, g * WA_BLOCK, 1), lambda h, i: (h, 0, 0))
    return pl.pallas_call(
        body, name="wa_bwd", grid=(hkv, nb // qb_step),
        in_specs=[q_spec, kv_spec, kv_spec, s_spec, q_spec],
        out_specs=[q_spec, kv_spec, kv_spec, s_spec],
        out_shape=[jax.ShapeDtypeStruct((S, W), F32), jax.ShapeDtypeStruct((S, KW), F32),
                   jax.ShapeDtypeStruct((S, KW), F32), jax.ShapeDtypeStruct(sink_col.shape, F32)],
        compiler_params=_params(24 * S * HEAD_DIM + (16 << 20), ("parallel", "arbitrary")),
    )(qb, kb, vb, sink_col, dob)


CONV_HALO = SUBLANES


def _conv_specs(tr, width, nblk, rows_inner):
    per = tr // CONV_HALO

    def spec(shape, row_block):
        if rows_inner:
            return pl.BlockSpec(shape, lambda j, i: (row_block(i), j))
        return pl.BlockSpec(shape, lambda i, j: (row_block(i), j))

    cur = spec((tr, width), lambda i: i)
    prev = spec((CONV_HALO, width), lambda i: jnp.maximum(i * per - 1, 0))
    nxt = spec((CONV_HALO, width), lambda i: jnp.minimum((i + 1) * per, nblk * per - 1))
    return prev, cur, nxt


def _extend(prev_ref, cur_ref, next_ref, i, nblk):
    p = jnp.where(i > 0, prev_ref[...].astype(F32), 0.0)
    n = jnp.where(i < nblk - 1, next_ref[...].astype(F32), 0.0)
    return jnp.concatenate([p, cur_ref[...].astype(F32), n], axis=0)


def _shift_down(x):
    return pltpu.roll(x, 1, axis=0)


def _shift_up(x):
    return pltpu.roll(x, x.shape[0] - 1, axis=0)


def _conv(ext, w_ref, b_ref):
    return _shift_down(ext) * w_ref[0:1, :] + ext * w_ref[1:2, :] + _shift_up(ext) * w_ref[2:3, :] + b_ref[...]


def _sigmoid(x):
    return 1.0 / (1.0 + jnp.exp(-x))


def _convgate_fwd(up_pre, cw, cb, tc):
    S, C2 = up_pre.shape
    tr = _tile(S, 512, 16)
    nblk = S // tr
    prev, cur, nxt = _conv_specs(tr, 2 * tc, nblk, False)

    def body(p_ref, c_ref, n_ref, w_ref, b_ref, a_ref):
        i = pl.program_id(0)
        u = _conv(_extend(p_ref, c_ref, n_ref, i, nblk), w_ref, b_ref)[CONV_HALO:CONV_HALO + tr]
        gate, up = u[:, :tc], u[:, tc:]
        a_ref[...] = (gate * _sigmoid(gate) * up).astype(BF16)

    return pl.pallas_call(
        body, name="convgate_fwd", grid=(nblk, C2 // (2 * tc)),
        in_specs=[prev, cur, nxt, pl.BlockSpec((3, 2 * tc), lambda i, j: (0, j)),
                  pl.BlockSpec((1, 2 * tc), lambda i, j: (0, j))],
        out_specs=pl.BlockSpec((tr, tc), lambda i, j: (i, j)),
        out_shape=jax.ShapeDtypeStruct((S, C2 // 2), BF16),
        compiler_params=_params(48 * tr * tc, ("parallel", "parallel")),
    )(up_pre, up_pre, up_pre, cw, cb)


def _convgate_bwd(up_pre, dact, cw, cb, tc):
    S, C2 = up_pre.shape
    tr = _tile(S, 256, 16)
    nblk = S // tr
    prev, cur, nxt = _conv_specs(tr, 2 * tc, nblk, True)
    dprev, dcur, dnxt = _conv_specs(tr, tc, nblk, True)

    def body(p_ref, c_ref, n_ref, dp_ref, dc_ref, dn_ref, w_ref, b_ref, dx_ref, dw_ref, db_ref):
        i = pl.program_id(1)
        ext = _extend(p_ref, c_ref, n_ref, i, nblk)
        da = _extend(dp_ref, dc_ref, dn_ref, i, nblk)
        ext_dn, ext_up = _shift_down(ext), _shift_up(ext)
        u = ext_dn * w_ref[0:1, :] + ext * w_ref[1:2, :] + ext_up * w_ref[2:3, :] + b_ref[...]
        gate, up = u[:, :tc], u[:, tc:]
        sg = _sigmoid(gate)
        silu = gate * sg
        du = jnp.concatenate([da * up * (sg + silu * (1.0 - sg)), da * silu], axis=1)
        dx = _shift_up(du) * w_ref[0:1, :] + du * w_ref[1:2, :] + _shift_down(du) * w_ref[2:3, :]
        mid = slice(CONV_HALO, CONV_HALO + tr)
        dx_ref[...] = dx[mid].astype(BF16)
        duc = du[mid]
        dw = jnp.concatenate([jnp.sum(duc * ext_dn[mid], axis=0, keepdims=True),
                              jnp.sum(duc * ext[mid], axis=0, keepdims=True),
                              jnp.sum(duc * ext_up[mid], axis=0, keepdims=True)], axis=0)
        db = jnp.sum(duc, axis=0, keepdims=True)

        @pl.when(i == 0)
        def _():
            dw_ref[...] = dw
            db_ref[...] = db

        @pl.when(i > 0)
        def _():
            dw_ref[...] += dw
            db_ref[...] += db

    return pl.pallas_call(
        body, name="convgate_bwd", grid=(C2 // (2 * tc), nblk),
        in_specs=[prev, cur, nxt, dprev, dcur, dnxt,
                  pl.BlockSpec((3, 2 * tc), lambda j, i: (0, j)), pl.BlockSpec((1, 2 * tc), lambda j, i: (0, j))],
        out_specs=[pl.BlockSpec((tr, 2 * tc), lambda j, i: (i, j)),
                   pl.BlockSpec((3, 2 * tc), lambda j, i: (0, j)), pl.BlockSpec((1, 2 * tc), lambda j, i: (0, j))],
        out_shape=[jax.ShapeDtypeStruct((S, C2), BF16), jax.ShapeDtypeStruct((3, C2), F32),
                   jax.ShapeDtypeStruct((1, C2), F32)],
        compiler_params=_params(160 * tr * tc, ("parallel", "arbitrary")),
    )(up_pre, up_pre, up_pre, dact, dact, dact, cw, cb)


def _rope_tables(positions):
    inv = ROPE_THETA ** (-jnp.arange(0, HEAD_DIM, 2, dtype=F32) / HEAD_DIM)
    ang = positions.astype(F32)[:, None] * inv[None, :]
    cos, sin = jnp.cos(ang), jnp.sin(ang)
    return jnp.concatenate([cos, cos], axis=1), jnp.concatenate([-sin, sin], axis=1)


def _sink_col(sink_l, hkv):
    g = sink_l.shape[0] // hkv
    return jnp.broadcast_to(sink_l.reshape(hkv, g, 1), (hkv, g, WA_BLOCK)).reshape(hkv, g * WA_BLOCK, 1)


def _sink_col_t(dcol):
    hkv, rows, _ = dcol.shape
    return jnp.sum(dcol.reshape(hkv, rows // WA_BLOCK, WA_BLOCK), axis=-1).reshape(-1)


def _to_paired(a, sh):
    return jnp.concatenate([a[:, sh.nat(t) * sh.tn:(sh.nat(t) + 1) * sh.tn] for t in range(sh.ntiles)], axis=1)


def _from_paired(a, sh):
    pos = {sh.nat(t): t for t in range(sh.ntiles)}
    return jnp.concatenate([a[:, pos[n] * sh.tn:(pos[n] + 1) * sh.tn] for n in range(sh.ntiles)], axis=1)


def _pack(arrs):
    flat = jnp.concatenate([a.reshape(-1).astype(F32) for a in arrs])
    unit = SUBLANES * LANES
    total = -(-flat.shape[0] // unit) * unit
    return jnp.pad(flat, (0, total - flat.shape[0])).reshape(-1, LANES)


def _unpack(buf, shapes):
    flat = buf.reshape(-1)
    out, off = [], 0
    for s in shapes:
        n = math.prod(s)
        out.append(flat[off:off + n].reshape(s))
        off += n
    return out


HBM_SPEC = pl.BlockSpec(memory_space=pltpu.HBM)
DMA_CHUNK_BYTES = 512 * 1024


def _row_chunks(rows, row_bytes, align):
    want = max(1, rows * row_bytes // DMA_CHUNK_BYTES)
    count = max(k for k in range(1, rows + 1) if rows % k == 0 and (rows // k) % align == 0 and (k <= want or k == 1))
    size = rows // count
    return [(j * size, size) for j in range(count)]


def _mesh_pos():
    return lax.axis_index("x"), lax.axis_index("y"), lax.axis_index("c")


def _other_chips(x, y):
    return [(1 - x, y), (x, 1 - y), (1 - x, 1 - y)]


def _remote(src, dst, send_sems, recv_sems, k, dev):
    return pltpu.make_async_remote_copy(src_ref=src, dst_ref=dst, send_sem=send_sems.at[k], recv_sem=recv_sems.at[k],
                                        device_id=dev, device_id_type=MESH)


class _Exchange:
    def __init__(self, operands, out_shapes, aliases, start, wait):
        self.operands, self.out_shapes, self.aliases = list(operands), list(out_shapes), dict(aliases)
        self.start, self.wait = start, wait
        self.n_sems = len(out_shapes)

    def scratch(self):
        return [pltpu.SemaphoreType.DMA((self.n_sems,)), pltpu.SemaphoreType.DMA((self.n_sems,))]


class _SemSlice:
    def __init__(self, sems, offset):
        self._sems, self._offset = sems, offset

    @property
    def at(self):
        return self

    def __getitem__(self, k):
        return self._sems.at[self._offset + k]


def _both(a, b):
    ia, oa = len(a.operands), a.n_sems

    def split(ins, outs, send_sems, recv_sems):
        return ((ins[:ia], outs[:oa], send_sems, recv_sems),
                (ins[ia:], outs[oa:], _SemSlice(send_sems, oa), _SemSlice(recv_sems, oa)))

    def start(*refs):
        first, second = split(*refs)
        a.start(*first)
        b.start(*second)

    def wait(*refs):
        first, second = split(*refs)
        a.wait(*first)
        b.wait(*second)

    aliases = dict(a.aliases)
    aliases.update({ia + i: oa + o for i, o in b.aliases.items()})
    return _Exchange(a.operands + b.operands, a.out_shapes + b.out_shapes, aliases, start, wait)


def _exchange_alone(name, ex):
    n_in = len(ex.operands)

    def body(*refs):
        ins, outs = refs[:n_in], refs[n_in:n_in + ex.n_sems]
        send_sems, recv_sems = refs[n_in + ex.n_sems:]
        ex.start(ins, outs, send_sems, recv_sems)
        ex.wait(ins, outs, send_sems, recv_sems)

    return pl.pallas_call(
        body, name=name, in_specs=[HBM_SPEC] * n_in, out_specs=[HBM_SPEC] * ex.n_sems, out_shape=ex.out_shapes,
        input_output_aliases=ex.aliases, scratch_shapes=ex.scratch(),
    )(*ex.operands)


def _ag_chip_exchange(gathered, small):
    n = len(gathered)
    arrays = list(gathered) + ([] if small is None else [small])

    def start(ins, outs, send_sems, recv_sems):
        x, y, c = _mesh_pos()
        q_me = 2 * x + y
        chips = _other_chips(x, y)
        for t in range(n):
            h = gathered[t].shape[1] // 2
            for s0, sz in _row_chunks(h, gathered[t].shape[2] * 2, 16):
                blk = outs[t].at[q_me, pl.ds(c * h + s0, sz)]
                for chip in chips:
                    _remote(blk, blk, send_sems, recv_sems, t, (*chip, c)).start()
        if small is not None:
            for chip in chips:
                _remote(outs[n].at[q_me], outs[n].at[q_me], send_sems, recv_sems, n, (*chip, c)).start()

    def wait(ins, outs, send_sems, recv_sems):
        x, y, c = _mesh_pos()
        for t in range(n):
            three = outs[t].at[pl.ds(0, 3), pl.ds(0, gathered[t].shape[1] // 2)]
            _remote(three, three, send_sems, recv_sems, t, (x, y, 1 - c)).wait()
        if small is not None:
            three = outs[n].at[pl.ds(0, 3)]
            _remote(three, three, send_sems, recv_sems, n, (x, y, 1 - c)).wait()

    return _Exchange(arrays, [jax.ShapeDtypeStruct(a.shape, a.dtype) for a in arrays],
                     {i: i for i in range(len(arrays))}, start, wait)


def _ag_sibling_pass(gathered):
    n = len(gathered)

    def start(ins, outs, send_sems, recv_sems):
        x, y, c = _mesh_pos()
        for t in range(n):
            h = gathered[t].shape[1] // 2
            for s0, sz in _row_chunks(h, gathered[t].shape[2] * 2, 16):
                for cx, cy in _other_chips(x, y):
                    blk = outs[t].at[2 * cx + cy, pl.ds(c * h + s0, sz)]
                    _remote(blk, blk, send_sems, recv_sems, t, (x, y, 1 - c)).start()

    def wait(ins, outs, send_sems, recv_sems):
        x, y, c = _mesh_pos()
        for t in range(n):
            three = outs[t].at[pl.ds(0, 3), pl.ds(0, gathered[t].shape[1] // 2)]
            _remote(three, three, send_sems, recv_sems, t, (x, y, 1 - c)).wait()

    return _Exchange(gathered, [jax.ShapeDtypeStruct(a.shape, a.dtype) for a in gathered],
                     {i: i for i in range(n)}, start, wait)


def _rs_sibling_exchange(grads):
    n = len(grads)
    halves = [g.shape[1] // 2 for g in grads]

    def start(ins, outs, send_sems, recv_sems):
        x, y, c = _mesh_pos()
        for t in range(n):
            for q in range(N_CHIPS):
                for s0, sz in _row_chunks(halves[t], grads[t].shape[2] * 4, SUBLANES):
                    _remote(ins[t].at[q, pl.ds((1 - c) * halves[t] + s0, sz)], outs[t].at[q, pl.ds(s0, sz)],
                            send_sems, recv_sems, t, (x, y, 1 - c)).start()

    def wait(ins, outs, send_sems, recv_sems):
        x, y, c = _mesh_pos()
        for t in range(n):
            _remote(outs[t], outs[t], send_sems, recv_sems, t, (x, y, 1 - c)).wait()

    return _Exchange(grads, [jax.ShapeDtypeStruct((N_CHIPS, h, g.shape[2]), F32) for g, h in zip(grads, halves)],
                     {}, start, wait)


def _rs_chip_exchange(pbs):
    n = len(pbs)

    def start(ins, outs, send_sems, recv_sems):
        x, y, c = _mesh_pos()
        for t in range(n):
            for s0, sz in _row_chunks(pbs[t].shape[1], pbs[t].shape[2] * 2, 16):
                for k, (cx, cy) in enumerate(_other_chips(x, y)):
                    _remote(ins[t].at[2 * cx + cy, pl.ds(s0, sz)], outs[t].at[k, pl.ds(s0, sz)],
                            send_sems, recv_sems, t, (cx, cy, c)).start()

    def wait(ins, outs, send_sems, recv_sems):
        x, y, c = _mesh_pos()
        for t in range(n):
            _remote(outs[t], outs[t], send_sems, recv_sems, t, (x, y, 1 - c)).wait()

    return _Exchange(pbs, [jax.ShapeDtypeStruct((3,) + p.shape[1:], BF16) for p in pbs], {}, start, wait)


def _rs_sibling_share(gs):
    n = len(gs)

    def start(ins, outs, send_sems, recv_sems):
        x, y, c = _mesh_pos()
        for t in range(n):
            h = gs[t].shape[0] // 2
            for s0, sz in _row_chunks(h, gs[t].shape[1] * 4, SUBLANES):
                rows = outs[t].at[pl.ds(c * h + s0, sz)]
                _remote(rows, rows, send_sems, recv_sems, t, (x, y, 1 - c)).start()

    def wait(ins, outs, send_sems, recv_sems):
        x, y, c = _mesh_pos()
        for t in range(n):
            half = outs[t].at[pl.ds(0, gs[t].shape[0] // 2)]
            _remote(half, half, send_sems, recv_sems, t, (x, y, 1 - c)).wait()

    return _Exchange(gs, [jax.ShapeDtypeStruct(g.shape, F32) for g in gs], {i: i for i in range(n)}, start, wait)


def _gather_small(buf):
    def body(in_ref, out_ref, send_sems, recv_sems, loc_sem):
        x, y, c = _mesh_pos()
        me = 4 * x + 2 * y + c
        local = pltpu.make_async_copy(in_ref, out_ref.at[me], loc_sem)
        local.start()
        sends, peers = [], []
        for k in range(1, N_DEV):
            bx, by, bc = (k >> 2) & 1, (k >> 1) & 1, k & 1
            peer = (x + bx - 2 * x * bx, y + by - 2 * y * by, c + bc - 2 * c * bc)
            peers.append(peer)
            sends.append(_remote(in_ref, out_ref.at[me], send_sems, recv_sems, k - 1, peer))
        for cp in sends:
            cp.start()
        for k, (px, py, pc) in enumerate(peers):
            slot = out_ref.at[4 * px + 2 * py + pc]
            _remote(slot, slot, send_sems, recv_sems, k, (px, py, pc)).wait_recv()
        for cp in sends:
            cp.wait_send()
        local.wait()

    return pl.pallas_call(
        body, name="gather_small", in_specs=[HBM_SPEC], out_specs=HBM_SPEC,
        out_shape=jax.ShapeDtypeStruct((N_DEV,) + buf.shape, F32),
        scratch_shapes=[pltpu.SemaphoreType.DMA((N_DEV - 1,)), pltpu.SemaphoreType.DMA((N_DEV - 1,)),
                        pltpu.SemaphoreType.DMA(())],
    )(buf)


def _ew_rows(rows, cols, bytes_per_elem):
    budget = 20 * 1024 * 1024
    return _tile(rows, max(16, budget // (2 * bytes_per_elem * cols) // 16 * 16), 16)


def _sum_pair(grad, recv):
    _, h, cols = recv.shape
    tr = _ew_rows(h, cols, 14)
    nb = h // tr
    half = pl.BlockSpec((None, tr, cols), lambda i, q: (q, i, 0))

    def body(a_ref, b_ref, f_ref, h_ref):
        s = a_ref[...] + b_ref[...]
        h_ref[...] = s.astype(BF16)

        @pl.when(pl.program_id(1) == 2 * lax.axis_index("x") + lax.axis_index("y"))
        def _():
            f_ref[...] = s

    return pl.pallas_call(
        body, name="rs_sum_pair", grid=(nb, N_CHIPS),
        in_specs=[pl.BlockSpec((None, tr, cols), lambda i, q: (q, lax.axis_index("c") * nb + i, 0)), half],
        out_specs=[pl.BlockSpec((tr, cols), lambda i, q: (i, 0)), half],
        out_shape=[jax.ShapeDtypeStruct((h, cols), F32), jax.ShapeDtypeStruct(recv.shape, BF16)],
        compiler_params=_params(28 * tr * cols, ("parallel", "arbitrary")),
    )(grad, recv)


def _sum_four(pf, recv):
    h, cols = pf.shape
    tr = _ew_rows(h, cols, 14)
    nb = h // tr

    def body(a_ref, r_ref, o_ref):
        o_ref[...] = ((a_ref[...] + r_ref[0].astype(F32)) + r_ref[1].astype(F32)) + r_ref[2].astype(F32)

    return pl.pallas_call(
        body, name="rs_sum_four", grid=(nb,),
        in_specs=[pl.BlockSpec((tr, cols), lambda i: (i, 0)),
                  pl.BlockSpec((3, tr, cols), lambda i: (0, i, 0))],
        out_specs=pl.BlockSpec((tr, cols), lambda i: (lax.axis_index("c") * nb + i, 0)),
        out_shape=jax.ShapeDtypeStruct((2 * h, cols), F32),
        compiler_params=_params(28 * tr * cols, ("parallel",)),
    )(pf, recv)


def _sum_eight(gathered):
    _, rows, cols = gathered.shape
    tr = _tile(rows, 512, SUBLANES)

    def body(g_ref, o_ref):
        s = g_ref[0]
        for d in range(1, N_DEV):
            s = s + g_ref[d]
        o_ref[...] = s

    return pl.pallas_call(
        body, name="sum_eight", grid=(rows // tr,),
        in_specs=[pl.BlockSpec((N_DEV, tr, cols), lambda i: (0, i, 0))],
        out_specs=_row_spec(tr, cols), out_shape=jax.ShapeDtypeStruct((rows, cols), F32),
        compiler_params=_params(None, ("parallel",)),
    )(gathered)


def _cast_layer(w, l):
    _, rows, cols = w.shape
    tr = _ew_rows(rows, cols, 6)

    def body(w_ref, o_ref):
        o_ref[...] = w_ref[...].astype(BF16)

    return pl.pallas_call(
        body, name="cast_bf16", grid=(rows // tr,),
        in_specs=[pl.BlockSpec((None, tr, cols), lambda i: (l, i, 0))],
        out_specs=pl.BlockSpec((None, tr, cols), lambda i: (2 * lax.axis_index("x") + lax.axis_index("y"), i, 0)),
        out_shape=jax.ShapeDtypeStruct((N_CHIPS, rows, cols), BF16),
        compiler_params=_params(12 * tr * cols, ("parallel",)),
    )(w)


def _adamw(g, w, m, v, l, prev):
    L, rows, cols = w.shape
    tr = _ew_rows(rows, cols, 32)
    layer = pl.BlockSpec((None, tr, cols), lambda i: (l, i, 0))

    def body(g_ref, w_ref, m_ref, v_ref, *rest):
        og_ref, od_ref, om_ref, ov_ref = rest[-4:]
        gv = g_ref[...]
        mn = ADAM_B1 * m_ref[...] + (1.0 - ADAM_B1) * gv
        vn = ADAM_B2 * v_ref[...] + (1.0 - ADAM_B2) * (gv * gv)
        m_hat = mn / ADAM_C1
        v_hat = vn / ADAM_C2
        od_ref[...] = -ADAM_LR * (m_hat / (jnp.sqrt(v_hat) + ADAM_EPS) + ADAM_WD * w_ref[...])
        og_ref[...] = gv
        om_ref[...] = mn
        ov_ref[...] = vn

    in_specs = [_row_spec(tr, cols), layer, layer, layer]
    args = [g, w, m, v]
    aliases = {}
    if prev is not None:
        in_specs += [HBM_SPEC] * 4
        args += list(prev)
        aliases = {4 + i: i for i in range(4)}
    return pl.pallas_call(
        body, name="adamw", grid=(rows // tr,), in_specs=in_specs, out_specs=[layer] * 4,
        out_shape=[jax.ShapeDtypeStruct((L, rows, cols), F32)] * 4, input_output_aliases=aliases,
        compiler_params=_params(64 * tr * cols, ("parallel",)),
    )(*args)


WEIGHT_ORDER = ("ln1_g", "w_in", "qn_a", "kn_a", "rpb", "qn_b", "kn_b", "sink", "on_a", "on_b", "w_out", "ln2_g",
                "w_up", "conv_w", "conv_b", "w_down")
BIG = ("w_in", "w_out", "w_up", "w_down")
SMALL = tuple(n for n in WEIGHT_ORDER if n not in BIG)


def _train_step(x, positions, target, w, m, v):
    S, D = x.shape
    L = w["w_in"].shape[0]
    naw = w["rpb"].shape[1] * HEAD_DIM
    waw = w["sink"].shape[1] * HEAD_DIM
    cin = w["w_in"].shape[2]
    kvw = (N_CHIPS * cin - 3 * naw - waw) // 2
    hkv = kvw // HEAD_DIM
    dims = (naw, waw, kvw)
    assert naw == waw, "the two head groups are normalised by one kernel and must be equally wide"
    cup = w["w_up"].shape[2]
    c2 = N_CHIPS * cup
    dff = c2 // 2
    mix = naw + waw
    sh_in = _ColShards(cin, _tile(cin, 1152, LANES), False)
    sh_up = _ColShards(cup, cup // 2 if (cup // 2) % LANES == 0 else cup, True)
    tc = sh_up.tn
    cos2, sgn_sin = _rope_tables(positions)
    q_me = 2 * lax.axis_index("x") + lax.axis_index("y")
    row = lambda a: a[None]

    def own_slots(l):
        small = lax.dynamic_update_slice_in_dim(jnp.zeros((N_CHIPS,) + w["conv_w"].shape[1:], F32),
                                                w["conv_w"][l][None], q_me, axis=0)
        return [_cast_layer(w[n], l) for n in BIG], small

    gath = {}

    def prepare(l):
        bufs, small = own_slots(l)
        gath.update({(l, n): b for n, b in zip(BIG, bufs)})
        gath[l, "conv_w"] = small

    def over_ici(l, names, small=False):
        keys = [(l, n) for n in names]
        ex = _ag_chip_exchange([gath[k] for k in keys], gath[l, "conv_w"] if small else None)
        return ex, keys + ([(l, "conv_w")] if small else [])

    def to_sibling(l, names):
        keys = [(l, n) for n in names]
        return _ag_sibling_pass([gath[k] for k in keys]), keys

    def together(parts):
        ex, keys = parts[0]
        for e, k in parts[1:]:
            ex, keys = _both(ex, e), keys + k
        return ex, keys

    def carried(call, parts):
        if not parts:
            return call(None)
        ex, keys = together(parts)
        out, results = call(ex)
        gath.update(dict(zip(keys, results)))
        return out

    prepare(0)
    ex, keys = over_ici(0, ("w_in", "w_out"), True)
    gath.update(dict(zip(keys, _exchange_alone("ag_first_over_ici", ex))))
    ex, keys = to_sibling(0, ("w_in", "w_out"))
    gath.update(dict(zip(keys, _exchange_alone("ag_first_to_sibling", ex))))
    saved = []
    for l in range(L):
        more = l + 1 < L
        if more:
            prepare(l + 1)
        g_cw = gath[l, "conv_w"]
        cw_p = _to_paired(jnp.transpose(g_cw, (1, 0, 2)).reshape(3, c2), sh_up)
        cb_p = _to_paired(row(w["conv_b"][l]), sh_up)
        gains = jnp.concatenate([row(w["qn_a"][l]), row(w["kn_a"][l]), row(w["qn_b"][l]), row(w["kn_b"][l]),
                                 jnp.zeros((SUBLANES - 4, HEAD_DIM), F32)], axis=0)
        btab = _na_bias_table(w["rpb"][l])
        sink_col = _sink_col(w["sink"][l], hkv)

        h = _rms_fwd(x, row(w["ln1_g"][l]))
        g_in = gath[l, "w_in"]
        proj = carried(lambda ex: _mm_cols_fwd("mm_proj", h, g_in, sh_in, F32, tm=1024, carry=ex),
                       [over_ici(0, ("w_up",))] if l == 0 else [to_sibling(l, ("w_down",))])
        qa, ka, va, qb, kb, vb = _qk_prep_fwd(proj, cos2, sgn_sin, gains, dims)
        oa = _na_fwd(qa, ka, va, btab)
        ob = _wa_fwd(qb, kb, vb, sink_col)
        o = _out_norm_fwd(oa, ob, row(w["on_a"][l]), row(w["on_b"][l]))
        g_out = gath[l, "w_out"].reshape(mix, D)
        x1 = carried(lambda ex: _mm_plain("mm_attn_out", o, g_out, NN, F32, tm=1024, tn=1024, tk=2048, res=x, carry=ex),
                     [to_sibling(0, ("w_up",)), over_ici(0, ("w_down",))] if l == 0 else [])
        h2 = _rms_fwd(x1, row(w["ln2_g"][l]))
        g_up = gath[l, "w_up"]
        up_pre = carried(lambda ex: _mm_cols_fwd("mm_up", h2, g_up, sh_up, F32, tm=1024, carry=ex),
                         ([to_sibling(0, ("w_down",))] if l == 0 else [])
                         + ([over_ici(l + 1, ("w_in", "w_out", "w_up"), True)] if more else []))
        act = _convgate_fwd(up_pre, cw_p, cb_p, tc)
        g_dn = gath[l, "w_down"].reshape(dff, D)
        x2 = carried(lambda ex: _mm_plain("mm_down", act, g_dn, NN, F32, tm=1024, tn=512, tk=5632, res=x1, carry=ex),
                     [to_sibling(l + 1, ("w_in", "w_out", "w_up")), over_ici(l + 1, ("w_down",))] if more else [])
        saved.append(dict(g_in=g_in, g_out=g_out, g_up=g_up, g_dn=g_dn, cw_p=cw_p, cb_p=cb_p, gains=gains, btab=btab,
                          sink_col=sink_col, x=x, h=h, proj=proj, qa=qa, ka=ka, va=va, qb=qb, kb=kb, vb=vb, oa=oa,
                          ob=ob, o=o, x1=x1, h2=h2, up_pre=up_pre, act=act))
        x = x2

    dx, dxb, loss_cols = _loss_grad(x, target)

    small_grads = [None] * L
    stacked = {n: None for n in BIG}

    def update(layer, shards):
        for n, g in zip(BIG, shards):
            stacked[n] = _adamw(g, w[n], m[n], v[n], layer, stacked[n])

    above = None
    for l in reversed(range(L)):
        s = saved[l]
        if above is None:
            dact = _mm_plain("mm_down_dx", dxb, s["g_dn"], NT, F32, tm=1024, tn=1408, tk=2048)
        else:
            dact, recvs = _mm_plain("mm_down_dx_rs", dxb, s["g_dn"], NT, F32, tm=1024, tn=1408, tk=2048,
                                    carry=_rs_sibling_exchange(above[1]))
            pairs = [_sum_pair(g, r) for g, r in zip(above[1], recvs)]
        dup_pre, dcw_p, dcb_p = _convgate_bwd(s["up_pre"], dact, s["cw_p"], s["cb_p"], tc)
        if above is None:
            dh2 = _mm_cols_bwd_x("mm_up_dx", dup_pre, s["g_up"], sh_up, F32, tm=1024, tn=2048)
        else:
            dh2, gots = _mm_cols_bwd_x("mm_up_dx_rs", dup_pre, s["g_up"], sh_up, F32, tm=1024, tn=2048,
                                       carry=_rs_chip_exchange([p[1] for p in pairs]))
            halves = [_sum_four(p[0], r) for p, r in zip(pairs, gots)]
        d_dn, (dx1, dx1b, d_ln2) = _mm_plain(
            "mm_down_dw", s["act"], dxb, TN, F32, tm=1408, tn=512, tk=2048,
            rider=lambda steps: _rms_bwd(dh2, s["x1"], row(w["ln2_g"][l]), dx, steps))
        d_dn = d_dn.reshape(N_CHIPS, dff // N_CHIPS, D)
        if above is None:
            do = _mm_plain("mm_attn_out_dx", dx1b, s["g_out"], NT, F32, tm=1024, tn=1024, tk=2048)
        else:
            do, shards = _mm_plain("mm_attn_out_dx_rs", dx1b, s["g_out"], NT, F32, tm=1024, tn=1024, tk=2048,
                                   carry=_rs_sibling_share(halves))
            update(above[0], shards)
        last = l == 0
        d_out, (doa, dob, d_on_a, d_on_b), *rest = _mm_plain(
            "mm_attn_out_dw", s["o"], dx1b, TN, F32, tm=1024, tn=512, tk=2048,
            carry=_rs_sibling_exchange([d_dn]) if last else None,
            rider=lambda steps: _out_norm_bwd(do, s["oa"], s["ob"], row(w["on_a"][l]), row(w["on_b"][l]), steps))
        d_out = d_out.reshape(N_CHIPS, mix // N_CHIPS, D)
        if last:
            pair_dn = _sum_pair(d_dn, rest[0][0])
        dqa, dka, dva, dbtab = _na_bwd(s["qa"], s["ka"], s["va"], s["btab"], doa)
        dqb, dkb, dvb, dsink_col = _wa_bwd(s["qb"], s["kb"], s["vb"], s["sink_col"], dob)
        d_up, (dproj, dgains) = _mm_cols_bwd_w(
            "mm_up_dw", s["h2"], dup_pre, sh_up, tm=1024, tk=2048,
            rider=lambda steps: _qk_prep_bwd(s["proj"], dqa, dka, dva, dqb, dkb, dvb, cos2, sgn_sin, s["gains"], dims, steps))
        if last:
            dh, (recv_up, got_dn) = _mm_cols_bwd_x(
                "mm_proj_dx", dproj, s["g_in"], sh_in, F32, tm=1024, tn=2048,
                carry=_both(_rs_sibling_exchange([d_up]), _rs_chip_exchange([pair_dn[1]])))
            pair_up = _sum_pair(d_up, recv_up)
        else:
            dh = _mm_cols_bwd_x("mm_proj_dx", dproj, s["g_in"], sh_in, F32, tm=1024, tn=2048)
        d_in, (dx, dxb, d_ln1), *rest = _mm_cols_bwd_w(
            "mm_proj_dw", s["h"], dproj, sh_in, tm=512, tk=2048,
            carry=_rs_chip_exchange([pair_up[1]]) if last else None,
            rider=lambda steps: _rms_bwd(dh, s["x"], row(w["ln1_g"][l]), dx1, steps))
        if last:
            got_up = rest[0][0]

        small_grads[l] = dict(
            ln1_g=d_ln1[0], qn_a=dgains[0], kn_a=dgains[1], rpb=_na_bias_table_t(dbtab), qn_b=dgains[2],
            kn_b=dgains[3], sink=_sink_col_t(dsink_col), on_a=d_on_a[0], on_b=d_on_b[0], ln2_g=d_ln2[0],
            conv_w=_from_paired(dcw_p, sh_up), conv_b=_from_paired(dcb_p, sh_up)[0])
        above = (l, [d_in, d_out, d_up, d_dn])
    recvs = _exchange_alone("rs_sibling_exchange", _rs_sibling_exchange([d_in, d_out]))
    pair_in, pair_out = _sum_pair(d_in, recvs[0]), _sum_pair(d_out, recvs[1])
    got_in, got_out = _exchange_alone("rs_chip_exchange", _rs_chip_exchange([pair_in[1], pair_out[1]]))
    halves = [_sum_four(p[0], g) for p, g in ((pair_in, got_in), (pair_out, got_out), (pair_up, got_up), (pair_dn, got_dn))]
    update(0, _exchange_alone("rs_sibling_share", _rs_sibling_share(halves)))
    grad_x = dx[None]

    full_shapes = {n: (w[n].shape[1:] if n != "conv_w" else (3, c2)) for n in SMALL}
    packed = _pack([loss_cols] + [small_grads[l][n] for l in range(L) for n in SMALL])
    total = _sum_eight(_gather_small(packed))
    parts = _unpack(total, [(D,)] + [full_shapes[n] for _ in range(L) for n in SMALL])
    loss = 0.5 * jnp.sum(parts[0]) / D
    per_layer = [dict(zip(SMALL, parts[1 + l * len(SMALL):1 + (l + 1) * len(SMALL)])) for l in range(L)]
    for l in range(L):
        per_layer[l]["conv_w"] = lax.dynamic_slice_in_dim(per_layer[l]["conv_w"], q_me * cup, cup, axis=1)
    small_g = {n: jnp.stack([per_layer[l][n] for l in range(L)]) for n in SMALL}
    outs = _adamw(_pack([small_g[n] for n in SMALL]), _pack([w[n] for n in SMALL])[None],
                  _pack([m[n] for n in SMALL])[None], _pack([v[n] for n in SMALL])[None], 0, None)
    small_out = [dict(zip(SMALL, _unpack(o[0], [w[n].shape for n in SMALL]))) for o in outs]

    result = [loss, grad_x]
    for i in range(4):
        result += [stacked[n][i] if n in BIG else small_out[i][n] for n in WEIGHT_ORDER]
    return tuple(result)


def kernel(x, positions, ln1_g, w_in, qn_a, kn_a, rpb, qn_b, kn_b, sink, on_a, on_b, w_out, ln2_g, w_up, conv_w, conv_b, w_down, loss_target, m_ln1_g, m_w_in, m_qn_a, m_kn_a, m_rpb, m_qn_b, m_kn_b, m_sink, m_on_a, m_on_b, m_w_out, m_ln2_g, m_w_up, m_conv_w, m_conv_b, m_w_down, v_ln1_g, v_w_in, v_qn_a, v_kn_a, v_rpb, v_qn_b, v_kn_b, v_sink, v_on_a, v_on_b, v_w_out, v_ln2_g, v_w_up, v_conv_w, v_conv_b, v_w_down):
    given = dict(locals())
    w = {n: given[n] for n in WEIGHT_ORDER}
    m = {n: given["m_" + n] for n in WEIGHT_ORDER}
    v = {n: given["v_" + n] for n in WEIGHT_ORDER}
    return _train_step(x[0], positions, loss_target[0], w, m, v)
```

```python
import functools
import math

import jax
import jax.numpy as jnp
import numpy as np
from jax import lax
from jax.experimental import pallas as pl
from jax.experimental.pallas import tpu as pltpu

F32 = jnp.float32
BF16 = jnp.bfloat16

HEAD_DIM = 128
GRID_W = 64
NA_WIN_R = 8
NA_WIN_C = 16
WA_WINDOW = 128
WA_BLOCK = 128
ROPE_THETA = 10000.0
EPS = 1e-6
NEG = -1e30
ATTN_SCALE = 1.0 / math.sqrt(HEAD_DIM)

ADAM_LR = 0.001
ADAM_B1 = 0.9
ADAM_B2 = 0.999
ADAM_EPS = 1e-08
ADAM_WD = 0.01
ADAM_STEP = 10
ADAM_C1 = 1.0 - ADAM_B1 ** ADAM_STEP
ADAM_C2 = 1.0 - ADAM_B2 ** ADAM_STEP

V7X_VMEM_BYTES = 64 * 1024 * 1024
V7X_VMEM_CAP = V7X_VMEM_BYTES - 6 * 1024 * 1024
LANES = 128
SUBLANES = 8
N_CHIPS = 4
N_DEV = 8
MESH = pl.DeviceIdType.MESH

NN = ((1,), (0,))
NT = ((1,), (1,))
TN = ((0,), (0,))


def _tile(n, pref, mult):
    best = None
    d = mult
    while d <= min(n, pref):
        if n % d == 0:
            best = d
        d += mult
    return n if best is None else best


def _nbytes(shape, dtype):
    n = 1
    for s in shape:
        if s is not None:
            n *= s
    return n * jnp.dtype(dtype).itemsize


def _params(est_bytes=None, sem=None, **kw):
    if est_bytes is not None:
        kw["vmem_limit_bytes"] = int(min(V7X_VMEM_CAP, max(32 * 1024 * 1024, est_bytes * 5 // 4 + (4 << 20))))
    if sem is not None:
        kw["dimension_semantics"] = sem
    return pltpu.CompilerParams(**kw)


def _dot(a, b, contract):
    return lax.dot_general(a.astype(BF16), b.astype(BF16), (contract, ((), ())), preferred_element_type=F32)


def _mm(name, a, b, *, grid, a_spec, b_spec, o_spec, out_shape, contract, res=None, res_spec=None, carry=None,
        rider=None):
    nk = grid[2]
    n_steps = grid[0] * grid[1] * grid[2]
    acc_shape = tuple(s for s in o_spec.block_shape if s is not None)
    n_in = 2 if res is None else 3
    r_in = 0 if rider is None else len(rider.operands)
    r_out = 0 if rider is None else len(rider.out_shapes)
    x_in = 0 if carry is None else len(carry.operands)
    x_out = 0 if carry is None else carry.n_sems

    def body(*refs):
        a_ref, b_ref = refs[:2]
        r_ref = None if res is None else refs[2]
        pos = n_in
        r_ins, pos = refs[pos:pos + r_in], pos + r_in
        x_ins, pos = refs[pos:pos + x_in], pos + x_in
        o_ref, pos = refs[pos], pos + 1
        r_outs, pos = refs[pos:pos + r_out], pos + r_out
        x_outs, pos = refs[pos:pos + x_out], pos + x_out
        scr = refs[pos:]
        step = (pl.program_id(0) * grid[1] + pl.program_id(1)) * grid[2] + pl.program_id(2)
        if carry is not None:
            sems, scr = scr[-2:], scr[:-2]

            @pl.when(step == 0)
            def _():
                carry.start(x_ins, x_outs, *sems)

        p = _dot(a_ref[...], b_ref[...], contract)
        if rider is not None:
            rider.body(step, r_ins, r_outs)

        def finish(acc):
            if r_ref is not None:
                acc = acc + r_ref[...]
            o_ref[...] = acc.astype(o_ref.dtype)

        if nk == 1:
            finish(p)
        else:
            acc_ref = scr[0]
            k = pl.program_id(2)

            @pl.when(k == 0)
            def _():
                acc_ref[...] = p

            @pl.when(jnp.logical_and(k > 0, k < nk - 1))
            def _():
                acc_ref[...] += p

            @pl.when(k == nk - 1)
            def _():
                finish(acc_ref[...] + p)

        if carry is not None:
            @pl.when(step == n_steps - 1)
            def _():
                carry.wait(x_ins, x_outs, *sems)

    in_specs = [a_spec, b_spec]
    args = [a, b]
    est = 2 * (_nbytes(a_spec.block_shape, a.dtype) + _nbytes(b_spec.block_shape, b.dtype)
               + _nbytes(o_spec.block_shape, out_shape.dtype)) + 2 * _nbytes(acc_shape, F32)
    if res is not None:
        in_specs.append(res_spec)
        args.append(res)
        est += 2 * _nbytes(res_spec.block_shape, res.dtype)
    scratch = [] if nk == 1 else [pltpu.VMEM(acc_shape, F32)]
    if carry is None and rider is None:
        return pl.pallas_call(
            body, name=name, grid=grid, in_specs=in_specs, out_specs=o_spec, out_shape=out_shape,
            scratch_shapes=scratch,
            compiler_params=_params(est, ("parallel", "parallel", "arbitrary")),
        )(*args)

    def by_step(spec):
        return pl.BlockSpec(spec.block_shape, lambda i, j, k: spec.index_map((i * grid[1] + j) * grid[2] + k))

    out_specs, out_shapes, aliases = [o_spec], [out_shape], {}
    if rider is not None:
        in_specs += [by_step(s) for s in rider.in_specs]
        args += rider.operands
        out_specs += [by_step(s) for s in rider.out_specs]
        out_shapes += rider.out_shapes
        est += rider.vmem_bytes
    if carry is not None:
        aliases = {len(args) + i: len(out_shapes) + o for i, o in carry.aliases.items()}
        in_specs += [HBM_SPEC] * x_in
        args += carry.operands
        out_specs += [HBM_SPEC] * x_out
        out_shapes += carry.out_shapes
        scratch += carry.scratch()
    outs = pl.pallas_call(
        body, name=name, grid=grid, in_specs=in_specs, out_specs=out_specs, out_shape=out_shapes,
        input_output_aliases=aliases, scratch_shapes=scratch,
        compiler_params=_params(est, ("arbitrary", "arbitrary", "arbitrary")),
    )(*args)
    results = [outs[0]]
    if rider is not None:
        results.append(list(outs[1:1 + r_out]))
    if carry is not None:
        results.append(list(outs[1 + r_out:]))
    return tuple(results)


def _mm_plain(name, a, b, contract, out_dtype, *, tm, tn, tk, res=None, carry=None, rider=None):
    if contract == NN:
        (M, K), N = a.shape, b.shape[1]
    elif contract == NT:
        (M, K), N = a.shape, b.shape[0]
    else:
        (K, M), N = a.shape, b.shape[1]
    tm, tn, tk = _tile(M, tm, LANES), _tile(N, tn, LANES), _tile(K, tk, LANES)
    grid = (M // tm, N // tn, K // tk)
    if contract == TN:
        a_spec = pl.BlockSpec((tk, tm), lambda i, j, k: (k, i))
    else:
        a_spec = pl.BlockSpec((tm, tk), lambda i, j, k: (i, k))
    if contract == NT:
        b_spec = pl.BlockSpec((tn, tk), lambda i, j, k: (j, k))
    else:
        b_spec = pl.BlockSpec((tk, tn), lambda i, j, k: (k, j))
    o_spec = pl.BlockSpec((tm, tn), lambda i, j, k: (i, j))
    return _mm(name, a, b, grid=grid, a_spec=a_spec, b_spec=b_spec, o_spec=o_spec,
               out_shape=jax.ShapeDtypeStruct((M, N), out_dtype), contract=contract,
               res=res, res_spec=None if res is None else pl.BlockSpec((tm, tn), lambda i, j, k: (i, j)), carry=carry,
               rider=None if rider is None else rider(math.prod(grid)))


class _ColShards:
    def __init__(self, cols_per_chip, tn, paired):
        self.c = cols_per_chip
        self.tn = tn
        self.tps = cols_per_chip // tn
        self.ntiles = N_CHIPS * self.tps
        self.paired = paired

    def nat(self, t):
        if not self.paired:
            return t
        return (t % 2) * (self.ntiles // 2) + t // 2

    def chip(self, t):
        return self.nat(t) // self.tps

    def within(self, t):
        return self.nat(t) % self.tps


def _mm_cols_fwd(name, a, wg, sh, out_dtype, *, tm, carry=None):
    S, K = a.shape
    tm = _tile(S, tm, LANES)
    grid = (S // tm, sh.ntiles, 1)
    return _mm(name, a, wg, grid=grid,
               a_spec=pl.BlockSpec((tm, K), lambda i, j, k: (i, 0)),
               b_spec=pl.BlockSpec((None, K, sh.tn), lambda i, j, k: (sh.chip(j), 0, sh.within(j))),
               o_spec=pl.BlockSpec((tm, sh.tn), lambda i, j, k: (i, j)),
               out_shape=jax.ShapeDtypeStruct((S, sh.ntiles * sh.tn), out_dtype), contract=NN, carry=carry)


def _mm_cols_bwd_x(name, dy, wg, sh, out_dtype, *, tm, tn, carry=None):
    S = dy.shape[0]
    K = wg.shape[1]
    tm, tn = _tile(S, tm, LANES), _tile(K, tn, LANES)
    grid = (S // tm, K // tn, sh.ntiles)
    return _mm(name, dy, wg, grid=grid,
               a_spec=pl.BlockSpec((tm, sh.tn), lambda i, j, k: (i, k)),
               b_spec=pl.BlockSpec((None, tn, sh.tn), lambda i, j, k: (sh.chip(k), j, sh.within(k))),
               o_spec=pl.BlockSpec((tm, tn), lambda i, j, k: (i, j)),
               out_shape=jax.ShapeDtypeStruct((S, K), out_dtype), contract=NT, carry=carry)


def _mm_cols_bwd_w(name, a, dy, sh, *, tm, tk, carry=None, rider=None):
    S, K = a.shape
    tm, tk = _tile(K, tm, LANES), _tile(S, tk, LANES)
    grid = (K // tm, sh.ntiles, S // tk)
    return _mm(name, a, dy, grid=grid,
               a_spec=pl.BlockSpec((tk, tm), lambda i, j, k: (k, i)),
               b_spec=pl.BlockSpec((tk, sh.tn), lambda i, j, k: (k, j)),
               o_spec=pl.BlockSpec((None, tm, sh.tn), lambda i, j, k: (sh.chip(j), i, sh.within(j))),
               out_shape=jax.ShapeDtypeStruct((N_CHIPS, K, sh.c), F32), contract=TN, carry=carry,
               rider=None if rider is None else rider(math.prod(grid)))


def _row_spec(tr, width):
    return pl.BlockSpec((tr, width), lambda i: (i, 0))


def _full_spec(shape):
    nd = len(shape)
    return pl.BlockSpec(shape, lambda i: (0,) * nd)


def _rms_fwd(x, g):
    S, D = x.shape
    tr = _tile(S, 512, 16)

    def body(x_ref, g_ref, h_ref):
        xv = x_ref[...]
        r = lax.rsqrt(jnp.mean(xv * xv, axis=-1, keepdims=True) + EPS)
        h_ref[...] = (xv * r * g_ref[...]).astype(BF16)

    return pl.pallas_call(
        body, name="rms_fwd", grid=(S // tr,),
        in_specs=[_row_spec(tr, D), _full_spec((1, D))], out_specs=_row_spec(tr, D),
        out_shape=jax.ShapeDtypeStruct((S, D), BF16),
        compiler_params=_params(12 * tr * D, ("parallel",)),
    )(x, g)


class _Rider:
    def __init__(self, operands, in_specs, out_specs, out_shapes, body, vmem_bytes):
        self.operands, self.in_specs, self.out_specs = list(operands), list(in_specs), list(out_specs)
        self.out_shapes, self.body, self.vmem_bytes = list(out_shapes), body, vmem_bytes


def _accumulate(step, ref, part):
    @pl.when(step == 0)
    def _():
        ref[...] = part

    @pl.when(step > 0)
    def _():
        ref[...] += part


def _rms_bwd(dh, x, g, dres, steps):
    S, D = x.shape
    tr = S // steps

    def body(step, ins, outs):
        dh_ref, x_ref, g_ref, dres_ref = ins
        dx_ref, dxb_ref, dg_ref = outs
        xv = x_ref[...]
        dhv = dh_ref[...]
        r = lax.rsqrt(jnp.mean(xv * xv, axis=-1, keepdims=True) + EPS)
        gy = dhv * g_ref[...]
        dot = jnp.mean(xv * gy, axis=-1, keepdims=True)
        dx = dres_ref[...] + (r * gy - xv * (r * r * r * dot))
        dx_ref[...] = dx
        dxb_ref[...] = dx.astype(BF16)
        _accumulate(step, dg_ref, jnp.sum(dhv * (xv * r), axis=0, keepdims=True))

    return _Rider([dh, x, g, dres],
                  [_row_spec(tr, D), _row_spec(tr, D), _full_spec((1, D)), _row_spec(tr, D)],
                  [_row_spec(tr, D), _row_spec(tr, D), _full_spec((1, D))],
                  [jax.ShapeDtypeStruct((S, D), F32), jax.ShapeDtypeStruct((S, D), BF16),
                   jax.ShapeDtypeStruct((1, D), F32)], body, 40 * tr * D)


def _out_norm_fwd(oa, ob, ga, gb):
    S, W = oa.shape
    tr = _tile(S, 512, 16)

    def body(oa_ref, ob_ref, ga_ref, gb_ref, o_ref):
        for src, g_ref, off in ((oa_ref, ga_ref, 0), (ob_ref, gb_ref, W)):
            v = src[...]
            r = lax.rsqrt(jnp.mean(v * v, axis=-1, keepdims=True) + EPS)
            o_ref[:, off:off + W] = (v * r * g_ref[...]).astype(BF16)

    return pl.pallas_call(
        body, name="out_norm_fwd", grid=(S // tr,),
        in_specs=[_row_spec(tr, W), _row_spec(tr, W), _full_spec((1, W)), _full_spec((1, W))],
        out_specs=_row_spec(tr, 2 * W), out_shape=jax.ShapeDtypeStruct((S, 2 * W), BF16),
        compiler_params=_params(24 * tr * W, ("parallel",)),
    )(oa, ob, ga, gb)


def _out_norm_bwd(do, oa, ob, ga, gb, steps):
    S, W = oa.shape
    tr = S // steps

    def body(step, ins, outs):
        do_ref, oa_ref, ob_ref, ga_ref, gb_ref = ins
        doa_ref, dob_ref, dga_ref, dgb_ref = outs
        parts = []
        for src, g_ref, off, d_ref in ((oa_ref, ga_ref, 0, doa_ref), (ob_ref, gb_ref, W, dob_ref)):
            v = src[...]
            dv = do_ref[:, off:off + W]
            r = lax.rsqrt(jnp.mean(v * v, axis=-1, keepdims=True) + EPS)
            gy = dv * g_ref[...]
            dot = jnp.mean(v * gy, axis=-1, keepdims=True)
            d_ref[...] = (r * gy - v * (r * r * r * dot)).astype(BF16)
            parts.append(jnp.sum(dv * (v * r), axis=0, keepdims=True))
        _accumulate(step, dga_ref, parts[0])
        _accumulate(step, dgb_ref, parts[1])

    return _Rider([do, oa, ob, ga, gb],
                  [_row_spec(tr, 2 * W), _row_spec(tr, W), _row_spec(tr, W), _full_spec((1, W)), _full_spec((1, W))],
                  [_row_spec(tr, W), _row_spec(tr, W), _full_spec((1, W)), _full_spec((1, W))],
                  [jax.ShapeDtypeStruct((S, W), BF16), jax.ShapeDtypeStruct((S, W), BF16),
                   jax.ShapeDtypeStruct((1, W), F32), jax.ShapeDtypeStruct((1, W), F32)], body, 48 * tr * W)


def _loss_grad(y, t):
    S, D = y.shape
    tr = _tile(S, 256, 16)

    def body(y_ref, t_ref, dy_ref, dyb_ref, ls_ref):
        e = y_ref[...] - t_ref[...]
        dy = e * (1.0 / D)
        dy_ref[...] = dy
        dyb_ref[...] = dy.astype(BF16)
        part = jnp.sum(e * e, axis=0, keepdims=True)

        @pl.when(pl.program_id(0) == 0)
        def _():
            ls_ref[...] = part

        @pl.when(pl.program_id(0) > 0)
        def _():
            ls_ref[...] += part

    return pl.pallas_call(
        body, name="loss_grad", grid=(S // tr,),
        in_specs=[_row_spec(tr, D), _row_spec(tr, D)],
        out_specs=[_row_spec(tr, D), _row_spec(tr, D), _full_spec((1, D))],
        out_shape=[jax.ShapeDtypeStruct((S, D), F32), jax.ShapeDtypeStruct((S, D), BF16),
                   jax.ShapeDtypeStruct((1, D), F32)],
        compiler_params=_params(32 * tr * D, ("arbitrary",)),
    )(y, t)


def _head_rms(x, g):
    r = lax.rsqrt(jnp.mean(x * x, axis=-1, keepdims=True) + EPS)
    return x * r * g


def _head_rms_bwd(x, dy, g):
    r = lax.rsqrt(jnp.mean(x * x, axis=-1, keepdims=True) + EPS)
    gy = dy * g
    dot = jnp.mean(x * gy, axis=-1, keepdims=True)
    return r * gy - x * (r * r * r * dot), dy * (x * r)


def _rope(y, cos2, sgn_sin):
    return y * cos2 + pltpu.roll(y, HEAD_DIM // 2, axis=1) * sgn_sin


def _rope_t(dy, cos2, sgn_sin):
    return dy * cos2 + pltpu.roll(dy * sgn_sin, HEAD_DIM // 2, axis=1)


def _qk_prep_fwd(proj, cos2, sgn_sin, gains, dims):
    S, P = proj.shape
    naw, waw, kvw = dims
    tr = _tile(S, 256, 16)
    hd = HEAD_DIM

    def body(p_ref, c_ref, s_ref, g_ref, qa_ref, ka_ref, va_ref, qb_ref, kb_ref, vb_ref):
        c2 = c_ref[...]
        ss = s_ref[...]
        off = 0
        for dst, width, gi, rot in ((qa_ref, naw, 0, False), (ka_ref, naw, 1, False), (va_ref, naw, None, False),
                                    (qb_ref, waw, 2, True), (kb_ref, kvw, 3, True), (vb_ref, kvw, None, False)):
            for h in range(width // hd):
                xh = p_ref[:, off + h * hd:off + (h + 1) * hd]
                if gi is not None:
                    xh = _head_rms(xh, g_ref[gi:gi + 1, :])
                    if rot:
                        xh = _rope(xh, c2, ss)
                dst[:, h * hd:(h + 1) * hd] = xh.astype(BF16)
            off += width

    widths = (naw, naw, naw, waw, kvw, kvw)
    return pl.pallas_call(
        body, name="qk_prep_fwd", grid=(S // tr,),
        in_specs=[_row_spec(tr, P), _row_spec(tr, hd), _row_spec(tr, hd), _full_spec((SUBLANES, hd))],
        out_specs=[_row_spec(tr, w) for w in widths],
        out_shape=[jax.ShapeDtypeStruct((S, w), BF16) for w in widths],
        compiler_params=_params(16 * tr * P, ("parallel",)),
    )(proj, cos2, sgn_sin, gains)


def _qk_prep_bwd(proj, dqa, dka, dva, dqb, dkb, dvb, cos2, sgn_sin, gains, dims, steps):
    S, P = proj.shape
    naw, waw, kvw = dims
    tr = S // steps
    hd = HEAD_DIM

    def body(step, ins, outs):
        p_ref, dqa_ref, dka_ref, dva_ref, dqb_ref, dkb_ref, dvb_ref, c_ref, s_ref, g_ref = ins
        dp_ref, dg_ref = outs
        c2 = c_ref[...]
        ss = s_ref[...]
        off = 0
        dgs = [jnp.zeros((1, hd), F32) for _ in range(4)]
        for src, width, gi, rot in ((dqa_ref, naw, 0, False), (dka_ref, naw, 1, False), (dva_ref, naw, None, False),
                                    (dqb_ref, waw, 2, True), (dkb_ref, kvw, 3, True), (dvb_ref, kvw, None, False)):
            for h in range(width // hd):
                dy = src[:, h * hd:(h + 1) * hd].astype(F32)
                if gi is not None:
                    if rot:
                        dy = _rope_t(dy, c2, ss)
                    xh = p_ref[:, off + h * hd:off + (h + 1) * hd]
                    dy, dgt = _head_rms_bwd(xh, dy, g_ref[gi:gi + 1, :])
                    dgs[gi] = dgs[gi] + jnp.sum(dgt, axis=0, keepdims=True)
                dp_ref[:, off + h * hd:off + (h + 1) * hd] = dy.astype(BF16)
            off += width
        _accumulate(step, dg_ref, jnp.concatenate(dgs + [jnp.zeros((SUBLANES - 4, hd), F32)], axis=0))

    return _Rider([proj, dqa, dka, dva, dqb, dkb, dvb, cos2, sgn_sin, gains],
                  [_row_spec(tr, P), _row_spec(tr, naw), _row_spec(tr, naw), _row_spec(tr, naw),
                   _row_spec(tr, waw), _row_spec(tr, kvw), _row_spec(tr, kvw),
                   _row_spec(tr, hd), _row_spec(tr, hd), _full_spec((SUBLANES, hd))],
                  [_row_spec(tr, P), _full_spec((SUBLANES, hd))],
                  [jax.ShapeDtypeStruct((S, P), BF16), jax.ShapeDtypeStruct((SUBLANES, hd), F32)], body, 24 * tr * P)


NA_KEYS = NA_WIN_R * GRID_W
NA_ROWS_PER_STEP = 32


def _na_col_geometry():
    c = np.arange(GRID_W)
    col_start = np.clip(c - NA_WIN_C // 2, 0, GRID_W - NA_WIN_C)
    mask = (c[None, :] >= col_start[:, None]) & (c[None, :] < col_start[:, None] + NA_WIN_C)
    dc = np.clip(c[None, :] - c[:, None], -(NA_WIN_C - 1), NA_WIN_C - 1) + (NA_WIN_C - 1)
    onehot = (dc[:, :, None] == np.arange(2 * NA_WIN_C - 1)[None, None, :]) & mask[:, :, None]
    return mask, onehot


def _na_bias_table(rpb_l):
    H = rpb_l.shape[0]
    mask, onehot = _na_col_geometry()
    t = jnp.sum(jnp.where(onehot[None, None], rpb_l[:, :, None, None, :], 0.0), axis=-1)
    t = jnp.where(mask[None, None], t, NEG)
    per_delta = [jnp.transpose(t[:, d:d + NA_WIN_R], (0, 2, 1, 3)).reshape(H, GRID_W, NA_KEYS) for d in range(NA_WIN_R)]
    return jnp.stack(per_delta, axis=1)


def _na_bias_table_t(db):
    H = db.shape[0]
    _, onehot = _na_col_geometry()
    d5 = db.reshape(H, NA_WIN_R, GRID_W, NA_WIN_R, GRID_W)
    folded = jnp.einsum("hdqwk,qkc->hdwc", d5, onehot.astype(np.float32), precision=lax.Precision.HIGHEST)
    return sum(jnp.pad(folded[:, d], ((0, 0), (d, NA_WIN_R - 1 - d), (0, 0))) for d in range(NA_WIN_R))


def _na_row_geometry(r, rows):
    start = jnp.clip(r - NA_WIN_R // 2, 0, rows - NA_WIN_R)
    return start, start - r + (NA_WIN_R - 1)


def _softmax_rows(s):
    m = jnp.max(s, axis=-1, keepdims=True)
    e = jnp.exp(s - m)
    return e / jnp.sum(e, axis=-1, keepdims=True)


def _na_fwd(qa, ka, va, btab):
    S, W = qa.shape
    H = W // HEAD_DIM
    rows = S // GRID_W
    assert rows >= NA_WIN_R
    rb = _tile(rows, NA_ROWS_PER_STEP, 1)
    tq = rb * GRID_W

    def body(q_ref, k_ref, v_ref, b_ref, o_ref):
        i = pl.program_id(1)

        geo = [_na_row_geometry(i * rb + j, rows) for j in range(rb)]
        toks = [pl.ds(j * GRID_W, GRID_W) for j in range(rb)]
        wins = [pl.ds(pl.multiple_of(start * GRID_W, GRID_W), NA_KEYS) for start, _ in geo]
        scores = [_dot(q_ref[toks[j], :], k_ref[wins[j], :], NT) for j in range(rb)]
        probs = [_softmax_rows(scores[j] * ATTN_SCALE + b_ref[geo[j][1]]) for j in range(rb)]
        for j in range(rb):
            o_ref[toks[j], :] = _dot(probs[j], v_ref[wins[j], :], NN)

    kv_spec = pl.BlockSpec((S, HEAD_DIM), lambda h, i: (0, h))
    return pl.pallas_call(
        body, name="na_fwd", grid=(H, rows // rb),
        in_specs=[pl.BlockSpec((tq, HEAD_DIM), lambda h, i: (i, h)), kv_spec, kv_spec,
                  pl.BlockSpec((None, NA_WIN_R, GRID_W, NA_KEYS), lambda h, i: (h, 0, 0, 0))],
        out_specs=pl.BlockSpec((tq, HEAD_DIM), lambda h, i: (i, h)),
        out_shape=jax.ShapeDtypeStruct((S, W), F32),
        compiler_params=_params(8 * S * HEAD_DIM + (8 << 20), ("parallel", "arbitrary")),
    )(qa, ka, va, btab)


def _na_bwd(qa, ka, va, btab, doa):
    S, W = qa.shape
    H = W // HEAD_DIM
    rows = S // GRID_W
    rb = _tile(rows, NA_ROWS_PER_STEP, 1)
    tq = rb * GRID_W

    def body(q_ref, k_ref, v_ref, b_ref, do_ref, dq_ref, dk_ref, dv_ref, db_ref):
        i = pl.program_id(1)

        @pl.when(i == 0)
        def _():
            dk_ref[...] = jnp.zeros_like(dk_ref)
            dv_ref[...] = jnp.zeros_like(dv_ref)
            db_ref[...] = jnp.zeros_like(db_ref)

        steps = range(rb)
        geo = [_na_row_geometry(i * rb + j, rows) for j in steps]
        toks = [pl.ds(j * GRID_W, GRID_W) for j in steps]
        wins = [pl.ds(pl.multiple_of(start * GRID_W, GRID_W), NA_KEYS) for start, _ in geo]
        scores = [_dot(q_ref[toks[j], :], k_ref[wins[j], :], NT) for j in steps]
        dps = [_dot(do_ref[toks[j], :], v_ref[wins[j], :], NT) for j in steps]
        probs = [_softmax_rows(scores[j] * ATTN_SCALE + b_ref[geo[j][1]]) for j in steps]
        dss = [probs[j] * (dps[j] - jnp.sum(probs[j] * dps[j], axis=-1, keepdims=True)) for j in steps]
        for j in steps:
            db_ref[geo[j][1]] += dss[j]
        dsb = [(dss[j] * ATTN_SCALE).astype(BF16) for j in steps]
        for j in steps:
            dq_ref[toks[j], :] = _dot(dsb[j], k_ref[wins[j], :], NN)
        dks = [_dot(dsb[j], q_ref[toks[j], :], TN) for j in steps]
        dvs = [_dot(probs[j], do_ref[toks[j], :], TN) for j in steps]
        for j in steps:
            dk_ref[wins[j], :] += dks[j]
            dv_ref[wins[j], :] += dvs[j]

    kv_spec = pl.BlockSpec((S, HEAD_DIM), lambda h, i: (0, h))
    q_spec = pl.BlockSpec((tq, HEAD_DIM), lambda h, i: (i, h))
    b_spec = pl.BlockSpec((None, NA_WIN_R, GRID_W, NA_KEYS), lambda h, i: (h, 0, 0, 0))
    return pl.pallas_call(
        body, name="na_bwd", grid=(H, rows // rb),
        in_specs=[q_spec, kv_spec, kv_spec, b_spec, q_spec],
        out_specs=[q_spec, kv_spec, kv_spec, b_spec],
        out_shape=[jax.ShapeDtypeStruct((S, W), F32), jax.ShapeDtypeStruct((S, W), F32),
                   jax.ShapeDtypeStruct((S, W), F32), jax.ShapeDtypeStruct(btab.shape, F32)],
        compiler_params=_params(24 * S * HEAD_DIM + (12 << 20), ("parallel", "arbitrary")),
    )(qa, ka, va, btab, doa)


WA_KEYS = 3 * WA_BLOCK
WA_BLOCKS_PER_STEP = 8


def _wa_mask(qk, n, start, g):
    s = qk * ATTN_SCALE
    qpos = n * WA_BLOCK + lax.broadcasted_iota(jnp.int32, (WA_BLOCK, WA_KEYS), 0)
    kpos = start + lax.broadcasted_iota(jnp.int32, (WA_BLOCK, WA_KEYS), 1)
    valid = jnp.abs(kpos - qpos) <= WA_WINDOW
    valid = jnp.concatenate([valid] * g, axis=0)
    return jnp.where(valid, s, NEG)


def _wa_probs(s, sink):
    m = jnp.maximum(jnp.max(s, axis=-1, keepdims=True), sink)
    e = jnp.exp(s - m)
    es = jnp.exp(sink - m)
    den = jnp.sum(e, axis=-1, keepdims=True) + es
    return e / den, es / den


def _wa_stack(ref, tok, g):
    return jnp.concatenate([ref[tok, t * HEAD_DIM:(t + 1) * HEAD_DIM] for t in range(g)], axis=0)


def _wa_fwd(qb, kb, vb, sink_col):
    S, W = qb.shape
    hkv = kb.shape[1] // HEAD_DIM
    g = W // HEAD_DIM // hkv
    nb = S // WA_BLOCK
    assert S >= WA_KEYS
    qb_step = _tile(nb, WA_BLOCKS_PER_STEP, 1)
    tq = qb_step * WA_BLOCK

    def body(q_ref, k_ref, v_ref, s_ref, o_ref):
        i = pl.program_id(1)

        steps = range(qb_step)
        ns = [i * qb_step + j for j in steps]
        toks = [pl.ds(j * WA_BLOCK, WA_BLOCK) for j in steps]
        starts = [pl.multiple_of(jnp.clip((n - 1) * WA_BLOCK, 0, S - WA_KEYS), WA_BLOCK) for n in ns]
        wins = [pl.ds(start, WA_KEYS) for start in starts]
        scores = [_dot(_wa_stack(q_ref, toks[j], g), k_ref[wins[j], :], NT) for j in steps]
        probs = [_wa_probs(_wa_mask(scores[j], ns[j], starts[j], g), s_ref[...])[0] for j in steps]
        outs = [_dot(probs[j], v_ref[wins[j], :], NN) for j in steps]
        for j in steps:
            for t in range(g):
                o_ref[toks[j], t * HEAD_DIM:(t + 1) * HEAD_DIM] = outs[j][t * WA_BLOCK:(t + 1) * WA_BLOCK]

    kv_spec = pl.BlockSpec((S, HEAD_DIM), lambda h, i: (0, h))
    q_spec = pl.BlockSpec((tq, g * HEAD_DIM), lambda h, i: (i, h))
    return pl.pallas_call(
        body, name="wa_fwd", grid=(hkv, nb // qb_step),
        in_specs=[q_spec, kv_spec, kv_spec, pl.BlockSpec((None, g * WA_BLOCK, 1), lambda h, i: (h, 0, 0))],
        out_specs=q_spec, out_shape=jax.ShapeDtypeStruct((S, W), F32),
        compiler_params=_params(8 * S * HEAD_DIM + (12 << 20), ("parallel", "arbitrary")),
    )(qb, kb, vb, sink_col)


def _wa_bwd(qb, kb, vb, sink_col, dob):
    S, W = qb.shape
    KW = kb.shape[1]
    hkv = KW // HEAD_DIM
    g = W // HEAD_DIM // hkv
    nb = S // WA_BLOCK
    qb_step = _tile(nb, WA_BLOCKS_PER_STEP, 1)
    tq = qb_step * WA_BLOCK

    def body(q_ref, k_ref, v_ref, s_ref, do_ref, dq_ref, dk_ref, dv_ref, dsink_ref):
        i = pl.program_id(1)

        @pl.when(i == 0)
        def _():
            dk_ref[...] = jnp.zeros_like(dk_ref)
            dv_ref[...] = jnp.zeros_like(dv_ref)
            dsink_ref[...] = jnp.zeros_like(dsink_ref)

        steps = range(qb_step)
        ns = [i * qb_step + j for j in steps]
        toks = [pl.ds(j * WA_BLOCK, WA_BLOCK) for j in steps]
        starts = [pl.multiple_of(jnp.clip((n - 1) * WA_BLOCK, 0, S - WA_KEYS), WA_BLOCK) for n in ns]
        wins = [pl.ds(start, WA_KEYS) for start in starts]
        qss = [_wa_stack(q_ref, toks[j], g) for j in steps]
        doss = [_wa_stack(do_ref, toks[j], g) for j in steps]
        scores = [_dot(qss[j], k_ref[wins[j], :], NT) for j in steps]
        dps = [_dot(doss[j], v_ref[wins[j], :], NT) for j in steps]
        pp = [_wa_probs(_wa_mask(scores[j], ns[j], starts[j], g), s_ref[...]) for j in steps]
        dsums = [jnp.sum(pp[j][0] * dps[j], axis=-1, keepdims=True) for j in steps]
        dsb = [(pp[j][0] * (dps[j] - dsums[j]) * ATTN_SCALE).astype(BF16) for j in steps]
        dsink_ref[...] -= sum(pp[j][1] * dsums[j] for j in steps)
        dqs = [_dot(dsb[j], k_ref[wins[j], :], NN) for j in steps]
        dks = [_dot(dsb[j], qss[j], TN) for j in steps]
        dvs = [_dot(pp[j][0], doss[j], TN) for j in steps]
        for j in steps:
            for t in range(g):
                dq_ref[toks[j], t * HEAD_DIM:(t + 1) * HEAD_DIM] = dqs[j][t * WA_BLOCK:(t + 1) * WA_BLOCK]
            dk_ref[wins[j], :] += dks[j]
            dv_ref[wins[j], :] += dvs[j]

    kv_spec = pl.BlockSpec((S, HEAD_DIM), lambda h, i: (0, h))
    q_spec = pl.BlockSpec((tq, g * HEAD_DIM), lambda h, i: (i, h))
    s_spec = pl.BlockSpec((None, g * WA_BLOCK, 1), lambda h, i: (h, 0, 0))
    return pl.pallas_call(
        body, name="wa_bwd", grid=(hkv, nb // qb_step),
        in_specs=[q_spec, kv_spec, kv_spec, s_spec, q_spec],
        out_specs=[q_spec, kv_spec, kv_spec, s_spec],
        out_shape=[jax.ShapeDtypeStruct((S, W), F32), jax.ShapeDtypeStruct((S, KW), F32),
                   jax.ShapeDtypeStruct((S, KW), F32), jax.ShapeDtypeStruct(sink_col.shape, F32)],
        compiler_params=_params(24 * S * HEAD_DIM + (16 << 20), ("parallel", "arbitrary")),
    )(qb, kb, vb, sink_col, dob)


CONV_HALO = SUBLANES


def _conv_specs(tr, width, nblk, rows_inner):
    per = tr // CONV_HALO

    def spec(shape, row_block):
        if rows_inner:
            return pl.BlockSpec(shape, lambda j, i: (row_block(i), j))
        return pl.BlockSpec(shape, lambda i, j: (row_block(i), j))

    cur = spec((tr, width), lambda i: i)
    prev = spec((CONV_HALO, width), lambda i: jnp.maximum(i * per - 1, 0))
    nxt = spec((CONV_HALO, width), lambda i: jnp.minimum((i + 1) * per, nblk * per - 1))
    return prev, cur, nxt


def _extend(prev_ref, cur_ref, next_ref, i, nblk):
    p = jnp.where(i > 0, prev_ref[...].astype(F32), 0.0)
    n = jnp.where(i < nblk - 1, next_ref[...].astype(F32), 0.0)
    return jnp.concatenate([p, cur_ref[...].astype(F32), n], axis=0)


def _shift_down(x):
    return pltpu.roll(x, 1, axis=0)


def _shift_up(x):
    return pltpu.roll(x, x.shape[0] - 1, axis=0)


def _conv(ext, w_ref, b_ref):
    return _shift_down(ext) * w_ref[0:1, :] + ext * w_ref[1:2, :] + _shift_up(ext) * w_ref[2:3, :] + b_ref[...]


def _sigmoid(x):
    return 1.0 / (1.0 + jnp.exp(-x))


def _convgate_fwd(up_pre, cw, cb, tc):
    S, C2 = up_pre.shape
    tr = _tile(S, 512, 16)
    nblk = S // tr
    prev, cur, nxt = _conv_specs(tr, 2 * tc, nblk, False)

    def body(p_ref, c_ref, n_ref, w_ref, b_ref, a_ref):
        i = pl.program_id(0)
        u = _conv(_extend(p_ref, c_ref, n_ref, i, nblk), w_ref, b_ref)[CONV_HALO:CONV_HALO + tr]
        gate, up = u[:, :tc], u[:, tc:]
        a_ref[...] = (gate * _sigmoid(gate) * up).astype(BF16)

    return pl.pallas_call(
        body, name="convgate_fwd", grid=(nblk, C2 // (2 * tc)),
        in_specs=[prev, cur, nxt, pl.BlockSpec((3, 2 * tc), lambda i, j: (0, j)),
                  pl.BlockSpec((1, 2 * tc), lambda i, j: (0, j))],
        out_specs=pl.BlockSpec((tr, tc), lambda i, j: (i, j)),
        out_shape=jax.ShapeDtypeStruct((S, C2 // 2), BF16),
        compiler_params=_params(48 * tr * tc, ("parallel", "parallel")),
    )(up_pre, up_pre, up_pre, cw, cb)


def _convgate_bwd(up_pre, dact, cw, cb, tc):
    S, C2 = up_pre.shape
    tr = _tile(S, 512, 16)
    nblk = S // tr
    prev, cur, nxt = _conv_specs(tr, 2 * tc, nblk, True)
    dprev, dcur, dnxt = _conv_specs(tr, tc, nblk, True)

    def body(p_ref, c_ref, n_ref, dp_ref, dc_ref, dn_ref, w_ref, b_ref, dx_ref, dw_ref, db_ref):
        i = pl.program_id(1)
        ext = _extend(p_ref, c_ref, n_ref, i, nblk)
        da = _extend(dp_ref, dc_ref, dn_ref, i, nblk)
        ext_dn, ext_up = _shift_down(ext), _shift_up(ext)
        u = ext_dn * w_ref[0:1, :] + ext * w_ref[1:2, :] + ext_up * w_ref[2:3, :] + b_ref[...]
        gate, up = u[:, :tc], u[:, tc:]
        sg = _sigmoid(gate)
        silu = gate * sg
        du = jnp.concatenate([da * up * (sg + silu * (1.0 - sg)), da * silu], axis=1)
        dx = _shift_up(du) * w_ref[0:1, :] + du * w_ref[1:2, :] + _shift_down(du) * w_ref[2:3, :]
        mid = slice(CONV_HALO, CONV_HALO + tr)
        dx_ref[...] = dx[mid].astype(BF16)
        duc = du[mid]
        dw = jnp.concatenate([jnp.sum(duc * ext_dn[mid], axis=0, keepdims=True),
                              jnp.sum(duc * ext[mid], axis=0, keepdims=True),
                              jnp.sum(duc * ext_up[mid], axis=0, keepdims=True)], axis=0)
        db = jnp.sum(duc, axis=0, keepdims=True)

        @pl.when(i == 0)
        def _():
            dw_ref[...] = dw
            db_ref[...] = db

        @pl.when(i > 0)
        def _():
            dw_ref[...] += dw
            db_ref[...] += db

    return pl.pallas_call(
        body, name="convgate_bwd", grid=(C2 // (2 * tc), nblk),
        in_specs=[prev, cur, nxt, dprev, dcur, dnxt,
                  pl.BlockSpec((3, 2 * tc), lambda j, i: (0, j)), pl.BlockSpec((1, 2 * tc), lambda j, i: (0, j))],
        out_specs=[pl.BlockSpec((tr, 2 * tc), lambda j, i: (i, j)),
                   pl.BlockSpec((3, 2 * tc), lambda j, i: (0, j)), pl.BlockSpec((1, 2 * tc), lambda j, i: (0, j))],
        out_shape=[jax.ShapeDtypeStruct((S, C2), BF16), jax.ShapeDtypeStruct((3, C2), F32),
                   jax.ShapeDtypeStruct((1, C2), F32)],
        compiler_params=_params(160 * tr * tc, ("parallel", "arbitrary")),
    )(up_pre, up_pre, up_pre, dact, dact, dact, cw, cb)


def _rope_tables(positions):
    inv = ROPE_THETA ** (-jnp.arange(0, HEAD_DIM, 2, dtype=F32) / HEAD_DIM)
    ang = positions.astype(F32)[:, None] * inv[None, :]
    cos, sin = jnp.cos(ang), jnp.sin(ang)
    return jnp.concatenate([cos, cos], axis=1), jnp.concatenate([-sin, sin], axis=1)


def _sink_col(sink_l, hkv):
    g = sink_l.shape[0] // hkv
    return jnp.broadcast_to(sink_l.reshape(hkv, g, 1), (hkv, g, WA_BLOCK)).reshape(hkv, g * WA_BLOCK, 1)


def _sink_col_t(dcol):
    hkv, rows, _ = dcol.shape
    return jnp.sum(dcol.reshape(hkv, rows // WA_BLOCK, WA_BLOCK), axis=-1).reshape(-1)


def _to_paired(a, sh):
    return jnp.concatenate([a[:, sh.nat(t) * sh.tn:(sh.nat(t) + 1) * sh.tn] for t in range(sh.ntiles)], axis=1)


def _from_paired(a, sh):
    pos = {sh.nat(t): t for t in range(sh.ntiles)}
    return jnp.concatenate([a[:, pos[n] * sh.tn:(pos[n] + 1) * sh.tn] for n in range(sh.ntiles)], axis=1)


def _pack(arrs):
    flat = jnp.concatenate([a.reshape(-1).astype(F32) for a in arrs])
    unit = SUBLANES * LANES
    total = -(-flat.shape[0] // unit) * unit
    return jnp.pad(flat, (0, total - flat.shape[0])).reshape(-1, LANES)


def _unpack(buf, shapes):
    flat = buf.reshape(-1)
    out, off = [], 0
    for s in shapes:
        n = math.prod(s)
        out.append(flat[off:off + n].reshape(s))
        off += n
    return out


HBM_SPEC = pl.BlockSpec(memory_space=pltpu.HBM)
DMA_CHUNK_BYTES = 512 * 1024


def _row_chunks(rows, row_bytes, align):
    want = max(1, rows * row_bytes // DMA_CHUNK_BYTES)
    count = max(k for k in range(1, rows + 1) if rows % k == 0 and (rows // k) % align == 0 and (k <= want or k == 1))
    size = rows // count
    return [(j * size, size) for j in range(count)]


def _mesh_pos():
    return lax.axis_index("x"), lax.axis_index("y"), lax.axis_index("c")


def _other_chips(x, y):
    return [(1 - x, y), (x, 1 - y), (1 - x, 1 - y)]


def _remote(src, dst, send_sems, recv_sems, k, dev):
    return pltpu.make_async_remote_copy(src_ref=src, dst_ref=dst, send_sem=send_sems.at[k], recv_sem=recv_sems.at[k],
                                        device_id=dev, device_id_type=MESH)


class _Exchange:
    def __init__(self, operands, out_shapes, aliases, start, wait):
        self.operands, self.out_shapes, self.aliases = list(operands), list(out_shapes), dict(aliases)
        self.start, self.wait = start, wait
        self.n_sems = len(out_shapes)

    def scratch(self):
        return [pltpu.SemaphoreType.DMA((self.n_sems,)), pltpu.SemaphoreType.DMA((self.n_sems,))]


class _SemSlice:
    def __init__(self, sems, offset):
        self._sems, self._offset = sems, offset

    @property
    def at(self):
        return self

    def __getitem__(self, k):
        return self._sems.at[self._offset + k]


def _both(a, b):
    ia, oa = len(a.operands), a.n_sems

    def split(ins, outs, send_sems, recv_sems):
        return ((ins[:ia], outs[:oa], send_sems, recv_sems),
                (ins[ia:], outs[oa:], _SemSlice(send_sems, oa), _SemSlice(recv_sems, oa)))

    def start(*refs):
        first, second = split(*refs)
        a.start(*first)
        b.start(*second)

    def wait(*refs):
        first, second = split(*refs)
        a.wait(*first)
        b.wait(*second)

    aliases = dict(a.aliases)
    aliases.update({ia + i: oa + o for i, o in b.aliases.items()})
    return _Exchange(a.operands + b.operands, a.out_shapes + b.out_shapes, aliases, start, wait)


def _exchange_alone(name, ex):
    n_in = len(ex.operands)

    def body(*refs):
        ins, outs = refs[:n_in], refs[n_in:n_in + ex.n_sems]
        send_sems, recv_sems = refs[n_in + ex.n_sems:]
        ex.start(ins, outs, send_sems, recv_sems)
        ex.wait(ins, outs, send_sems, recv_sems)

    return pl.pallas_call(
        body, name=name, in_specs=[HBM_SPEC] * n_in, out_specs=[HBM_SPEC] * ex.n_sems, out_shape=ex.out_shapes,
        input_output_aliases=ex.aliases, scratch_shapes=ex.scratch(),
    )(*ex.operands)


def _ag_chip_exchange(gathered, small):
    n = len(gathered)
    arrays = list(gathered) + ([] if small is None else [small])

    def start(ins, outs, send_sems, recv_sems):
        x, y, c = _mesh_pos()
        q_me = 2 * x + y
        chips = _other_chips(x, y)
        for t in range(n):
            h = gathered[t].shape[1] // 2
            for s0, sz in _row_chunks(h, gathered[t].shape[2] * 2, 16):
                blk = outs[t].at[q_me, pl.ds(c * h + s0, sz)]
                for chip in chips:
                    _remote(blk, blk, send_sems, recv_sems, t, (*chip, c)).start()
        if small is not None:
            for chip in chips:
                _remote(outs[n].at[q_me], outs[n].at[q_me], send_sems, recv_sems, n, (*chip, c)).start()

    def wait(ins, outs, send_sems, recv_sems):
        x, y, c = _mesh_pos()
        for t in range(n):
            three = outs[t].at[pl.ds(0, 3), pl.ds(0, gathered[t].shape[1] // 2)]
            _remote(three, three, send_sems, recv_sems, t, (x, y, 1 - c)).wait()
        if small is not None:
            three = outs[n].at[pl.ds(0, 3)]
            _remote(three, three, send_sems, recv_sems, n, (x, y, 1 - c)).wait()

    return _Exchange(arrays, [jax.ShapeDtypeStruct(a.shape, a.dtype) for a in arrays],
                     {i: i for i in range(len(arrays))}, start, wait)


def _ag_sibling_pass(gathered):
    n = len(gathered)

    def start(ins, outs, send_sems, recv_sems):
        x, y, c = _mesh_pos()
        for t in range(n):
            h = gathered[t].shape[1] // 2
            for s0, sz in _row_chunks(h, gathered[t].shape[2] * 2, 16):
                for cx, cy in _other_chips(x, y):
                    blk = outs[t].at[2 * cx + cy, pl.ds(c * h + s0, sz)]
                    _remote(blk, blk, send_sems, recv_sems, t, (x, y, 1 - c)).start()

    def wait(ins, outs, send_sems, recv_sems):
        x, y, c = _mesh_pos()
        for t in range(n):
            three = outs[t].at[pl.ds(0, 3), pl.ds(0, gathered[t].shape[1] // 2)]
            _remote(three, three, send_sems, recv_sems, t, (x, y, 1 - c)).wait()

    return _Exchange(gathered, [jax.ShapeDtypeStruct(a.shape, a.dtype) for a in gathered],
                     {i: i for i in range(n)}, start, wait)


def _rs_sibling_exchange(grads):
    n = len(grads)
    halves = [g.shape[1] // 2 for g in grads]

    def start(ins, outs, send_sems, recv_sems):
        x, y, c = _mesh_pos()
        for t in range(n):
            for q in range(N_CHIPS):
                for s0, sz in _row_chunks(halves[t], grads[t].shape[2] * 4, SUBLANES):
                    _remote(ins[t].at[q, pl.ds((1 - c) * halves[t] + s0, sz)], outs[t].at[q, pl.ds(s0, sz)],
                            send_sems, recv_sems, t, (x, y, 1 - c)).start()

    def wait(ins, outs, send_sems, recv_sems):
        x, y, c = _mesh_pos()
        for t in range(n):
            _remote(outs[t], outs[t], send_sems, recv_sems, t, (x, y, 1 - c)).wait()

    return _Exchange(grads, [jax.ShapeDtypeStruct((N_CHIPS, h, g.shape[2]), F32) for g, h in zip(grads, halves)],
                     {}, start, wait)


def _rs_chip_exchange(pbs):
    n = len(pbs)

    def start(ins, outs, send_sems, recv_sems):
        x, y, c = _mesh_pos()
        for t in range(n):
            for s0, sz in _row_chunks(pbs[t].shape[1], pbs[t].shape[2] * 2, 16):
                for k, (cx, cy) in enumerate(_other_chips(x, y)):
                    _remote(ins[t].at[2 * cx + cy, pl.ds(s0, sz)], outs[t].at[k, pl.ds(s0, sz)],
                            send_sems, recv_sems, t, (cx, cy, c)).start()

    def wait(ins, outs, send_sems, recv_sems):
        x, y, c = _mesh_pos()
        for t in range(n):
            _remote(outs[t], outs[t], send_sems, recv_sems, t, (x, y, 1 - c)).wait()

    return _Exchange(pbs, [jax.ShapeDtypeStruct((3,) + p.shape[1:], BF16) for p in pbs], {}, start, wait)


def _rs_sibling_share(gs):
    n = len(gs)

    def start(ins, outs, send_sems, recv_sems):
        x, y, c = _mesh_pos()
        for t in range(n):
            h = gs[t].shape[0] // 2
            for s0, sz in _row_chunks(h, gs[t].shape[1] * 4, SUBLANES):
                rows = outs[t].at[pl.ds(c * h + s0, sz)]
                _remote(rows, rows, send_sems, recv_sems, t, (x, y, 1 - c)).start()

    def wait(ins, outs, send_sems, recv_sems):
        x, y, c = _mesh_pos()
        for t in range(n):
            half = outs[t].at[pl.ds(0, gs[t].shape[0] // 2)]
            _remote(half, half, send_sems, recv_sems, t, (x, y, 1 - c)).wait()

    return _Exchange(gs, [jax.ShapeDtypeStruct(g.shape, F32) for g in gs], {i: i for i in range(n)}, start, wait)


def _gather_small(buf):
    def body(in_ref, out_ref, send_sems, recv_sems, loc_sem):
        x, y, c = _mesh_pos()
        me = 4 * x + 2 * y + c
        local = pltpu.make_async_copy(in_ref, out_ref.at[me], loc_sem)
        local.start()
        sends, peers = [], []
        for k in range(1, N_DEV):
            bx, by, bc = (k >> 2) & 1, (k >> 1) & 1, k & 1
            peer = (x + bx - 2 * x * bx, y + by - 2 * y * by, c + bc - 2 * c * bc)
            peers.append(peer)
            sends.append(_remote(in_ref, out_ref.at[me], send_sems, recv_sems, k - 1, peer))
        for cp in sends:
            cp.start()
        for k, (px, py, pc) in enumerate(peers):
            slot = out_ref.at[4 * px + 2 * py + pc]
            _remote(slot, slot, send_sems, recv_sems, k, (px, py, pc)).wait_recv()
        for cp in sends:
            cp.wait_send()
        local.wait()

    return pl.pallas_call(
        body, name="gather_small", in_specs=[HBM_SPEC], out_specs=HBM_SPEC,
        out_shape=jax.ShapeDtypeStruct((N_DEV,) + buf.shape, F32),
        scratch_shapes=[pltpu.SemaphoreType.DMA((N_DEV - 1,)), pltpu.SemaphoreType.DMA((N_DEV - 1,)),
                        pltpu.SemaphoreType.DMA(())],
    )(buf)


def _ew_rows(rows, cols, bytes_per_elem):
    budget = 20 * 1024 * 1024
    return _tile(rows, max(16, budget // (2 * bytes_per_elem * cols) // 16 * 16), 16)


def _sum_pair(grad, recv):
    _, h, cols = recv.shape
    tr = _ew_rows(h, cols, 14)
    nb = h // tr
    half = pl.BlockSpec((None, tr, cols), lambda i, q: (q, i, 0))

    def body(a_ref, b_ref, f_ref, h_ref):
        s = a_ref[...] + b_ref[...]
        h_ref[...] = s.astype(BF16)

        @pl.when(pl.program_id(1) == 2 * lax.axis_index("x") + lax.axis_index("y"))
        def _():
            f_ref[...] = s

    return pl.pallas_call(
        body, name="rs_sum_pair", grid=(nb, N_CHIPS),
        in_specs=[pl.BlockSpec((None, tr, cols), lambda i, q: (q, lax.axis_index("c") * nb + i, 0)), half],
        out_specs=[pl.BlockSpec((tr, cols), lambda i, q: (i, 0)), half],
        out_shape=[jax.ShapeDtypeStruct((h, cols), F32), jax.ShapeDtypeStruct(recv.shape, BF16)],
        compiler_params=_params(28 * tr * cols, ("parallel", "arbitrary")),
    )(grad, recv)


def _sum_four(pf, recv):
    h, cols = pf.shape
    tr = _ew_rows(h, cols, 14)
    nb = h // tr

    def body(a_ref, r_ref, o_ref):
        o_ref[...] = ((a_ref[...] + r_ref[0].astype(F32)) + r_ref[1].astype(F32)) + r_ref[2].astype(F32)

    return pl.pallas_call(
        body, name="rs_sum_four", grid=(nb,),
        in_specs=[pl.BlockSpec((tr, cols), lambda i: (i, 0)),
                  pl.BlockSpec((3, tr, cols), lambda i: (0, i, 0))],
        out_specs=pl.BlockSpec((tr, cols), lambda i: (lax.axis_index("c") * nb + i, 0)),
        out_shape=jax.ShapeDtypeStruct((2 * h, cols), F32),
        compiler_params=_params(28 * tr * cols, ("parallel",)),
    )(pf, recv)


def _sum_eight(gathered):
    _, rows, cols = gathered.shape
    tr = _tile(rows, 512, SUBLANES)

    def body(g_ref, o_ref):
        s = g_ref[0]
        for d in range(1, N_DEV):
            s = s + g_ref[d]
        o_ref[...] = s

    return pl.pallas_call(
        body, name="sum_eight", grid=(rows // tr,),
        in_specs=[pl.BlockSpec((N_DEV, tr, cols), lambda i: (0, i, 0))],
        out_specs=_row_spec(tr, cols), out_shape=jax.ShapeDtypeStruct((rows, cols), F32),
        compiler_params=_params(None, ("parallel",)),
    )(gathered)


def _cast_layer(w, l):
    _, rows, cols = w.shape
    tr = _ew_rows(rows, cols, 6)

    def body(w_ref, o_ref):
        o_ref[...] = w_ref[...].astype(BF16)

    return pl.pallas_call(
        body, name="cast_bf16", grid=(rows // tr,),
        in_specs=[pl.BlockSpec((None, tr, cols), lambda i: (l, i, 0))],
        out_specs=pl.BlockSpec((None, tr, cols), lambda i: (2 * lax.axis_index("x") + lax.axis_index("y"), i, 0)),
        out_shape=jax.ShapeDtypeStruct((N_CHIPS, rows, cols), BF16),
        compiler_params=_params(12 * tr * cols, ("parallel",)),
    )(w)


def _adamw(g, w, m, v, l, prev):
    L, rows, cols = w.shape
    tr = _ew_rows(rows, cols, 32)
    layer = pl.BlockSpec((None, tr, cols), lambda i: (l, i, 0))

    def body(g_ref, w_ref, m_ref, v_ref, *rest):
        og_ref, od_ref, om_ref, ov_ref = rest[-4:]
        gv = g_ref[...]
        mn = ADAM_B1 * m_ref[...] + (1.0 - ADAM_B1) * gv
        vn = ADAM_B2 * v_ref[...] + (1.0 - ADAM_B2) * (gv * gv)
        m_hat = mn / ADAM_C1
        v_hat = vn / ADAM_C2
        od_ref[...] = -ADAM_LR * (m_hat / (jnp.sqrt(v_hat) + ADAM_EPS) + ADAM_WD * w_ref[...])
        og_ref[...] = gv
        om_ref[...] = mn
        ov_ref[...] = vn

    in_specs = [_row_spec(tr, cols), layer, layer, layer]
    args = [g, w, m, v]
    aliases = {}
    if prev is not None:
        in_specs += [HBM_SPEC] * 4
        args += list(prev)
        aliases = {4 + i: i for i in range(4)}
    return pl.pallas_call(
        body, name="adamw", grid=(rows // tr,), in_specs=in_specs, out_specs=[layer] * 4,
        out_shape=[jax.ShapeDtypeStruct((L, rows, cols), F32)] * 4, input_output_aliases=aliases,
        compiler_params=_params(64 * tr * cols, ("parallel",)),
    )(*args)


WEIGHT_ORDER = ("ln1_g", "w_in", "qn_a", "kn_a", "rpb", "qn_b", "kn_b", "sink", "on_a", "on_b", "w_out", "ln2_g",
                "w_up", "conv_w", "conv_b", "w_down")
BIG = ("w_in", "w_out", "w_up", "w_down")
SMALL = tuple(n for n in WEIGHT_ORDER if n not in BIG)


def _train_step(x, positions, target, w, m, v):
    S, D = x.shape
    L = w["w_in"].shape[0]
    naw = w["rpb"].shape[1] * HEAD_DIM
    waw = w["sink"].shape[1] * HEAD_DIM
    cin = w["w_in"].shape[2]
    kvw = (N_CHIPS * cin - 3 * naw - waw) // 2
    hkv = kvw // HEAD_DIM
    dims = (naw, waw, kvw)
    assert naw == waw, "the two head groups are normalised by one kernel and must be equally wide"
    cup = w["w_up"].shape[2]
    c2 = N_CHIPS * cup
    dff = c2 // 2
    mix = naw + waw
    sh_in = _ColShards(cin, _tile(cin, 1152, LANES), False)
    sh_up = _ColShards(cup, cup // 2 if (cup // 2) % LANES == 0 else cup, True)
    tc = sh_up.tn
    cos2, sgn_sin = _rope_tables(positions)
    q_me = 2 * lax.axis_index("x") + lax.axis_index("y")
    row = lambda a: a[None]

    def own_slots(l):
        small = lax.dynamic_update_slice_in_dim(jnp.zeros((N_CHIPS,) + w["conv_w"].shape[1:], F32),
                                                w["conv_w"][l][None], q_me, axis=0)
        return [_cast_layer(w[n], l) for n in BIG], small

    gath = {}

    def prepare(l):
        bufs, small = own_slots(l)
        gath.update({(l, n): b for n, b in zip(BIG, bufs)})
        gath[l, "conv_w"] = small

    def over_ici(l, names, small=False):
        keys = [(l, n) for n in names]
        ex = _ag_chip_exchange([gath[k] for k in keys], gath[l, "conv_w"] if small else None)
        return ex, keys + ([(l, "conv_w")] if small else [])

    def to_sibling(l, names):
        keys = [(l, n) for n in names]
        return _ag_sibling_pass([gath[k] for k in keys]), keys

    def together(parts):
        ex, keys = parts[0]
        for e, k in parts[1:]:
            ex, keys = _both(ex, e), keys + k
        return ex, keys

    def carried(call, parts):
        if not parts:
            return call(None)
        ex, keys = together(parts)
        out, results = call(ex)
        gath.update(dict(zip(keys, results)))
        return out

    prepare(0)
    ex, keys = over_ici(0, ("w_in", "w_out"), True)
    gath.update(dict(zip(keys, _exchange_alone("ag_first_over_ici", ex))))
    ex, keys = to_sibling(0, ("w_in", "w_out"))
    gath.update(dict(zip(keys, _exchange_alone("ag_first_to_sibling", ex))))
    saved = []
    for l in range(L):
        more = l + 1 < L
        if more:
            prepare(l + 1)
        g_cw = gath[l, "conv_w"]
        cw_p = _to_paired(jnp.transpose(g_cw, (1, 0, 2)).reshape(3, c2), sh_up)
        cb_p = _to_paired(row(w["conv_b"][l]), sh_up)
        gains = jnp.concatenate([row(w["qn_a"][l]), row(w["kn_a"][l]), row(w["qn_b"][l]), row(w["kn_b"][l]),
                                 jnp.zeros((SUBLANES - 4, HEAD_DIM), F32)], axis=0)
        btab = _na_bias_table(w["rpb"][l])
        sink_col = _sink_col(w["sink"][l], hkv)

        h = _rms_fwd(x, row(w["ln1_g"][l]))
        g_in = gath[l, "w_in"]
        proj = carried(lambda ex: _mm_cols_fwd("mm_proj", h, g_in, sh_in, F32, tm=1024, carry=ex),
                       [over_ici(0, ("w_up",))] if l == 0 else [])
        qa, ka, va, qb, kb, vb = _qk_prep_fwd(proj, cos2, sgn_sin, gains, dims)
        oa = _na_fwd(qa, ka, va, btab)
        ob = _wa_fwd(qb, kb, vb, sink_col)
        o = _out_norm_fwd(oa, ob, row(w["on_a"][l]), row(w["on_b"][l]))
        g_out = gath[l, "w_out"].reshape(mix, D)
        x1 = carried(lambda ex: _mm_plain("mm_attn_out", o, g_out, NN, F32, tm=1024, tn=1024, tk=2048, res=x, carry=ex),
                     [to_sibling(0, ("w_up",)), over_ici(0, ("w_down",))] if l == 0 else [])
        h2 = _rms_fwd(x1, row(w["ln2_g"][l]))
        g_up = gath[l, "w_up"]
        up_pre = carried(lambda ex: _mm_cols_fwd("mm_up", h2, g_up, sh_up, F32, tm=1024, carry=ex),
                         [to_sibling(l, ("w_down",))]
                         + ([over_ici(l + 1, ("w_in", "w_out", "w_up"), True)] if more else []))
        act = _convgate_fwd(up_pre, cw_p, cb_p, tc)
        g_dn = gath[l, "w_down"].reshape(dff, D)
        x2 = carried(lambda ex: _mm_plain("mm_down", act, g_dn, NN, F32, tm=1024, tn=512, tk=5632, res=x1, carry=ex),
                     [to_sibling(l + 1, ("w_in", "w_out", "w_up")), over_ici(l + 1, ("w_down",))] if more else [])
        saved.append(dict(g_in=g_in, g_out=g_out, g_up=g_up, g_dn=g_dn, cw_p=cw_p, cb_p=cb_p, gains=gains, btab=btab,
                          sink_col=sink_col, x=x, h=h, proj=proj, qa=qa, ka=ka, va=va, qb=qb, kb=kb, vb=vb, oa=oa,
                          ob=ob, o=o, x1=x1, h2=h2, up_pre=up_pre, act=act))
        x = x2

    dx, dxb, loss_cols = _loss_grad(x, target)

    small_grads = [None] * L
    stacked = {n: None for n in BIG}

    def update(layer, shards):
        for n, g in zip(BIG, shards):
            stacked[n] = _adamw(g, w[n], m[n], v[n], layer, stacked[n])

    above = None
    for l in reversed(range(L)):
        s = saved[l]
        if above is None:
            dact = _mm_plain("mm_down_dx", dxb, s["g_dn"], NT, F32, tm=1024, tn=1408, tk=2048)
        else:
            dact, recvs = _mm_plain("mm_down_dx_rs", dxb, s["g_dn"], NT, F32, tm=1024, tn=1408, tk=2048,
                                    carry=_rs_sibling_exchange(above[1]))
            pairs = [_sum_pair(g, r) for g, r in zip(above[1], recvs)]
        dup_pre, dcw_p, dcb_p = _convgate_bwd(s["up_pre"], dact, s["cw_p"], s["cb_p"], tc)
        if above is None:
            dh2 = _mm_cols_bwd_x("mm_up_dx", dup_pre, s["g_up"], sh_up, F32, tm=1024, tn=2048)
        else:
            dh2, gots = _mm_cols_bwd_x("mm_up_dx_rs", dup_pre, s["g_up"], sh_up, F32, tm=1024, tn=2048,
                                       carry=_rs_chip_exchange([p[1] for p in pairs]))
            halves = [_sum_four(p[0], r) for p, r in zip(pairs, gots)]
        d_dn, (dx1, dx1b, d_ln2) = _mm_plain(
            "mm_down_dw", s["act"], dxb, TN, F32, tm=1408, tn=512, tk=2048,
            rider=lambda steps: _rms_bwd(dh2, s["x1"], row(w["ln2_g"][l]), dx, steps))
        d_dn = d_dn.reshape(N_CHIPS, dff // N_CHIPS, D)
        if above is None:
            do = _mm_plain("mm_attn_out_dx", dx1b, s["g_out"], NT, F32, tm=1024, tn=1024, tk=2048)
        else:
            do, shards = _mm_plain("mm_attn_out_dx_rs", dx1b, s["g_out"], NT, F32, tm=1024, tn=1024, tk=2048,
                                   carry=_rs_sibling_share(halves))
            update(above[0], shards)
        last = l == 0
        d_out, (doa, dob, d_on_a, d_on_b), *rest = _mm_plain(
            "mm_attn_out_dw", s["o"], dx1b, TN, F32, tm=1024, tn=512, tk=2048,
            carry=_rs_sibling_exchange([d_dn]) if last else None,
            rider=lambda steps: _out_norm_bwd(do, s["oa"], s["ob"], row(w["on_a"][l]), row(w["on_b"][l]), steps))
        d_out = d_out.reshape(N_CHIPS, mix // N_CHIPS, D)
        if last:
            pair_dn = _sum_pair(d_dn, rest[0][0])
        dqa, dka, dva, dbtab = _na_bwd(s["qa"], s["ka"], s["va"], s["btab"], doa)
        dqb, dkb, dvb, dsink_col = _wa_bwd(s["qb"], s["kb"], s["vb"], s["sink_col"], dob)
        d_up, (dproj, dgains) = _mm_cols_bwd_w(
            "mm_up_dw", s["h2"], dup_pre, sh_up, tm=1024, tk=2048,
            rider=lambda steps: _qk_prep_bwd(s["proj"], dqa, dka, dva, dqb, dkb, dvb, cos2, sgn_sin, s["gains"], dims, steps))
        if last:
            dh, (recv_up, got_dn) = _mm_cols_bwd_x(
                "mm_proj_dx", dproj, s["g_in"], sh_in, F32, tm=1024, tn=2048,
                carry=_both(_rs_sibling_exchange([d_up]), _rs_chip_exchange([pair_dn[1]])))
            pair_up = _sum_pair(d_up, recv_up)
        else:
            dh = _mm_cols_bwd_x("mm_proj_dx", dproj, s["g_in"], sh_in, F32, tm=1024, tn=2048)
        d_in, (dx, dxb, d_ln1), *rest = _mm_cols_bwd_w(
            "mm_proj_dw", s["h"], dproj, sh_in, tm=512, tk=2048,
            carry=_rs_chip_exchange([pair_up[1]]) if last else None,
            rider=lambda steps: _rms_bwd(dh, s["x"], row(w["ln1_g"][l]), dx1, steps))
        if last:
            got_up = rest[0][0]

        small_grads[l] = dict(
            ln1_g=d_ln1[0], qn_a=dgains[0], kn_a=dgains[1], rpb=_na_bias_table_t(dbtab), qn_b=dgains[2],
            kn_b=dgains[3], sink=_sink_col_t(dsink_col), on_a=d_on_a[0], on_b=d_on_b[0], ln2_g=d_ln2[0],
            conv_w=_from_paired(dcw_p, sh_up), conv_b=_from_paired(dcb_p, sh_up)[0])
        above = (l, [d_in, d_out, d_up, d_dn])
    recvs = _exchange_alone("rs_sibling_exchange", _rs_sibling_exchange([d_in, d_out]))
    pair_in, pair_out = _sum_pair(d_in, recvs[0]), _sum_pair(d_out, recvs[1])
    got_in, got_out = _exchange_alone("rs_chip_exchange", _rs_chip_exchange([pair_in[1], pair_out[1]]))
    halves = [_sum_four(p[0], g) for p, g in ((pair_in, got_in), (pair_out, got_out), (pair_up, got_up), (pair_dn, got_dn))]
    update(0, _exchange_alone("rs_sibling_share", _rs_sibling_share(halves)))
    grad_x = dx[None]

    full_shapes = {n: (w[n].shape[1:] if n != "conv_w" else (3, c2)) for n in SMALL}
    packed = _pack([loss_cols] + [small_grads[l][n] for l in range(L) for n in SMALL])
    total = _sum_eight(_gather_small(packed))
    parts = _unpack(total, [(D,)] + [full_shapes[n] for _ in range(L) for n in SMALL])
    loss = 0.5 * jnp.sum(parts[0]) / D
    per_layer = [dict(zip(SMALL, parts[1 + l * len(SMALL):1 + (l + 1) * len(SMALL)])) for l in range(L)]
    for l in range(L):
        per_layer[l]["conv_w"] = lax.dynamic_slice_in_dim(per_layer[l]["conv_w"], q_me * cup, cup, axis=1)
    small_g = {n: jnp.stack([per_layer[l][n] for l in range(L)]) for n in SMALL}
    outs = _adamw(_pack([small_g[n] for n in SMALL]), _pack([w[n] for n in SMALL])[None],
                  _pack([m[n] for n in SMALL])[None], _pack([v[n] for n in SMALL])[None], 0, None)
    small_out = [dict(zip(SMALL, _unpack(o[0], [w[n].shape for n in SMALL]))) for o in outs]

    result = [loss, grad_x]
    for i in range(4):
        result += [stacked[n][i] if n in BIG else small_out[i][n] for n in WEIGHT_ORDER]
    return tuple(result)


def kernel(x, positions, ln1_g, w_in, qn_a, kn_a, rpb, qn_b, kn_b, sink, on_a, on_b, w_out, ln2_g, w_up, conv_w, conv_b, w_down, loss_target, m_ln1_g, m_w_in, m_qn_a, m_kn_a, m_rpb, m_qn_b, m_kn_b, m_sink, m_on_a, m_on_b, m_w_out, m_ln2_g, m_w_up, m_conv_w, m_conv_b, m_w_down, v_ln1_g, v_w_in, v_qn_a, v_kn_a, v_rpb, v_qn_b, v_kn_b, v_sink, v_on_a, v_on_b, v_w_out, v_ln2_g, v_w_up, v_conv_w, v_conv_b, v_w_down):
    given = dict(locals())
    w = {n: given[n] for n in WEIGHT_ORDER}
    m = {n: given["m_" + n] for n in WEIGHT_ORDER}
    v = {n: given["v_" + n] for n in WEIGHT_ORDER}
    return _train_step(x[0], positions, loss_target[0], w, m, v)
```

```python
import functools
import math

import jax
import jax.numpy as jnp
import numpy as np
from jax import lax
from jax.experimental import pallas as pl
from jax.experimental.pallas import tpu as pltpu

F32 = jnp.float32
BF16 = jnp.bfloat16

HEAD_DIM = 128
GRID_W = 64
NA_WIN_R = 8
NA_WIN_C = 16
WA_WINDOW = 128
WA_BLOCK = 128
ROPE_THETA = 10000.0
EPS = 1e-6
NEG = -1e30
ATTN_SCALE = 1.0 / math.sqrt(HEAD_DIM)

ADAM_LR = 0.001
ADAM_B1 = 0.9
ADAM_B2 = 0.999
ADAM_EPS = 1e-08
ADAM_WD = 0.01
ADAM_STEP = 10
ADAM_C1 = 1.0 - ADAM_B1 ** ADAM_STEP
ADAM_C2 = 1.0 - ADAM_B2 ** ADAM_STEP

V7X_VMEM_BYTES = 64 * 1024 * 1024
V7X_VMEM_CAP = V7X_VMEM_BYTES - 6 * 1024 * 1024
LANES = 128
SUBLANES = 8
N_CHIPS = 4
N_DEV = 8
MESH = pl.DeviceIdType.MESH

NN = ((1,), (0,))
NT = ((1,), (1,))
TN = ((0,), (0,))


def _tile(n, pref, mult):
    best = None
    d = mult
    while d <= min(n, pref):
        if n % d == 0:
            best = d
        d += mult
    return n if best is None else best


def _nbytes(shape, dtype):
    n = 1
    for s in shape:
        if s is not None:
            n *= s
    return n * jnp.dtype(dtype).itemsize


def _params(est_bytes=None, sem=None, **kw):
    if est_bytes is not None:
        kw["vmem_limit_bytes"] = int(min(V7X_VMEM_CAP, max(32 * 1024 * 1024, est_bytes * 5 // 4 + (4 << 20))))
    if sem is not None:
        kw["dimension_semantics"] = sem
    return pltpu.CompilerParams(**kw)


def _dot(a, b, contract):
    return lax.dot_general(a.astype(BF16), b.astype(BF16), (contract, ((), ())), preferred_element_type=F32)


def _mm(name, a, b, *, grid, a_spec, b_spec, o_spec, out_shape, contract, res=None, res_spec=None, carry=None,
        rider=None, ring_a=False):
    nk = grid[2]
    n_steps = grid[0] * grid[1] * grid[2]
    acc_shape = tuple(s for s in o_spec.block_shape if s is not None)
    n_in = 2 if res is None else 3
    r_in = 0 if rider is None else len(rider.operands)
    r_out = 0 if rider is None else len(rider.out_shapes)
    x_in = 0 if carry is None else len(carry.operands)
    x_out = 0 if carry is None else carry.n_sems

    def body(*refs):
        a_ref, b_ref = refs[:2]
        r_ref = None if res is None else refs[2]
        pos = n_in
        r_ins, pos = refs[pos:pos + r_in], pos + r_in
        x_ins, pos = refs[pos:pos + x_in], pos + x_in
        o_ref, pos = refs[pos], pos + 1
        r_outs, pos = refs[pos:pos + r_out], pos + r_out
        x_outs, pos = refs[pos:pos + x_out], pos + x_out
        scr = refs[pos:]
        step = (pl.program_id(0) * grid[1] + pl.program_id(1)) * grid[2] + pl.program_id(2)
        if carry is not None:
            sems, scr = scr[-2:], scr[:-2]

            @pl.when(step == 0)
            def _():
                carry.start(x_ins, x_outs, *sems)

        if not ring_a:
            a_val = a_ref[...]
        else:
            ring, ring_sems = scr[-2:]
            scr = scr[:-2]

            def fetch(t):
                bi, bj = a_spec.index_map(t // (grid[1] * grid[2]), (t // grid[2]) % grid[1], t % grid[2])
                rows, cols = a_spec.block_shape
                return pltpu.make_async_copy(a_ref.at[pl.ds(bi * rows, rows), pl.ds(bj * cols, cols)],
                                             ring.at[t % 3], ring_sems.at[t % 3])

            @pl.when(step == 0)
            def _():
                fetch(step).start()
                if n_steps > 1:
                    fetch(step + 1).start()

            @pl.when(step + 2 < n_steps)
            def _():
                fetch(step + 2).start()

            fetch(step).wait()
            a_val = ring[step % 3]
        p = _dot(a_val, b_ref[...], contract)
        if rider is not None:
            rider.body(step, r_ins, r_outs)

        def finish(acc):
            if r_ref is not None:
                acc = acc + r_ref[...]
            o_ref[...] = acc.astype(o_ref.dtype)

        if nk == 1:
            finish(p)
        else:
            acc_ref = scr[0]
            k = pl.program_id(2)

            @pl.when(k == 0)
            def _():
                acc_ref[...] = p

            @pl.when(jnp.logical_and(k > 0, k < nk - 1))
            def _():
                acc_ref[...] += p

            @pl.when(k == nk - 1)
            def _():
                finish(acc_ref[...] + p)

        if carry is not None:
            @pl.when(step == n_steps - 1)
            def _():
                carry.wait(x_ins, x_outs, *sems)

    in_specs = [a_spec, b_spec]
    args = [a, b]
    est = 2 * (_nbytes(a_spec.block_shape, a.dtype) + _nbytes(b_spec.block_shape, b.dtype)
               + _nbytes(o_spec.block_shape, out_shape.dtype)) + 2 * _nbytes(acc_shape, F32)
    if res is not None:
        in_specs.append(res_spec)
        args.append(res)
        est += 2 * _nbytes(res_spec.block_shape, res.dtype)
    scratch = [] if nk == 1 else [pltpu.VMEM(acc_shape, F32)]
    if carry is None and rider is None:
        return pl.pallas_call(
            body, name=name, grid=grid, in_specs=in_specs, out_specs=o_spec, out_shape=out_shape,
            scratch_shapes=scratch,
            compiler_params=_params(est, ("parallel", "parallel", "arbitrary")),
        )(*args)

    def by_step(spec):
        return pl.BlockSpec(spec.block_shape, lambda i, j, k: spec.index_map((i * grid[1] + j) * grid[2] + k))

    out_specs, out_shapes, aliases = [o_spec], [out_shape], {}
    if ring_a:
        in_specs[0] = HBM_SPEC
        scratch += [pltpu.VMEM((3,) + tuple(a_spec.block_shape), a.dtype), pltpu.SemaphoreType.DMA((3,))]
        est += _nbytes(a_spec.block_shape, a.dtype)
    if rider is not None:
        in_specs += [by_step(s) for s in rider.in_specs]
        args += rider.operands
        out_specs += [by_step(s) for s in rider.out_specs]
        out_shapes += rider.out_shapes
        est += rider.vmem_bytes
    if carry is not None:
        aliases = {len(args) + i: len(out_shapes) + o for i, o in carry.aliases.items()}
        in_specs += [HBM_SPEC] * x_in
        args += carry.operands
        out_specs += [HBM_SPEC] * x_out
        out_shapes += carry.out_shapes
        scratch += carry.scratch()
    outs = pl.pallas_call(
        body, name=name, grid=grid, in_specs=in_specs, out_specs=out_specs, out_shape=out_shapes,
        input_output_aliases=aliases, scratch_shapes=scratch,
        compiler_params=_params(est, ("arbitrary", "arbitrary", "arbitrary")),
    )(*args)
    results = [outs[0]]
    if rider is not None:
        results.append(list(outs[1:1 + r_out]))
    if carry is not None:
        results.append(list(outs[1 + r_out:]))
    return tuple(results)


def _mm_plain(name, a, b, contract, out_dtype, *, tm, tn, tk, res=None, carry=None, rider=None, ring_a=False):
    if contract == NN:
        (M, K), N = a.shape, b.shape[1]
    elif contract == NT:
        (M, K), N = a.shape, b.shape[0]
    else:
        (K, M), N = a.shape, b.shape[1]
    tm, tn, tk = _tile(M, tm, LANES), _tile(N, tn, LANES), _tile(K, tk, LANES)
    grid = (M // tm, N // tn, K // tk)
    if contract == TN:
        a_spec = pl.BlockSpec((tk, tm), lambda i, j, k: (k, i))
    else:
        a_spec = pl.BlockSpec((tm, tk), lambda i, j, k: (i, k))
    if contract == NT:
        b_spec = pl.BlockSpec((tn, tk), lambda i, j, k: (j, k))
    else:
        b_spec = pl.BlockSpec((tk, tn), lambda i, j, k: (k, j))
    o_spec = pl.BlockSpec((tm, tn), lambda i, j, k: (i, j))
    return _mm(name, a, b, grid=grid, a_spec=a_spec, b_spec=b_spec, o_spec=o_spec,
               out_shape=jax.ShapeDtypeStruct((M, N), out_dtype), contract=contract,
               res=res, res_spec=None if res is None else pl.BlockSpec((tm, tn), lambda i, j, k: (i, j)), carry=carry,
               rider=None if rider is None else rider(math.prod(grid)), ring_a=ring_a)


class _ColShards:
    def __init__(self, cols_per_chip, tn, paired):
        self.c = cols_per_chip
        self.tn = tn
        self.tps = cols_per_chip // tn
        self.ntiles = N_CHIPS * self.tps
        self.paired = paired

    def nat(self, t):
        if not self.paired:
            return t
        return (t % 2) * (self.ntiles // 2) + t // 2

    def chip(self, t):
        return self.nat(t) // self.tps

    def within(self, t):
        return self.nat(t) % self.tps


def _mm_cols_fwd(name, a, wg, sh, out_dtype, *, tm, carry=None):
    S, K = a.shape
    tm = _tile(S, tm, LANES)
    grid = (S // tm, sh.ntiles, 1)
    return _mm(name, a, wg, grid=grid,
               a_spec=pl.BlockSpec((tm, K), lambda i, j, k: (i, 0)),
               b_spec=pl.BlockSpec((None, K, sh.tn), lambda i, j, k: (sh.chip(j), 0, sh.within(j))),
               o_spec=pl.BlockSpec((tm, sh.tn), lambda i, j, k: (i, j)),
               out_shape=jax.ShapeDtypeStruct((S, sh.ntiles * sh.tn), out_dtype), contract=NN, carry=carry)


def _mm_cols_bwd_x(name, dy, wg, sh, out_dtype, *, tm, tn, carry=None):
    S = dy.shape[0]
    K = wg.shape[1]
    tm, tn = _tile(S, tm, LANES), _tile(K, tn, LANES)
    grid = (S // tm, K // tn, sh.ntiles)
    return _mm(name, dy, wg, grid=grid,
               a_spec=pl.BlockSpec((tm, sh.tn), lambda i, j, k: (i, k)),
               b_spec=pl.BlockSpec((None, tn, sh.tn), lambda i, j, k: (sh.chip(k), j, sh.within(k))),
               o_spec=pl.BlockSpec((tm, tn), lambda i, j, k: (i, j)),
               out_shape=jax.ShapeDtypeStruct((S, K), out_dtype), contract=NT, carry=carry)


def _mm_cols_bwd_w(name, a, dy, sh, *, tm, tk, carry=None, rider=None, ring_a=False):
    S, K = a.shape
    tm, tk = _tile(K, tm, LANES), _tile(S, tk, LANES)
    grid = (K // tm, sh.ntiles, S // tk)
    return _mm(name, a, dy, grid=grid,
               a_spec=pl.BlockSpec((tk, tm), lambda i, j, k: (k, i)),
               b_spec=pl.BlockSpec((tk, sh.tn), lambda i, j, k: (k, j)),
               o_spec=pl.BlockSpec((None, tm, sh.tn), lambda i, j, k: (sh.chip(j), i, sh.within(j))),
               out_shape=jax.ShapeDtypeStruct((N_CHIPS, K, sh.c), F32), contract=TN, carry=carry,
               rider=None if rider is None else rider(math.prod(grid)), ring_a=ring_a)


def _row_spec(tr, width):
    return pl.BlockSpec((tr, width), lambda i: (i, 0))


def _full_spec(shape):
    nd = len(shape)
    return pl.BlockSpec(shape, lambda i: (0,) * nd)


def _rms_fwd(x, g):
    S, D = x.shape
    tr = _tile(S, 512, 16)

    def body(x_ref, g_ref, h_ref):
        xv = x_ref[...]
        r = lax.rsqrt(jnp.mean(xv * xv, axis=-1, keepdims=True) + EPS)
        h_ref[...] = (xv * r * g_ref[...]).astype(BF16)

    return pl.pallas_call(
        body, name="rms_fwd", grid=(S // tr,),
        in_specs=[_row_spec(tr, D), _full_spec((1, D))], out_specs=_row_spec(tr, D),
        out_shape=jax.ShapeDtypeStruct((S, D), BF16),
        compiler_params=_params(12 * tr * D, ("parallel",)),
    )(x, g)


class _Rider:
    def __init__(self, operands, in_specs, out_specs, out_shapes, body, vmem_bytes):
        self.operands, self.in_specs, self.out_specs = list(operands), list(in_specs), list(out_specs)
        self.out_shapes, self.body, self.vmem_bytes = list(out_shapes), body, vmem_bytes


def _accumulate(step, ref, part):
    @pl.when(step == 0)
    def _():
        ref[...] = part

    @pl.when(step > 0)
    def _():
        ref[...] += part


def _rms_bwd(dh, x, g, dres, steps):
    S, D = x.shape
    tr = S // steps

    def body(step, ins, outs):
        dh_ref, x_ref, g_ref, dres_ref = ins
        dx_ref, dxb_ref, dg_ref = outs
        xv = x_ref[...]
        dhv = dh_ref[...]
        r = lax.rsqrt(jnp.mean(xv * xv, axis=-1, keepdims=True) + EPS)
        gy = dhv * g_ref[...]
        dot = jnp.mean(xv * gy, axis=-1, keepdims=True)
        dx = dres_ref[...] + (r * gy - xv * (r * r * r * dot))
        dx_ref[...] = dx
        dxb_ref[...] = dx.astype(BF16)
        _accumulate(step, dg_ref, jnp.sum(dhv * (xv * r), axis=0, keepdims=True))

    return _Rider([dh, x, g, dres],
                  [_row_spec(tr, D), _row_spec(tr, D), _full_spec((1, D)), _row_spec(tr, D)],
                  [_row_spec(tr, D), _row_spec(tr, D), _full_spec((1, D))],
                  [jax.ShapeDtypeStruct((S, D), F32), jax.ShapeDtypeStruct((S, D), BF16),
                   jax.ShapeDtypeStruct((1, D), F32)], body, 40 * tr * D)


def _out_norm_fwd(oa, ob, ga, gb):
    S, W = oa.shape
    tr = _tile(S, 512, 16)

    def body(oa_ref, ob_ref, ga_ref, gb_ref, o_ref):
        for src, g_ref, off in ((oa_ref, ga_ref, 0), (ob_ref, gb_ref, W)):
            v = src[...]
            r = lax.rsqrt(jnp.mean(v * v, axis=-1, keepdims=True) + EPS)
            o_ref[:, off:off + W] = (v * r * g_ref[...]).astype(BF16)

    return pl.pallas_call(
        body, name="out_norm_fwd", grid=(S // tr,),
        in_specs=[_row_spec(tr, W), _row_spec(tr, W), _full_spec((1, W)), _full_spec((1, W))],
        out_specs=_row_spec(tr, 2 * W), out_shape=jax.ShapeDtypeStruct((S, 2 * W), BF16),
        compiler_params=_params(24 * tr * W, ("parallel",)),
    )(oa, ob, ga, gb)


def _out_norm_bwd(do, oa, ob, ga, gb, steps):
    S, W = oa.shape
    tr = S // steps

    def body(step, ins, outs):
        do_ref, oa_ref, ob_ref, ga_ref, gb_ref = ins
        doa_ref, dob_ref, dga_ref, dgb_ref = outs
        parts = []
        for src, g_ref, off, d_ref in ((oa_ref, ga_ref, 0, doa_ref), (ob_ref, gb_ref, W, dob_ref)):
            v = src[...]
            dv = do_ref[:, off:off + W]
            r = lax.rsqrt(jnp.mean(v * v, axis=-1, keepdims=True) + EPS)
            gy = dv * g_ref[...]
            dot = jnp.mean(v * gy, axis=-1, keepdims=True)
            d_ref[...] = (r * gy - v * (r * r * r * dot)).astype(BF16)
            parts.append(jnp.sum(dv * (v * r), axis=0, keepdims=True))
        _accumulate(step, dga_ref, parts[0])
        _accumulate(step, dgb_ref, parts[1])

    return _Rider([do, oa, ob, ga, gb],
                  [_row_spec(tr, 2 * W), _row_spec(tr, W), _row_spec(tr, W), _full_spec((1, W)), _full_spec((1, W))],
                  [_row_spec(tr, W), _row_spec(tr, W), _full_spec((1, W)), _full_spec((1, W))],
                  [jax.ShapeDtypeStruct((S, W), BF16), jax.ShapeDtypeStruct((S, W), BF16),
                   jax.ShapeDtypeStruct((1, W), F32), jax.ShapeDtypeStruct((1, W), F32)], body, 48 * tr * W)


def _loss_grad(y, t):
    S, D = y.shape
    tr = _tile(S, 256, 16)

    def body(y_ref, t_ref, dy_ref, dyb_ref, ls_ref):
        e = y_ref[...] - t_ref[...]
        dy = e * (1.0 / D)
        dy_ref[...] = dy
        dyb_ref[...] = dy.astype(BF16)
        part = jnp.sum(e * e, axis=0, keepdims=True)

        @pl.when(pl.program_id(0) == 0)
        def _():
            ls_ref[...] = part

        @pl.when(pl.program_id(0) > 0)
        def _():
            ls_ref[...] += part

    return pl.pallas_call(
        body, name="loss_grad", grid=(S // tr,),
        in_specs=[_row_spec(tr, D), _row_spec(tr, D)],
        out_specs=[_row_spec(tr, D), _row_spec(tr, D), _full_spec((1, D))],
        out_shape=[jax.ShapeDtypeStruct((S, D), F32), jax.ShapeDtypeStruct((S, D), BF16),
                   jax.ShapeDtypeStruct((1, D), F32)],
        compiler_params=_params(32 * tr * D, ("arbitrary",)),
    )(y, t)


def _head_rms(x, g):
    r = lax.rsqrt(jnp.mean(x * x, axis=-1, keepdims=True) + EPS)
    return x * r * g


def _head_rms_bwd(x, dy, g):
    r = lax.rsqrt(jnp.mean(x * x, axis=-1, keepdims=True) + EPS)
    gy = dy * g
    dot = jnp.mean(x * gy, axis=-1, keepdims=True)
    return r * gy - x * (r * r * r * dot), dy * (x * r)


def _rope(y, cos2, sgn_sin):
    return y * cos2 + pltpu.roll(y, HEAD_DIM // 2, axis=1) * sgn_sin


def _rope_t(dy, cos2, sgn_sin):
    return dy * cos2 + pltpu.roll(dy * sgn_sin, HEAD_DIM // 2, axis=1)


def _qk_prep_fwd(proj, cos2, sgn_sin, gains, dims):
    S, P = proj.shape
    naw, waw, kvw = dims
    tr = _tile(S, 256, 16)
    hd = HEAD_DIM

    def body(p_ref, c_ref, s_ref, g_ref, qa_ref, ka_ref, va_ref, qb_ref, kb_ref, vb_ref):
        c2 = c_ref[...]
        ss = s_ref[...]
        off = 0
        for dst, width, gi, rot in ((qa_ref, naw, 0, False), (ka_ref, naw, 1, False), (va_ref, naw, None, False),
                                    (qb_ref, waw, 2, True), (kb_ref, kvw, 3, True), (vb_ref, kvw, None, False)):
            for h in range(width // hd):
                xh = p_ref[:, off + h * hd:off + (h + 1) * hd]
                if gi is not None:
                    xh = _head_rms(xh, g_ref[gi:gi + 1, :])
                    if rot:
                        xh = _rope(xh, c2, ss)
                dst[:, h * hd:(h + 1) * hd] = xh.astype(BF16)
            off += width

    widths = (naw, naw, naw, waw, kvw, kvw)
    return pl.pallas_call(
        body, name="qk_prep_fwd", grid=(S // tr,),
        in_specs=[_row_spec(tr, P), _row_spec(tr, hd), _row_spec(tr, hd), _full_spec((SUBLANES, hd))],
        out_specs=[_row_spec(tr, w) for w in widths],
        out_shape=[jax.ShapeDtypeStruct((S, w), BF16) for w in widths],
        compiler_params=_params(16 * tr * P, ("parallel",)),
    )(proj, cos2, sgn_sin, gains)


def _qk_prep_bwd(proj, dqa, dka, dva, dqb, dkb, dvb, cos2, sgn_sin, gains, dims, steps):
    S, P = proj.shape
    naw, waw, kvw = dims
    tr = S // steps
    hd = HEAD_DIM

    def body(step, ins, outs):
        p_ref, dqa_ref, dka_ref, dva_ref, dqb_ref, dkb_ref, dvb_ref, c_ref, s_ref, g_ref = ins
        dp_ref, dg_ref = outs
        c2 = c_ref[...]
        ss = s_ref[...]
        off = 0
        dgs = [jnp.zeros((1, hd), F32) for _ in range(4)]
        for src, width, gi, rot in ((dqa_ref, naw, 0, False), (dka_ref, naw, 1, False), (dva_ref, naw, None, False),
                                    (dqb_ref, waw, 2, True), (dkb_ref, kvw, 3, True), (dvb_ref, kvw, None, False)):
            for h in range(width // hd):
                dy = src[:, h * hd:(h + 1) * hd].astype(F32)
                if gi is not None:
                    if rot:
                        dy = _rope_t(dy, c2, ss)
                    xh = p_ref[:, off + h * hd:off + (h + 1) * hd]
                    dy, dgt = _head_rms_bwd(xh, dy, g_ref[gi:gi + 1, :])
                    dgs[gi] = dgs[gi] + jnp.sum(dgt, axis=0, keepdims=True)
                dp_ref[:, off + h * hd:off + (h + 1) * hd] = dy.astype(BF16)
            off += width
        _accumulate(step, dg_ref, jnp.concatenate(dgs + [jnp.zeros((SUBLANES - 4, hd), F32)], axis=0))

    return _Rider([proj, dqa, dka, dva, dqb, dkb, dvb, cos2, sgn_sin, gains],
                  [_row_spec(tr, P), _row_spec(tr, naw), _row_spec(tr, naw), _row_spec(tr, naw),
                   _row_spec(tr, waw), _row_spec(tr, kvw), _row_spec(tr, kvw),
                   _row_spec(tr, hd), _row_spec(tr, hd), _full_spec((SUBLANES, hd))],
                  [_row_spec(tr, P), _full_spec((SUBLANES, hd))],
                  [jax.ShapeDtypeStruct((S, P), BF16), jax.ShapeDtypeStruct((SUBLANES, hd), F32)], body, 24 * tr * P)


NA_KEYS = NA_WIN_R * GRID_W
NA_ROWS_PER_STEP = 32


def _na_col_geometry():
    c = np.arange(GRID_W)
    col_start = np.clip(c - NA_WIN_C // 2, 0, GRID_W - NA_WIN_C)
    mask = (c[None, :] >= col_start[:, None]) & (c[None, :] < col_start[:, None] + NA_WIN_C)
    dc = np.clip(c[None, :] - c[:, None], -(NA_WIN_C - 1), NA_WIN_C - 1) + (NA_WIN_C - 1)
    onehot = (dc[:, :, None] == np.arange(2 * NA_WIN_C - 1)[None, None, :]) & mask[:, :, None]
    return mask, onehot


def _na_bias_table(rpb_l):
    H = rpb_l.shape[0]
    mask, onehot = _na_col_geometry()
    t = jnp.sum(jnp.where(onehot[None, None], rpb_l[:, :, None, None, :], 0.0), axis=-1)
    t = jnp.where(mask[None, None], t, NEG)
    per_delta = [jnp.transpose(t[:, d:d + NA_WIN_R], (0, 2, 1, 3)).reshape(H, GRID_W, NA_KEYS) for d in range(NA_WIN_R)]
    return jnp.stack(per_delta, axis=1)


def _na_bias_table_t(db):
    H = db.shape[0]
    _, onehot = _na_col_geometry()
    d5 = db.reshape(H, NA_WIN_R, GRID_W, NA_WIN_R, GRID_W)
    folded = jnp.einsum("hdqwk,qkc->hdwc", d5, onehot.astype(np.float32), precision=lax.Precision.HIGHEST)
    return sum(jnp.pad(folded[:, d], ((0, 0), (d, NA_WIN_R - 1 - d), (0, 0))) for d in range(NA_WIN_R))


def _na_row_geometry(r, rows):
    start = jnp.clip(r - NA_WIN_R // 2, 0, rows - NA_WIN_R)
    return start, start - r + (NA_WIN_R - 1)


def _softmax_rows(s):
    m = jnp.max(s, axis=-1, keepdims=True)
    e = jnp.exp(s - m)
    return e / jnp.sum(e, axis=-1, keepdims=True)


def _na_fwd(qa, ka, va, btab):
    S, W = qa.shape
    H = W // HEAD_DIM
    rows = S // GRID_W
    assert rows >= NA_WIN_R
    rb = _tile(rows, NA_ROWS_PER_STEP, 1)
    tq = rb * GRID_W

    def body(q_ref, k_ref, v_ref, b_ref, o_ref):
        i = pl.program_id(1)

        geo = [_na_row_geometry(i * rb + j, rows) for j in range(rb)]
        toks = [pl.ds(j * GRID_W, GRID_W) for j in range(rb)]
        wins = [pl.ds(pl.multiple_of(start * GRID_W, GRID_W), NA_KEYS) for start, _ in geo]
        scores = [_dot(q_ref[toks[j], :], k_ref[wins[j], :], NT) for j in range(rb)]
        probs = [_softmax_rows(scores[j] * ATTN_SCALE + b_ref[geo[j][1]]) for j in range(rb)]
        for j in range(rb):
            o_ref[toks[j], :] = _dot(probs[j], v_ref[wins[j], :], NN)

    kv_spec = pl.BlockSpec((S, HEAD_DIM), lambda h, i: (0, h))
    return pl.pallas_call(
        body, name="na_fwd", grid=(H, rows // rb),
        in_specs=[pl.BlockSpec((tq, HEAD_DIM), lambda h, i: (i, h)), kv_spec, kv_spec,
                  pl.BlockSpec((None, NA_WIN_R, GRID_W, NA_KEYS), lambda h, i: (h, 0, 0, 0))],
        out_specs=pl.BlockSpec((tq, HEAD_DIM), lambda h, i: (i, h)),
        out_shape=jax.ShapeDtypeStruct((S, W), F32),
        compiler_params=_params(8 * S * HEAD_DIM + (8 << 20), ("parallel", "arbitrary")),
    )(qa, ka, va, btab)


def _na_bwd(qa, ka, va, btab, doa):
    S, W = qa.shape
    H = W // HEAD_DIM
    rows = S // GRID_W
    rb = _tile(rows, NA_ROWS_PER_STEP, 1)
    tq = rb * GRID_W

    def body(q_ref, k_ref, v_ref, b_ref, do_ref, dq_ref, dk_ref, dv_ref, db_ref):
        i = pl.program_id(1)

        @pl.when(i == 0)
        def _():
            dk_ref[...] = jnp.zeros_like(dk_ref)
            dv_ref[...] = jnp.zeros_like(dv_ref)
            db_ref[...] = jnp.zeros_like(db_ref)

        steps = range(rb)
        geo = [_na_row_geometry(i * rb + j, rows) for j in steps]
        toks = [pl.ds(j * GRID_W, GRID_W) for j in steps]
        wins = [pl.ds(pl.multiple_of(start * GRID_W, GRID_W), NA_KEYS) for start, _ in geo]
        scores = [_dot(q_ref[toks[j], :], k_ref[wins[j], :], NT) for j in steps]
        dps = [_dot(do_ref[toks[j], :], v_ref[wins[j], :], NT) for j in steps]
        probs = [_softmax_rows(scores[j] * ATTN_SCALE + b_ref[geo[j][1]]) for j in steps]
        dss = [probs[j] * (dps[j] - jnp.sum(probs[j] * dps[j], axis=-1, keepdims=True)) for j in steps]
        for j in steps:
            db_ref[geo[j][1]] += dss[j]
        dsb = [(dss[j] * ATTN_SCALE).astype(BF16) for j in steps]
        for j in steps:
            dq_ref[toks[j], :] = _dot(dsb[j], k_ref[wins[j], :], NN)
        dks = [_dot(dsb[j], q_ref[toks[j], :], TN) for j in steps]
        dvs = [_dot(probs[j], do_ref[toks[j], :], TN) for j in steps]
        for j in steps:
            dk_ref[wins[j], :] += dks[j]
            dv_ref[wins[j], :] += dvs[j]

    kv_spec = pl.BlockSpec((S, HEAD_DIM), lambda h, i: (0, h))
    q_spec = pl.BlockSpec((tq, HEAD_DIM), lambda h, i: (i, h))
    b_spec = pl.BlockSpec((None, NA_WIN_R, GRID_W, NA_KEYS), lambda h, i: (h, 0, 0, 0))
    return pl.pallas_call(
        body, name="na_bwd", grid=(H, rows // rb),
        in_specs=[q_spec, kv_spec, kv_spec, b_spec, q_spec],
        out_specs=[q_spec, kv_spec, kv_spec, b_spec],
        out_shape=[jax.ShapeDtypeStruct((S, W), F32), jax.ShapeDtypeStruct((S, W), F32),
                   jax.ShapeDtypeStruct((S, W), F32), jax.ShapeDtypeStruct(btab.shape, F32)],
        compiler_params=_params(24 * S * HEAD_DIM + (12 << 20), ("parallel", "arbitrary")),
    )(qa, ka, va, btab, doa)


WA_KEYS = 3 * WA_BLOCK
WA_BLOCKS_PER_STEP = 8


def _wa_mask(qk, n, start, g):
    s = qk * ATTN_SCALE
    qpos = n * WA_BLOCK + lax.broadcasted_iota(jnp.int32, (WA_BLOCK, WA_KEYS), 0)
    kpos = start + lax.broadcasted_iota(jnp.int32, (WA_BLOCK, WA_KEYS), 1)
    valid = jnp.abs(kpos - qpos) <= WA_WINDOW
    valid = jnp.concatenate([valid] * g, axis=0)
    return jnp.where(valid, s, NEG)


def _wa_probs(s, sink):
    m = jnp.maximum(jnp.max(s, axis=-1, keepdims=True), sink)
    e = jnp.exp(s - m)
    es = jnp.exp(sink - m)
    den = jnp.sum(e, axis=-1, keepdims=True) + es
    return e / den, es / den


def _wa_stack(ref, tok, g):
    return jnp.concatenate([ref[tok, t * HEAD_DIM:(t + 1) * HEAD_DIM] for t in range(g)], axis=0)


def _wa_fwd(qb, kb, vb, sink_col):
    S, W = qb.shape
    hkv = kb.shape[1] // HEAD_DIM
    g = W // HEAD_DIM // hkv
    nb = S // WA_BLOCK
    assert S >= WA_KEYS
    qb_step = _tile(nb, WA_BLOCKS_PER_STEP, 1)
    tq = qb_step * WA_BLOCK

    def body(q_ref, k_ref, v_ref, s_ref, o_ref):
        i = pl.program_id(1)

        steps = range(qb_step)
        ns = [i * qb_step + j for j in steps]
        toks = [pl.ds(j * WA_BLOCK, WA_BLOCK) for j in steps]
        starts = [pl.multiple_of(jnp.clip((n - 1) * WA_BLOCK, 0, S - WA_KEYS), WA_BLOCK) for n in ns]
        wins = [pl.ds(start, WA_KEYS) for start in starts]
        scores = [_dot(_wa_stack(q_ref, toks[j], g), k_ref[wins[j], :], NT) for j in steps]
        probs = [_wa_probs(_wa_mask(scores[j], ns[j], starts[j], g), s_ref[...])[0] for j in steps]
        outs = [_dot(probs[j], v_ref[wins[j], :], NN) for j in steps]
        for j in steps:
            for t in range(g):
                o_ref[toks[j], t * HEAD_DIM:(t + 1) * HEAD_DIM] = outs[j][t * WA_BLOCK:(t + 1) * WA_BLOCK]

    kv_spec = pl.BlockSpec((S, HEAD_DIM), lambda h, i: (0, h))
    q_spec = pl.BlockSpec((tq, g * HEAD_DIM), lambda h, i: (i, h))
    return pl.pallas_call(
        body, name="wa_fwd", grid=(hkv, nb // qb_step),
        in_specs=[q_spec, kv_spec, kv_spec, pl.BlockSpec((None, g * WA_BLOCK, 1), lambda h, i: (h, 0, 0))],
        out_specs=q_spec, out_shape=jax.ShapeDtypeStruct((S, W), F32),
        compiler_params=_params(8 * S * HEAD_DIM + (12 << 20), ("parallel", "arbitrary")),
    )(qb, kb, vb, sink_col)


def _wa_bwd(qb, kb, vb, sink_col, dob):
    S, W = qb.shape
    KW = kb.shape[1]
    hkv = KW // HEAD_DIM
    g = W // HEAD_DIM // hkv
    nb = S // WA_BLOCK
    qb_step = _tile(nb, WA_BLOCKS_PER_STEP, 1)
    tq = qb_step * WA_BLOCK

    def body(q_ref, k_ref, v_ref, s_ref, do_ref, dq_ref, dk_ref, dv_ref, dsink_ref):
        i = pl.program_id(1)

        @pl.when(i == 0)
        def _():
            dk_ref[...] = jnp.zeros_like(dk_ref)
            dv_ref[...] = jnp.zeros_like(dv_ref)
            dsink_ref[...] = jnp.zeros_like(dsink_ref)

        steps = range(qb_step)
        ns = [i * qb_step + j for j in steps]
        toks = [pl.ds(j * WA_BLOCK, WA_BLOCK) for j in steps]
        starts = [pl.multiple_of(jnp.clip((n - 1) * WA_BLOCK, 0, S - WA_KEYS), WA_BLOCK) for n in ns]
        wins = [pl.ds(start, WA_KEYS) for start in starts]
        qss = [_wa_stack(q_ref, toks[j], g) for j in steps]
        doss = [_wa_stack(do_ref, toks[j], g) for j in steps]
        scores = [_dot(qss[j], k_ref[wins[j], :], NT) for j in steps]
        dps = [_dot(doss[j], v_ref[wins[j], :], NT) for j in steps]
        pp = [_wa_probs(_wa_mask(scores[j], ns[j], starts[j], g), s_ref[...]) for j in steps]
        dsums = [jnp.sum(pp[j][0] * dps[j], axis=-1, keepdims=True) for j in steps]
        dsb = [(pp[j][0] * (dps[j] - dsums[j]) * ATTN_SCALE).astype(BF16) for j in steps]
        dsink_ref[...] -= sum(pp[j][1] * dsums[j] for j in steps)
        dqs = [_dot(dsb[j], k_ref[wins[j], :], NN) for j in steps]
        dks = [_dot(dsb[j], qss[j], TN) for j in steps]
        dvs = [_dot(pp[j][0], doss[j], TN) for j in steps]
        for j in steps:
            for t in range(g):
                dq_ref[toks[j], t * HEAD_DIM:(t + 1) * HEAD_DIM] = dqs[j][t * WA_BLOCK:(t + 1) * WA_BLOCK]
            dk_ref[wins[j], :] += dks[j]
            dv_ref[wins[j], :] += dvs[j]

    kv_spec = pl.BlockSpec((S, HEAD_DIM), lambda h, i: (0, h))
    q_spec = pl.BlockSpec((tq, g * HEAD_DIM), lambda h, i: (i, h))
    s_spec = pl.BlockSpec((None, g * WA_BLOCK, 1), lambda h, i: (h, 0, 0))
    return pl.pallas_call(
        body, name="wa_bwd", grid=(hkv, nb // qb_step),
        in_specs=[q_spec, kv_spec, kv_spec, s_spec, q_spec],
        out_specs=[q_spec, kv_spec, kv_spec, s_spec],
        out_shape=[jax.ShapeDtypeStruct((S, W), F32), jax.ShapeDtypeStruct((S, KW), F32),
                   jax.ShapeDtypeStruct((S, KW), F32), jax.ShapeDtypeStruct(sink_col.shape, F32)],
        compiler_params=_params(24 * S * HEAD_DIM + (16 << 20), ("parallel", "arbitrary")),
    )(qb, kb, vb, sink_col, dob)


CONV_HALO = SUBLANES


def _conv_specs(tr, width, nblk, rows_inner):
    per = tr // CONV_HALO

    def spec(shape, row_block):
        if rows_inner:
            return pl.BlockSpec(shape, lambda j, i: (row_block(i), j))
        return pl.BlockSpec(shape, lambda i, j: (row_block(i), j))

    cur = spec((tr, width), lambda i: i)
    prev = spec((CONV_HALO, width), lambda i: jnp.maximum(i * per - 1, 0))
    nxt = spec((CONV_HALO, width), lambda i: jnp.minimum((i + 1) * per, nblk * per - 1))
    return prev, cur, nxt


def _extend(prev_ref, cur_ref, next_ref, i, nblk):
    p = jnp.where(i > 0, prev_ref[...].astype(F32), 0.0)
    n = jnp.where(i < nblk - 1, next_ref[...].astype(F32), 0.0)
    return jnp.concatenate([p, cur_ref[...].astype(F32), n], axis=0)


def _shift_down(x):
    return pltpu.roll(x, 1, axis=0)


def _shift_up(x):
    return pltpu.roll(x, x.shape[0] - 1, axis=0)


def _conv(ext, w_ref, b_ref):
    return _shift_down(ext) * w_ref[0:1, :] + ext * w_ref[1:2, :] + _shift_up(ext) * w_ref[2:3, :] + b_ref[...]


def _sigmoid(x):
    return 1.0 / (1.0 + jnp.exp(-x))


def _convgate_fwd(up_pre, cw, cb, tc):
    S, C2 = up_pre.shape
    tr = _tile(S, 512, 16)
    nblk = S // tr
    prev, cur, nxt = _conv_specs(tr, 2 * tc, nblk, False)

    def body(p_ref, c_ref, n_ref, w_ref, b_ref, a_ref):
        i = pl.program_id(0)
        u = _conv(_extend(p_ref, c_ref, n_ref, i, nblk), w_ref, b_ref)[CONV_HALO:CONV_HALO + tr]
        gate, up = u[:, :tc], u[:, tc:]
        a_ref[...] = (gate * _sigmoid(gate) * up).astype(BF16)

    return pl.pallas_call(
        body, name="convgate_fwd", grid=(nblk, C2 // (2 * tc)),
        in_specs=[prev, cur, nxt, pl.BlockSpec((3, 2 * tc), lambda i, j: (0, j)),
                  pl.BlockSpec((1, 2 * tc), lambda i, j: (0, j))],
        out_specs=pl.BlockSpec((tr, tc), lambda i, j: (i, j)),
        out_shape=jax.ShapeDtypeStruct((S, C2 // 2), BF16),
        compiler_params=_params(48 * tr * tc, ("parallel", "parallel")),
    )(up_pre, up_pre, up_pre, cw, cb)


def _convgate_bwd(up_pre, dact, cw, cb, tc):
    S, C2 = up_pre.shape
    tr = _tile(S, 512, 16)
    nblk = S // tr
    prev, cur, nxt = _conv_specs(tr, 2 * tc, nblk, True)
    dprev, dcur, dnxt = _conv_specs(tr, tc, nblk, True)

    def body(p_ref, c_ref, n_ref, dp_ref, dc_ref, dn_ref, w_ref, b_ref, dx_ref, dw_ref, db_ref):
        i = pl.program_id(1)
        ext = _extend(p_ref, c_ref, n_ref, i, nblk)
        da = _extend(dp_ref, dc_ref, dn_ref, i, nblk)
        ext_dn, ext_up = _shift_down(ext), _shift_up(ext)
        u = ext_dn * w_ref[0:1, :] + ext * w_ref[1:2, :] + ext_up * w_ref[2:3, :] + b_ref[...]
        gate, up = u[:, :tc], u[:, tc:]
        sg = _sigmoid(gate)
        silu = gate * sg
        du = jnp.concatenate([da * up * (sg + silu * (1.0 - sg)), da * silu], axis=1)
        dx = _shift_up(du) * w_ref[0:1, :] + du * w_ref[1:2, :] + _shift_down(du) * w_ref[2:3, :]
        mid = slice(CONV_HALO, CONV_HALO + tr)
        dx_ref[...] = dx[mid].astype(BF16)
        duc = du[mid]
        dw = jnp.concatenate([jnp.sum(duc * ext_dn[mid], axis=0, keepdims=True),
                              jnp.sum(duc * ext[mid], axis=0, keepdims=True),
                              jnp.sum(duc * ext_up[mid], axis=0, keepdims=True)], axis=0)
        db = jnp.sum(duc, axis=0, keepdims=True)

        @pl.when(i == 0)
        def _():
            dw_ref[...] = dw
            db_ref[...] = db

        @pl.when(i > 0)
        def _():
            dw_ref[...] += dw
            db_ref[...] += db

    return pl.pallas_call(
        body, name="convgate_bwd", grid=(C2 // (2 * tc), nblk),
        in_specs=[prev, cur, nxt, dprev, dcur, dnxt,
                  pl.BlockSpec((3, 2 * tc), lambda j, i: (0, j)), pl.BlockSpec((1, 2 * tc), lambda j, i: (0, j))],
        out_specs=[pl.BlockSpec((tr, 2 * tc), lambda j, i: (i, j)),
                   pl.BlockSpec((3, 2 * tc), lambda j, i: (0, j)), pl.BlockSpec((1, 2 * tc), lambda j, i: (0, j))],
        out_shape=[jax.ShapeDtypeStruct((S, C2), BF16), jax.ShapeDtypeStruct((3, C2), F32),
                   jax.ShapeDtypeStruct((1, C2), F32)],
        compiler_params=_params(160 * tr * tc, ("parallel", "arbitrary")),
    )(up_pre, up_pre, up_pre, dact, dact, dact, cw, cb)


def _rope_tables(positions):
    inv = ROPE_THETA ** (-jnp.arange(0, HEAD_DIM, 2, dtype=F32) / HEAD_DIM)
    ang = positions.astype(F32)[:, None] * inv[None, :]
    cos, sin = jnp.cos(ang), jnp.sin(ang)
    return jnp.concatenate([cos, cos], axis=1), jnp.concatenate([-sin, sin], axis=1)


def _sink_col(sink_l, hkv):
    g = sink_l.shape[0] // hkv
    return jnp.broadcast_to(sink_l.reshape(hkv, g, 1), (hkv, g, WA_BLOCK)).reshape(hkv, g * WA_BLOCK, 1)


def _sink_col_t(dcol):
    hkv, rows, _ = dcol.shape
    return jnp.sum(dcol.reshape(hkv, rows // WA_BLOCK, WA_BLOCK), axis=-1).reshape(-1)


def _to_paired(a, sh):
    return jnp.concatenate([a[:, sh.nat(t) * sh.tn:(sh.nat(t) + 1) * sh.tn] for t in range(sh.ntiles)], axis=1)


def _from_paired(a, sh):
    pos = {sh.nat(t): t for t in range(sh.ntiles)}
    return jnp.concatenate([a[:, pos[n] * sh.tn:(pos[n] + 1) * sh.tn] for n in range(sh.ntiles)], axis=1)


def _pack(arrs):
    flat = jnp.concatenate([a.reshape(-1).astype(F32) for a in arrs])
    unit = SUBLANES * LANES
    total = -(-flat.shape[0] // unit) * unit
    return jnp.pad(flat, (0, total - flat.shape[0])).reshape(-1, LANES)


def _unpack(buf, shapes):
    flat = buf.reshape(-1)
    out, off = [], 0
    for s in shapes:
        n = math.prod(s)
        out.append(flat[off:off + n].reshape(s))
        off += n
    return out


HBM_SPEC = pl.BlockSpec(memory_space=pltpu.HBM)
DMA_CHUNK_BYTES = 512 * 1024


def _row_chunks(rows, row_bytes, align):
    want = max(1, rows * row_bytes // DMA_CHUNK_BYTES)
    count = max(k for k in range(1, rows + 1) if rows % k == 0 and (rows // k) % align == 0 and (k <= want or k == 1))
    size = rows // count
    return [(j * size, size) for j in range(count)]


def _mesh_pos():
    return lax.axis_index("x"), lax.axis_index("y"), lax.axis_index("c")


def _other_chips(x, y):
    return [(1 - x, y), (x, 1 - y), (1 - x, 1 - y)]


def _remote(src, dst, send_sems, recv_sems, k, dev):
    return pltpu.make_async_remote_copy(src_ref=src, dst_ref=dst, send_sem=send_sems.at[k], recv_sem=recv_sems.at[k],
                                        device_id=dev, device_id_type=MESH)


class _Exchange:
    def __init__(self, operands, out_shapes, aliases, start, wait):
        self.operands, self.out_shapes, self.aliases = list(operands), list(out_shapes), dict(aliases)
        self.start, self.wait = start, wait
        self.n_sems = len(out_shapes)

    def scratch(self):
        return [pltpu.SemaphoreType.DMA((self.n_sems,)), pltpu.SemaphoreType.DMA((self.n_sems,))]


class _SemSlice:
    def __init__(self, sems, offset):
        self._sems, self._offset = sems, offset

    @property
    def at(self):
        return self

    def __getitem__(self, k):
        return self._sems.at[self._offset + k]


def _both(a, b):
    ia, oa = len(a.operands), a.n_sems

    def split(ins, outs, send_sems, recv_sems):
        return ((ins[:ia], outs[:oa], send_sems, recv_sems),
                (ins[ia:], outs[oa:], _SemSlice(send_sems, oa), _SemSlice(recv_sems, oa)))

    def start(*refs):
        first, second = split(*refs)
        a.start(*first)
        b.start(*second)

    def wait(*refs):
        first, second = split(*refs)
        a.wait(*first)
        b.wait(*second)

    aliases = dict(a.aliases)
    aliases.update({ia + i: oa + o for i, o in b.aliases.items()})
    return _Exchange(a.operands + b.operands, a.out_shapes + b.out_shapes, aliases, start, wait)


def _exchange_alone(name, ex):
    n_in = len(ex.operands)

    def body(*refs):
        ins, outs = refs[:n_in], refs[n_in:n_in + ex.n_sems]
        send_sems, recv_sems = refs[n_in + ex.n_sems:]
        ex.start(ins, outs, send_sems, recv_sems)
        ex.wait(ins, outs, send_sems, recv_sems)

    return pl.pallas_call(
        body, name=name, in_specs=[HBM_SPEC] * n_in, out_specs=[HBM_SPEC] * ex.n_sems, out_shape=ex.out_shapes,
        input_output_aliases=ex.aliases, scratch_shapes=ex.scratch(),
    )(*ex.operands)


def _ag_chip_exchange(gathered, small):
    n = len(gathered)
    arrays = list(gathered) + ([] if small is None else [small])

    def start(ins, outs, send_sems, recv_sems):
        x, y, c = _mesh_pos()
        q_me = 2 * x + y
        chips = _other_chips(x, y)
        for t in range(n):
            h = gathered[t].shape[1] // 2
            for s0, sz in _row_chunks(h, gathered[t].shape[2] * 2, 16):
                blk = outs[t].at[q_me, pl.ds(c * h + s0, sz)]
                for chip in chips:
                    _remote(blk, blk, send_sems, recv_sems, t, (*chip, c)).start()
        if small is not None:
            for chip in chips:
                _remote(outs[n].at[q_me], outs[n].at[q_me], send_sems, recv_sems, n, (*chip, c)).start()

    def wait(ins, outs, send_sems, recv_sems):
        x, y, c = _mesh_pos()
        for t in range(n):
            three = outs[t].at[pl.ds(0, 3), pl.ds(0, gathered[t].shape[1] // 2)]
            _remote(three, three, send_sems, recv_sems, t, (x, y, 1 - c)).wait()
        if small is not None:
            three = outs[n].at[pl.ds(0, 3)]
            _remote(three, three, send_sems, recv_sems, n, (x, y, 1 - c)).wait()

    return _Exchange(arrays, [jax.ShapeDtypeStruct(a.shape, a.dtype) for a in arrays],
                     {i: i for i in range(len(arrays))}, start, wait)


def _ag_sibling_pass(gathered):
    n = len(gathered)

    def start(ins, outs, send_sems, recv_sems):
        x, y, c = _mesh_pos()
        for t in range(n):
            h = gathered[t].shape[1] // 2
            for s0, sz in _row_chunks(h, gathered[t].shape[2] * 2, 16):
                for cx, cy in _other_chips(x, y):
                    blk = outs[t].at[2 * cx + cy, pl.ds(c * h + s0, sz)]
                    _remote(blk, blk, send_sems, recv_sems, t, (x, y, 1 - c)).start()

    def wait(ins, outs, send_sems, recv_sems):
        x, y, c = _mesh_pos()
        for t in range(n):
            three = outs[t].at[pl.ds(0, 3), pl.ds(0, gathered[t].shape[1] // 2)]
            _remote(three, three, send_sems, recv_sems, t, (x, y, 1 - c)).wait()

    return _Exchange(gathered, [jax.ShapeDtypeStruct(a.shape, a.dtype) for a in gathered],
                     {i: i for i in range(n)}, start, wait)


def _rs_sibling_exchange(grads):
    n = len(grads)
    halves = [g.shape[1] // 2 for g in grads]

    def start(ins, outs, send_sems, recv_sems):
        x, y, c = _mesh_pos()
        for t in range(n):
            for q in range(N_CHIPS):
                for s0, sz in _row_chunks(halves[t], grads[t].shape[2] * 4, SUBLANES):
                    _remote(ins[t].at[q, pl.ds((1 - c) * halves[t] + s0, sz)], outs[t].at[q, pl.ds(s0, sz)],
                            send_sems, recv_sems, t, (x, y, 1 - c)).start()

    def wait(ins, outs, send_sems, recv_sems):
        x, y, c = _mesh_pos()
        for t in range(n):
            _remote(outs[t], outs[t], send_sems, recv_sems, t, (x, y, 1 - c)).wait()

    return _Exchange(grads, [jax.ShapeDtypeStruct((N_CHIPS, h, g.shape[2]), F32) for g, h in zip(grads, halves)],
                     {}, start, wait)


def _rs_chip_exchange(pbs):
    n = len(pbs)

    def start(ins, outs, send_sems, recv_sems):
        x, y, c = _mesh_pos()
        for t in range(n):
            for s0, sz in _row_chunks(pbs[t].shape[1], pbs[t].shape[2] * 2, 16):
                for k, (cx, cy) in enumerate(_other_chips(x, y)):
                    _remote(ins[t].at[2 * cx + cy, pl.ds(s0, sz)], outs[t].at[k, pl.ds(s0, sz)],
                            send_sems, recv_sems, t, (cx, cy, c)).start()

    def wait(ins, outs, send_sems, recv_sems):
        x, y, c = _mesh_pos()
        for t in range(n):
            _remote(outs[t], outs[t], send_sems, recv_sems, t, (x, y, 1 - c)).wait()

    return _Exchange(pbs, [jax.ShapeDtypeStruct((3,) + p.shape[1:], BF16) for p in pbs], {}, start, wait)


def _rs_sibling_share(gs):
    n = len(gs)

    def start(ins, outs, send_sems, recv_sems):
        x, y, c = _mesh_pos()
        for t in range(n):
            h = gs[t].shape[0] // 2
            for s0, sz in _row_chunks(h, gs[t].shape[1] * 4, SUBLANES):
                rows = outs[t].at[pl.ds(c * h + s0, sz)]
                _remote(rows, rows, send_sems, recv_sems, t, (x, y, 1 - c)).start()

    def wait(ins, outs, send_sems, recv_sems):
        x, y, c = _mesh_pos()
        for t in range(n):
            half = outs[t].at[pl.ds(0, gs[t].shape[0] // 2)]
            _remote(half, half, send_sems, recv_sems, t, (x, y, 1 - c)).wait()

    return _Exchange(gs, [jax.ShapeDtypeStruct(g.shape, F32) for g in gs], {i: i for i in range(n)}, start, wait)


def _gather_small(buf):
    def body(in_ref, out_ref, send_sems, recv_sems, loc_sem):
        x, y, c = _mesh_pos()
        me = 4 * x + 2 * y + c
        local = pltpu.make_async_copy(in_ref, out_ref.at[me], loc_sem)
        local.start()
        sends, peers = [], []
        for k in range(1, N_DEV):
            bx, by, bc = (k >> 2) & 1, (k >> 1) & 1, k & 1
            peer = (x + bx - 2 * x * bx, y + by - 2 * y * by, c + bc - 2 * c * bc)
            peers.append(peer)
            sends.append(_remote(in_ref, out_ref.at[me], send_sems, recv_sems, k - 1, peer))
        for cp in sends:
            cp.start()
        for k, (px, py, pc) in enumerate(peers):
            slot = out_ref.at[4 * px + 2 * py + pc]
            _remote(slot, slot, send_sems, recv_sems, k, (px, py, pc)).wait_recv()
        for cp in sends:
            cp.wait_send()
        local.wait()

    return pl.pallas_call(
        body, name="gather_small", in_specs=[HBM_SPEC], out_specs=HBM_SPEC,
        out_shape=jax.ShapeDtypeStruct((N_DEV,) + buf.shape, F32),
        scratch_shapes=[pltpu.SemaphoreType.DMA((N_DEV - 1,)), pltpu.SemaphoreType.DMA((N_DEV - 1,)),
                        pltpu.SemaphoreType.DMA(())],
    )(buf)


def _ew_rows(rows, cols, bytes_per_elem):
    budget = 20 * 1024 * 1024
    return _tile(rows, max(16, budget // (2 * bytes_per_elem * cols) // 16 * 16), 16)


def _sum_pair(grad, recv):
    _, h, cols = recv.shape
    tr = _ew_rows(h, cols, 14)
    nb = h // tr
    half = pl.BlockSpec((None, tr, cols), lambda i, q: (q, i, 0))

    def body(a_ref, b_ref, f_ref, h_ref):
        s = a_ref[...] + b_ref[...]
        h_ref[...] = s.astype(BF16)

        @pl.when(pl.program_id(1) == 2 * lax.axis_index("x") + lax.axis_index("y"))
        def _():
            f_ref[...] = s

    return pl.pallas_call(
        body, name="rs_sum_pair", grid=(nb, N_CHIPS),
        in_specs=[pl.BlockSpec((None, tr, cols), lambda i, q: (q, lax.axis_index("c") * nb + i, 0)), half],
        out_specs=[pl.BlockSpec((tr, cols), lambda i, q: (i, 0)), half],
        out_shape=[jax.ShapeDtypeStruct((h, cols), F32), jax.ShapeDtypeStruct(recv.shape, BF16)],
        compiler_params=_params(28 * tr * cols, ("parallel", "arbitrary")),
    )(grad, recv)


def _sum_four(pf, recv):
    h, cols = pf.shape
    tr = _ew_rows(h, cols, 14)
    nb = h // tr

    def body(a_ref, r_ref, o_ref):
        o_ref[...] = ((a_ref[...] + r_ref[0].astype(F32)) + r_ref[1].astype(F32)) + r_ref[2].astype(F32)

    return pl.pallas_call(
        body, name="rs_sum_four", grid=(nb,),
        in_specs=[pl.BlockSpec((tr, cols), lambda i: (i, 0)),
                  pl.BlockSpec((3, tr, cols), lambda i: (0, i, 0))],
        out_specs=pl.BlockSpec((tr, cols), lambda i: (lax.axis_index("c") * nb + i, 0)),
        out_shape=jax.ShapeDtypeStruct((2 * h, cols), F32),
        compiler_params=_params(28 * tr * cols, ("parallel",)),
    )(pf, recv)


def _sum_eight(gathered):
    _, rows, cols = gathered.shape
    tr = _tile(rows, 512, SUBLANES)

    def body(g_ref, o_ref):
        s = g_ref[0]
        for d in range(1, N_DEV):
            s = s + g_ref[d]
        o_ref[...] = s

    return pl.pallas_call(
        body, name="sum_eight", grid=(rows // tr,),
        in_specs=[pl.BlockSpec((N_DEV, tr, cols), lambda i: (0, i, 0))],
        out_specs=_row_spec(tr, cols), out_shape=jax.ShapeDtypeStruct((rows, cols), F32),
        compiler_params=_params(None, ("parallel",)),
    )(gathered)


def _cast_layer(w, l):
    _, rows, cols = w.shape
    tr = _ew_rows(rows, cols, 6)

    def body(w_ref, o_ref):
        o_ref[...] = w_ref[...].astype(BF16)

    return pl.pallas_call(
        body, name="cast_bf16", grid=(rows // tr,),
        in_specs=[pl.BlockSpec((None, tr, cols), lambda i: (l, i, 0))],
        out_specs=pl.BlockSpec((None, tr, cols), lambda i: (2 * lax.axis_index("x") + lax.axis_index("y"), i, 0)),
        out_shape=jax.ShapeDtypeStruct((N_CHIPS, rows, cols), BF16),
        compiler_params=_params(12 * tr * cols, ("parallel",)),
    )(w)


def _adamw(g, w, m, v, l, prev):
    L, rows, cols = w.shape
    tr = _ew_rows(rows, cols, 32)
    layer = pl.BlockSpec((None, tr, cols), lambda i: (l, i, 0))

    def body(g_ref, w_ref, m_ref, v_ref, *rest):
        og_ref, od_ref, om_ref, ov_ref = rest[-4:]
        gv = g_ref[...]
        mn = ADAM_B1 * m_ref[...] + (1.0 - ADAM_B1) * gv
        vn = ADAM_B2 * v_ref[...] + (1.0 - ADAM_B2) * (gv * gv)
        m_hat = mn / ADAM_C1
        v_hat = vn / ADAM_C2
        od_ref[...] = -ADAM_LR * (m_hat / (jnp.sqrt(v_hat) + ADAM_EPS) + ADAM_WD * w_ref[...])
        og_ref[...] = gv
        om_ref[...] = mn
        ov_ref[...] = vn

    in_specs = [_row_spec(tr, cols), layer, layer, layer]
    args = [g, w, m, v]
    aliases = {}
    if prev is not None:
        in_specs += [HBM_SPEC] * 4
        args += list(prev)
        aliases = {4 + i: i for i in range(4)}
    return pl.pallas_call(
        body, name="adamw", grid=(rows // tr,), in_specs=in_specs, out_specs=[layer] * 4,
        out_shape=[jax.ShapeDtypeStruct((L, rows, cols), F32)] * 4, input_output_aliases=aliases,
        compiler_params=_params(64 * tr * cols, ("parallel",)),
    )(*args)


WEIGHT_ORDER = ("ln1_g", "w_in", "qn_a", "kn_a", "rpb", "qn_b", "kn_b", "sink", "on_a", "on_b", "w_out", "ln2_g",
                "w_up", "conv_w", "conv_b", "w_down")
BIG = ("w_in", "w_out", "w_up", "w_down")
SMALL = tuple(n for n in WEIGHT_ORDER if n not in BIG)


def _train_step(x, positions, target, w, m, v):
    S, D = x.shape
    L = w["w_in"].shape[0]
    naw = w["rpb"].shape[1] * HEAD_DIM
    waw = w["sink"].shape[1] * HEAD_DIM
    cin = w["w_in"].shape[2]
    kvw = (N_CHIPS * cin - 3 * naw - waw) // 2
    hkv = kvw // HEAD_DIM
    dims = (naw, waw, kvw)
    assert naw == waw, "the two head groups are normalised by one kernel and must be equally wide"
    cup = w["w_up"].shape[2]
    c2 = N_CHIPS * cup
    dff = c2 // 2
    mix = naw + waw
    sh_in = _ColShards(cin, _tile(cin, 1152, LANES), False)
    sh_up = _ColShards(cup, cup // 2 if (cup // 2) % LANES == 0 else cup, True)
    tc = sh_up.tn
    cos2, sgn_sin = _rope_tables(positions)
    q_me = 2 * lax.axis_index("x") + lax.axis_index("y")
    row = lambda a: a[None]

    def own_slots(l):
        small = lax.dynamic_update_slice_in_dim(jnp.zeros((N_CHIPS,) + w["conv_w"].shape[1:], F32),
                                                w["conv_w"][l][None], q_me, axis=0)
        return [_cast_layer(w[n], l) for n in BIG], small

    gath = {}

    def prepare(l):
        bufs, small = own_slots(l)
        gath.update({(l, n): b for n, b in zip(BIG, bufs)})
        gath[l, "conv_w"] = small

    def over_ici(l, names, small=False):
        keys = [(l, n) for n in names]
        ex = _ag_chip_exchange([gath[k] for k in keys], gath[l, "conv_w"] if small else None)
        return ex, keys + ([(l, "conv_w")] if small else [])

    def to_sibling(l, names):
        keys = [(l, n) for n in names]
        return _ag_sibling_pass([gath[k] for k in keys]), keys

    def together(parts):
        ex, keys = parts[0]
        for e, k in parts[1:]:
            ex, keys = _both(ex, e), keys + k
        return ex, keys

    def carried(call, parts):
        if not parts:
            return call(None)
        ex, keys = together(parts)
        out, results = call(ex)
        gath.update(dict(zip(keys, results)))
        return out

    prepare(0)
    ex, keys = over_ici(0, ("w_in", "w_out"), True)
    gath.update(dict(zip(keys, _exchange_alone("ag_first_over_ici", ex))))
    ex, keys = to_sibling(0, ("w_in", "w_out"))
    gath.update(dict(zip(keys, _exchange_alone("ag_first_to_sibling", ex))))
    saved = []
    for l in range(L):
        more = l + 1 < L
        if more:
            prepare(l + 1)
        g_cw = gath[l, "conv_w"]
        cw_p = _to_paired(jnp.transpose(g_cw, (1, 0, 2)).reshape(3, c2), sh_up)
        cb_p = _to_paired(row(w["conv_b"][l]), sh_up)
        gains = jnp.concatenate([row(w["qn_a"][l]), row(w["kn_a"][l]), row(w["qn_b"][l]), row(w["kn_b"][l]),
                                 jnp.zeros((SUBLANES - 4, HEAD_DIM), F32)], axis=0)
        btab = _na_bias_table(w["rpb"][l])
        sink_col = _sink_col(w["sink"][l], hkv)

        h = _rms_fwd(x, row(w["ln1_g"][l]))
        g_in = gath[l, "w_in"]
        proj = carried(lambda ex: _mm_cols_fwd("mm_proj", h, g_in, sh_in, F32, tm=1024, carry=ex),
                       [over_ici(0, ("w_up",))] if l == 0 else [])
        qa, ka, va, qb, kb, vb = _qk_prep_fwd(proj, cos2, sgn_sin, gains, dims)
        oa = _na_fwd(qa, ka, va, btab)
        ob = _wa_fwd(qb, kb, vb, sink_col)
        o = _out_norm_fwd(oa, ob, row(w["on_a"][l]), row(w["on_b"][l]))
        g_out = gath[l, "w_out"].reshape(mix, D)
        x1 = carried(lambda ex: _mm_plain("mm_attn_out", o, g_out, NN, F32, tm=1024, tn=1024, tk=2048, res=x, carry=ex),
                     [to_sibling(0, ("w_up",)), over_ici(0, ("w_down",))] if l == 0 else [])
        h2 = _rms_fwd(x1, row(w["ln2_g"][l]))
        g_up = gath[l, "w_up"]
        up_pre = carried(lambda ex: _mm_cols_fwd("mm_up", h2, g_up, sh_up, F32, tm=1024, carry=ex),
                         [to_sibling(l, ("w_down",))]
                         + ([over_ici(l + 1, ("w_in", "w_out", "w_up"), True)] if more else []))
        act = _convgate_fwd(up_pre, cw_p, cb_p, tc)
        g_dn = gath[l, "w_down"].reshape(dff, D)
        x2 = carried(lambda ex: _mm_plain("mm_down", act, g_dn, NN, F32, tm=1024, tn=512, tk=5632, res=x1, carry=ex),
                     [to_sibling(l + 1, ("w_in", "w_out", "w_up")), over_ici(l + 1, ("w_down",))] if more else [])
        saved.append(dict(g_in=g_in, g_out=g_out, g_up=g_up, g_dn=g_dn, cw_p=cw_p, cb_p=cb_p, gains=gains, btab=btab,
                          sink_col=sink_col, x=x, h=h, proj=proj, qa=qa, ka=ka, va=va, qb=qb, kb=kb, vb=vb, oa=oa,
                          ob=ob, o=o, x1=x1, h2=h2, up_pre=up_pre, act=act))
        x = x2

    dx, dxb, loss_cols = _loss_grad(x, target)

    small_grads = [None] * L
    stacked = {n: None for n in BIG}

    def update(layer, shards):
        for n, g in zip(BIG, shards):
            stacked[n] = _adamw(g, w[n], m[n], v[n], layer, stacked[n])

    above = None
    for l in reversed(range(L)):
        s = saved[l]
        if above is None:
            dact = _mm_plain("mm_down_dx", dxb, s["g_dn"], NT, F32, tm=1024, tn=1408, tk=2048)
        else:
            dact, recvs = _mm_plain("mm_down_dx_rs", dxb, s["g_dn"], NT, F32, tm=1024, tn=1408, tk=2048,
                                    carry=_rs_sibling_exchange(above[1]))
            pairs = [_sum_pair(g, r) for g, r in zip(above[1], recvs)]
        dup_pre, dcw_p, dcb_p = _convgate_bwd(s["up_pre"], dact, s["cw_p"], s["cb_p"], tc)
        if above is None:
            dh2 = _mm_cols_bwd_x("mm_up_dx", dup_pre, s["g_up"], sh_up, F32, tm=1024, tn=2048)
        else:
            dh2, gots = _mm_cols_bwd_x("mm_up_dx_rs", dup_pre, s["g_up"], sh_up, F32, tm=1024, tn=2048,
                                       carry=_rs_chip_exchange([p[1] for p in pairs]))
            halves = [_sum_four(p[0], r) for p, r in zip(pairs, gots)]
        d_dn, (dx1, dx1b, d_ln2) = _mm_plain(
            "mm_down_dw", s["act"], dxb, TN, F32, tm=1408, tn=512, tk=2048, ring_a=True,
            rider=lambda steps: _rms_bwd(dh2, s["x1"], row(w["ln2_g"][l]), dx, steps))
        d_dn = d_dn.reshape(N_CHIPS, dff // N_CHIPS, D)
        if above is None:
            do = _mm_plain("mm_attn_out_dx", dx1b, s["g_out"], NT, F32, tm=1024, tn=1024, tk=2048)
        else:
            do, shards = _mm_plain("mm_attn_out_dx_rs", dx1b, s["g_out"], NT, F32, tm=1024, tn=1024, tk=2048,
                                   carry=_rs_sibling_share(halves))
            update(above[0], shards)
        last = l == 0
        d_out, (doa, dob, d_on_a, d_on_b), *rest = _mm_plain(
            "mm_attn_out_dw", s["o"], dx1b, TN, F32, tm=1024, tn=512, tk=2048,
            carry=_rs_sibling_exchange([d_dn]) if last else None,
            rider=lambda steps: _out_norm_bwd(do, s["oa"], s["ob"], row(w["on_a"][l]), row(w["on_b"][l]), steps))
        d_out = d_out.reshape(N_CHIPS, mix // N_CHIPS, D)
        if last:
            pair_dn = _sum_pair(d_dn, rest[0][0])
        dqa, dka, dva, dbtab = _na_bwd(s["qa"], s["ka"], s["va"], s["btab"], doa)
        dqb, dkb, dvb, dsink_col = _wa_bwd(s["qb"], s["kb"], s["vb"], s["sink_col"], dob)
        d_up, (dproj, dgains) = _mm_cols_bwd_w(
            "mm_up_dw", s["h2"], dup_pre, sh_up, tm=1024, tk=2048,
            rider=lambda steps: _qk_prep_bwd(s["proj"], dqa, dka, dva, dqb, dkb, dvb, cos2, sgn_sin, s["gains"], dims, steps))
        if last:
            dh, (recv_up, got_dn) = _mm_cols_bwd_x(
                "mm_proj_dx", dproj, s["g_in"], sh_in, F32, tm=1024, tn=2048,
                carry=_both(_rs_sibling_exchange([d_up]), _rs_chip_exchange([pair_dn[1]])))
            pair_up = _sum_pair(d_up, recv_up)
        else:
            dh = _mm_cols_bwd_x("mm_proj_dx", dproj, s["g_in"], sh_in, F32, tm=1024, tn=2048)
        d_in, (dx, dxb, d_ln1), *rest = _mm_cols_bwd_w(
            "mm_proj_dw", s["h"], dproj, sh_in, tm=512, tk=2048, ring_a=True,
            carry=_rs_chip_exchange([pair_up[1]]) if last else None,
            rider=lambda steps: _rms_bwd(dh, s["x"], row(w["ln1_g"][l]), dx1, steps))
        if last:
            got_up = rest[0][0]

        small_grads[l] = dict(
            ln1_g=d_ln1[0], qn_a=dgains[0], kn_a=dgains[1], rpb=_na_bias_table_t(dbtab), qn_b=dgains[2],
            kn_b=dgains[3], sink=_sink_col_t(dsink_col), on_a=d_on_a[0], on_b=d_on_b[0], ln2_g=d_ln2[0],
            conv_w=_from_paired(dcw_p, sh_up), conv_b=_from_paired(dcb_p, sh_up)[0])
        above = (l, [d_in, d_out, d_up, d_dn])
    recvs = _exchange_alone("rs_sibling_exchange", _rs_sibling_exchange([d_in, d_out]))
    pair_in, pair_out = _sum_pair(d_in, recvs[0]), _sum_pair(d_out, recvs[1])
    got_in, got_out = _exchange_alone("rs_chip_exchange", _rs_chip_exchange([pair_in[1], pair_out[1]]))
    halves = [_sum_four(p[0], g) for p, g in ((pair_in, got_in), (pair_out, got_out), (pair_up, got_up), (pair_dn, got_dn))]
    update(0, _exchange_alone("rs_sibling_share", _rs_sibling_share(halves)))
    grad_x = dx[None]

    full_shapes = {n: (w[n].shape[1:] if n != "conv_w" else (3, c2)) for n in SMALL}
    packed = _pack([loss_cols] + [small_grads[l][n] for l in range(L) for n in SMALL])
    total = _sum_eight(_gather_small(packed))
    parts = _unpack(total, [(D,)] + [full_shapes[n] for _ in range(L) for n in SMALL])
    loss = 0.5 * jnp.sum(parts[0]) / D
    per_layer = [dict(zip(SMALL, parts[1 + l * len(SMALL):1 + (l + 1) * len(SMALL)])) for l in range(L)]
    for l in range(L):
        per_layer[l]["conv_w"] = lax.dynamic_slice_in_dim(per_layer[l]["conv_w"], q_me * cup, cup, axis=1)
    small_g = {n: jnp.stack([per_layer[l][n] for l in range(L)]) for n in SMALL}
    outs = _adamw(_pack([small_g[n] for n in SMALL]), _pack([w[n] for n in SMALL])[None],
                  _pack([m[n] for n in SMALL])[None], _pack([v[n] for n in SMALL])[None], 0, None)
    small_out = [dict(zip(SMALL, _unpack(o[0], [w[n].shape for n in SMALL]))) for o in outs]

    result = [loss, grad_x]
    for i in range(4):
        result += [stacked[n][i] if n in BIG else small_out[i][n] for n in WEIGHT_ORDER]
    return tuple(result)


def kernel(x, positions, ln1_g, w_in, qn_a, kn_a, rpb, qn_b, kn_b, sink, on_a, on_b, w_out, ln2_g, w_up, conv_w, conv_b, w_down, loss_target, m_ln1_g, m_w_in, m_qn_a, m_kn_a, m_rpb, m_qn_b, m_kn_b, m_sink, m_on_a, m_on_b, m_w_out, m_ln2_g, m_w_up, m_conv_w, m_conv_b, m_w_down, v_ln1_g, v_w_in, v_qn_a, v_kn_a, v_rpb, v_qn_b, v_kn_b, v_sink, v_on_a, v_on_b, v_w_out, v_ln2_g, v_w_up, v_conv_w, v_conv_b, v_w_down):
    given = dict(locals())
    w = {n: given[n] for n in WEIGHT_ORDER}
    m = {n: given["m_" + n] for n in WEIGHT_ORDER}
    v = {n: given["v_" + n] for n in WEIGHT_ORDER}
    return _train_step(x[0], positions, loss_target[0], w, m, v)
```
